```python
import math
import jax, jax.numpy as jnp
from jax import lax
import numpy as np

D_MODEL = 1024
BATCH = 4
SEQ = 4096
DEPTH = 2
DEC_BATCH = 32
DEC_SEQ = 8
PAST_LEN = 8192
PAGE_SIZE = 128

N_META = 16
N_BRANCH = 4
BRANCH_W = 256
D_FF = 2816
CONV_C = 256
CONV_K = 31
RET_HEADS = 4
RET_DK = 64
RET_DV = 64
GLA_HEADS = 4
GLA_DK = 32
GLA_DV = 64
GLA_RANK = 16
GLA_TAU = 16.0
DIFF_HEADS = 4
DIFF_HD = 32
DIFF_VD = 2 * DIFF_HD
REL_BUCKETS = 32
REL_MAX_DIST = 128
CHUNK = 128
Q_BLOCK = 128
ROPE_BASE = 10000.0
EPS = 1e-6
IN_SIZES = (CONV_C, CONV_C,
            RET_HEADS * RET_DK, RET_HEADS * RET_DK, RET_HEADS * RET_DV, RET_HEADS * RET_DV,
            GLA_HEADS * GLA_DK, GLA_HEADS * GLA_DK, GLA_HEADS * GLA_DV, GLA_HEADS * GLA_DV, GLA_RANK,
            DIFF_HEADS * 2 * DIFF_HD, DIFF_HEADS * 2 * DIFF_HD, DIFF_HEADS * DIFF_VD,
            N_BRANCH * D_MODEL)
N_IN = sum(IN_SIZES)

kernel_name = "hybrid_gated_branch_decoder_step"


def rmsnorm(x, g):
    xf = x.astype(jnp.float32)
    y = xf * lax.rsqrt(jnp.mean(xf * xf, axis=-1, keepdims=True) + EPS)
    return (y * g.astype(jnp.float32)).astype(x.dtype)


def layernorm(x, g, b):
    xf = x.astype(jnp.float32)
    mu = jnp.mean(xf, axis=-1, keepdims=True)
    xc = xf - mu
    y = xc * lax.rsqrt(jnp.mean(xc * xc, axis=-1, keepdims=True) + EPS)
    return (y * g.astype(jnp.float32) + b.astype(jnp.float32)).astype(x.dtype)


def head_norm(o, g, center):
    of = o.astype(jnp.float32)
    if center:
        of = of - jnp.mean(of, axis=-1, keepdims=True)
    y = of * lax.rsqrt(jnp.mean(of * of, axis=-1, keepdims=True) + EPS) * g.astype(jnp.float32)
    return y.reshape(o.shape[0], o.shape[1], -1).astype(o.dtype)


def swiglu(x, wg, wu, wd):
    return (jax.nn.silu(x @ wg) * (x @ wu)) @ wd


def rope(x, pos):
    half = x.shape[-1] // 2
    inv = ROPE_BASE ** (-jnp.arange(half, dtype=jnp.float32) / half)
    ang = pos.astype(jnp.float32)[:, None] * inv[None, :]
    cos = jnp.cos(ang)[None, :, None, :]
    sin = jnp.sin(ang)[None, :, None, :]
    xf = x.astype(jnp.float32)
    x1, x2 = xf[..., :half], xf[..., half:]
    return jnp.concatenate([x1 * cos - x2 * sin, x1 * sin + x2 * cos], axis=-1).astype(x.dtype)


def decayed_linear_attention(q, k, v, log_decay, state0, front_pad):
    B, L, H, _ = q.shape
    DV = v.shape[-1]
    total = L + front_pad
    c = min(CHUNK, total)
    n = -(-total // c)
    back = n * c - total

    def to_chunks(a):
        a = jnp.pad(a.astype(jnp.float32), ((0, 0), (front_pad, back), (0, 0), (0, 0)))
        return a.reshape(B, n, c, H, a.shape[-1]).transpose(1, 0, 3, 2, 4)

    qc, kc, vc, gc = to_chunks(q), to_chunks(k), to_chunks(v), to_chunks(log_decay)
    causal = jnp.tril(jnp.ones((c, c), dtype=bool))[None, None, :, :, None]

    def step(S, inp):
        qb, kb, vb, gb = inp
        b = jnp.cumsum(gb, axis=2)
        rel = b[:, :, :, None, :] - b[:, :, None, :, :]
        decay = jnp.exp(jnp.where(causal, rel, -jnp.inf))
        scores = jnp.sum(qb[:, :, :, None, :] * kb[:, :, None, :, :] * decay, axis=-1)
        o = (jnp.einsum('bhts,bhsv->bhtv', scores, vb)
             + jnp.einsum('bhtk,bhkv->bhtv', qb * jnp.exp(b), S))
        b_last = b[:, :, -1:, :]
        S_new = (jnp.exp(b_last[:, :, 0, :])[..., None] * S
                 + jnp.einsum('bhsk,bhsv->bhkv', kb * jnp.exp(b_last - b), vb))
        return S_new, o

    S_fin, o = lax.scan(step, state0.astype(jnp.float32), (qc, kc, vc, gc))
    o = o.transpose(1, 0, 3, 2, 4).reshape(B, n * c, H, DV)[:, front_pad:front_pad + L]
    return o.astype(v.dtype), S_fin.astype(state0.dtype)


def t5_bucket(rel):
    n = jnp.maximum(rel, 0)
    max_exact = REL_BUCKETS // 2
    nf = jnp.maximum(n, 1).astype(jnp.float32)
    large = max_exact + (jnp.log(nf / max_exact) / math.log(REL_MAX_DIST / max_exact)
                         * (REL_BUCKETS - max_exact)).astype(jnp.int32)
    large = jnp.minimum(large, REL_BUCKETS - 1)
    return jnp.where(n < max_exact, n, large)


def diff_attention(q1, q2, k1, k2, v, q_pos, k_pos, rel_bias, lam):
    B, Lq, H, d = q1.shape
    qb = min(Q_BLOCK, Lq)
    nb = -(-Lq // qb)
    padq = nb * qb - Lq

    def prep(a):
        a = jnp.pad(a, ((0, 0), (0, padq), (0, 0), (0, 0)))
        return a.reshape(B, nb, qb, H, d).transpose(1, 0, 2, 3, 4)

    pos_blocks = jnp.pad(q_pos, (0, padq), mode='edge').reshape(nb, qb)
    scale = d ** -0.5
    lam32 = lam.astype(jnp.float32)

    def block(args):
        a1, a2, p = args
        rel = p[:, None] - k_pos[None, :]
        bias = rel_bias[t5_bucket(rel)].transpose(2, 0, 1).astype(jnp.float32)
        mask = rel >= 0

        def probs(a, kk):
            s = jnp.einsum('bqhd,bkhd->bhqk', a, kk).astype(jnp.float32) * scale + bias
            return jax.nn.softmax(jnp.where(mask, s, -jnp.inf), axis=-1)

        attn = probs(a1, k1) - lam32 * probs(a2, k2)
        return jnp.einsum('bhqk,bkhv->bqhv', attn.astype(v.dtype), v)

    o = lax.map(block, (prep(q1), prep(q2), pos_blocks))
    return o.transpose(1, 0, 2, 3, 4).reshape(B, nb * qb, H, v.shape[-1])[:, :Lq]


def trunk_layer(h, pos, k_pos, front_pad, conv_past, ret_s0, gla_s0, k_past, v_past,
                rel_bias, lam_init, w):
    B, L, _ = h.shape
    h = h + 0.5 * swiglu(rmsnorm(h, w['norm_ffn1']), w['ffn1_gate'], w['ffn1_up'], w['ffn1_down'])
    x = rmsnorm(h, w['norm_mix'])
    z = x @ w['w_in'] + w['b_in']
    split_at = [int(s) for s in np.cumsum(IN_SIZES)[:-1]]
    (c_a, c_g, r_q, r_k, r_v, r_g, g_q, g_k, g_v, g_r, g_low,
     d_q, d_k, d_v, gates) = jnp.split(z, split_at, axis=-1)

    u = c_a * jax.nn.sigmoid(c_g)
    u_ext = jnp.concatenate([conv_past.astype(u.dtype), u], axis=1)
    conv = lax.conv_general_dilated(u_ext, w['conv_w'][:, None, :], window_strides=(1,),
                                    padding='VALID', dimension_numbers=('NWC', 'WIO', 'NWC'),
                                    feature_group_count=CONV_C)
    br_conv = jax.nn.silu(layernorm(conv + w['conv_b'], w['conv_ln_g'], w['conv_ln_b']))
    conv_state = u_ext[:, u_ext.shape[1] - (CONV_K - 1):]

    rq = rope(r_q.reshape(B, L, RET_HEADS, RET_DK), pos)
    rk = rope(r_k.reshape(B, L, RET_HEADS, RET_DK), pos) * RET_DK ** -0.5
    rv = r_v.reshape(B, L, RET_HEADS, RET_DV)
    log_gamma = jnp.log1p(-jnp.exp2(-5.0 - jnp.arange(RET_HEADS, dtype=jnp.float32)))
    ret_decay = jnp.broadcast_to(log_gamma[:, None], (B, L, RET_HEADS, 1))
    r_o, ret_state = decayed_linear_attention(rq, rk, rv, ret_decay, ret_s0, front_pad)
    br_ret = jax.nn.silu(r_g) * head_norm(r_o, w['ret_gn'], center=True)

    gq = g_q.reshape(B, L, GLA_HEADS, GLA_DK) * GLA_DK ** -0.5
    gk = g_k.reshape(B, L, GLA_HEADS, GLA_DK)
    gv = g_v.reshape(B, L, GLA_HEADS, GLA_DV)
    g_pre = (g_low @ w['gla_alpha_w'] + w['gla_alpha_b']).astype(jnp.float32)
    gla_decay = (jax.nn.log_sigmoid(g_pre) / GLA_TAU).reshape(B, L, GLA_HEADS, GLA_DK)
    g_o, gla_state = decayed_linear_attention(gq, gk, gv, gla_decay, gla_s0, front_pad)
    br_gla = jax.nn.silu(g_r) * head_norm(g_o, w['gla_gn'], center=False)

    dq = rmsnorm(d_q.reshape(B, L, DIFF_HEADS, 2, DIFF_HD), w['q_norm'])
    dk = rmsnorm(d_k.reshape(B, L, DIFF_HEADS, 2, DIFF_HD), w['k_norm'])
    dv = d_v.reshape(B, L, DIFF_HEADS, DIFF_VD)
    k_rows = dk.reshape(B, L, DIFF_HEADS, 2 * DIFF_HD)
    if k_past is None:
        k_all, v_all = dk, dv
    else:
        k_all = jnp.concatenate([k_past.astype(dk.dtype).reshape(B, -1, DIFF_HEADS, 2, DIFF_HD), dk], axis=1)
        v_all = jnp.concatenate([v_past.astype(dv.dtype), dv], axis=1)
    lam = (jnp.exp(jnp.sum(w['lam_q1'].astype(jnp.float32) * w['lam_k1'].astype(jnp.float32)))
           - jnp.exp(jnp.sum(w['lam_q2'].astype(jnp.float32) * w['lam_k2'].astype(jnp.float32)))
           + lam_init)
    d_o = diff_attention(dq[..., 0, :], dq[..., 1, :], k_all[..., 0, :], k_all[..., 1, :], v_all,
                         pos, k_pos, rel_bias, lam)
    br_diff = head_norm(d_o, w['diff_subln'], center=False) * (1.0 - lam_init)

    branches = jnp.stack([br_conv, br_ret, br_gla, br_diff], axis=2)
    gate = jax.nn.sigmoid(gates.reshape(B, L, N_BRANCH, D_MODEL))
    merged = jnp.einsum('blnc,ncd,blnd->bld', branches, w['w_branch'], gate)
    h = h + merged @ w['w_out']
    h = h + 0.5 * swiglu(rmsnorm(h, w['norm_ffn2']), w['ffn2_gate'], w['ffn2_up'], w['ffn2_down'])
    return h, (k_rows, dv, ret_state, gla_state, conv_state)


def setup_inputs(seed: int = 0) -> dict:
    key = jax.random.key(seed)
    keys = iter(jax.random.split(key, 64))
    f32 = jnp.float32

    def nrm(shape, scale):
        return jax.random.normal(next(keys), shape, f32) * scale

    def gain(shape):
        return 1.0 + nrm(shape, 0.05)

    n_pages = PAST_LEN // PAGE_SIZE
    n_pool = (DEC_BATCH * n_pages * 5) // 4
    page_table = jax.random.permutation(next(keys), n_pool)[:DEC_BATCH * n_pages]
    page_table = page_table.reshape(DEC_BATCH, n_pages).astype(jnp.int32)
    return {
        'x_prompt': nrm((BATCH, SEQ, D_MODEL), 1.0),
        'x_sample': nrm((DEC_BATCH, DEC_SEQ, D_MODEL), 1.0),
        'cache_k': nrm((DEPTH, n_pool, PAGE_SIZE, DIFF_HEADS, 2 * DIFF_HD), 1.0),
        'cache_v': nrm((DEPTH, n_pool, PAGE_SIZE, DIFF_HEADS, DIFF_VD), 1.0),
        'page_table': page_table,
        'state_ret': nrm((DEPTH, DEC_BATCH, RET_HEADS, RET_DK, RET_DV), 0.3),
        'state_gla': nrm((DEPTH, DEC_BATCH, GLA_HEADS, GLA_DK, GLA_DV), 0.3),
        'state_conv': nrm((DEPTH, DEC_BATCH, CONV_K - 1, CONV_C), 0.5),
        'meta_tokens': nrm((N_META, D_MODEL), 1.0),
        'rel_bias': nrm((REL_BUCKETS, DIFF_HEADS), 0.5),
        'norm_ffn1': gain((DEPTH, D_MODEL)),
        'ffn1_gate': nrm((DEPTH, D_MODEL, D_FF), D_MODEL ** -0.5),
        'ffn1_up': nrm((DEPTH, D_MODEL, D_FF), D_MODEL ** -0.5),
        'ffn1_down': nrm((DEPTH, D_FF, D_MODEL), D_FF ** -0.5),
        'norm_mix': gain((DEPTH, D_MODEL)),
        'w_in': nrm((DEPTH, D_MODEL, N_IN), D_MODEL ** -0.5),
        'b_in': nrm((DEPTH, N_IN), 0.02),
        'conv_w': nrm((DEPTH, CONV_K, CONV_C), CONV_K ** -0.5),
        'conv_b': nrm((DEPTH, CONV_C), 0.02),
        'conv_ln_g': gain((DEPTH, CONV_C)),
        'conv_ln_b': nrm((DEPTH, CONV_C), 0.02),
        'ret_gn': gain((DEPTH, RET_HEADS, RET_DV)),
        'gla_alpha_w': nrm((DEPTH, GLA_RANK, GLA_HEADS * GLA_DK), GLA_RANK ** -0.5),
        'gla_alpha_b': nrm((DEPTH, GLA_HEADS * GLA_DK), 0.02),
        'gla_gn': gain((DEPTH, GLA_HEADS, GLA_DV)),
        'q_norm': gain((DEPTH, DIFF_HD)),
        'k_norm': gain((DEPTH, DIFF_HD)),
        'lam_q1': nrm((DEPTH, DIFF_HD), 0.1),
        'lam_k1': nrm((DEPTH, DIFF_HD), 0.1),
        'lam_q2': nrm((DEPTH, DIFF_HD), 0.1),
        'lam_k2': nrm((DEPTH, DIFF_HD), 0.1),
        'diff_subln': gain((DEPTH, DIFF_VD)),
        'w_branch': nrm((DEPTH, N_BRANCH, BRANCH_W, D_MODEL), BRANCH_W ** -0.5),
        'w_out': nrm((DEPTH, D_MODEL, D_MODEL), D_MODEL ** -0.5),
        'norm_ffn2': gain((DEPTH, D_MODEL)),
        'ffn2_gate': nrm((DEPTH, D_MODEL, D_FF), D_MODEL ** -0.5),
        'ffn2_up': nrm((DEPTH, D_MODEL, D_FF), D_MODEL ** -0.5),
        'ffn2_down': nrm((DEPTH, D_FF, D_MODEL), D_FF ** -0.5),
    }


def reference(x_prompt, x_sample, cache_k, cache_v, page_table, state_ret, state_gla, state_conv,
              meta_tokens, rel_bias, norm_ffn1, ffn1_gate, ffn1_up, ffn1_down, norm_mix, w_in, b_in,
              conv_w, conv_b, conv_ln_g, conv_ln_b, ret_gn, gla_alpha_w, gla_alpha_b, gla_gn,
              q_norm, k_norm, lam_q1, lam_k1, lam_q2, lam_k2, diff_subln, w_branch, w_out,
              norm_ffn2, ffn2_gate, ffn2_up, ffn2_down):
    B, S, _ = x_prompt.shape
    DB, DS, _ = x_sample.shape
    L = S + N_META
    meta = jnp.broadcast_to(meta_tokens[None].astype(x_prompt.dtype), (B, N_META, D_MODEL))
    h_p = jnp.concatenate([meta, x_prompt], axis=1)
    h_s = x_sample
    pos_p = jnp.arange(L, dtype=jnp.int32)
    pos_s = PAST_LEN + jnp.arange(DS, dtype=jnp.int32)
    kpos_s = jnp.arange(PAST_LEN + DS, dtype=jnp.int32)
    front_pad = (CHUNK - N_META % CHUNK) % CHUNK

    kp_l, vp_l, rp_l, gp_l, cp_l = [], [], [], [], []
    ks_l, vs_l, rs_l, gs_l, cs_l = [], [], [], [], []
    for l in range(DEPTH):
        w = dict(norm_ffn1=norm_ffn1[l], ffn1_gate=ffn1_gate[l], ffn1_up=ffn1_up[l],
                 ffn1_down=ffn1_down[l], norm_mix=norm_mix[l], w_in=w_in[l], b_in=b_in[l],
                 conv_w=conv_w[l], conv_b=conv_b[l], conv_ln_g=conv_ln_g[l], conv_ln_b=conv_ln_b[l],
                 ret_gn=ret_gn[l], gla_alpha_w=gla_alpha_w[l], gla_alpha_b=gla_alpha_b[l],
                 gla_gn=gla_gn[l], q_norm=q_norm[l], k_norm=k_norm[l], lam_q1=lam_q1[l],
                 lam_k1=lam_k1[l], lam_q2=lam_q2[l], lam_k2=lam_k2[l], diff_subln=diff_subln[l],
                 w_branch=w_branch[l], w_out=w_out[l], norm_ffn2=norm_ffn2[l],
                 ffn2_gate=ffn2_gate[l], ffn2_up=ffn2_up[l], ffn2_down=ffn2_down[l])
        lam_init = 0.8 - 0.6 * math.exp(-0.3 * l)
        h_p, (kp, vp, rp, gp, cp) = trunk_layer(
            h_p, pos_p, pos_p, front_pad,
            jnp.zeros((B, CONV_K - 1, CONV_C), h_p.dtype),
            jnp.zeros((B, RET_HEADS, RET_DK, RET_DV), state_ret.dtype),
            jnp.zeros((B, GLA_HEADS, GLA_DK, GLA_DV), state_gla.dtype),
            None, None, rel_bias, lam_init, w)
        k_past = jnp.take(cache_k[l], page_table, axis=0).reshape(DB, PAST_LEN, DIFF_HEADS, 2 * DIFF_HD)
        v_past = jnp.take(cache_v[l], page_table, axis=0).reshape(DB, PAST_LEN, DIFF_HEADS, DIFF_VD)
        h_s, (ks, vs, rs, gs, cs) = trunk_layer(
            h_s, pos_s, kpos_s, 0, state_conv[l], state_ret[l], state_gla[l],
            k_past, v_past, rel_bias, lam_init, w)
        kp_l.append(kp); vp_l.append(vp); rp_l.append(rp); gp_l.append(gp); cp_l.append(cp)
        ks_l.append(ks); vs_l.append(vs); rs_l.append(rs); gs_l.append(gs); cs_l.append(cs)

    y_prompt = h_p[:, N_META:]
    y_sample = h_s
    k_prompt = jnp.stack(kp_l, axis=0)
    v_prompt = jnp.stack(vp_l, axis=0)
    k_sample = jnp.stack(ks_l, axis=0)
    v_sample = jnp.stack(vs_l, axis=0)
    ret_prompt = jnp.stack(rp_l, axis=0)
    ret_sample = jnp.stack(rs_l, axis=0)
    gla_prompt = jnp.stack(gp_l, axis=0)
    gla_sample = jnp.stack(gs_l, axis=0)
    conv_prompt = jnp.stack(cp_l, axis=0)
    conv_sample = jnp.stack(cs_l, axis=0)
    return (y_prompt, y_sample, k_prompt, v_prompt, k_sample, v_sample,
            ret_prompt, ret_sample, gla_prompt, gla_sample, conv_prompt, conv_sample)
```

```python
import functools
import math

import numpy as np
import jax
import jax.numpy as jnp
from jax import lax
from jax.experimental import pallas as pl
from jax.experimental.pallas import tpu as pltpu

F32 = jnp.float32
BF16 = jnp.bfloat16

D_MODEL = 1024
N_META = 16
N_BRANCH = 4
BRANCH_W = 256
D_FF = 2816
CONV_C = 256
CONV_K = 31
HEADS = 4
RET_DK = 64
RET_DV = 64
GLA_DK = 32
GLA_DV = 64
GLA_RANK = 16
GLA_TAU = 16.0
DIFF_HD = 32
DIFF_VD = 64
REL_BUCKETS = 32
REL_MAX_DIST = 128
ROPE_BASE = 10000.0
EPS = 1e-6
NEG = -1e30

N_LIN = 2304
OFF_GLOW = N_LIN
OFF_DIFF = OFF_GLOW + GLA_RANK
OFF_GATES = OFF_DIFF + 3 * 256
N_IN = OFF_GATES + N_BRANCH * D_MODEL

LANE = 128
ROW_TILE = 256
SEQ_BLOCK = 128
FAR_BLOCK = 512
HIST = 32
VMEM_LIMIT = 56 * 1024 * 1024
PAGES_PER_STEP = 16


def _sigmoid(x):
    return 1.0 / (1.0 + jnp.exp(-x))


def _silu(x):
    return x * _sigmoid(x)


def _log_sigmoid(x):
    return jnp.minimum(x, 0.0) - jnp.log1p(jnp.exp(-jnp.abs(x)))


def _rms(x, g):
    return x * lax.rsqrt(jnp.mean(x * x, axis=-1, keepdims=True) + EPS) * g


def _dot(a, b):
    return jnp.dot(a, b, preferred_element_type=F32)


def _dot_nt(a, b):
    return lax.dot_general(a, b, (((1,), (1,)), ((), ())), preferred_element_type=F32)


def _dot_tn(a, b):
    return lax.dot_general(a, b, (((0,), (0,)), ((), ())), preferred_element_type=F32)


def _split_bf16(x):
    hi = x.astype(BF16)
    lo = (x - hi.astype(F32)).astype(BF16)
    return hi, lo


def _group_mean(x, group):
    n = x.shape[-1]
    r = lax.broadcasted_iota(jnp.int32, (n, n), 0) // group
    c = lax.broadcasted_iota(jnp.int32, (n, n), 1) // group
    avg = jnp.where(r == c, 1.0 / group, 0.0).astype(BF16)
    hi, lo = _split_bf16(x)
    return _dot(hi, avg) + _dot(lo, avg)


def _const_spec(shape):
    nd = len(shape)
    return pl.BlockSpec(shape, lambda *_: (0,) * nd, pipeline_mode=pl.Buffered(1))


def _ffn(x, wg_ref, wu_ref, wd_ref):
    gate = _dot(x, wg_ref[...])
    up = _dot(x, wu_ref[...])
    act = (_silu(gate) * up).astype(BF16)
    return _dot(act, wd_ref[...])


def _head_kernel(h_ref, g1_ref, wg_ref, wu_ref, wd_ref, gm_ref, wlin_ref, blin_ref, wlow_ref, blow_ref,
                 aw_ref, ab_ref, wdf_ref, bdf_ref, qg_ref, kg_ref,
                 h1_ref, zlin_ref, gdec_ref, dq_ref, dk_ref, dv_ref):
    h = h_ref[...]
    h1 = h + 0.5 * _ffn(_rms(h, g1_ref[...]).astype(BF16), wg_ref, wu_ref, wd_ref)
    h1_ref[...] = h1
    x = _rms(h1, gm_ref[...]).astype(BF16)
    zlin_ref[...] = _dot(x, wlin_ref[...]) + blin_ref[...]
    g_low = _dot(x, wlow_ref[...]) + blow_ref[...]
    g_pre = _dot(g_low.astype(BF16), aw_ref[...]) + ab_ref[...]
    gdec_ref[...] = _log_sigmoid(g_pre) * (1.0 / GLA_TAU)
    zd = _dot(x, wdf_ref[...]) + bdf_ref[...]
    d_q = zd[:, 0:256]
    d_k = zd[:, 256:512]
    dq_ref[...] = d_q * lax.rsqrt(_group_mean(d_q * d_q, DIFF_HD) + EPS) * qg_ref[...]
    dk_ref[...] = d_k * lax.rsqrt(_group_mean(d_k * d_k, DIFF_HD) + EPS) * kg_ref[...]
    dv_ref[...] = zd[:, 512:768]


def _head_call(h, w):
    rows = h.shape[0]
    row = lambda n: pl.BlockSpec((ROW_TILE, n), lambda i: (i, 0))
    consts = [w['g1'], w['wg1'], w['wu1'], w['wd1'], w['gm'], w['wlin'], w['blin'], w['wlow'], w['blow'],
              w['aw'], w['ab'], w['wdf'], w['bdf'], w['qg'], w['kg']]
    widths = (D_MODEL, N_LIN, LANE, 256, 256, 256)
    return pl.pallas_call(
        _head_kernel,
        grid=(rows // ROW_TILE,),
        in_specs=[row(D_MODEL)] + [_const_spec(c.shape) for c in consts],
        out_specs=[row(n) for n in widths],
        out_shape=[jax.ShapeDtypeStruct((rows, n), F32) for n in widths],
        compiler_params=pltpu.CompilerParams(dimension_semantics=("arbitrary",),
                                             vmem_limit_bytes=VMEM_LIMIT),
        name="head",
    )(h, *consts)


def _tail_kernel(h1_ref, brl_ref, brd_ref, gm_ref, wgt_ref, bgt_ref, wb_ref, wo_ref, g2_ref,
                 wg_ref, wu_ref, wd_ref, out_ref):
    h1 = h1_ref[...]
    x = _rms(h1, gm_ref[...]).astype(BF16)
    merged = None
    for n in range(N_BRANCH):
        if n < 3:
            br = brl_ref[:, n * BRANCH_W:(n + 1) * BRANCH_W]
        else:
            br = brd_ref[...]
        gate = _dot(x, wgt_ref[:, n * D_MODEL:(n + 1) * D_MODEL]) + bgt_ref[:, n * D_MODEL:(n + 1) * D_MODEL]
        term = _dot(br.astype(BF16), wb_ref[n]) * _sigmoid(gate)
        merged = term if merged is None else merged + term
    h2 = h1 + _dot(merged.astype(BF16), wo_ref[...])
    out_ref[...] = h2 + 0.5 * _ffn(_rms(h2, g2_ref[...]).astype(BF16), wg_ref, wu_ref, wd_ref)


def _tail_call(h1, br_lin, br_diff, w):
    rows = h1.shape[0]
    row = lambda n: pl.BlockSpec((ROW_TILE, n), lambda i: (i, 0))
    consts = [w['gm'], w['wgt'], w['bgt'], w['wb'], w['wo'], w['g2'], w['wg2'], w['wu2'], w['wd2']]
    return pl.pallas_call(
        _tail_kernel,
        grid=(rows // ROW_TILE,),
        in_specs=[row(D_MODEL), row(3 * BRANCH_W), row(BRANCH_W)] + [_const_spec(c.shape) for c in consts],
        out_specs=row(D_MODEL),
        out_shape=jax.ShapeDtypeStruct((rows, D_MODEL), F32),
        compiler_params=pltpu.CompilerParams(dimension_semantics=("arbitrary",),
                                             vmem_limit_bytes=VMEM_LIMIT),
        name="tail",
    )(h1, br_lin, br_diff, *consts)


def _mix_kernel(*refs, chunk, n_chunks, seq_len, aliased):
    (z_ref, gd_ref, cos_ref, sin_ref, cpast_ref, sr0_ref, sg0_ref, cw_ref, cb_ref, lng_ref, lnb_ref,
     rgn_ref, ggn_ref) = refs[:13]
    refs = refs[13 + (1 if aliased else 0):]
    br_ref, sr_out, sg_out, cs_out, sr_s, sg_s, u_s = refs
    C = chunk
    c = pl.program_id(1)
    low = C >= 16
    cast = (lambda a: a.astype(BF16)) if low else (lambda a: a)

    t_col = lax.broadcasted_iota(jnp.int32, (C, 1), 0)
    s_row = lax.broadcasted_iota(jnp.int32, (1, C), 1)
    causal = t_col >= s_row
    tf = t_col.astype(F32)
    padded = n_chunks * C > seq_len
    if padded:
        valid = (c * C + t_col) < seq_len
        nvf = jnp.zeros((1, 1), F32) + jnp.minimum(seq_len - c * C, C).astype(F32)
        keep = lambda a: jnp.where(valid, a, 0.0)
    else:
        nvf = jnp.full((1, 1), float(C), F32)
        keep = lambda a: a

    @pl.when(c == 0)
    def _init():
        sr_s[...] = sr0_ref[0]
        sg_s[...] = sg0_ref[0]
        u_s[0:HIST - (CONV_K - 1), :] = jnp.zeros((HIST - (CONV_K - 1), CONV_C), F32)
        u_s[HIST - (CONV_K - 1):HIST, :] = cpast_ref[0]

    @pl.when(c > 0)
    def _shift():
        u_s[0:HIST, :] = u_s[C:C + HIST, :]

    u = z_ref[:, 0:256] * _sigmoid(z_ref[:, 256:512])
    u_s[HIST:HIST + C, :] = u
    acc = jnp.zeros((C, CONV_C), F32)
    for j in range(CONV_K):
        lo = HIST - (CONV_K - 1) + j
        acc = acc + cw_ref[j:j + 1, :] * u_s[lo:lo + C, :]
    conv = acc + cb_ref[...]
    xc = conv - jnp.mean(conv, axis=-1, keepdims=True)
    ln = xc * lax.rsqrt(jnp.mean(xc * xc, axis=-1, keepdims=True) + EPS) * lng_ref[...] + lnb_ref[...]
    br_ref[:, 0:256] = _silu(ln)

    lane = lax.broadcasted_iota(jnp.int32, (1, 256), 1)
    first_half = (lane % RET_DK) < (RET_DK // 2)
    cos = cos_ref[...]
    sin = sin_ref[...]

    def rope(a):
        swapped = jnp.where(first_half, pltpu.roll(a, 256 - RET_DK // 2, 1), pltpu.roll(a, RET_DK // 2, 1))
        return a * cos + swapped * sin

    rq = rope(z_ref[:, 512:768])
    rk = keep(rope(z_ref[:, 768:1024]) * RET_DK ** -0.5)
    rv = keep(z_ref[:, 1024:1280])
    dts = (t_col - s_row).astype(F32)
    for h in range(HEADS):
        lg = math.log1p(-(2.0 ** (-5 - h)))
        sl = slice(h * RET_DK, (h + 1) * RET_DK)
        qh, kh, vh = rq[:, sl], rk[:, sl], cast(rv[:, sl])
        decay = jnp.where(causal, jnp.exp(dts * lg), 0.0)
        a = _dot_nt(cast(qh), cast(kh)) * decay
        s_old = sr_s[h]
        o = _dot(cast(a), vh) + _dot(cast(qh * jnp.exp((tf + 1.0) * lg)), cast(s_old))
        sr_s[h] = jnp.exp(nvf * lg) * s_old + _dot_tn(cast(kh * jnp.exp((nvf - 1.0 - tf) * lg)), vh)
        oc = o - jnp.mean(o, axis=-1, keepdims=True)
        y = oc * lax.rsqrt(jnp.mean(oc * oc, axis=-1, keepdims=True) + EPS) * rgn_ref[:, sl]
        br_ref[:, 256 + h * RET_DV:256 + (h + 1) * RET_DV] = _silu(z_ref[:, 1280 + h * 64:1280 + (h + 1) * 64]) * y

    gq = z_ref[:, 1536:1664] * GLA_DK ** -0.5
    gk = keep(z_ref[:, 1664:1792])
    gv = keep(z_ref[:, 1792:2048])
    g = keep(gd_ref[...])
    tri = jnp.where(causal, 1.0, 0.0)
    if low:
        g_hi, g_lo = _split_bf16(g)
        tri = tri.astype(BF16)
        bcum = _dot(tri, g_hi) + _dot(tri, g_lo)
    else:
        bcum = _dot(tri, g)
    mid = C // 2 - 1
    b_mid = bcum[mid:mid + 1, :]
    b_last = bcum[C - 1:C, :]
    q_intra = gq * jnp.exp(bcum - b_mid)
    k_intra = gk * jnp.exp(b_mid - bcum)
    q_inter = gq * jnp.exp(bcum)
    k_state = gk * jnp.exp(b_last - bcum)
    e_last = jnp.where(t_col == C - 1, jnp.exp(bcum), 0.0)
    ones = jnp.ones((C, GLA_DV), BF16 if low else F32)
    if low:
        e_hi, e_lo = _split_bf16(e_last)
    for h in range(HEADS):
        sl = slice(h * GLA_DK, (h + 1) * GLA_DK)
        vs = slice(h * GLA_DV, (h + 1) * GLA_DV)
        vh = cast(gv[:, vs])
        a = jnp.where(causal, _dot_nt(cast(q_intra[:, sl]), cast(k_intra[:, sl])), 0.0)
        s_old = sg_s[h]
        o = _dot(cast(a), vh) + _dot(cast(q_inter[:, sl]), cast(s_old))
        if low:
            e_col = _dot_tn(e_hi[:, sl], ones) + _dot_tn(e_lo[:, sl], ones)
        else:
            e_col = _dot_tn(e_last[:, sl], ones)
        sg_s[h] = e_col * s_old + _dot_tn(cast(k_state[:, sl]), vh)
        y = o * lax.rsqrt(jnp.mean(o * o, axis=-1, keepdims=True) + EPS) * ggn_ref[:, vs]
        br_ref[:, 512 + h * GLA_DV:512 + (h + 1) * GLA_DV] = _silu(z_ref[:, 2048 + h * 64:2048 + (h + 1) * 64]) * y

    @pl.when(c == n_chunks - 1)
    def _final():
        sr_out[0] = sr_s[...]
        sg_out[0] = sg_s[...]
        n_last = seq_len - (n_chunks - 1) * C
        cs_out[0] = u_s[HIST + n_last - (CONV_K - 1):HIST + n_last, :]


def _mix_call(zlin, gdec, cos, sin, conv_past, ret0, gla0, w, *, n_seq, chunk, n_chunks, seq_len,
              row_block0, br_prev=None):
    rows = zlin.shape[0]
    aliased = br_prev is not None
    blk = lambda n: pl.BlockSpec((chunk, n), lambda s, c: (row_block0 + s * n_chunks + c, 0))
    per_seq = lambda shp: pl.BlockSpec((1,) + shp, lambda s, c: (s,) + (0,) * len(shp))
    const = lambda a: pl.BlockSpec(a.shape, lambda s, c: (0,) * a.ndim)
    consts = [w['cw'], w['cb'], w['lng'], w['lnb'], w['rgn'], w['ggn']]
    in_specs = ([blk(N_LIN), blk(LANE),
                 pl.BlockSpec((chunk, 256), lambda s, c: (c, 0)), pl.BlockSpec((chunk, 256), lambda s, c: (c, 0)),
                 per_seq((CONV_K - 1, CONV_C)), per_seq((HEADS, RET_DK, RET_DV)), per_seq((HEADS, GLA_DK, GLA_DV))]
                + [const(a) for a in consts])
    args = [zlin, gdec, cos, sin, conv_past, ret0, gla0] + consts
    aliases = {}
    if aliased:
        in_specs.append(pl.BlockSpec(memory_space=pl.ANY))
        aliases = {len(args): 0}
        args.append(br_prev)
    return pl.pallas_call(
        functools.partial(_mix_kernel, chunk=chunk, n_chunks=n_chunks, seq_len=seq_len, aliased=aliased),
        grid=(n_seq, n_chunks),
        in_specs=in_specs,
        out_specs=[blk(3 * BRANCH_W), per_seq((HEADS, RET_DK, RET_DV)), per_seq((HEADS, GLA_DK, GLA_DV)),
                   per_seq((CONV_K - 1, CONV_C))],
        out_shape=[jax.ShapeDtypeStruct((rows, 3 * BRANCH_W), F32),
                   jax.ShapeDtypeStruct((n_seq, HEADS, RET_DK, RET_DV), F32),
                   jax.ShapeDtypeStruct((n_seq, HEADS, GLA_DK, GLA_DV), F32),
                   jax.ShapeDtypeStruct((n_seq, CONV_K - 1, CONV_C), F32)],
        scratch_shapes=[pltpu.VMEM((HEADS, RET_DK, RET_DV), F32), pltpu.VMEM((HEADS, GLA_DK, GLA_DV), F32),
                        pltpu.VMEM((HIST + chunk, CONV_C), F32)],
        input_output_aliases=aliases,
        compiler_params=pltpu.CompilerParams(dimension_semantics=("arbitrary", "arbitrary")),
        name="mix_sample" if aliased else "mix_prompt",
    )(*args)


def _t5_bucket_np(rel):
    rel = np.asarray(rel)
    n = np.maximum(rel, 0)
    max_exact = REL_BUCKETS // 2
    nf = np.maximum(n, 1).astype(np.float64)
    large = max_exact + (np.log(nf / max_exact) / math.log(REL_MAX_DIST / max_exact)
                         * (REL_BUCKETS - max_exact)).astype(np.int64)
    large = np.minimum(large, REL_BUCKETS - 1)
    return np.where(n < max_exact, n, large).astype(np.int32)


def _lambda(lamv_ref, lam_init):
    a = jnp.sum(lamv_ref[0:1, :] * lamv_ref[1:2, :], axis=-1, keepdims=True)
    b = jnp.sum(lamv_ref[2:3, :] * lamv_ref[3:4, :], axis=-1, keepdims=True)
    return jnp.exp(a) - jnp.exp(b) + lam_init


def _sub_norm(d, sub_ref, lam_init):
    return d * lax.rsqrt(jnp.mean(d * d, axis=-1, keepdims=True) + EPS) * sub_ref[...] * (1.0 - lam_init)


def _masked_queries(q, n_rows):
    lane = lax.broadcasted_iota(jnp.int32, (1, 256), 1)
    out = []
    for hj in range(2 * HEADS):
        lo = hj * DIFF_HD
        out.append(jnp.where((lane >= lo) & (lane < lo + DIFF_HD), q, 0.0))
    return out


def _flash_kernel(bt_ref, rb_ref, lamv_ref, sub_ref, q_ref, k_ref, v_ref, o_ref,
                  bias_s, qs_s, m_s, l_s, al_s, p_s, acc_s, *, lam_init):
    QB = SEQ_BLOCK
    NR = 2 * HEADS * QB
    b = pl.program_id(0)
    i = pl.program_id(1)

    @pl.when((b == 0) & (i == 0))
    def _tables():
        for kind, src in ((1, 1), (2, 0)):
            bt = bt_ref[src]
            tiles = [jnp.zeros((QB, QB), F32) for _ in range(HEADS)]
            for bk in range(REL_BUCKETS):
                hit = bt == bk
                for h in range(HEADS):
                    tiles[h] = jnp.where(hit, rb_ref[bk, h], tiles[h])
            for h in range(HEADS):
                bias_s[kind, h] = jnp.where(bt < 0, NEG, tiles[h])
        for h in range(HEADS):
            bias_s[0, h] = jnp.zeros((QB, QB), F32) + rb_ref[REL_BUCKETS - 1, h]

    q = q_ref[...] * DIFF_HD ** -0.5
    for hj, qm in enumerate(_masked_queries(q, QB)):
        qs_s[hj * QB:(hj + 1) * QB, :] = qm.astype(BF16)
    m_s[...] = jnp.full((NR, 1), NEG, F32)
    l_s[...] = jnp.zeros((NR, 1), F32)
    acc_s[...] = jnp.zeros((NR, 256), F32)

    def update(width, score_fn, v):
        for hj in range(2 * HEADS):
            rows = slice(hj * QB, (hj + 1) * QB)
            s = score_fn(hj, rows)
            m_old = m_s[rows, :]
            m_new = jnp.maximum(m_old, jnp.max(s, axis=-1, keepdims=True))
            alpha = jnp.exp(m_old - m_new)
            p = jnp.exp(s - m_new)
            l_s[rows, :] = alpha * l_s[rows, :] + jnp.sum(p, axis=-1, keepdims=True)
            m_s[rows, :] = m_new
            al_s[rows, :] = alpha
            p_s[rows, 0:width] = p.astype(BF16)
        acc_s[...] = al_s[...] * acc_s[...] + _dot(p_s[:, 0:width], v)

    n_far = jnp.maximum(i - 1, 0) // (FAR_BLOCK // QB)

    def far_body(kb, carry):
        off = pl.multiple_of(kb * FAR_BLOCK, FAR_BLOCK)
        k = k_ref[pl.ds(off, FAR_BLOCK), :].astype(BF16)
        v = v_ref[pl.ds(off, FAR_BLOCK), :].astype(BF16)
        s_all = _dot_nt(qs_s[...], k)
        update(FAR_BLOCK, lambda hj, rows: s_all[rows, :] + rb_ref[REL_BUCKETS - 1, hj // 2], v)
        return carry

    lax.fori_loop(0, n_far, far_body, 0)

    def near_body(kb, carry):
        off = pl.multiple_of(kb * QB, QB)
        kind = jnp.where(kb == i, 2, jnp.where(kb == i - 1, 1, 0))
        k = k_ref[pl.ds(off, QB), :].astype(BF16)
        v = v_ref[pl.ds(off, QB), :].astype(BF16)
        s_all = _dot_nt(qs_s[...], k)
        update(QB, lambda hj, rows: s_all[rows, :] + bias_s[kind, hj // 2], v)
        return carry

    lax.fori_loop(n_far * (FAR_BLOCK // QB), i + 1, near_body, 0)

    lam = _lambda(lamv_ref, lam_init)
    for h in range(HEADS):
        r1 = slice((2 * h) * QB, (2 * h + 1) * QB)
        r2 = slice((2 * h + 1) * QB, (2 * h + 2) * QB)
        cols = slice(h * DIFF_VD, (h + 1) * DIFF_VD)
        d = acc_s[r1, cols] / l_s[r1, :] - lam * (acc_s[r2, cols] / l_s[r2, :])
        o_ref[:, cols] = _sub_norm(d, sub_ref, lam_init)


def _flash_call(bt, rel_bias, lamv, sub, dq, dk, dv, *, n_seq, seq_rows, lam_init):
    rows = dq.shape[0]
    nq = seq_rows // SEQ_BLOCK
    nr = 2 * HEADS * SEQ_BLOCK
    const = lambda a: pl.BlockSpec(a.shape, lambda b, i: (0,) * a.ndim)
    return pl.pallas_call(
        functools.partial(_flash_kernel, lam_init=lam_init),
        grid=(n_seq, nq),
        in_specs=[const(bt), pl.BlockSpec(memory_space=pltpu.SMEM), const(lamv), const(sub),
                  pl.BlockSpec((SEQ_BLOCK, 256), lambda b, i: (b * nq + i, 0)),
                  pl.BlockSpec((seq_rows, 256), lambda b, i: (b, 0)),
                  pl.BlockSpec((seq_rows, 256), lambda b, i: (b, 0))],
        out_specs=pl.BlockSpec((SEQ_BLOCK, 256), lambda b, i: (b * nq + i, 0)),
        out_shape=jax.ShapeDtypeStruct((rows, 256), F32),
        scratch_shapes=[pltpu.VMEM((3, HEADS, SEQ_BLOCK, SEQ_BLOCK), F32), pltpu.VMEM((nr, 256), BF16),
                        pltpu.VMEM((nr, 1), F32), pltpu.VMEM((nr, 1), F32), pltpu.VMEM((nr, 1), F32),
                        pltpu.VMEM((nr, FAR_BLOCK), BF16), pltpu.VMEM((nr, 256), F32)],
        compiler_params=pltpu.CompilerParams(dimension_semantics=("arbitrary", "arbitrary"),
                                             vmem_limit_bytes=VMEM_LIMIT),
        name="attn_prompt",
    )(bt, rel_bias, lamv, sub, dq, dk, dv)


def _decode_kernel(pt_ref, bt_ref, rb_ref, lamv_ref, sub_ref, q_ref, kn_ref, vn_ref, *refs,
                   pages_per_step, n_steps, dec_seq, lam_init):
    G = pages_per_step
    k_refs = refs[:G]
    v_refs = refs[G:2 * G]
    o_ref, bias_pg_s, bias_new_s, far_s, qt_s, m_s, l_s, acc_s = refs[2 * G + 1:]
    del pt_ref
    b = pl.program_id(0)
    g = pl.program_id(1)
    last = g == n_steps - 1
    NQ = 2 * HEADS * dec_seq

    lane = lax.broadcasted_iota(jnp.int32, (1, LANE), 1)
    lane_head = lane // (2 * dec_seq)

    def bias_lanes(bk):
        out = jnp.zeros((1, LANE), F32)
        for h in range(HEADS):
            out = jnp.where(lane_head == h, rb_ref[bk, h], out)
        return out

    @pl.when((b == 0) & (g == 0))
    def _tables():
        bt = bt_ref[...]
        tile = jnp.zeros(bt.shape, F32)
        for bk in range(REL_BUCKETS):
            tile = jnp.where(bt == bk, bias_lanes(bk), tile)
        tile = jnp.where(bt < 0, NEG, tile)
        bias_pg_s[...] = tile[0:bias_pg_s.shape[0], :]
        bias_new_s[...] = tile[bias_pg_s.shape[0]:, :]
        far_s[...] = bias_lanes(REL_BUCKETS - 1)

    @pl.when(g == 0)
    def _init():
        q = q_ref[...] * DIFF_HD ** -0.5
        qs = jnp.concatenate(_masked_queries(q, dec_seq) + [jnp.zeros((LANE - NQ, 256), F32)], axis=0)
        qt_s[...] = qs.T
        m_s[...] = jnp.full((1, LANE), NEG, F32)
        l_s[...] = jnp.zeros((1, LANE), F32)
        acc_s[...] = jnp.zeros((LANE, 256), F32)

    eye = (lax.broadcasted_iota(jnp.int32, (LANE, LANE), 0) == lax.broadcasted_iota(jnp.int32, (LANE, LANE), 1))

    def to_col(row):
        return jnp.sum(jnp.where(eye, row, 0.0), axis=1, keepdims=True)

    def update(scores, pv_fn):
        m_old = m_s[...]
        m_new = m_old
        for s in scores:
            m_new = jnp.maximum(m_new, jnp.max(s, axis=0, keepdims=True))
        alpha = jnp.exp(m_old - m_new)
        probs = [jnp.exp(s - m_new) for s in scores]
        l_new = alpha * l_s[...]
        for p in probs:
            l_new = l_new + jnp.sum(p, axis=0, keepdims=True)
        l_s[...] = l_new
        m_s[...] = m_new
        acc_s[...] = to_col(alpha) * acc_s[...] + pv_fn(probs)

    qt = qt_s[...]
    qt_bf = qt.astype(BF16)
    scores = []
    for p in range(G):
        s = _dot(k_refs[p][...].astype(BF16), qt_bf)
        if p == G - 1:
            s = s + jnp.where(last, bias_pg_s[...], far_s[...])
        else:
            s = s + far_s[...]
        scores.append(s)

    def pv_pages(probs):
        out = None
        for p in range(G):
            t = _dot_tn(probs[p].astype(BF16), v_refs[p][...].astype(BF16))
            out = t if out is None else out + t
        return out

    update(scores, pv_pages)

    @pl.when(last)
    def _finish():
        s_new = _dot(kn_ref[...], qt) + bias_new_s[...]
        update([s_new], lambda probs: _dot_tn(probs[0], vn_ref[...]))
        o_all = acc_s[...] / to_col(l_s[...])
        lam = _lambda(lamv_ref, lam_init)
        for h in range(HEADS):
            r1 = (2 * h) * dec_seq
            r2 = (2 * h + 1) * dec_seq
            cols = slice(h * DIFF_VD, (h + 1) * DIFF_VD)
            d = o_all[r1:r1 + dec_seq, cols] - lam * o_all[r2:r2 + dec_seq, cols]
            o_ref[:, cols] = _sub_norm(d, sub_ref, lam_init)


def _decode_call(page_table, bt, rel_bias, lamv, sub, dq, dk, dv, cache_k, cache_v, br_prev, *,
                 layer, n_seq, dec_seq, row_block0, lam_init):
    n_pages = page_table.shape[1]
    page = cache_k.shape[2]
    G = min(PAGES_PER_STEP, n_pages)
    n_steps = n_pages // G
    const = lambda a: pl.BlockSpec(a.shape, lambda b, g, pt: (0,) * a.ndim)
    new = pl.BlockSpec((dec_seq, 256), lambda b, g, pt: (row_block0 + b, 0))

    def page_spec(p):
        return pl.BlockSpec((None, None, page, 256),
                            lambda b, g, pt: (layer, pt[b * n_pages + g * G + p], 0, 0))

    in_specs = ([const(bt), pl.BlockSpec(memory_space=pltpu.SMEM), const(lamv), const(sub), new, new, new]
                + [page_spec(p) for p in range(G)] + [page_spec(p) for p in range(G)]
                + [pl.BlockSpec(memory_space=pl.ANY)])
    args = [bt, rel_bias, lamv, sub, dq, dk, dv] + [cache_k] * G + [cache_v] * G + [br_prev]
    return pl.pallas_call(
        functools.partial(_decode_kernel, pages_per_step=G, n_steps=n_steps, dec_seq=dec_seq, lam_init=lam_init),
        grid_spec=pltpu.PrefetchScalarGridSpec(
            num_scalar_prefetch=1,
            grid=(n_seq, n_steps),
            in_specs=in_specs,
            out_specs=new,
            scratch_shapes=[pltpu.VMEM((page, LANE), F32), pltpu.VMEM((dec_seq, LANE), F32),
                            pltpu.VMEM((1, LANE), F32), pltpu.VMEM((256, LANE), F32),
                            pltpu.VMEM((1, LANE), F32), pltpu.VMEM((1, LANE), F32), pltpu.VMEM((LANE, 256), F32)]),
        out_shape=jax.ShapeDtypeStruct(br_prev.shape, F32),
        input_output_aliases={len(args): 0},
        compiler_params=pltpu.CompilerParams(dimension_semantics=("arbitrary", "arbitrary"),
                                             vmem_limit_bytes=VMEM_LIMIT),
        name="attn_sample",
    )(page_table.reshape(-1), *args)


def _rope_tables(pos):
    half = RET_DK // 2
    inv = ROPE_BASE ** (-np.arange(half, dtype=np.float64) / half)
    ang = np.asarray(pos, np.float64)[:, None] * inv[None, :]
    cos = np.concatenate([np.cos(ang), np.cos(ang)], axis=1)
    sin = np.concatenate([-np.sin(ang), np.sin(ang)], axis=1)
    return (jnp.asarray(np.tile(cos, (1, HEADS)), F32), jnp.asarray(np.tile(sin, (1, HEADS)), F32))


def _prompt_bucket_tiles():
    t = np.arange(SEQ_BLOCK)
    rel = t[:, None] - t[None, :]
    diag = np.where(rel >= 0, _t5_bucket_np(rel), -1)
    sub = _t5_bucket_np(rel + SEQ_BLOCK)
    return jnp.asarray(np.stack([diag, sub]).astype(np.int32))


def _sample_bucket_tile(page, dec_seq):
    lane = np.arange(LANE)
    iq = lane % dec_seq
    tk = np.arange(page)[:, None]
    past = _t5_bucket_np(page + iq[None, :] - tk)
    tn = np.arange(dec_seq)[:, None]
    rel_new = iq[None, :] - tn
    new = np.where(rel_new >= 0, _t5_bucket_np(rel_new), -1)
    return jnp.asarray(np.concatenate([past, new], axis=0).astype(np.int32))


def kernel(x_prompt, x_sample, cache_k, cache_v, page_table, state_ret, state_gla, state_conv, meta_tokens,
           rel_bias, norm_ffn1, ffn1_gate, ffn1_up, ffn1_down, norm_mix, w_in, b_in, conv_w, conv_b, conv_ln_g,
           conv_ln_b, ret_gn, gla_alpha_w, gla_alpha_b, gla_gn, q_norm, k_norm, lam_q1, lam_k1, lam_q2, lam_k2,
           diff_subln, w_branch, w_out, norm_ffn2, ffn2_gate, ffn2_up, ffn2_down):
    B, S, D = x_prompt.shape
    DB, DS, _ = x_sample.shape
    depth = w_in.shape[0]
    L = S + N_META
    Lp = -(-L // SEQ_BLOCK) * SEQ_BLOCK
    n_chunks = Lp // SEQ_BLOCK
    n_pool, page = cache_k.shape[1], cache_k.shape[2]
    past_len = page_table.shape[1] * page
    rows_p = B * Lp
    rows = rows_p + DB * DS
    assert rows % ROW_TILE == 0 and rows_p % DS == 0 and DS % 8 == 0

    meta = jnp.broadcast_to(meta_tokens[None], (B, N_META, D))
    h_p = jnp.concatenate([meta, x_prompt, jnp.zeros((B, Lp - L, D), F32)], axis=1)
    h = jnp.concatenate([h_p.reshape(rows_p, D), x_sample.reshape(DB * DS, D)], axis=0)

    cos_p, sin_p = _rope_tables(np.arange(Lp))
    cos_s, sin_s = _rope_tables(past_len + np.arange(DS))
    bt_prompt = _prompt_bucket_tiles()
    bt_sample = _sample_bucket_tile(page, DS)
    ck = cache_k.reshape(depth, n_pool, page, HEADS * 2 * DIFF_HD)
    cv = cache_v.reshape(depth, n_pool, page, HEADS * DIFF_VD)
    zeros_conv = jnp.zeros((B, CONV_K - 1, CONV_C), F32)
    zeros_ret = jnp.zeros((B, HEADS, RET_DK, RET_DV), F32)
    zeros_gla = jnp.zeros((B, HEADS, GLA_DK, GLA_DV), F32)

    outs = {k: [] for k in ('kp', 'vp', 'ks', 'vs', 'rp', 'rs', 'gp', 'gs', 'cp', 'cs')}
    row2 = lambda a: a.reshape(1, -1).astype(F32)
    for l in range(depth):
        lam_init = 0.8 - 0.6 * math.exp(-0.3 * l)
        wi, bi = w_in[l], b_in[l]
        w = dict(
            g1=row2(norm_ffn1[l]), wg1=ffn1_gate[l].astype(BF16), wu1=ffn1_up[l].astype(BF16),
            wd1=ffn1_down[l].astype(BF16), gm=row2(norm_mix[l]),
            wlin=wi[:, :N_LIN].astype(BF16), blin=row2(bi[:N_LIN]),
            wlow=wi[:, OFF_GLOW:OFF_DIFF].astype(BF16), blow=row2(bi[OFF_GLOW:OFF_DIFF]),
            aw=gla_alpha_w[l].astype(BF16), ab=row2(gla_alpha_b[l]),
            wdf=wi[:, OFF_DIFF:OFF_GATES].astype(BF16), bdf=row2(bi[OFF_DIFF:OFF_GATES]),
            qg=row2(jnp.tile(q_norm[l], 2 * HEADS)), kg=row2(jnp.tile(k_norm[l], 2 * HEADS)),
            wgt=wi[:, OFF_GATES:].astype(BF16), bgt=row2(bi[OFF_GATES:]),
            wb=w_branch[l].astype(BF16), wo=w_out[l].astype(BF16), g2=row2(norm_ffn2[l]),
            wg2=ffn2_gate[l].astype(BF16), wu2=ffn2_up[l].astype(BF16), wd2=ffn2_down[l].astype(BF16),
            cw=conv_w[l], cb=row2(conv_b[l]), lng=row2(conv_ln_g[l]), lnb=row2(conv_ln_b[l]),
            rgn=row2(ret_gn[l]), ggn=row2(gla_gn[l]),
        )
        lamv = jnp.stack([lam_q1[l], lam_k1[l], lam_q2[l], lam_k2[l]]).astype(F32)
        sub = row2(diff_subln[l])

        h1, zlin, gdec, dq, dk, dv = _head_call(h, w)

        br_lin, rp, gp, cp = _mix_call(zlin, gdec, cos_p, sin_p, zeros_conv, zeros_ret, zeros_gla, w,
                                       n_seq=B, chunk=SEQ_BLOCK, n_chunks=n_chunks, seq_len=L, row_block0=0)
        br_lin, rs, gs, cs = _mix_call(zlin, gdec, cos_s, sin_s, state_conv[l], state_ret[l], state_gla[l], w,
                                       n_seq=DB, chunk=DS, n_chunks=1, seq_len=DS, row_block0=rows_p // DS,
                                       br_prev=br_lin)
        br_diff = _flash_call(bt_prompt, rel_bias, lamv, sub, dq, dk, dv, n_seq=B, seq_rows=Lp, lam_init=lam_init)
        br_diff = _decode_call(page_table, bt_sample, rel_bias, lamv, sub, dq, dk, dv, ck, cv, br_diff,
                               layer=l, n_seq=DB, dec_seq=DS, row_block0=rows_p // DS, lam_init=lam_init)

        h = _tail_call(h1, br_lin, br_diff, w)

        outs['kp'].append(dk[:rows_p].reshape(B, Lp, HEADS, 2 * DIFF_HD)[:, :L])
        outs['vp'].append(dv[:rows_p].reshape(B, Lp, HEADS, DIFF_VD)[:, :L])
        outs['ks'].append(dk[rows_p:].reshape(DB, DS, HEADS, 2 * DIFF_HD))
        outs['vs'].append(dv[rows_p:].reshape(DB, DS, HEADS, DIFF_VD))
        outs['rp'].append(rp); outs['rs'].append(rs)
        outs['gp'].append(gp); outs['gs'].append(gs)
        outs['cp'].append(cp); outs['cs'].append(cs)

    y_prompt = h[:rows_p].reshape(B, Lp, D)[:, N_META:L]
    y_sample = h[rows_p:].reshape(DB, DS, D)
    st = lambda k: jnp.stack(outs[k], axis=0)
    return (y_prompt, y_sample, st('kp'), st('vp'), st('ks'), st('vs'),
            st('rp'), st('rs'), st('gp'), st('gs'), st('cp'), st('cs'))
```

```python
import functools
import math

import numpy as np
import jax
import jax.numpy as jnp
from jax import lax
from jax.experimental import pallas as pl
from jax.experimental.pallas import tpu as pltpu

F32 = jnp.float32
BF16 = jnp.bfloat16

D_MODEL = 1024
N_META = 16
N_BRANCH = 4
BRANCH_W = 256
D_FF = 2816
CONV_C = 256
CONV_K = 31
HEADS = 4
RET_DK = 64
RET_DV = 64
GLA_DK = 32
GLA_DV = 64
GLA_RANK = 16
GLA_TAU = 16.0
DIFF_HD = 32
DIFF_VD = 64
REL_BUCKETS = 32
REL_MAX_DIST = 128
ROPE_BASE = 10000.0
EPS = 1e-6
NEG = -1e30
LOG2E = math.log2(math.e)

N_LIN = 2304
OFF_GLOW = N_LIN
OFF_DIFF = OFF_GLOW + GLA_RANK
OFF_GATES = OFF_DIFF + 3 * 256
N_IN = OFF_GATES + N_BRANCH * D_MODEL

LANE = 128
ROW_TILE = 256
SEQ_BLOCK = 128
FAR_BLOCK = 512
HIST = 32
VMEM_LIMIT = 56 * 1024 * 1024
PAGES_PER_STEP = 16


def _sigmoid(x):
    return 1.0 / (1.0 + jnp.exp(-x))


def _silu(x):
    return x * _sigmoid(x)


def _log_sigmoid(x):
    return jnp.minimum(x, 0.0) - jnp.log1p(jnp.exp(-jnp.abs(x)))


def _rms(x, g):
    return x * lax.rsqrt(jnp.mean(x * x, axis=-1, keepdims=True) + EPS) * g


def _dot(a, b):
    return jnp.dot(a, b, preferred_element_type=F32)


def _dot_nt(a, b):
    return lax.dot_general(a, b, (((1,), (1,)), ((), ())), preferred_element_type=F32)


def _dot_tn(a, b):
    return lax.dot_general(a, b, (((0,), (0,)), ((), ())), preferred_element_type=F32)


def _split_bf16(x):
    hi = x.astype(BF16)
    lo = (x - hi.astype(F32)).astype(BF16)
    return hi, lo


def _group_mean(x, group):
    n = x.shape[-1]
    r = lax.broadcasted_iota(jnp.int32, (n, n), 0) // group
    c = lax.broadcasted_iota(jnp.int32, (n, n), 1) // group
    avg = jnp.where(r == c, 1.0 / group, 0.0).astype(BF16)
    hi, lo = _split_bf16(x)
    return _dot(hi, avg) + _dot(lo, avg)


def _const_spec(shape):
    nd = len(shape)
    return pl.BlockSpec(shape, lambda *_: (0,) * nd, pipeline_mode=pl.Buffered(1))


def _ffn(x, wg_ref, wu_ref, wd_ref):
    gate = _dot(x, wg_ref[...])
    up = _dot(x, wu_ref[...])
    act = (_silu(gate) * up).astype(BF16)
    return _dot(act, wd_ref[...])


def _head_kernel(h_ref, g1_ref, wg_ref, wu_ref, wd_ref, gm_ref, wlin_ref, blin_ref, wlow_ref, blow_ref,
                 aw_ref, ab_ref, wdf_ref, bdf_ref, qg_ref, kg_ref,
                 h1_ref, zlin_ref, gdec_ref, dq_ref, dk_ref, dv_ref, dqt_ref, dvt_ref):
    h = h_ref[...]
    h1 = h + 0.5 * _ffn(_rms(h, g1_ref[...]).astype(BF16), wg_ref, wu_ref, wd_ref)
    h1_ref[...] = h1
    x = _rms(h1, gm_ref[...]).astype(BF16)
    zlin_ref[...] = _dot(x, wlin_ref[...]) + blin_ref[...]
    g_low = _dot(x, wlow_ref[...]) + blow_ref[...]
    g_pre = _dot(g_low.astype(BF16), aw_ref[...]) + ab_ref[...]
    gdec_ref[...] = _log_sigmoid(g_pre) * (1.0 / GLA_TAU)
    zd = _dot(x, wdf_ref[...]) + bdf_ref[...]
    d_q = zd[:, 0:256]
    d_k = zd[:, 256:512]
    d_v = zd[:, 512:768]
    q_n = d_q * lax.rsqrt(_group_mean(d_q * d_q, DIFF_HD) + EPS) * qg_ref[...]
    dq_ref[...] = q_n
    dqt_ref[...] = q_n.T
    dk_ref[...] = d_k * lax.rsqrt(_group_mean(d_k * d_k, DIFF_HD) + EPS) * kg_ref[...]
    dv_ref[...] = d_v
    dvt_ref[...] = d_v.T


def _head_call(h, w):
    rows = h.shape[0]
    row = lambda n: pl.BlockSpec((ROW_TILE, n), lambda i: (i, 0))
    col = pl.BlockSpec((256, ROW_TILE), lambda i: (0, i))
    consts = [w['g1'], w['wg1'], w['wu1'], w['wd1'], w['gm'], w['wlin'], w['blin'], w['wlow'], w['blow'],
              w['aw'], w['ab'], w['wdf'], w['bdf'], w['qg'], w['kg']]
    widths = (D_MODEL, N_LIN, LANE, 256, 256, 256)
    return pl.pallas_call(
        _head_kernel,
        grid=(rows // ROW_TILE,),
        in_specs=[row(D_MODEL)] + [_const_spec(c.shape) for c in consts],
        out_specs=[row(n) for n in widths] + [col, col],
        out_shape=([jax.ShapeDtypeStruct((rows, n), F32) for n in widths]
                   + [jax.ShapeDtypeStruct((256, rows), F32)] * 2),
        compiler_params=pltpu.CompilerParams(dimension_semantics=("arbitrary",),
                                             vmem_limit_bytes=VMEM_LIMIT),
        name="head",
    )(h, *consts)


def _tail_kernel(h1_ref, brl_ref, brd_ref, gm_ref, wgt_ref, bgt_ref, wb_ref, wo_ref, g2_ref,
                 wg_ref, wu_ref, wd_ref, out_ref):
    h1 = h1_ref[...]
    x = _rms(h1, gm_ref[...]).astype(BF16)
    merged = None
    for n in range(N_BRANCH):
        if n < 3:
            br = brl_ref[:, n * BRANCH_W:(n + 1) * BRANCH_W]
        else:
            br = brd_ref[...]
        gate = _dot(x, wgt_ref[:, n * D_MODEL:(n + 1) * D_MODEL]) + bgt_ref[:, n * D_MODEL:(n + 1) * D_MODEL]
        term = _dot(br.astype(BF16), wb_ref[n]) * _sigmoid(gate)
        merged = term if merged is None else merged + term
    h2 = h1 + _dot(merged.astype(BF16), wo_ref[...])
    out_ref[...] = h2 + 0.5 * _ffn(_rms(h2, g2_ref[...]).astype(BF16), wg_ref, wu_ref, wd_ref)


def _tail_call(h1, br_lin, br_diff, w):
    rows = h1.shape[0]
    row = lambda n: pl.BlockSpec((ROW_TILE, n), lambda i: (i, 0))
    consts = [w['gm'], w['wgt'], w['bgt'], w['wb'], w['wo'], w['g2'], w['wg2'], w['wu2'], w['wd2']]
    return pl.pallas_call(
        _tail_kernel,
        grid=(rows // ROW_TILE,),
        in_specs=[row(D_MODEL), row(3 * BRANCH_W), row(BRANCH_W)] + [_const_spec(c.shape) for c in consts],
        out_specs=row(D_MODEL),
        out_shape=jax.ShapeDtypeStruct((rows, D_MODEL), F32),
        compiler_params=pltpu.CompilerParams(dimension_semantics=("arbitrary",),
                                             vmem_limit_bytes=VMEM_LIMIT),
        name="tail",
    )(h1, br_lin, br_diff, *consts)


def _mix_kernel(*refs, chunk, n_chunks, seq_len, aliased):
    (z_ref, gd_ref, cos_ref, sin_ref, cpast_ref, sr0_ref, sg0_ref, cw_ref, cb_ref, lng_ref, lnb_ref,
     rgn_ref, ggn_ref) = refs[:13]
    refs = refs[13 + (1 if aliased else 0):]
    br_ref, sr_out, sg_out, cs_out, sr_s, sg_s, u_s = refs
    C = chunk
    c = pl.program_id(1)
    low = C >= 16
    cast = (lambda a: a.astype(BF16)) if low else (lambda a: a)

    t_col = lax.broadcasted_iota(jnp.int32, (C, 1), 0)
    s_row = lax.broadcasted_iota(jnp.int32, (1, C), 1)
    causal = t_col >= s_row
    tf = t_col.astype(F32)
    padded = n_chunks * C > seq_len
    if padded:
        valid = (c * C + t_col) < seq_len
        nvf = jnp.zeros((1, 1), F32) + jnp.minimum(seq_len - c * C, C).astype(F32)
        keep = lambda a: jnp.where(valid, a, 0.0)
    else:
        nvf = jnp.full((1, 1), float(C), F32)
        keep = lambda a: a

    @pl.when(c == 0)
    def _init():
        sr_s[...] = sr0_ref[0]
        sg_s[...] = sg0_ref[0]
        u_s[0:HIST - (CONV_K - 1), :] = jnp.zeros((HIST - (CONV_K - 1), CONV_C), F32)
        u_s[HIST - (CONV_K - 1):HIST, :] = cpast_ref[0]

    @pl.when(c > 0)
    def _shift():
        u_s[0:HIST, :] = u_s[C:C + HIST, :]

    u = z_ref[:, 0:256] * _sigmoid(z_ref[:, 256:512])
    u_s[HIST:HIST + C, :] = u
    acc = jnp.zeros((C, CONV_C), F32)
    for j in range(CONV_K):
        lo = HIST - (CONV_K - 1) + j
        acc = acc + cw_ref[j:j + 1, :] * u_s[lo:lo + C, :]
    conv = acc + cb_ref[...]
    xc = conv - jnp.mean(conv, axis=-1, keepdims=True)
    ln = xc * lax.rsqrt(jnp.mean(xc * xc, axis=-1, keepdims=True) + EPS) * lng_ref[...] + lnb_ref[...]
    br_ref[:, 0:256] = _silu(ln)

    lane = lax.broadcasted_iota(jnp.int32, (1, 256), 1)
    first_half = (lane % RET_DK) < (RET_DK // 2)
    cos = cos_ref[...]
    sin = sin_ref[...]

    def rope(a):
        swapped = jnp.where(first_half, pltpu.roll(a, 256 - RET_DK // 2, 1), pltpu.roll(a, RET_DK // 2, 1))
        return a * cos + swapped * sin

    rq = rope(z_ref[:, 512:768])
    rk = keep(rope(z_ref[:, 768:1024]) * RET_DK ** -0.5)
    rv = keep(z_ref[:, 1024:1280])
    dts = (t_col - s_row).astype(F32)
    for h in range(HEADS):
        lg = math.log1p(-(2.0 ** (-5 - h)))
        sl = slice(h * RET_DK, (h + 1) * RET_DK)
        qh, kh, vh = rq[:, sl], rk[:, sl], cast(rv[:, sl])
        decay = jnp.where(causal, jnp.exp(dts * lg), 0.0)
        a = _dot_nt(cast(qh), cast(kh)) * decay
        s_old = sr_s[h]
        o = _dot(cast(a), vh) + _dot(cast(qh * jnp.exp((tf + 1.0) * lg)), cast(s_old))
        sr_s[h] = jnp.exp(nvf * lg) * s_old + _dot_tn(cast(kh * jnp.exp((nvf - 1.0 - tf) * lg)), vh)
        oc = o - jnp.mean(o, axis=-1, keepdims=True)
        y = oc * lax.rsqrt(jnp.mean(oc * oc, axis=-1, keepdims=True) + EPS) * rgn_ref[:, sl]
        br_ref[:, 256 + h * RET_DV:256 + (h + 1) * RET_DV] = _silu(z_ref[:, 1280 + h * 64:1280 + (h + 1) * 64]) * y

    gq = z_ref[:, 1536:1664] * GLA_DK ** -0.5
    gk = keep(z_ref[:, 1664:1792])
    gv = keep(z_ref[:, 1792:2048])
    g = keep(gd_ref[...])
    tri = jnp.where(causal, 1.0, 0.0)
    if low:
        g_hi, g_lo = _split_bf16(g)
        tri = tri.astype(BF16)
        bcum = _dot(tri, g_hi) + _dot(tri, g_lo)
    else:
        bcum = _dot(tri, g)
    mid = C // 2 - 1
    b_mid = bcum[mid:mid + 1, :]
    b_last = bcum[C - 1:C, :]
    q_intra = gq * jnp.exp(bcum - b_mid)
    k_intra = gk * jnp.exp(b_mid - bcum)
    q_inter = gq * jnp.exp(bcum)
    k_state = gk * jnp.exp(b_last - bcum)
    e_last = jnp.where(t_col == C - 1, jnp.exp(bcum), 0.0)
    ones = jnp.ones((C, GLA_DV), BF16 if low else F32)
    if low:
        e_hi, e_lo = _split_bf16(e_last)
    for h in range(HEADS):
        sl = slice(h * GLA_DK, (h + 1) * GLA_DK)
        vs = slice(h * GLA_DV, (h + 1) * GLA_DV)
        vh = cast(gv[:, vs])
        a = jnp.where(causal, _dot_nt(cast(q_intra[:, sl]), cast(k_intra[:, sl])), 0.0)
        s_old = sg_s[h]
        o = _dot(cast(a), vh) + _dot(cast(q_inter[:, sl]), cast(s_old))
        if low:
            e_col = _dot_tn(e_hi[:, sl], ones) + _dot_tn(e_lo[:, sl], ones)
        else:
            e_col = _dot_tn(e_last[:, sl], ones)
        sg_s[h] = e_col * s_old + _dot_tn(cast(k_state[:, sl]), vh)
        y = o * lax.rsqrt(jnp.mean(o * o, axis=-1, keepdims=True) + EPS) * ggn_ref[:, vs]
        br_ref[:, 512 + h * GLA_DV:512 + (h + 1) * GLA_DV] = _silu(z_ref[:, 2048 + h * 64:2048 + (h + 1) * 64]) * y

    @pl.when(c == n_chunks - 1)
    def _final():
        sr_out[0] = sr_s[...]
        sg_out[0] = sg_s[...]
        n_last = seq_len - (n_chunks - 1) * C
        cs_out[0] = u_s[HIST + n_last - (CONV_K - 1):HIST + n_last, :]


def _mix_call(zlin, gdec, cos, sin, conv_past, ret0, gla0, w, *, n_seq, chunk, n_chunks, seq_len,
              row_block0, br_prev=None):
    rows = zlin.shape[0]
    aliased = br_prev is not None
    blk = lambda n: pl.BlockSpec((chunk, n), lambda s, c: (row_block0 + s * n_chunks + c, 0))
    per_seq = lambda shp: pl.BlockSpec((1,) + shp, lambda s, c: (s,) + (0,) * len(shp))
    const = lambda a: pl.BlockSpec(a.shape, lambda s, c: (0,) * a.ndim)
    consts = [w['cw'], w['cb'], w['lng'], w['lnb'], w['rgn'], w['ggn']]
    in_specs = ([blk(N_LIN), blk(LANE),
                 pl.BlockSpec((chunk, 256), lambda s, c: (c, 0)), pl.BlockSpec((chunk, 256), lambda s, c: (c, 0)),
                 per_seq((CONV_K - 1, CONV_C)), per_seq((HEADS, RET_DK, RET_DV)), per_seq((HEADS, GLA_DK, GLA_DV))]
                + [const(a) for a in consts])
    args = [zlin, gdec, cos, sin, conv_past, ret0, gla0] + consts
    aliases = {}
    if aliased:
        in_specs.append(pl.BlockSpec(memory_space=pl.ANY))
        aliases = {len(args): 0}
        args.append(br_prev)
    return pl.pallas_call(
        functools.partial(_mix_kernel, chunk=chunk, n_chunks=n_chunks, seq_len=seq_len, aliased=aliased),
        grid=(n_seq, n_chunks),
        in_specs=in_specs,
        out_specs=[blk(3 * BRANCH_W), per_seq((HEADS, RET_DK, RET_DV)), per_seq((HEADS, GLA_DK, GLA_DV)),
                   per_seq((CONV_K - 1, CONV_C))],
        out_shape=[jax.ShapeDtypeStruct((rows, 3 * BRANCH_W), F32),
                   jax.ShapeDtypeStruct((n_seq, HEADS, RET_DK, RET_DV), F32),
                   jax.ShapeDtypeStruct((n_seq, HEADS, GLA_DK, GLA_DV), F32),
                   jax.ShapeDtypeStruct((n_seq, CONV_K - 1, CONV_C), F32)],
        scratch_shapes=[pltpu.VMEM((HEADS, RET_DK, RET_DV), F32), pltpu.VMEM((HEADS, GLA_DK, GLA_DV), F32),
                        pltpu.VMEM((HIST + chunk, CONV_C), F32)],
        input_output_aliases=aliases,
        compiler_params=pltpu.CompilerParams(dimension_semantics=("arbitrary", "arbitrary")),
        name="mix_sample" if aliased else "mix_prompt",
    )(*args)


def _t5_bucket_np(rel):
    rel = np.asarray(rel)
    n = np.maximum(rel, 0)
    max_exact = REL_BUCKETS // 2
    nf = np.maximum(n, 1).astype(np.float64)
    large = max_exact + (np.log(nf / max_exact) / math.log(REL_MAX_DIST / max_exact)
                         * (REL_BUCKETS - max_exact)).astype(np.int64)
    large = np.minimum(large, REL_BUCKETS - 1)
    return np.where(n < max_exact, n, large).astype(np.int32)


def _lambda(lamv_ref, lam_init):
    a = jnp.sum(lamv_ref[0:1, :] * lamv_ref[1:2, :], axis=-1, keepdims=True)
    b = jnp.sum(lamv_ref[2:3, :] * lamv_ref[3:4, :], axis=-1, keepdims=True)
    return jnp.exp(a) - jnp.exp(b) + lam_init


def _flash_kernel(bt_ref, rb_ref, lamv_ref, sub_ref, qt_ref, k_ref, vt_ref, o_ref,
                  bias_s, qbd_s, m_s, l_s, acc_s, *, lam_init, n_blocks):
    QB = SEQ_BLOCK
    SUB = FAR_BLOCK // QB
    b = pl.program_id(0)
    i = pl.program_id(1)

    @pl.when((b == 0) & (i == 0))
    def _tables():
        for kind, src in ((1, 1), (2, 0)):
            bt = bt_ref[src]
            tiles = [jnp.zeros((QB, QB), F32) for _ in range(HEADS)]
            for bk in range(REL_BUCKETS):
                hit = bt == bk
                for h in range(HEADS):
                    tiles[h] = jnp.where(hit, rb_ref[bk, h] * LOG2E, tiles[h])
            for h in range(HEADS):
                bias_s[kind, h] = jnp.where(bt < 0, NEG, tiles[h])
        for h in range(HEADS):
            bias_s[0, h] = jnp.zeros((QB, QB), F32) + rb_ref[REL_BUCKETS - 1, h] * LOG2E
            bias_s[3, h] = jnp.full((QB, QB), NEG, F32)

    qt = qt_ref[...] * (DIFF_HD ** -0.5 * LOG2E)
    row = lax.broadcasted_iota(jnp.int32, (256, 1), 0)
    for hj in range(2 * HEADS):
        own = (row >= hj * DIFF_HD) & (row < (hj + 1) * DIFF_HD)
        qbd_s[:, hj * QB:(hj + 1) * QB] = jnp.where(own, qt, 0.0).astype(BF16)
    m_s[...] = jnp.full(m_s.shape, NEG, F32)
    l_s[...] = jnp.zeros(l_s.shape, F32)
    acc_s[...] = jnp.zeros(acc_s.shape, F32)

    def accumulate(h, st, shift, vt):
        m_old = m_s[h]
        m_new = jnp.maximum(m_old, jnp.max(st, axis=0, keepdims=True) + shift)
        alpha = jnp.exp2(m_old - m_new)
        pt = jnp.exp2(st - (m_new - shift))
        l_s[h] = alpha * l_s[h] + jnp.sum(pt, axis=0, keepdims=True)
        m_s[h] = m_new
        acc_s[h] = alpha * acc_s[h] + _dot(vt[h * DIFF_VD:(h + 1) * DIFF_VD, :], pt.astype(BF16))

    n_far = jnp.maximum(i - 1, 0) // SUB

    def far_body(kb, carry):
        off = pl.multiple_of(kb * FAR_BLOCK, FAR_BLOCK)
        k = k_ref[pl.ds(off, FAR_BLOCK), :].astype(BF16)
        vt = vt_ref[:, pl.ds(off, FAR_BLOCK)].astype(BF16)
        for h in range(HEADS):
            st = _dot(k, qbd_s[:, 2 * h * QB:(2 * h + 2) * QB])
            accumulate(h, st, rb_ref[REL_BUCKETS - 1, h] * LOG2E, vt)
        return carry

    lax.fori_loop(0, n_far, far_body, 0)

    def near_body(kb, carry):
        first = kb * SUB
        blk0 = jnp.minimum(first, n_blocks - SUB)
        off = pl.multiple_of(blk0 * QB, QB)
        k = k_ref[pl.ds(off, FAR_BLOCK), :].astype(BF16)
        vt = vt_ref[:, pl.ds(off, FAR_BLOCK)].astype(BF16)
        kinds = []
        for c in range(SUB):
            blk = blk0 + c
            kind = jnp.where(blk == i, 2, jnp.where(blk == i - 1, 1, 0))
            kinds.append(jnp.where((blk > i) | (blk < first), 3, kind))
        for h in range(HEADS):
            st = _dot(k, qbd_s[:, 2 * h * QB:(2 * h + 2) * QB])
            parts = []
            for c in range(SUB):
                tile = bias_s[kinds[c], h]
                parts.append(st[c * QB:(c + 1) * QB, :] + jnp.concatenate([tile, tile], axis=1))
            accumulate(h, jnp.concatenate(parts, axis=0), 0.0, vt)
        return carry

    lax.fori_loop(n_far, i // SUB + 1, near_body, 0)

    lam = _lambda(lamv_ref, lam_init)
    outs = []
    for h in range(HEADS):
        ot = acc_s[h] / l_s[h]
        d = ot[:, 0:QB] - lam * ot[:, QB:2 * QB]
        y = d * lax.rsqrt(jnp.mean(d * d, axis=0, keepdims=True) + EPS) * sub_ref[...] * (1.0 - lam_init)
        outs.append(y)
    o_ref[...] = jnp.concatenate(outs, axis=0).T


def _flash_call(bt, rel_bias, lamv, sub_col, dqt, dk, dvt, *, n_seq, seq_rows, lam_init):
    rows = dk.shape[0]
    nq = seq_rows // SEQ_BLOCK
    assert seq_rows >= FAR_BLOCK
    const = lambda a: pl.BlockSpec(a.shape, lambda b, i: (0,) * a.ndim)
    return pl.pallas_call(
        functools.partial(_flash_kernel, lam_init=lam_init, n_blocks=nq),
        grid=(n_seq, nq),
        in_specs=[const(bt), pl.BlockSpec(memory_space=pltpu.SMEM), const(lamv), const(sub_col),
                  pl.BlockSpec((256, SEQ_BLOCK), lambda b, i: (0, b * nq + i)),
                  pl.BlockSpec((seq_rows, 256), lambda b, i: (b, 0)),
                  pl.BlockSpec((256, seq_rows), lambda b, i: (0, b))],
        out_specs=pl.BlockSpec((SEQ_BLOCK, 256), lambda b, i: (b * nq + i, 0)),
        out_shape=jax.ShapeDtypeStruct((rows, 256), F32),
        scratch_shapes=[pltpu.VMEM((4, HEADS, SEQ_BLOCK, SEQ_BLOCK), F32),
                        pltpu.VMEM((256, 2 * HEADS * SEQ_BLOCK), BF16),
                        pltpu.VMEM((HEADS, 1, 2 * SEQ_BLOCK), F32), pltpu.VMEM((HEADS, 1, 2 * SEQ_BLOCK), F32),
                        pltpu.VMEM((HEADS, DIFF_VD, 2 * SEQ_BLOCK), F32)],
        compiler_params=pltpu.CompilerParams(dimension_semantics=("arbitrary", "arbitrary"),
                                             vmem_limit_bytes=VMEM_LIMIT),
        name="attn_prompt",
    )(bt, rel_bias, lamv, sub_col, dqt, dk, dvt)


def _decode_kernel(pt_ref, bt_ref, rb_ref, lamv_ref, sub_ref, q_ref, kn_ref, vn_ref, *refs,
                   pages_per_step, n_steps, dec_seq, lam_init):
    G = pages_per_step
    kt_refs = refs[:G]
    vt_refs = refs[G:2 * G]
    o_ref, bias_pg_s, bias_new_s, far_s, qs_s, m_s, l_s, acc_s = refs[2 * G + 1:]
    del pt_ref
    b = pl.program_id(0)
    g = pl.program_id(1)
    last = g == n_steps - 1
    NQ = 2 * HEADS * dec_seq
    seqs_per_block = kn_ref.shape[0] // dec_seq

    row_head = lax.broadcasted_iota(jnp.int32, (NQ, 1), 0) // (2 * dec_seq)

    def bias_rows(bk):
        out = jnp.zeros((NQ, 1), F32)
        for h in range(HEADS):
            out = jnp.where(row_head == h, rb_ref[bk, h], out)
        return out

    @pl.when((b == 0) & (g == 0))
    def _tables():
        for src, dst in ((0, bias_pg_s), (1, bias_new_s)):
            bt = bt_ref[src]
            tile = jnp.zeros(bt.shape, F32)
            for bk in range(REL_BUCKETS):
                tile = jnp.where(bt == bk, bias_rows(bk), tile)
            dst[...] = jnp.where(bt < 0, NEG, tile)
        far_s[...] = jnp.zeros(far_s.shape, F32) + bias_rows(REL_BUCKETS - 1)

    @pl.when(g == 0)
    def _init():
        q = q_ref[...] * DIFF_HD ** -0.5
        lane = lax.broadcasted_iota(jnp.int32, (1, 256), 1)
        for hj in range(2 * HEADS):
            own = (lane >= hj * DIFF_HD) & (lane < (hj + 1) * DIFF_HD)
            qs_s[hj * dec_seq:(hj + 1) * dec_seq, :] = jnp.where(own, q, 0.0)
        m_s[...] = jnp.full(m_s.shape, NEG, F32)
        l_s[...] = jnp.zeros(l_s.shape, F32)
        acc_s[...] = jnp.zeros(acc_s.shape, F32)

    def update(scores, pv_fn):
        m_old = m_s[...]
        s_max = scores[0]
        for s in scores[1:]:
            s_max = jnp.maximum(s_max, s)
        m_new = jnp.maximum(m_old, jnp.max(s_max, axis=1, keepdims=True))
        alpha = jnp.exp(m_old - m_new)
        probs = [jnp.exp(s - m_new) for s in scores]
        p_sum = probs[0]
        for p in probs[1:]:
            p_sum = p_sum + p
        l_s[...] = alpha * l_s[...] + jnp.sum(p_sum, axis=1, keepdims=True)
        m_s[...] = m_new
        acc_s[...] = jnp.concatenate([alpha, alpha], axis=1) * acc_s[...] + pv_fn(probs)

    qs = qs_s[...].astype(BF16)
    scores = []
    for p in range(G):
        s = _dot(qs, kt_refs[p][...].astype(BF16))
        if p == G - 1:
            s = s + jnp.where(last, bias_pg_s[...], far_s[...])
        else:
            s = s + far_s[...]
        scores.append(s)

    def pv_pages(probs):
        out = None
        for p in range(G):
            t = _dot_nt(probs[p].astype(BF16), vt_refs[p][...].astype(BF16))
            out = t if out is None else out + t
        return out

    update(scores, pv_pages)

    @pl.when(last)
    def _finish():
        key_seq = lax.broadcasted_iota(jnp.int32, (1, kn_ref.shape[0]), 1) // dec_seq
        s_new = _dot_nt(qs, kn_ref[...].astype(BF16))
        s_new = jnp.where(key_seq == b % seqs_per_block, s_new + bias_new_s[...], NEG)
        update([s_new], lambda probs: _dot(probs[0].astype(BF16), vn_ref[...].astype(BF16)))
        l_all = l_s[...]
        o_all = acc_s[...] / jnp.concatenate([l_all, l_all], axis=1)
        lam = _lambda(lamv_ref, lam_init)
        for h in range(HEADS):
            r1 = (2 * h) * dec_seq
            r2 = (2 * h + 1) * dec_seq
            cols = slice(h * DIFF_VD, (h + 1) * DIFF_VD)
            d = o_all[r1:r1 + dec_seq, cols] - lam * o_all[r2:r2 + dec_seq, cols]
            y = d * lax.rsqrt(jnp.mean(d * d, axis=-1, keepdims=True) + EPS) * sub_ref[...] * (1.0 - lam_init)
            o_ref[:, cols] = y


def _decode_call(page_table, bt, rel_bias, lamv, sub, dq, dk, dv, cache_kt, cache_vt, br_prev, *,
                 layer, n_seq, dec_seq, row0, lam_init):
    n_pages = page_table.shape[1]
    page = cache_kt.shape[3]
    G = min(PAGES_PER_STEP, n_pages)
    n_steps = n_pages // G
    nq = 2 * HEADS * dec_seq
    new_rows = LANE
    per_blk = new_rows // dec_seq
    const = lambda a: pl.BlockSpec(a.shape, lambda b, g, pt: (0,) * a.ndim)
    own = pl.BlockSpec((dec_seq, 256), lambda b, g, pt: (row0 // dec_seq + b, 0))
    new = pl.BlockSpec((new_rows, 256), lambda b, g, pt: (row0 // new_rows + b // per_blk, 0))

    def page_spec(p):
        return pl.BlockSpec((None, None, 256, page),
                            lambda b, g, pt: (layer, pt[b * n_pages + g * G + p], 0, 0))

    in_specs = ([const(bt), pl.BlockSpec(memory_space=pltpu.SMEM), const(lamv), const(sub), own, new, new]
                + [page_spec(p) for p in range(G)] + [page_spec(p) for p in range(G)]
                + [pl.BlockSpec(memory_space=pl.ANY)])
    args = [bt, rel_bias, lamv, sub, dq, dk, dv] + [cache_kt] * G + [cache_vt] * G + [br_prev]
    return pl.pallas_call(
        functools.partial(_decode_kernel, pages_per_step=G, n_steps=n_steps, dec_seq=dec_seq, lam_init=lam_init),
        grid_spec=pltpu.PrefetchScalarGridSpec(
            num_scalar_prefetch=1,
            grid=(n_seq, n_steps),
            in_specs=in_specs,
            out_specs=own,
            scratch_shapes=[pltpu.VMEM((nq, page), F32), pltpu.VMEM((nq, new_rows), F32),
                            pltpu.VMEM((nq, page), F32), pltpu.VMEM((nq, 256), F32),
                            pltpu.VMEM((nq, LANE), F32), pltpu.VMEM((nq, LANE), F32), pltpu.VMEM((nq, 256), F32)]),
        out_shape=jax.ShapeDtypeStruct(br_prev.shape, F32),
        input_output_aliases={len(args): 0},
        compiler_params=pltpu.CompilerParams(dimension_semantics=("arbitrary", "arbitrary"),
                                             vmem_limit_bytes=VMEM_LIMIT),
        name="attn_sample",
    )(page_table.reshape(-1), *args)


def _rope_tables(pos):
    half = RET_DK // 2
    inv = ROPE_BASE ** (-np.arange(half, dtype=np.float64) / half)
    ang = np.asarray(pos, np.float64)[:, None] * inv[None, :]
    cos = np.concatenate([np.cos(ang), np.cos(ang)], axis=1)
    sin = np.concatenate([-np.sin(ang), np.sin(ang)], axis=1)
    return (jnp.asarray(np.tile(cos, (1, HEADS)), F32), jnp.asarray(np.tile(sin, (1, HEADS)), F32))


def _prompt_bucket_tiles():
    t = np.arange(SEQ_BLOCK)
    rel = t[None, :] - t[:, None]
    diag = np.where(rel >= 0, _t5_bucket_np(rel), -1)
    sub = _t5_bucket_np(rel + SEQ_BLOCK)
    return jnp.asarray(np.stack([diag, sub]).astype(np.int32))


def _sample_bucket_tiles(page, dec_seq):
    assert page == LANE
    iq = (np.arange(2 * HEADS * dec_seq) % dec_seq)[:, None]
    past = _t5_bucket_np(page + iq - np.arange(page)[None, :])
    rel_new = iq - (np.arange(LANE) % dec_seq)[None, :]
    new = np.where(rel_new >= 0, _t5_bucket_np(rel_new), -1)
    return jnp.asarray(np.stack([past, new]).astype(np.int32))


def kernel(x_prompt, x_sample, cache_k, cache_v, page_table, state_ret, state_gla, state_conv, meta_tokens,
           rel_bias, norm_ffn1, ffn1_gate, ffn1_up, ffn1_down, norm_mix, w_in, b_in, conv_w, conv_b, conv_ln_g,
           conv_ln_b, ret_gn, gla_alpha_w, gla_alpha_b, gla_gn, q_norm, k_norm, lam_q1, lam_k1, lam_q2, lam_k2,
           diff_subln, w_branch, w_out, norm_ffn2, ffn2_gate, ffn2_up, ffn2_down):
    B, S, D = x_prompt.shape
    DB, DS, _ = x_sample.shape
    depth = w_in.shape[0]
    L = S + N_META
    Lp = -(-L // SEQ_BLOCK) * SEQ_BLOCK
    n_chunks = Lp // SEQ_BLOCK
    n_pool, page = cache_k.shape[1], cache_k.shape[2]
    past_len = page_table.shape[1] * page
    rows_p = B * Lp
    rows = rows_p + DB * DS
    assert rows % ROW_TILE == 0 and rows_p % LANE == 0 and (DB * DS) % LANE == 0 and DS % 8 == 0

    meta = jnp.broadcast_to(meta_tokens[None], (B, N_META, D))
    h_p = jnp.concatenate([meta, x_prompt, jnp.zeros((B, Lp - L, D), F32)], axis=1)
    h = jnp.concatenate([h_p.reshape(rows_p, D), x_sample.reshape(DB * DS, D)], axis=0)

    cos_p, sin_p = _rope_tables(np.arange(Lp))
    cos_s, sin_s = _rope_tables(past_len + np.arange(DS))
    bt_prompt = _prompt_bucket_tiles()
    bt_sample = _sample_bucket_tiles(page, DS)
    ckt = jnp.transpose(cache_k, (0, 1, 3, 4, 2)).reshape(depth, n_pool, HEADS * 2 * DIFF_HD, page)
    cvt = jnp.transpose(cache_v, (0, 1, 3, 4, 2)).reshape(depth, n_pool, HEADS * DIFF_VD, page)
    zeros_conv = jnp.zeros((B, CONV_K - 1, CONV_C), F32)
    zeros_ret = jnp.zeros((B, HEADS, RET_DK, RET_DV), F32)
    zeros_gla = jnp.zeros((B, HEADS, GLA_DK, GLA_DV), F32)

    outs = {k: [] for k in ('kp', 'vp', 'ks', 'vs', 'rp', 'rs', 'gp', 'gs', 'cp', 'cs')}
    row2 = lambda a: a.reshape(1, -1).astype(F32)
    for l in range(depth):
        lam_init = 0.8 - 0.6 * math.exp(-0.3 * l)
        wi, bi = w_in[l], b_in[l]
        w = dict(
            g1=row2(norm_ffn1[l]), wg1=ffn1_gate[l].astype(BF16), wu1=ffn1_up[l].astype(BF16),
            wd1=ffn1_down[l].astype(BF16), gm=row2(norm_mix[l]),
            wlin=wi[:, :N_LIN].astype(BF16), blin=row2(bi[:N_LIN]),
            wlow=wi[:, OFF_GLOW:OFF_DIFF].astype(BF16), blow=row2(bi[OFF_GLOW:OFF_DIFF]),
            aw=gla_alpha_w[l].astype(BF16), ab=row2(gla_alpha_b[l]),
            wdf=wi[:, OFF_DIFF:OFF_GATES].astype(BF16), bdf=row2(bi[OFF_DIFF:OFF_GATES]),
            qg=row2(jnp.tile(q_norm[l], 2 * HEADS)), kg=row2(jnp.tile(k_norm[l], 2 * HEADS)),
            wgt=wi[:, OFF_GATES:].astype(BF16), bgt=row2(bi[OFF_GATES:]),
            wb=w_branch[l].astype(BF16), wo=w_out[l].astype(BF16), g2=row2(norm_ffn2[l]),
            wg2=ffn2_gate[l].astype(BF16), wu2=ffn2_up[l].astype(BF16), wd2=ffn2_down[l].astype(BF16),
            cw=conv_w[l], cb=row2(conv_b[l]), lng=row2(conv_ln_g[l]), lnb=row2(conv_ln_b[l]),
            rgn=row2(ret_gn[l]), ggn=row2(gla_gn[l]),
        )
        lamv = jnp.stack([lam_q1[l], lam_k1[l], lam_q2[l], lam_k2[l]]).astype(F32)
        sub = row2(diff_subln[l])

        h1, zlin, gdec, dq, dk, dv, dqt, dvt = _head_call(h, w)

        br_lin, rp, gp, cp = _mix_call(zlin, gdec, cos_p, sin_p, zeros_conv, zeros_ret, zeros_gla, w,
                                       n_seq=B, chunk=SEQ_BLOCK, n_chunks=n_chunks, seq_len=L, row_block0=0)
        br_lin, rs, gs, cs = _mix_call(zlin, gdec, cos_s, sin_s, state_conv[l], state_ret[l], state_gla[l], w,
                                       n_seq=DB, chunk=DS, n_chunks=1, seq_len=DS, row_block0=rows_p // DS,
                                       br_prev=br_lin)
        br_diff = _flash_call(bt_prompt, rel_bias, lamv, sub.reshape(-1, 1), dqt, dk, dvt,
                              n_seq=B, seq_rows=Lp, lam_init=lam_init)
        br_diff = _decode_call(page_table, bt_sample, rel_bias, lamv, sub, dq, dk, dv, ckt, cvt, br_diff,
                               layer=l, n_seq=DB, dec_seq=DS, row0=rows_p, lam_init=lam_init)

        h = _tail_call(h1, br_lin, br_diff, w)

        outs['kp'].append(dk[:rows_p].reshape(B, Lp, HEADS, 2 * DIFF_HD)[:, :L])
        outs['vp'].append(dv[:rows_p].reshape(B, Lp, HEADS, DIFF_VD)[:, :L])
        outs['ks'].append(dk[rows_p:].reshape(DB, DS, HEADS, 2 * DIFF_HD))
        outs['vs'].append(dv[rows_p:].reshape(DB, DS, HEADS, DIFF_VD))
        outs['rp'].append(rp); outs['rs'].append(rs)
        outs['gp'].append(gp); outs['gs'].append(gs)
        outs['cp'].append(cp); outs['cs'].append(cs)

    y_prompt = h[:rows_p].reshape(B, Lp, D)[:, N_META:L]
    y_sample = h[rows_p:].reshape(DB, DS, D)
    st = lambda k: jnp.stack(outs[k], axis=0)
    return (y_prompt, y_sample, st('kp'), st('vp'), st('ks'), st('vs'),
            st('rp'), st('rs'), st('gp'), st('gs'), st('cp'), st('cs'))
```

```python
import functools
import math

import numpy as np
import jax
import jax.numpy as jnp
from jax import lax
from jax.experimental import pallas as pl
from jax.experimental.pallas import tpu as pltpu

F32 = jnp.float32
BF16 = jnp.bfloat16

D_MODEL = 1024
N_META = 16
N_BRANCH = 4
BRANCH_W = 256
D_FF = 2816
CONV_C = 256
CONV_K = 31
HEADS = 4
RET_DK = 64
RET_DV = 64
GLA_DK = 32
GLA_DV = 64
GLA_RANK = 16
GLA_TAU = 16.0
DIFF_HD = 32
DIFF_VD = 64
REL_BUCKETS = 32
REL_MAX_DIST = 128
ROPE_BASE = 10000.0
EPS = 1e-6
NEG = -1e30
LOG2E = math.log2(math.e)
MAX_EXP2_SPAN = 100.0

N_LIN = 2304
OFF_GLOW = N_LIN
OFF_DIFF = OFF_GLOW + GLA_RANK
OFF_GATES = OFF_DIFF + 3 * 256
N_IN = OFF_GATES + N_BRANCH * D_MODEL

LANE = 128
ROW_TILE = 256
SEQ_BLOCK = 128
FAR_BLOCK = 512
HIST = 32
VMEM_LIMIT = 56 * 1024 * 1024
PAGES_PER_STEP = 16


def _sigmoid(x):
    return 1.0 / (1.0 + jnp.exp(-x))


def _silu(x):
    return x * _sigmoid(x)


def _log_sigmoid(x):
    return jnp.minimum(x, 0.0) - jnp.log1p(jnp.exp(-jnp.abs(x)))


def _rms(x, g):
    return x * lax.rsqrt(jnp.mean(x * x, axis=-1, keepdims=True) + EPS) * g


def _dot(a, b):
    return jnp.dot(a, b, preferred_element_type=F32)


def _dot_nt(a, b):
    return lax.dot_general(a, b, (((1,), (1,)), ((), ())), preferred_element_type=F32)


def _dot_tn(a, b):
    return lax.dot_general(a, b, (((0,), (0,)), ((), ())), preferred_element_type=F32)


def _split_bf16(x):
    hi = x.astype(BF16)
    lo = (x - hi.astype(F32)).astype(BF16)
    return hi, lo


def _group_mean(x, group):
    n = x.shape[-1]
    r = lax.broadcasted_iota(jnp.int32, (n, n), 0) // group
    c = lax.broadcasted_iota(jnp.int32, (n, n), 1) // group
    avg = jnp.where(r == c, 1.0 / group, 0.0).astype(BF16)
    hi, lo = _split_bf16(x)
    return _dot(hi, avg) + _dot(lo, avg)


def _const_spec(shape):
    nd = len(shape)
    return pl.BlockSpec(shape, lambda *_: (0,) * nd, pipeline_mode=pl.Buffered(1))


def _ffn(x, wg_ref, wu_ref, wd_ref):
    gate = _dot(x, wg_ref[...])
    up = _dot(x, wu_ref[...])
    act = (_silu(gate) * up).astype(BF16)
    return _dot(act, wd_ref[...])


def _head_kernel(h_ref, g1_ref, wg_ref, wu_ref, wd_ref, gm_ref, wlin_ref, blin_ref, wlow_ref, blow_ref,
                 aw_ref, ab_ref, wdf_ref, bdf_ref, qg_ref, kg_ref,
                 h1_ref, zlin_ref, gdec_ref, dq_ref, dk_ref, dv_ref, dqt_ref, dkt_ref, dvt_ref):
    h = h_ref[...]
    h1 = h + 0.5 * _ffn(_rms(h, g1_ref[...]).astype(BF16), wg_ref, wu_ref, wd_ref)
    h1_ref[...] = h1
    x = _rms(h1, gm_ref[...]).astype(BF16)
    zlin_ref[...] = _dot(x, wlin_ref[...]) + blin_ref[...]
    g_low = _dot(x, wlow_ref[...]) + blow_ref[...]
    g_pre = _dot(g_low.astype(BF16), aw_ref[...]) + ab_ref[...]
    gdec_ref[...] = _log_sigmoid(g_pre) * (1.0 / GLA_TAU)
    zd = _dot(x, wdf_ref[...]) + bdf_ref[...]
    d_q = zd[:, 0:256]
    d_k = zd[:, 256:512]
    d_v = zd[:, 512:768]
    q_n = d_q * lax.rsqrt(_group_mean(d_q * d_q, DIFF_HD) + EPS) * qg_ref[...]
    dq_ref[...] = q_n
    dqt_ref[...] = q_n.T
    k_n = d_k * lax.rsqrt(_group_mean(d_k * d_k, DIFF_HD) + EPS) * kg_ref[...]
    dk_ref[...] = k_n
    dkt_ref[...] = k_n.T
    dv_ref[...] = d_v
    dvt_ref[...] = d_v.T


def _head_call(h, w):
    rows = h.shape[0]
    row = lambda n: pl.BlockSpec((ROW_TILE, n), lambda i: (i, 0))
    col = pl.BlockSpec((256, ROW_TILE), lambda i: (0, i))
    consts = [w['g1'], w['wg1'], w['wu1'], w['wd1'], w['gm'], w['wlin'], w['blin'], w['wlow'], w['blow'],
              w['aw'], w['ab'], w['wdf'], w['bdf'], w['qg'], w['kg']]
    widths = (D_MODEL, N_LIN, LANE, 256, 256, 256)
    return pl.pallas_call(
        _head_kernel,
        grid=(rows // ROW_TILE,),
        in_specs=[row(D_MODEL)] + [_const_spec(c.shape) for c in consts],
        out_specs=[row(n) for n in widths] + [col, col, col],
        out_shape=([jax.ShapeDtypeStruct((rows, n), F32) for n in widths]
                   + [jax.ShapeDtypeStruct((256, rows), F32)] * 3),
        compiler_params=pltpu.CompilerParams(dimension_semantics=("arbitrary",),
                                             vmem_limit_bytes=VMEM_LIMIT),
        name="head",
    )(h, *consts)


def _tail_kernel(h1_ref, brlp_ref, brls_ref, brdp_ref, brds_ref, gm_ref, wgt_ref, bgt_ref, wb_ref, wo_ref, g2_ref,
                 wg_ref, wu_ref, wd_ref, out_ref, *, prompt_tiles):
    h1 = h1_ref[...]
    x = _rms(h1, gm_ref[...]).astype(BF16)
    is_sample = pl.program_id(0) >= prompt_tiles
    merged = None
    for n in range(N_BRANCH):
        if n < 3:
            cols = slice(n * BRANCH_W, (n + 1) * BRANCH_W)
            br = jnp.where(is_sample, brls_ref[:, cols], brlp_ref[:, cols])
        else:
            br = jnp.where(is_sample, brds_ref[...], brdp_ref[...])
        gate = _dot(x, wgt_ref[:, n * D_MODEL:(n + 1) * D_MODEL]) + bgt_ref[:, n * D_MODEL:(n + 1) * D_MODEL]
        term = _dot(br.astype(BF16), wb_ref[n]) * _sigmoid(gate)
        merged = term if merged is None else merged + term
    h2 = h1 + _dot(merged.astype(BF16), wo_ref[...])
    out_ref[...] = h2 + 0.5 * _ffn(_rms(h2, g2_ref[...]).astype(BF16), wg_ref, wu_ref, wd_ref)


def _tail_call(h1, brl_p, brl_s, brd_p, brd_s, w):
    rows = h1.shape[0]
    p_tiles = brl_p.shape[0] // ROW_TILE
    row = lambda n: pl.BlockSpec((ROW_TILE, n), lambda i: (i, 0))
    row_p = lambda n: pl.BlockSpec((ROW_TILE, n), lambda i: (jnp.minimum(i, p_tiles - 1), 0))
    row_s = lambda n: pl.BlockSpec((ROW_TILE, n), lambda i: (jnp.maximum(i - p_tiles, 0), 0))
    consts = [w['gm'], w['wgt'], w['bgt'], w['wb'], w['wo'], w['g2'], w['wg2'], w['wu2'], w['wd2']]
    return pl.pallas_call(
        functools.partial(_tail_kernel, prompt_tiles=p_tiles),
        grid=(rows // ROW_TILE,),
        in_specs=[row(D_MODEL), row_p(3 * BRANCH_W), row_s(3 * BRANCH_W), row_p(BRANCH_W), row_s(BRANCH_W)]
                 + [_const_spec(c.shape) for c in consts],
        out_specs=row(D_MODEL),
        out_shape=jax.ShapeDtypeStruct((rows, D_MODEL), F32),
        compiler_params=pltpu.CompilerParams(dimension_semantics=("arbitrary",),
                                             vmem_limit_bytes=VMEM_LIMIT),
        name="tail",
    )(h1, brl_p, brl_s, brd_p, brd_s, *consts)


def _mix_kernel(z_ref, gd_ref, cos_ref, sin_ref, cpast_ref, sr0_ref, sg0_ref, cw_ref, cb_ref, lng_ref, lnb_ref,
                rgn_ref, ggn_ref, br_ref, sr_out, sg_out, cs_out, sr_s, sg_s, u_s, *, chunk, n_chunks, seq_len):
    C = chunk
    c = pl.program_id(1)
    low = C >= 16
    cast = (lambda a: a.astype(BF16)) if low else (lambda a: a)

    t_col = lax.broadcasted_iota(jnp.int32, (C, 1), 0)
    s_row = lax.broadcasted_iota(jnp.int32, (1, C), 1)
    causal = t_col >= s_row
    tf = t_col.astype(F32)
    padded = n_chunks * C > seq_len
    if padded:
        valid = (c * C + t_col) < seq_len
        nvf = jnp.zeros((1, 1), F32) + jnp.minimum(seq_len - c * C, C).astype(F32)
        keep = lambda a: jnp.where(valid, a, 0.0)
    else:
        nvf = jnp.full((1, 1), float(C), F32)
        keep = lambda a: a

    @pl.when(c == 0)
    def _init():
        sr_s[...] = sr0_ref[0]
        sg_s[...] = sg0_ref[0]
        u_s[0:HIST - (CONV_K - 1), :] = jnp.zeros((HIST - (CONV_K - 1), CONV_C), F32)
        u_s[HIST - (CONV_K - 1):HIST, :] = cpast_ref[0]

    @pl.when(c > 0)
    def _shift():
        u_s[0:HIST, :] = u_s[C:C + HIST, :]

    u = z_ref[:, 0:256] * _sigmoid(z_ref[:, 256:512])
    u_s[HIST:HIST + C, :] = u
    acc = jnp.zeros((C, CONV_C), F32)
    for j in range(CONV_K):
        lo = HIST - (CONV_K - 1) + j
        acc = acc + cw_ref[j:j + 1, :] * u_s[lo:lo + C, :]
    conv = acc + cb_ref[...]
    xc = conv - jnp.mean(conv, axis=-1, keepdims=True)
    ln = xc * lax.rsqrt(jnp.mean(xc * xc, axis=-1, keepdims=True) + EPS) * lng_ref[...] + lnb_ref[...]
    br_ref[:, 0:256] = _silu(ln)

    lane = lax.broadcasted_iota(jnp.int32, (1, 256), 1)
    first_half = (lane % RET_DK) < (RET_DK // 2)
    cos = cos_ref[...]
    sin = sin_ref[...]

    def rope(a):
        swapped = jnp.where(first_half, pltpu.roll(a, 256 - RET_DK // 2, 1), pltpu.roll(a, RET_DK // 2, 1))
        return a * cos + swapped * sin

    rq = rope(z_ref[:, 512:768])
    rk = keep(rope(z_ref[:, 768:1024]) * RET_DK ** -0.5)
    rv = keep(z_ref[:, 1024:1280])
    dts = (t_col - s_row).astype(F32)
    for h in range(HEADS):
        lg = math.log1p(-(2.0 ** (-5 - h)))
        sl = slice(h * RET_DK, (h + 1) * RET_DK)
        qh, kh, vh = rq[:, sl], rk[:, sl], cast(rv[:, sl])
        decay = jnp.where(causal, jnp.exp(dts * lg), 0.0)
        a = _dot_nt(cast(qh), cast(kh)) * decay
        s_old = sr_s[h]
        o = _dot(cast(a), vh) + _dot(cast(qh * jnp.exp((tf + 1.0) * lg)), cast(s_old))
        sr_s[h] = jnp.exp(nvf * lg) * s_old + _dot_tn(cast(kh * jnp.exp((nvf - 1.0 - tf) * lg)), vh)
        oc = o - jnp.mean(o, axis=-1, keepdims=True)
        y = oc * lax.rsqrt(jnp.mean(oc * oc, axis=-1, keepdims=True) + EPS) * rgn_ref[:, sl]
        br_ref[:, 256 + h * RET_DV:256 + (h + 1) * RET_DV] = _silu(z_ref[:, 1280 + h * 64:1280 + (h + 1) * 64]) * y

    gq = z_ref[:, 1536:1664] * GLA_DK ** -0.5
    gk = keep(z_ref[:, 1664:1792])
    gv = keep(z_ref[:, 1792:2048])
    g = keep(gd_ref[...])
    tri = jnp.where(causal, 1.0, 0.0)
    if low:
        g_hi, g_lo = _split_bf16(g)
        tri = tri.astype(BF16)
        bcum = _dot(tri, g_hi) + _dot(tri, g_lo)
    else:
        bcum = _dot(tri, g)
    mid = C // 2 - 1
    b_mid = bcum[mid:mid + 1, :]
    b_last = bcum[C - 1:C, :]
    q_intra = gq * jnp.exp(bcum - b_mid)
    k_intra = gk * jnp.exp(b_mid - bcum)
    q_inter = gq * jnp.exp(bcum)
    k_state = gk * jnp.exp(b_last - bcum)
    e_last = jnp.where(t_col == C - 1, jnp.exp(bcum), 0.0)
    ones = jnp.ones((C, GLA_DV), BF16 if low else F32)
    if low:
        e_hi, e_lo = _split_bf16(e_last)
    for h in range(HEADS):
        sl = slice(h * GLA_DK, (h + 1) * GLA_DK)
        vs = slice(h * GLA_DV, (h + 1) * GLA_DV)
        vh = cast(gv[:, vs])
        a = jnp.where(causal, _dot_nt(cast(q_intra[:, sl]), cast(k_intra[:, sl])), 0.0)
        s_old = sg_s[h]
        o = _dot(cast(a), vh) + _dot(cast(q_inter[:, sl]), cast(s_old))
        if low:
            e_col = _dot_tn(e_hi[:, sl], ones) + _dot_tn(e_lo[:, sl], ones)
        else:
            e_col = _dot_tn(e_last[:, sl], ones)
        sg_s[h] = e_col * s_old + _dot_tn(cast(k_state[:, sl]), vh)
        y = o * lax.rsqrt(jnp.mean(o * o, axis=-1, keepdims=True) + EPS) * ggn_ref[:, vs]
        br_ref[:, 512 + h * GLA_DV:512 + (h + 1) * GLA_DV] = _silu(z_ref[:, 2048 + h * 64:2048 + (h + 1) * 64]) * y

    @pl.when(c == n_chunks - 1)
    def _final():
        sr_out[0] = sr_s[...]
        sg_out[0] = sg_s[...]
        n_last = seq_len - (n_chunks - 1) * C
        cs_out[0] = u_s[HIST + n_last - (CONV_K - 1):HIST + n_last, :]


def _mix_call(zlin, gdec, cos, sin, conv_past, ret0, gla0, w, *, n_seq, chunk, n_chunks, seq_len,
              row_block0, name):
    blk = lambda n: pl.BlockSpec((chunk, n), lambda s, c: (row_block0 + s * n_chunks + c, 0))
    out_blk = pl.BlockSpec((chunk, 3 * BRANCH_W), lambda s, c: (s * n_chunks + c, 0))
    per_seq = lambda shp: pl.BlockSpec((1,) + shp, lambda s, c: (s,) + (0,) * len(shp))
    const = lambda a: pl.BlockSpec(a.shape, lambda s, c: (0,) * a.ndim)
    consts = [w['cw'], w['cb'], w['lng'], w['lnb'], w['rgn'], w['ggn']]
    in_specs = ([blk(N_LIN), blk(LANE),
                 pl.BlockSpec((chunk, 256), lambda s, c: (c, 0)), pl.BlockSpec((chunk, 256), lambda s, c: (c, 0)),
                 per_seq((CONV_K - 1, CONV_C)), per_seq((HEADS, RET_DK, RET_DV)), per_seq((HEADS, GLA_DK, GLA_DV))]
                + [const(a) for a in consts])
    args = [zlin, gdec, cos, sin, conv_past, ret0, gla0] + consts
    return pl.pallas_call(
        functools.partial(_mix_kernel, chunk=chunk, n_chunks=n_chunks, seq_len=seq_len),
        grid=(n_seq, n_chunks),
        in_specs=in_specs,
        out_specs=[out_blk, per_seq((HEADS, RET_DK, RET_DV)), per_seq((HEADS, GLA_DK, GLA_DV)),
                   per_seq((CONV_K - 1, CONV_C))],
        out_shape=[jax.ShapeDtypeStruct((n_seq * n_chunks * chunk, 3 * BRANCH_W), F32),
                   jax.ShapeDtypeStruct((n_seq, HEADS, RET_DK, RET_DV), F32),
                   jax.ShapeDtypeStruct((n_seq, HEADS, GLA_DK, GLA_DV), F32),
                   jax.ShapeDtypeStruct((n_seq, CONV_K - 1, CONV_C), F32)],
        scratch_shapes=[pltpu.VMEM((HEADS, RET_DK, RET_DV), F32), pltpu.VMEM((HEADS, GLA_DK, GLA_DV), F32),
                        pltpu.VMEM((HIST + chunk, CONV_C), F32)],
        compiler_params=pltpu.CompilerParams(dimension_semantics=("arbitrary", "arbitrary")),
        name=name,
    )(*args)


def _t5_bucket_np(rel):
    rel = np.asarray(rel)
    n = np.maximum(rel, 0)
    max_exact = REL_BUCKETS // 2
    nf = np.maximum(n, 1).astype(np.float64)
    large = max_exact + (np.log(nf / max_exact) / math.log(REL_MAX_DIST / max_exact)
                         * (REL_BUCKETS - max_exact)).astype(np.int64)
    large = np.minimum(large, REL_BUCKETS - 1)
    return np.where(n < max_exact, n, large).astype(np.int32)


def _lambda(lamv_ref, lam_init):
    a = jnp.sum(lamv_ref[0:1, :] * lamv_ref[1:2, :], axis=-1, keepdims=True)
    b = jnp.sum(lamv_ref[2:3, :] * lamv_ref[3:4, :], axis=-1, keepdims=True)
    return jnp.exp(a) - jnp.exp(b) + lam_init


def _flash_kernel(bt_ref, rb_ref, top_ref, lamv_ref, sub_ref, qt_ref, k_ref, vt_ref, o_ref,
                  bias_s, qbd_s, m_s, l_s, acc_s, *, lam_init, n_blocks):
    QB = SEQ_BLOCK
    SUB = FAR_BLOCK // QB
    b = pl.program_id(0)
    i = pl.program_id(1)

    @pl.when((b == 0) & (i == 0))
    def _tables():
        for kind, src in ((1, 1), (2, 0)):
            bt = bt_ref[src]
            tiles = [jnp.zeros((QB, QB), F32) for _ in range(HEADS)]
            for bk in range(REL_BUCKETS):
                hit = bt == bk
                for h in range(HEADS):
                    tiles[h] = jnp.where(hit, rb_ref[bk, h] * LOG2E, tiles[h])
            for h in range(HEADS):
                bias_s[kind, h] = jnp.where(bt < 0, NEG, tiles[h])
        for h in range(HEADS):
            bias_s[0, h] = jnp.zeros((QB, QB), F32) + rb_ref[REL_BUCKETS - 1, h] * LOG2E
            bias_s[3, h] = jnp.full((QB, QB), NEG, F32)

    qt = qt_ref[...] * (DIFF_HD ** -0.5 * LOG2E)
    row = lax.broadcasted_iota(jnp.int32, (256, 1), 0)
    for hj in range(2 * HEADS):
        own = (row >= hj * DIFF_HD) & (row < (hj + 1) * DIFF_HD)
        qbd_s[:, hj * QB:(hj + 1) * QB] = jnp.where(own, qt, 0.0).astype(BF16)
    m_s[...] = jnp.full(m_s.shape, NEG, F32)
    l_s[...] = jnp.zeros(l_s.shape, F32)
    acc_s[...] = jnp.zeros(acc_s.shape, F32)

    def online(h, st, shift, vt):
        m_old = m_s[h]
        m_new = jnp.maximum(m_old, jnp.max(st, axis=0, keepdims=True) + shift)
        alpha = jnp.exp2(m_old - m_new)
        pt = jnp.exp2(st - (m_new - shift))
        l_s[h] = alpha * l_s[h] + jnp.sum(pt, axis=0, keepdims=True)
        m_s[h] = m_new
        acc_s[h] = alpha * acc_s[h] + _dot(vt[h * DIFF_VD:(h + 1) * DIFF_VD, :], pt.astype(BF16))

    def fixed(h, st, shift, vt):
        pt = jnp.exp2(st - (top_ref[h] - shift))
        l_s[h] = l_s[h] + jnp.sum(pt, axis=0, keepdims=True)
        acc_s[h] = acc_s[h] + _dot(vt[h * DIFF_VD:(h + 1) * DIFF_VD, :], pt.astype(BF16))

    n_far = jnp.maximum(i - 1, 0) // SUB

    def sweep(accumulate):
        def far_body(kb, carry):
            off = pl.multiple_of(kb * FAR_BLOCK, FAR_BLOCK)
            k = k_ref[pl.ds(off, FAR_BLOCK), :].astype(BF16)
            vt = vt_ref[:, pl.ds(off, FAR_BLOCK)].astype(BF16)
            for h in range(HEADS):
                st = _dot(k, qbd_s[:, 2 * h * QB:(2 * h + 2) * QB])
                accumulate(h, st, rb_ref[REL_BUCKETS - 1, h] * LOG2E, vt)
            return carry

        lax.fori_loop(0, n_far, far_body, 0)

        def near_body(kb, carry):
            first = kb * SUB
            blk0 = jnp.minimum(first, n_blocks - SUB)
            off = pl.multiple_of(blk0 * QB, QB)
            k = k_ref[pl.ds(off, FAR_BLOCK), :].astype(BF16)
            vt = vt_ref[:, pl.ds(off, FAR_BLOCK)].astype(BF16)
            kinds = []
            for c in range(SUB):
                blk = blk0 + c
                kind = jnp.where(blk == i, 2, jnp.where(blk == i - 1, 1, 0))
                kinds.append(jnp.where((blk > i) | (blk < first), 3, kind))
            for h in range(HEADS):
                st = _dot(k, qbd_s[:, 2 * h * QB:(2 * h + 2) * QB])
                parts = []
                for c in range(SUB):
                    tile = bias_s[kinds[c], h]
                    parts.append(st[c * QB:(c + 1) * QB, :] + jnp.concatenate([tile, tile], axis=1))
                accumulate(h, jnp.concatenate(parts, axis=0), 0.0, vt)
            return carry

        lax.fori_loop(n_far, i // SUB + 1, near_body, 0)

    bounded = top_ref[HEADS] > 0.5

    @pl.when(bounded)
    def _fixed():
        sweep(fixed)

    @pl.when(jnp.logical_not(bounded))
    def _online():
        sweep(online)

    lam = _lambda(lamv_ref, lam_init)
    outs = []
    for h in range(HEADS):
        ot = acc_s[h] / l_s[h]
        d = ot[:, 0:QB] - lam * ot[:, QB:2 * QB]
        y = d * lax.rsqrt(jnp.mean(d * d, axis=0, keepdims=True) + EPS) * sub_ref[...] * (1.0 - lam_init)
        outs.append(y)
    o_ref[...] = jnp.concatenate(outs, axis=0).T


def _score_top(rel_bias, q_gain, k_gain):
    reach = DIFF_HD ** 0.5 * jnp.max(jnp.abs(q_gain * k_gain))
    top = (reach + jnp.max(rel_bias, axis=0)) * LOG2E
    low = (-reach + jnp.min(rel_bias, axis=0)) * LOG2E
    ok = jnp.all(top - low < MAX_EXP2_SPAN).astype(F32)
    return jnp.concatenate([top, ok[None]]).astype(F32)


def _flash_call(bt, rel_bias, top, lamv, sub_col, dqt, dk, dvt, *, n_seq, seq_rows, lam_init):
    rows = n_seq * seq_rows
    nq = seq_rows // SEQ_BLOCK
    assert seq_rows >= FAR_BLOCK
    const = lambda a: pl.BlockSpec(a.shape, lambda b, i: (0,) * a.ndim)
    smem = pl.BlockSpec(memory_space=pltpu.SMEM)
    return pl.pallas_call(
        functools.partial(_flash_kernel, lam_init=lam_init, n_blocks=nq),
        grid=(n_seq, nq),
        in_specs=[const(bt), smem, smem, const(lamv), const(sub_col),
                  pl.BlockSpec((256, SEQ_BLOCK), lambda b, i: (0, b * nq + i)),
                  pl.BlockSpec((seq_rows, 256), lambda b, i: (b, 0)),
                  pl.BlockSpec((256, seq_rows), lambda b, i: (0, b))],
        out_specs=pl.BlockSpec((SEQ_BLOCK, 256), lambda b, i: (b * nq + i, 0)),
        out_shape=jax.ShapeDtypeStruct((rows, 256), F32),
        scratch_shapes=[pltpu.VMEM((4, HEADS, SEQ_BLOCK, SEQ_BLOCK), F32),
                        pltpu.VMEM((256, 2 * HEADS * SEQ_BLOCK), BF16),
                        pltpu.VMEM((HEADS, 1, 2 * SEQ_BLOCK), F32), pltpu.VMEM((HEADS, 1, 2 * SEQ_BLOCK), F32),
                        pltpu.VMEM((HEADS, DIFF_VD, 2 * SEQ_BLOCK), F32)],
        compiler_params=pltpu.CompilerParams(dimension_semantics=("arbitrary", "arbitrary"),
                                             vmem_limit_bytes=VMEM_LIMIT),
        name="attn_prompt",
    )(bt, rel_bias, top, lamv, sub_col, dqt, dk, dvt)


def _decode_kernel(pt_ref, bt_ref, rb_ref, lamv_ref, sub_ref, q_ref, kn_ref, vn_ref, *refs,
                   pages_per_step, n_steps, dec_seq, lam_init):
    G = pages_per_step
    kt_refs = refs[:G]
    vt_refs = refs[G:2 * G]
    o_ref, bias_pg_s, bias_new_s, far_s, qs_s, m_s, l_s, acc_s = refs[2 * G:]
    del pt_ref
    b = pl.program_id(0)
    g = pl.program_id(1)
    last = g == n_steps - 1
    NQ = 2 * HEADS * dec_seq
    seqs_per_block = kn_ref.shape[0] // dec_seq

    row_head = lax.broadcasted_iota(jnp.int32, (NQ, 1), 0) // (2 * dec_seq)

    def bias_rows(bk):
        out = jnp.zeros((NQ, 1), F32)
        for h in range(HEADS):
            out = jnp.where(row_head == h, rb_ref[bk, h], out)
        return out

    @pl.when((b == 0) & (g == 0))
    def _tables():
        for src, dst in ((0, bias_pg_s), (1, bias_new_s)):
            bt = bt_ref[src]
            tile = jnp.zeros(bt.shape, F32)
            for bk in range(REL_BUCKETS):
                tile = jnp.where(bt == bk, bias_rows(bk), tile)
            dst[...] = jnp.where(bt < 0, NEG, tile)
        far_s[...] = jnp.zeros(far_s.shape, F32) + bias_rows(REL_BUCKETS - 1)

    @pl.when(g == 0)
    def _init():
        q = q_ref[...] * DIFF_HD ** -0.5
        lane = lax.broadcasted_iota(jnp.int32, (1, 256), 1)
        for hj in range(2 * HEADS):
            own = (lane >= hj * DIFF_HD) & (lane < (hj + 1) * DIFF_HD)
            qs_s[hj * dec_seq:(hj + 1) * dec_seq, :] = jnp.where(own, q, 0.0)
        m_s[...] = jnp.full(m_s.shape, NEG, F32)
        l_s[...] = jnp.zeros(l_s.shape, F32)
        acc_s[...] = jnp.zeros(acc_s.shape, F32)

    def update(scores, pv_fn):
        m_old = m_s[...]
        s_max = scores[0]
        for s in scores[1:]:
            s_max = jnp.maximum(s_max, s)
        m_new = jnp.maximum(m_old, jnp.max(s_max, axis=1, keepdims=True))
        alpha = jnp.exp(m_old - m_new)
        probs = [jnp.exp(s - m_new) for s in scores]
        p_sum = probs[0]
        for p in probs[1:]:
            p_sum = p_sum + p
        l_s[...] = alpha * l_s[...] + jnp.sum(p_sum, axis=1, keepdims=True)
        m_s[...] = m_new
        acc_s[...] = jnp.concatenate([alpha, alpha], axis=1) * acc_s[...] + pv_fn(probs)

    qs = qs_s[...].astype(BF16)
    scores = []
    for p in range(G):
        s = _dot(qs, kt_refs[p][...].astype(BF16))
        if p == G - 1:
            s = s + jnp.where(last, bias_pg_s[...], far_s[...])
        else:
            s = s + far_s[...]
        scores.append(s)

    def pv_pages(probs):
        out = None
        for p in range(G):
            t = _dot_nt(probs[p].astype(BF16), vt_refs[p][...].astype(BF16))
            out = t if out is None else out + t
        return out

    update(scores, pv_pages)

    @pl.when(last)
    def _finish():
        key_seq = lax.broadcasted_iota(jnp.int32, (1, kn_ref.shape[0]), 1) // dec_seq
        s_new = _dot_nt(qs, kn_ref[...].astype(BF16))
        s_new = jnp.where(key_seq == b % seqs_per_block, s_new + bias_new_s[...], NEG)
        update([s_new], lambda probs: _dot(probs[0].astype(BF16), vn_ref[...].astype(BF16)))
        l_all = l_s[...]
        o_all = acc_s[...] / jnp.concatenate([l_all, l_all], axis=1)
        lam = _lambda(lamv_ref, lam_init)
        for h in range(HEADS):
            r1 = (2 * h) * dec_seq
            r2 = (2 * h + 1) * dec_seq
            cols = slice(h * DIFF_VD, (h + 1) * DIFF_VD)
            d = o_all[r1:r1 + dec_seq, cols] - lam * o_all[r2:r2 + dec_seq, cols]
            y = d * lax.rsqrt(jnp.mean(d * d, axis=-1, keepdims=True) + EPS) * sub_ref[...] * (1.0 - lam_init)
            o_ref[:, cols] = y


def _decode_call(page_table, bt, rel_bias, lamv, sub, dq, dk, dv, cache_kt, cache_vt, *,
                 layer, n_seq, dec_seq, row0, lam_init):
    n_pages = page_table.shape[1]
    page = cache_kt.shape[3]
    G = min(PAGES_PER_STEP, n_pages)
    n_steps = n_pages // G
    nq = 2 * HEADS * dec_seq
    new_rows = LANE
    per_blk = new_rows // dec_seq
    const = lambda a: pl.BlockSpec(a.shape, lambda b, g, pt: (0,) * a.ndim)
    own = pl.BlockSpec((dec_seq, 256), lambda b, g, pt: (row0 // dec_seq + b, 0))
    new = pl.BlockSpec((new_rows, 256), lambda b, g, pt: (row0 // new_rows + b // per_blk, 0))

    def page_spec(p):
        return pl.BlockSpec((None, None, 256, page),
                            lambda b, g, pt: (layer, pt[b * n_pages + g * G + p], 0, 0))

    in_specs = ([const(bt), pl.BlockSpec(memory_space=pltpu.SMEM), const(lamv), const(sub), own, new, new]
                + [page_spec(p) for p in range(G)] + [page_spec(p) for p in range(G)])
    args = [bt, rel_bias, lamv, sub, dq, dk, dv] + [cache_kt] * G + [cache_vt] * G
    return pl.pallas_call(
        functools.partial(_decode_kernel, pages_per_step=G, n_steps=n_steps, dec_seq=dec_seq, lam_init=lam_init),
        grid_spec=pltpu.PrefetchScalarGridSpec(
            num_scalar_prefetch=1,
            grid=(n_seq, n_steps),
            in_specs=in_specs,
            out_specs=pl.BlockSpec((dec_seq, 256), lambda b, g, pt: (b, 0)),
            scratch_shapes=[pltpu.VMEM((nq, page), F32), pltpu.VMEM((nq, new_rows), F32),
                            pltpu.VMEM((nq, page), F32), pltpu.VMEM((nq, 256), F32),
                            pltpu.VMEM((nq, LANE), F32), pltpu.VMEM((nq, LANE), F32), pltpu.VMEM((nq, 256), F32)]),
        out_shape=jax.ShapeDtypeStruct((n_seq * dec_seq, 256), F32),
        compiler_params=pltpu.CompilerParams(dimension_semantics=("arbitrary", "arbitrary"),
                                             vmem_limit_bytes=VMEM_LIMIT),
        name="attn_sample",
    )(page_table.reshape(-1), *args)


def _rope_tables(pos):
    half = RET_DK // 2
    inv = ROPE_BASE ** (-np.arange(half, dtype=np.float64) / half)
    ang = np.asarray(pos, np.float64)[:, None] * inv[None, :]
    cos = np.concatenate([np.cos(ang), np.cos(ang)], axis=1)
    sin = np.concatenate([-np.sin(ang), np.sin(ang)], axis=1)
    return (jnp.asarray(np.tile(cos, (1, HEADS)), F32), jnp.asarray(np.tile(sin, (1, HEADS)), F32))


def _prompt_bucket_tiles():
    t = np.arange(SEQ_BLOCK)
    rel = t[None, :] - t[:, None]
    diag = np.where(rel >= 0, _t5_bucket_np(rel), -1)
    sub = _t5_bucket_np(rel + SEQ_BLOCK)
    return jnp.asarray(np.stack([diag, sub]).astype(np.int32))


def _sample_bucket_tiles(page, dec_seq):
    assert page == LANE
    iq = (np.arange(2 * HEADS * dec_seq) % dec_seq)[:, None]
    past = _t5_bucket_np(page + iq - np.arange(page)[None, :])
    rel_new = iq - (np.arange(LANE) % dec_seq)[None, :]
    new = np.where(rel_new >= 0, _t5_bucket_np(rel_new), -1)
    return jnp.asarray(np.stack([past, new]).astype(np.int32))


def kernel(x_prompt, x_sample, cache_k, cache_v, page_table, state_ret, state_gla, state_conv, meta_tokens,
           rel_bias, norm_ffn1, ffn1_gate, ffn1_up, ffn1_down, norm_mix, w_in, b_in, conv_w, conv_b, conv_ln_g,
           conv_ln_b, ret_gn, gla_alpha_w, gla_alpha_b, gla_gn, q_norm, k_norm, lam_q1, lam_k1, lam_q2, lam_k2,
           diff_subln, w_branch, w_out, norm_ffn2, ffn2_gate, ffn2_up, ffn2_down):
    B, S, D = x_prompt.shape
    DB, DS, _ = x_sample.shape
    depth = w_in.shape[0]
    L = S + N_META
    Lp = -(-L // SEQ_BLOCK) * SEQ_BLOCK
    n_chunks = Lp // SEQ_BLOCK
    n_pool, page = cache_k.shape[1], cache_k.shape[2]
    past_len = page_table.shape[1] * page
    rows_p = B * Lp
    rows = rows_p + DB * DS
    assert rows % ROW_TILE == 0 and rows_p % LANE == 0 and (DB * DS) % LANE == 0 and DS % 8 == 0

    pieces = []
    for b in range(B):
        pieces += [meta_tokens.astype(F32), x_prompt[b], jnp.zeros((Lp - L, D), F32)]
    h = jnp.concatenate(pieces + [x_sample.reshape(DB * DS, D)], axis=0)

    cos_p, sin_p = _rope_tables(np.arange(Lp))
    cos_s, sin_s = _rope_tables(past_len + np.arange(DS))
    bt_prompt = _prompt_bucket_tiles()
    bt_sample = _sample_bucket_tiles(page, DS)
    ckt = jnp.transpose(cache_k, (0, 1, 3, 4, 2)).reshape(depth, n_pool, HEADS * 2 * DIFF_HD, page)
    cvt = jnp.transpose(cache_v, (0, 1, 3, 4, 2)).reshape(depth, n_pool, HEADS * DIFF_VD, page)
    zeros_conv = jnp.zeros((B, CONV_K - 1, CONV_C), F32)
    zeros_ret = jnp.zeros((B, HEADS, RET_DK, RET_DV), F32)
    zeros_gla = jnp.zeros((B, HEADS, GLA_DK, GLA_DV), F32)

    outs = {k: [] for k in ('kp', 'vp', 'ks', 'vs', 'rp', 'rs', 'gp', 'gs', 'cp', 'cs')}
    row2 = lambda a: a.reshape(1, -1).astype(F32)
    for l in range(depth):
        lam_init = 0.8 - 0.6 * math.exp(-0.3 * l)
        wi, bi = w_in[l], b_in[l]
        w = dict(
            g1=row2(norm_ffn1[l]), wg1=ffn1_gate[l].astype(BF16), wu1=ffn1_up[l].astype(BF16),
            wd1=ffn1_down[l].astype(BF16), gm=row2(norm_mix[l]),
            wlin=wi[:, :N_LIN].astype(BF16), blin=row2(bi[:N_LIN]),
            wlow=wi[:, OFF_GLOW:OFF_DIFF].astype(BF16), blow=row2(bi[OFF_GLOW:OFF_DIFF]),
            aw=gla_alpha_w[l].astype(BF16), ab=row2(gla_alpha_b[l]),
            wdf=wi[:, OFF_DIFF:OFF_GATES].astype(BF16), bdf=row2(bi[OFF_DIFF:OFF_GATES]),
            qg=row2(jnp.tile(q_norm[l], 2 * HEADS)), kg=row2(jnp.tile(k_norm[l], 2 * HEADS)),
            wgt=wi[:, OFF_GATES:].astype(BF16), bgt=row2(bi[OFF_GATES:]),
            wb=w_branch[l].astype(BF16), wo=w_out[l].astype(BF16), g2=row2(norm_ffn2[l]),
            wg2=ffn2_gate[l].astype(BF16), wu2=ffn2_up[l].astype(BF16), wd2=ffn2_down[l].astype(BF16),
            cw=conv_w[l], cb=row2(conv_b[l]), lng=row2(conv_ln_g[l]), lnb=row2(conv_ln_b[l]),
            rgn=row2(ret_gn[l]), ggn=row2(gla_gn[l]),
        )
        lamv = jnp.stack([lam_q1[l], lam_k1[l], lam_q2[l], lam_k2[l]]).astype(F32)
        sub = row2(diff_subln[l])

        h1, zlin, gdec, dq, dk, dv, dqt, dkt, dvt = _head_call(h, w)

        brl_p, rp, gp, cp = _mix_call(zlin, gdec, cos_p, sin_p, zeros_conv, zeros_ret, zeros_gla, w,
                                      n_seq=B, chunk=SEQ_BLOCK, n_chunks=n_chunks, seq_len=L, row_block0=0,
                                      name="mix_prompt")
        brl_s, rs, gs, cs = _mix_call(zlin, gdec, cos_s, sin_s, state_conv[l], state_ret[l], state_gla[l], w,
                                      n_seq=DB, chunk=DS, n_chunks=1, seq_len=DS, row_block0=rows_p // DS,
                                      name="mix_sample")
        top = _score_top(rel_bias, q_norm[l], k_norm[l])
        brd_p = _flash_call(bt_prompt, rel_bias, top, lamv, sub.reshape(-1, 1), dqt, dk, dvt,
                            n_seq=B, seq_rows=Lp, lam_init=lam_init)
        brd_s = _decode_call(page_table, bt_sample, rel_bias, lamv, sub, dq, dk, dv, ckt, cvt,
                             layer=l, n_seq=DB, dec_seq=DS, row0=rows_p, lam_init=lam_init)

        h = _tail_call(h1, brl_p, brl_s, brd_p, brd_s, w)

        seq_major = lambda t: jnp.transpose(t[:, :rows_p].reshape(HEADS, -1, B, Lp)[..., :L], (2, 3, 0, 1))
        outs['kp'].append(seq_major(dkt))
        outs['vp'].append(seq_major(dvt))
        outs['ks'].append(dk[rows_p:].reshape(DB, DS, HEADS, 2 * DIFF_HD))
        outs['vs'].append(dv[rows_p:].reshape(DB, DS, HEADS, DIFF_VD))
        outs['rp'].append(rp); outs['rs'].append(rs)
        outs['gp'].append(gp); outs['gs'].append(gs)
        outs['cp'].append(cp); outs['cs'].append(cs)

    y_prompt = jnp.stack([h[b * Lp + N_META:b * Lp + L] for b in range(B)], axis=0)
    y_sample = h[rows_p:].reshape(DB, DS, D)
    st = lambda k: jnp.stack(outs[k], axis=0)
    return (y_prompt, y_sample, st('kp'), st('vp'), st('ks'), st('vs'),
            st('rp'), st('rs'), st('gp'), st('gs'), st('cp'), st('cs'))
```

```python
import functools
import math

import numpy as np
import jax
import jax.numpy as jnp
from jax import lax
from jax.experimental import pallas as pl
from jax.experimental.pallas import tpu as pltpu

F32 = jnp.float32
BF16 = jnp.bfloat16

D_MODEL = 1024
N_META = 16
N_BRANCH = 4
BRANCH_W = 256
D_FF = 2816
CONV_C = 256
CONV_K = 31
HEADS = 4
RET_DK = 64
RET_DV = 64
GLA_DK = 32
GLA_DV = 64
GLA_RANK = 16
GLA_TAU = 16.0
DIFF_HD = 32
DIFF_VD = 64
REL_BUCKETS = 32
REL_MAX_DIST = 128
ROPE_BASE = 10000.0
EPS = 1e-6
NEG = -1e30
LOG2E = math.log2(math.e)
MAX_EXP2_SPAN = 100.0

N_LIN = 2304
OFF_GLOW = N_LIN
OFF_DIFF = OFF_GLOW + GLA_RANK
OFF_GATES = OFF_DIFF + 3 * 256
N_IN = OFF_GATES + N_BRANCH * D_MODEL

LANE = 128
ROW_TILE = 256
SEQ_BLOCK = 128
FAR_BLOCK = 512
HIST = 32
VMEM_LIMIT = 56 * 1024 * 1024
PAGES_PER_STEP = 16


def _sigmoid(x):
    return 1.0 / (1.0 + jnp.exp(-x))


def _silu(x):
    return x * _sigmoid(x)


def _log_sigmoid(x):
    return jnp.minimum(x, 0.0) - jnp.log1p(jnp.exp(-jnp.abs(x)))


def _rms(x, g):
    return x * lax.rsqrt(jnp.mean(x * x, axis=-1, keepdims=True) + EPS) * g


def _dot(a, b):
    return jnp.dot(a, b, preferred_element_type=F32)


def _dot_nt(a, b):
    return lax.dot_general(a, b, (((1,), (1,)), ((), ())), preferred_element_type=F32)


def _dot_tn(a, b):
    return lax.dot_general(a, b, (((0,), (0,)), ((), ())), preferred_element_type=F32)


def _split_bf16(x):
    hi = x.astype(BF16)
    lo = (x - hi.astype(F32)).astype(BF16)
    return hi, lo


def _group_mean(x, group):
    n = x.shape[-1]
    r = lax.broadcasted_iota(jnp.int32, (n, n), 0) // group
    c = lax.broadcasted_iota(jnp.int32, (n, n), 1) // group
    avg = jnp.where(r == c, 1.0 / group, 0.0).astype(BF16)
    hi, lo = _split_bf16(x)
    return _dot(hi, avg) + _dot(lo, avg)


def _const_spec(shape):
    nd = len(shape)
    return pl.BlockSpec(shape, lambda *_: (0,) * nd, pipeline_mode=pl.Buffered(1))


def _ffn(x, wg_ref, wu_ref, wd_ref):
    gate = _dot(x, wg_ref[...])
    up = _dot(x, wu_ref[...])
    act = (_silu(gate) * up).astype(BF16)
    return _dot(act, wd_ref[...])


def _head_kernel(h_ref, g1_ref, wg_ref, wu_ref, wd_ref, gm_ref, wlin_ref, blin_ref, wlow_ref, blow_ref,
                 aw_ref, ab_ref, wdf_ref, bdf_ref, qg_ref, kg_ref,
                 h1_ref, zlin_ref, gdec_ref, dq_ref, dk_ref, dv_ref, dqt_ref, dkt_ref, dvt_ref):
    h = h_ref[...]
    h1 = h + 0.5 * _ffn(_rms(h, g1_ref[...]).astype(BF16), wg_ref, wu_ref, wd_ref)
    h1_ref[...] = h1
    x = _rms(h1, gm_ref[...]).astype(BF16)
    zlin_ref[...] = _dot(x, wlin_ref[...]) + blin_ref[...]
    g_low = _dot(x, wlow_ref[...]) + blow_ref[...]
    g_pre = _dot(g_low.astype(BF16), aw_ref[...]) + ab_ref[...]
    gdec_ref[...] = _log_sigmoid(g_pre) * (1.0 / GLA_TAU)
    zd = _dot(x, wdf_ref[...]) + bdf_ref[...]
    d_q = zd[:, 0:256]
    d_k = zd[:, 256:512]
    d_v = zd[:, 512:768]
    q_n = d_q * lax.rsqrt(_group_mean(d_q * d_q, DIFF_HD) + EPS) * qg_ref[...]
    dq_ref[...] = q_n
    dqt_ref[...] = q_n.T
    k_n = d_k * lax.rsqrt(_group_mean(d_k * d_k, DIFF_HD) + EPS) * kg_ref[...]
    dk_ref[...] = k_n
    dkt_ref[...] = k_n.T
    dv_ref[...] = d_v
    dvt_ref[...] = d_v.T


def _head_call(h, w):
    rows = h.shape[0]
    row = lambda n: pl.BlockSpec((ROW_TILE, n), lambda i: (i, 0))
    col = pl.BlockSpec((256, ROW_TILE), lambda i: (0, i))
    consts = [w['g1'], w['wg1'], w['wu1'], w['wd1'], w['gm'], w['wlin'], w['blin'], w['wlow'], w['blow'],
              w['aw'], w['ab'], w['wdf'], w['bdf'], w['qg'], w['kg']]
    widths = (D_MODEL, N_LIN, LANE, 256, 256, 256)
    return pl.pallas_call(
        _head_kernel,
        grid=(rows // ROW_TILE,),
        in_specs=[row(D_MODEL)] + [_const_spec(c.shape) for c in consts],
        out_specs=[row(n) for n in widths] + [col, col, col],
        out_shape=([jax.ShapeDtypeStruct((rows, n), F32) for n in widths]
                   + [jax.ShapeDtypeStruct((256, rows), F32)] * 3),
        compiler_params=pltpu.CompilerParams(dimension_semantics=("arbitrary",),
                                             vmem_limit_bytes=VMEM_LIMIT),
        name="head",
    )(h, *consts)


def _tail_kernel(h1_ref, brlp_ref, brls_ref, brdp_ref, brds_ref, gm_ref, wgt_ref, bgt_ref, wb_ref, wo_ref, g2_ref,
                 wg_ref, wu_ref, wd_ref, out_ref, *, prompt_tiles):
    h1 = h1_ref[...]
    x = _rms(h1, gm_ref[...]).astype(BF16)
    is_sample = pl.program_id(0) >= prompt_tiles
    merged = None
    for n in range(N_BRANCH):
        if n < 3:
            cols = slice(n * BRANCH_W, (n + 1) * BRANCH_W)
            br = jnp.where(is_sample, brls_ref[:, cols], brlp_ref[:, cols])
        else:
            br = jnp.where(is_sample, brds_ref[...], brdp_ref[...])
        gate = _dot(x, wgt_ref[:, n * D_MODEL:(n + 1) * D_MODEL]) + bgt_ref[:, n * D_MODEL:(n + 1) * D_MODEL]
        term = _dot(br.astype(BF16), wb_ref[n]) * _sigmoid(gate)
        merged = term if merged is None else merged + term
    h2 = h1 + _dot(merged.astype(BF16), wo_ref[...])
    out_ref[...] = h2 + 0.5 * _ffn(_rms(h2, g2_ref[...]).astype(BF16), wg_ref, wu_ref, wd_ref)


def _tail_call(h1, brl_p, brl_s, brd_p, brd_s, w):
    rows = h1.shape[0]
    p_tiles = brl_p.shape[0] // ROW_TILE
    row = lambda n: pl.BlockSpec((ROW_TILE, n), lambda i: (i, 0))
    row_p = lambda n: pl.BlockSpec((ROW_TILE, n), lambda i: (jnp.minimum(i, p_tiles - 1), 0))
    row_s = lambda n: pl.BlockSpec((ROW_TILE, n), lambda i: (jnp.maximum(i - p_tiles, 0), 0))
    consts = [w['gm'], w['wgt'], w['bgt'], w['wb'], w['wo'], w['g2'], w['wg2'], w['wu2'], w['wd2']]
    return pl.pallas_call(
        functools.partial(_tail_kernel, prompt_tiles=p_tiles),
        grid=(rows // ROW_TILE,),
        in_specs=[row(D_MODEL), row_p(3 * BRANCH_W), row_s(3 * BRANCH_W), row_p(BRANCH_W), row_s(BRANCH_W)]
                 + [_const_spec(c.shape) for c in consts],
        out_specs=row(D_MODEL),
        out_shape=jax.ShapeDtypeStruct((rows, D_MODEL), F32),
        compiler_params=pltpu.CompilerParams(dimension_semantics=("arbitrary",),
                                             vmem_limit_bytes=VMEM_LIMIT),
        name="tail",
    )(h1, brl_p, brl_s, brd_p, brd_s, *consts)


def _mix_kernel(z_ref, gd_ref, cos_ref, sin_ref, cpast_ref, sr0_ref, sg0_ref, cw_ref, cb_ref, lng_ref, lnb_ref,
                rgn_ref, ggn_ref, br_ref, sr_out, sg_out, cs_out, sr_s, sg_s, u_s, *, chunk, n_chunks, seq_len):
    C = chunk
    c = pl.program_id(1)
    low = C >= 16
    cast = (lambda a: a.astype(BF16)) if low else (lambda a: a)

    t_col = lax.broadcasted_iota(jnp.int32, (C, 1), 0)
    s_row = lax.broadcasted_iota(jnp.int32, (1, C), 1)
    causal = t_col >= s_row
    tf = t_col.astype(F32)
    padded = n_chunks * C > seq_len
    if padded:
        valid = (c * C + t_col) < seq_len
        nvf = jnp.zeros((1, 1), F32) + jnp.minimum(seq_len - c * C, C).astype(F32)
        keep = lambda a: jnp.where(valid, a, 0.0)
    else:
        nvf = jnp.full((1, 1), float(C), F32)
        keep = lambda a: a

    @pl.when(c == 0)
    def _init():
        sr_s[...] = sr0_ref[0]
        sg_s[...] = sg0_ref[0]
        u_s[0:HIST - (CONV_K - 1), :] = jnp.zeros((HIST - (CONV_K - 1), CONV_C), F32)
        u_s[HIST - (CONV_K - 1):HIST, :] = cpast_ref[0]

    @pl.when(c > 0)
    def _shift():
        u_s[0:HIST, :] = u_s[C:C + HIST, :]

    u = z_ref[:, 0:256] * _sigmoid(z_ref[:, 256:512])
    u_s[HIST:HIST + C, :] = u
    acc = jnp.zeros((C, CONV_C), F32)
    for j in range(CONV_K):
        lo = HIST - (CONV_K - 1) + j
        acc = acc + cw_ref[j:j + 1, :] * u_s[lo:lo + C, :]
    conv = acc + cb_ref[...]
    xc = conv - jnp.mean(conv, axis=-1, keepdims=True)
    ln = xc * lax.rsqrt(jnp.mean(xc * xc, axis=-1, keepdims=True) + EPS) * lng_ref[...] + lnb_ref[...]
    br_ref[:, 0:256] = _silu(ln)

    lane = lax.broadcasted_iota(jnp.int32, (1, 256), 1)
    first_half = (lane % RET_DK) < (RET_DK // 2)
    cos = cos_ref[...]
    sin = sin_ref[...]

    def rope(a):
        swapped = jnp.where(first_half, pltpu.roll(a, 256 - RET_DK // 2, 1), pltpu.roll(a, RET_DK // 2, 1))
        return a * cos + swapped * sin

    rq = rope(z_ref[:, 512:768])
    rk = keep(rope(z_ref[:, 768:1024]) * RET_DK ** -0.5)
    rv = keep(z_ref[:, 1024:1280])
    dts = (t_col - s_row).astype(F32)
    for h in range(HEADS):
        lg = math.log1p(-(2.0 ** (-5 - h)))
        sl = slice(h * RET_DK, (h + 1) * RET_DK)
        qh, kh, vh = rq[:, sl], rk[:, sl], cast(rv[:, sl])
        decay = jnp.where(causal, jnp.exp(dts * lg), 0.0)
        a = _dot_nt(cast(qh), cast(kh)) * decay
        s_old = sr_s[h]
        o = _dot(cast(a), vh) + _dot(cast(qh * jnp.exp((tf + 1.0) * lg)), cast(s_old))
        sr_s[h] = jnp.exp(nvf * lg) * s_old + _dot_tn(cast(kh * jnp.exp((nvf - 1.0 - tf) * lg)), vh)
        oc = o - jnp.mean(o, axis=-1, keepdims=True)
        y = oc * lax.rsqrt(jnp.mean(oc * oc, axis=-1, keepdims=True) + EPS) * rgn_ref[:, sl]
        br_ref[:, 256 + h * RET_DV:256 + (h + 1) * RET_DV] = _silu(z_ref[:, 1280 + h * 64:1280 + (h + 1) * 64]) * y

    gq = z_ref[:, 1536:1664] * GLA_DK ** -0.5
    gk = keep(z_ref[:, 1664:1792])
    gv = keep(z_ref[:, 1792:2048])
    g = keep(gd_ref[...])
    tri = jnp.where(causal, 1.0, 0.0)
    if low:
        g_hi, g_lo = _split_bf16(g)
        tri = tri.astype(BF16)
        bcum = _dot(tri, g_hi) + _dot(tri, g_lo)
    else:
        bcum = _dot(tri, g)
    mid = C // 2 - 1
    b_mid = bcum[mid:mid + 1, :]
    b_last = bcum[C - 1:C, :]
    q_intra = gq * jnp.exp(bcum - b_mid)
    k_intra = gk * jnp.exp(b_mid - bcum)
    q_inter = gq * jnp.exp(bcum)
    k_state = gk * jnp.exp(b_last - bcum)
    e_last = jnp.where(t_col == C - 1, jnp.exp(bcum), 0.0)
    ones = jnp.ones((C, GLA_DV), BF16 if low else F32)
    if low:
        e_hi, e_lo = _split_bf16(e_last)
    for h in range(HEADS):
        sl = slice(h * GLA_DK, (h + 1) * GLA_DK)
        vs = slice(h * GLA_DV, (h + 1) * GLA_DV)
        vh = cast(gv[:, vs])
        a = jnp.where(causal, _dot_nt(cast(q_intra[:, sl]), cast(k_intra[:, sl])), 0.0)
        s_old = sg_s[h]
        o = _dot(cast(a), vh) + _dot(cast(q_inter[:, sl]), cast(s_old))
        if low:
            e_col = _dot_tn(e_hi[:, sl], ones) + _dot_tn(e_lo[:, sl], ones)
        else:
            e_col = _dot_tn(e_last[:, sl], ones)
        sg_s[h] = e_col * s_old + _dot_tn(cast(k_state[:, sl]), vh)
        y = o * lax.rsqrt(jnp.mean(o * o, axis=-1, keepdims=True) + EPS) * ggn_ref[:, vs]
        br_ref[:, 512 + h * GLA_DV:512 + (h + 1) * GLA_DV] = _silu(z_ref[:, 2048 + h * 64:2048 + (h + 1) * 64]) * y

    @pl.when(c == n_chunks - 1)
    def _final():
        sr_out[0] = sr_s[...]
        sg_out[0] = sg_s[...]
        n_last = seq_len - (n_chunks - 1) * C
        cs_out[0] = u_s[HIST + n_last - (CONV_K - 1):HIST + n_last, :]


def _mix_call(zlin, gdec, cos, sin, conv_past, ret0, gla0, w, *, n_seq, chunk, n_chunks, seq_len,
              row_block0, name):
    blk = lambda n: pl.BlockSpec((chunk, n), lambda s, c: (row_block0 + s * n_chunks + c, 0))
    out_blk = pl.BlockSpec((chunk, 3 * BRANCH_W), lambda s, c: (s * n_chunks + c, 0))
    per_seq = lambda shp: pl.BlockSpec((1,) + shp, lambda s, c: (s,) + (0,) * len(shp))
    const = lambda a: pl.BlockSpec(a.shape, lambda s, c: (0,) * a.ndim)
    consts = [w['cw'], w['cb'], w['lng'], w['lnb'], w['rgn'], w['ggn']]
    in_specs = ([blk(N_LIN), blk(LANE),
                 pl.BlockSpec((chunk, 256), lambda s, c: (c, 0)), pl.BlockSpec((chunk, 256), lambda s, c: (c, 0)),
                 per_seq((CONV_K - 1, CONV_C)), per_seq((HEADS, RET_DK, RET_DV)), per_seq((HEADS, GLA_DK, GLA_DV))]
                + [const(a) for a in consts])
    args = [zlin, gdec, cos, sin, conv_past, ret0, gla0] + consts
    return pl.pallas_call(
        functools.partial(_mix_kernel, chunk=chunk, n_chunks=n_chunks, seq_len=seq_len),
        grid=(n_seq, n_chunks),
        in_specs=in_specs,
        out_specs=[out_blk, per_seq((HEADS, RET_DK, RET_DV)), per_seq((HEADS, GLA_DK, GLA_DV)),
                   per_seq((CONV_K - 1, CONV_C))],
        out_shape=[jax.ShapeDtypeStruct((n_seq * n_chunks * chunk, 3 * BRANCH_W), F32),
                   jax.ShapeDtypeStruct((n_seq, HEADS, RET_DK, RET_DV), F32),
                   jax.ShapeDtypeStruct((n_seq, HEADS, GLA_DK, GLA_DV), F32),
                   jax.ShapeDtypeStruct((n_seq, CONV_K - 1, CONV_C), F32)],
        scratch_shapes=[pltpu.VMEM((HEADS, RET_DK, RET_DV), F32), pltpu.VMEM((HEADS, GLA_DK, GLA_DV), F32),
                        pltpu.VMEM((HIST + chunk, CONV_C), F32)],
        compiler_params=pltpu.CompilerParams(dimension_semantics=("arbitrary", "arbitrary")),
        name=name,
    )(*args)


def _t5_bucket_np(rel):
    rel = np.asarray(rel)
    n = np.maximum(rel, 0)
    max_exact = REL_BUCKETS // 2
    nf = np.maximum(n, 1).astype(np.float64)
    large = max_exact + (np.log(nf / max_exact) / math.log(REL_MAX_DIST / max_exact)
                         * (REL_BUCKETS - max_exact)).astype(np.int64)
    large = np.minimum(large, REL_BUCKETS - 1)
    return np.where(n < max_exact, n, large).astype(np.int32)


def _lambda(lamv_ref, lam_init):
    a = jnp.sum(lamv_ref[0:1, :] * lamv_ref[1:2, :], axis=-1, keepdims=True)
    b = jnp.sum(lamv_ref[2:3, :] * lamv_ref[3:4, :], axis=-1, keepdims=True)
    return jnp.exp(a) - jnp.exp(b) + lam_init


def _flash_kernel(bt_ref, rb_ref, top_ref, lamv_ref, sub_ref, qt_ref, k_ref, vt_ref, o_ref,
                  bias_s, qbd_s, m_s, l_s, acc_s, sta_s, stb_s, *, lam_init, n_blocks):
    QB = SEQ_BLOCK
    SUB = FAR_BLOCK // QB
    b = pl.program_id(0)
    i = pl.program_id(1)

    @pl.when((b == 0) & (i == 0))
    def _tables():
        for kind, src in ((1, 1), (2, 0)):
            bt = bt_ref[src]
            tiles = [jnp.zeros((QB, QB), F32) for _ in range(HEADS)]
            for bk in range(REL_BUCKETS):
                hit = bt == bk
                for h in range(HEADS):
                    tiles[h] = jnp.where(hit, rb_ref[bk, h] * LOG2E, tiles[h])
            for h in range(HEADS):
                bias_s[kind, h] = jnp.where(bt < 0, NEG, tiles[h])
        for h in range(HEADS):
            bias_s[0, h] = jnp.zeros((QB, QB), F32) + rb_ref[REL_BUCKETS - 1, h] * LOG2E
            bias_s[3, h] = jnp.full((QB, QB), NEG, F32)

    qt = qt_ref[...] * (DIFF_HD ** -0.5 * LOG2E)
    row = lax.broadcasted_iota(jnp.int32, (256, 1), 0)
    for hj in range(2 * HEADS):
        own = (row >= hj * DIFF_HD) & (row < (hj + 1) * DIFF_HD)
        qbd_s[:, hj * QB:(hj + 1) * QB] = jnp.where(own, qt, 0.0).astype(BF16)
    m_s[...] = jnp.full(m_s.shape, NEG, F32)
    l_s[...] = jnp.zeros(l_s.shape, F32)
    acc_s[...] = jnp.zeros(acc_s.shape, F32)

    def online(h, st, shift, vt):
        m_old = m_s[h]
        m_new = jnp.maximum(m_old, jnp.max(st, axis=0, keepdims=True) + shift)
        alpha = jnp.exp2(m_old - m_new)
        pt = jnp.exp2(st - (m_new - shift))
        l_s[h] = alpha * l_s[h] + jnp.sum(pt, axis=0, keepdims=True)
        m_s[h] = m_new
        acc_s[h] = alpha * acc_s[h] + _dot(vt[h * DIFF_VD:(h + 1) * DIFF_VD, :], pt.astype(BF16))

    def fixed(h, st, shift, vt):
        pt = jnp.exp2(st - (top_ref[h] - shift))
        l_s[h] = l_s[h] + jnp.sum(pt, axis=0, keepdims=True)
        acc_s[h] = acc_s[h] + _dot(vt[h * DIFF_VD:(h + 1) * DIFF_VD, :], pt.astype(BF16))

    n_far = jnp.maximum(i - 1, 0) // SUB

    def sweep(accumulate):
        def far_body(kb, carry):
            off = pl.multiple_of(kb * FAR_BLOCK, FAR_BLOCK)
            k = k_ref[pl.ds(off, FAR_BLOCK), :].astype(BF16)
            vt = vt_ref[:, pl.ds(off, FAR_BLOCK)].astype(BF16)
            for h in range(HEADS):
                st = _dot(k, qbd_s[:, 2 * h * QB:(2 * h + 2) * QB])
                accumulate(h, st, rb_ref[REL_BUCKETS - 1, h] * LOG2E, vt)
            return carry

        lax.fori_loop(0, n_far, far_body, 0)

        def near_body(kb, carry):
            first = kb * SUB
            blk0 = jnp.minimum(first, n_blocks - SUB)
            off = pl.multiple_of(blk0 * QB, QB)
            k = k_ref[pl.ds(off, FAR_BLOCK), :].astype(BF16)
            vt = vt_ref[:, pl.ds(off, FAR_BLOCK)].astype(BF16)
            kinds = []
            for c in range(SUB):
                blk = blk0 + c
                kind = jnp.where(blk == i, 2, jnp.where(blk == i - 1, 1, 0))
                kinds.append(jnp.where((blk > i) | (blk < first), 3, kind))
            for h in range(HEADS):
                st = _dot(k, qbd_s[:, 2 * h * QB:(2 * h + 2) * QB])
                parts = []
                for c in range(SUB):
                    tile = bias_s[kinds[c], h]
                    parts.append(st[c * QB:(c + 1) * QB, :] + jnp.concatenate([tile, tile], axis=1))
                accumulate(h, jnp.concatenate(parts, axis=0), 0.0, vt)
            return carry

        lax.fori_loop(n_far, i // SUB + 1, near_body, 0)

    bounded = top_ref[HEADS] > 0.5

    def window(j):
        first = j * SUB
        return first, jnp.minimum(first, n_blocks - SUB)

    def score(j, buf):
        _, blk0 = window(j)
        k = k_ref[pl.ds(pl.multiple_of(blk0 * QB, QB), FAR_BLOCK), :].astype(BF16)
        for h in range(HEADS):
            buf[h] = _dot(k, qbd_s[:, 2 * h * QB:(2 * h + 2) * QB])

    def consume(j, buf):
        first, blk0 = window(j)
        vt = vt_ref[:, pl.ds(pl.multiple_of(blk0 * QB, QB), FAR_BLOCK)].astype(BF16)
        kinds = []
        for c in range(SUB):
            blk = blk0 + c
            kind = jnp.where(blk == i, 2, jnp.where(blk == i - 1, 1, 0))
            kinds.append(jnp.where((blk > i) | (blk < first), 3, kind))
        for h in range(HEADS):
            parts = []
            for c in range(SUB):
                tile = bias_s[kinds[c], h]
                parts.append(buf[h, c * QB:(c + 1) * QB, :] + jnp.concatenate([tile, tile], axis=1))
            fixed(h, jnp.concatenate(parts, axis=0), 0.0, vt)

    @pl.when(bounded)
    def _fixed():
        n_win = i // SUB + 1
        score(0, sta_s)

        def pair(m, carry):
            score(2 * m + 1, stb_s)
            consume(2 * m, sta_s)
            score(2 * m + 2, sta_s)
            consume(2 * m + 1, stb_s)
            return carry

        lax.fori_loop(0, n_win // 2, pair, 0)

        @pl.when(n_win % 2 == 1)
        def _tail():
            consume(n_win - 1, sta_s)

    @pl.when(jnp.logical_not(bounded))
    def _online():
        sweep(online)

    lam = _lambda(lamv_ref, lam_init)
    outs = []
    for h in range(HEADS):
        ot = acc_s[h] / l_s[h]
        d = ot[:, 0:QB] - lam * ot[:, QB:2 * QB]
        y = d * lax.rsqrt(jnp.mean(d * d, axis=0, keepdims=True) + EPS) * sub_ref[...] * (1.0 - lam_init)
        outs.append(y)
    o_ref[...] = jnp.concatenate(outs, axis=0).T


def _score_top(rel_bias, q_gain, k_gain):
    reach = DIFF_HD ** 0.5 * jnp.max(jnp.abs(q_gain * k_gain))
    top = (reach + jnp.max(rel_bias, axis=0)) * LOG2E
    low = (-reach + jnp.min(rel_bias, axis=0)) * LOG2E
    ok = jnp.all(top - low < MAX_EXP2_SPAN).astype(F32)
    return jnp.concatenate([top, ok[None]]).astype(F32)


def _flash_call(bt, rel_bias, top, lamv, sub_col, dqt, dk, dvt, *, n_seq, seq_rows, lam_init):
    rows = n_seq * seq_rows
    nq = seq_rows // SEQ_BLOCK
    assert seq_rows >= FAR_BLOCK
    const = lambda a: pl.BlockSpec(a.shape, lambda b, i: (0,) * a.ndim)
    smem = pl.BlockSpec(memory_space=pltpu.SMEM)
    return pl.pallas_call(
        functools.partial(_flash_kernel, lam_init=lam_init, n_blocks=nq),
        grid=(n_seq, nq),
        in_specs=[const(bt), smem, smem, const(lamv), const(sub_col),
                  pl.BlockSpec((256, SEQ_BLOCK), lambda b, i: (0, b * nq + i)),
                  pl.BlockSpec((seq_rows, 256), lambda b, i: (b, 0)),
                  pl.BlockSpec((256, seq_rows), lambda b, i: (0, b))],
        out_specs=pl.BlockSpec((SEQ_BLOCK, 256), lambda b, i: (b * nq + i, 0)),
        out_shape=jax.ShapeDtypeStruct((rows, 256), F32),
        scratch_shapes=[pltpu.VMEM((4, HEADS, SEQ_BLOCK, SEQ_BLOCK), F32),
                        pltpu.VMEM((256, 2 * HEADS * SEQ_BLOCK), BF16),
                        pltpu.VMEM((HEADS, 1, 2 * SEQ_BLOCK), F32), pltpu.VMEM((HEADS, 1, 2 * SEQ_BLOCK), F32),
                        pltpu.VMEM((HEADS, DIFF_VD, 2 * SEQ_BLOCK), F32),
                        pltpu.VMEM((HEADS, FAR_BLOCK, 2 * SEQ_BLOCK), F32),
                        pltpu.VMEM((HEADS, FAR_BLOCK, 2 * SEQ_BLOCK), F32)],
        compiler_params=pltpu.CompilerParams(dimension_semantics=("arbitrary", "arbitrary"),
                                             vmem_limit_bytes=VMEM_LIMIT),
        name="attn_prompt",
    )(bt, rel_bias, top, lamv, sub_col, dqt, dk, dvt)


def _decode_kernel(pt_ref, bt_ref, rb_ref, lamv_ref, sub_ref, q_ref, kn_ref, vn_ref, *refs,
                   pages_per_step, n_steps, dec_seq, lam_init):
    G = pages_per_step
    kt_refs = refs[:G]
    vt_refs = refs[G:2 * G]
    o_ref, bias_pg_s, bias_new_s, far_s, qs_s, m_s, l_s, acc_s = refs[2 * G:]
    del pt_ref
    b = pl.program_id(0)
    g = pl.program_id(1)
    last = g == n_steps - 1
    NQ = 2 * HEADS * dec_seq
    seqs_per_block = kn_ref.shape[0] // dec_seq

    row_head = lax.broadcasted_iota(jnp.int32, (NQ, 1), 0) // (2 * dec_seq)

    def bias_rows(bk):
        out = jnp.zeros((NQ, 1), F32)
        for h in range(HEADS):
            out = jnp.where(row_head == h, rb_ref[bk, h], out)
        return out

    @pl.when((b == 0) & (g == 0))
    def _tables():
        for src, dst in ((0, bias_pg_s), (1, bias_new_s)):
            bt = bt_ref[src]
            tile = jnp.zeros(bt.shape, F32)
            for bk in range(REL_BUCKETS):
                tile = jnp.where(bt == bk, bias_rows(bk), tile)
            dst[...] = jnp.where(bt < 0, NEG, tile)
        far_s[...] = jnp.zeros(far_s.shape, F32) + bias_rows(REL_BUCKETS - 1)

    @pl.when(g == 0)
    def _init():
        q = q_ref[...] * DIFF_HD ** -0.5
        lane = lax.broadcasted_iota(jnp.int32, (1, 256), 1)
        for hj in range(2 * HEADS):
            own = (lane >= hj * DIFF_HD) & (lane < (hj + 1) * DIFF_HD)
            qs_s[hj * dec_seq:(hj + 1) * dec_seq, :] = jnp.where(own, q, 0.0)
        m_s[...] = jnp.full(m_s.shape, NEG, F32)
        l_s[...] = jnp.zeros(l_s.shape, F32)
        acc_s[...] = jnp.zeros(acc_s.shape, F32)

    def update(scores, pv_fn):
        m_old = m_s[...]
        s_max = scores[0]
        for s in scores[1:]:
            s_max = jnp.maximum(s_max, s)
        m_new = jnp.maximum(m_old, jnp.max(s_max, axis=1, keepdims=True))
        alpha = jnp.exp(m_old - m_new)
        probs = [jnp.exp(s - m_new) for s in scores]
        p_sum = probs[0]
        for p in probs[1:]:
            p_sum = p_sum + p
        l_s[...] = alpha * l_s[...] + jnp.sum(p_sum, axis=1, keepdims=True)
        m_s[...] = m_new
        acc_s[...] = jnp.concatenate([alpha, alpha], axis=1) * acc_s[...] + pv_fn(probs)

    qs = qs_s[...].astype(BF16)
    scores = []
    for p in range(G):
        s = _dot(qs, kt_refs[p][...].astype(BF16))
        if p == G - 1:
            s = s + jnp.where(last, bias_pg_s[...], far_s[...])
        else:
            s = s + far_s[...]
        scores.append(s)

    def pv_pages(probs):
        out = None
        for p in range(G):
            t = _dot_nt(probs[p].astype(BF16), vt_refs[p][...].astype(BF16))
            out = t if out is None else out + t
        return out

    update(scores, pv_pages)

    @pl.when(last)
    def _finish():
        key_seq = lax.broadcasted_iota(jnp.int32, (1, kn_ref.shape[0]), 1) // dec_seq
        s_new = _dot_nt(qs, kn_ref[...].astype(BF16))
        s_new = jnp.where(key_seq == b % seqs_per_block, s_new + bias_new_s[...], NEG)
        update([s_new], lambda probs: _dot(probs[0].astype(BF16), vn_ref[...].astype(BF16)))
        l_all = l_s[...]
        o_all = acc_s[...] / jnp.concatenate([l_all, l_all], axis=1)
        lam = _lambda(lamv_ref, lam_init)
        for h in range(HEADS):
            r1 = (2 * h) * dec_seq
            r2 = (2 * h + 1) * dec_seq
            cols = slice(h * DIFF_VD, (h + 1) * DIFF_VD)
            d = o_all[r1:r1 + dec_seq, cols] - lam * o_all[r2:r2 + dec_seq, cols]
            y = d * lax.rsqrt(jnp.mean(d * d, axis=-1, keepdims=True) + EPS) * sub_ref[...] * (1.0 - lam_init)
            o_ref[:, cols] = y


def _decode_call(page_table, bt, rel_bias, lamv, sub, dq, dk, dv, cache_kt, cache_vt, *,
                 layer, n_seq, dec_seq, row0, lam_init):
    n_pages = page_table.shape[1]
    page = cache_kt.shape[3]
    G = min(PAGES_PER_STEP, n_pages)
    n_steps = n_pages // G
    nq = 2 * HEADS * dec_seq
    new_rows = LANE
    per_blk = new_rows // dec_seq
    const = lambda a: pl.BlockSpec(a.shape, lambda b, g, pt: (0,) * a.ndim)
    own = pl.BlockSpec((dec_seq, 256), lambda b, g, pt: (row0 // dec_seq + b, 0))
    new = pl.BlockSpec((new_rows, 256), lambda b, g, pt: (row0 // new_rows + b // per_blk, 0))

    def page_spec(p):
        return pl.BlockSpec((None, None, 256, page),
                            lambda b, g, pt: (layer, pt[b * n_pages + g * G + p], 0, 0))

    in_specs = ([const(bt), pl.BlockSpec(memory_space=pltpu.SMEM), const(lamv), const(sub), own, new, new]
                + [page_spec(p) for p in range(G)] + [page_spec(p) for p in range(G)])
    args = [bt, rel_bias, lamv, sub, dq, dk, dv] + [cache_kt] * G + [cache_vt] * G
    return pl.pallas_call(
        functools.partial(_decode_kernel, pages_per_step=G, n_steps=n_steps, dec_seq=dec_seq, lam_init=lam_init),
        grid_spec=pltpu.PrefetchScalarGridSpec(
            num_scalar_prefetch=1,
            grid=(n_seq, n_steps),
            in_specs=in_specs,
            out_specs=pl.BlockSpec((dec_seq, 256), lambda b, g, pt: (b, 0)),
            scratch_shapes=[pltpu.VMEM((nq, page), F32), pltpu.VMEM((nq, new_rows), F32),
                            pltpu.VMEM((nq, page), F32), pltpu.VMEM((nq, 256), F32),
                            pltpu.VMEM((nq, LANE), F32), pltpu.VMEM((nq, LANE), F32), pltpu.VMEM((nq, 256), F32)]),
        out_shape=jax.ShapeDtypeStruct((n_seq * dec_seq, 256), F32),
        compiler_params=pltpu.CompilerParams(dimension_semantics=("arbitrary", "arbitrary"),
                                             vmem_limit_bytes=VMEM_LIMIT),
        name="attn_sample",
    )(page_table.reshape(-1), *args)


def _rope_tables(pos):
    half = RET_DK // 2
    inv = ROPE_BASE ** (-np.arange(half, dtype=np.float64) / half)
    ang = np.asarray(pos, np.float64)[:, None] * inv[None, :]
    cos = np.concatenate([np.cos(ang), np.cos(ang)], axis=1)
    sin = np.concatenate([-np.sin(ang), np.sin(ang)], axis=1)
    return (jnp.asarray(np.tile(cos, (1, HEADS)), F32), jnp.asarray(np.tile(sin, (1, HEADS)), F32))


def _prompt_bucket_tiles():
    t = np.arange(SEQ_BLOCK)
    rel = t[None, :] - t[:, None]
    diag = np.where(rel >= 0, _t5_bucket_np(rel), -1)
    sub = _t5_bucket_np(rel + SEQ_BLOCK)
    return jnp.asarray(np.stack([diag, sub]).astype(np.int32))


def _sample_bucket_tiles(page, dec_seq):
    assert page == LANE
    iq = (np.arange(2 * HEADS * dec_seq) % dec_seq)[:, None]
    past = _t5_bucket_np(page + iq - np.arange(page)[None, :])
    rel_new = iq - (np.arange(LANE) % dec_seq)[None, :]
    new = np.where(rel_new >= 0, _t5_bucket_np(rel_new), -1)
    return jnp.asarray(np.stack([past, new]).astype(np.int32))


def kernel(x_prompt, x_sample, cache_k, cache_v, page_table, state_ret, state_gla, state_conv, meta_tokens,
           rel_bias, norm_ffn1, ffn1_gate, ffn1_up, ffn1_down, norm_mix, w_in, b_in, conv_w, conv_b, conv_ln_g,
           conv_ln_b, ret_gn, gla_alpha_w, gla_alpha_b, gla_gn, q_norm, k_norm, lam_q1, lam_k1, lam_q2, lam_k2,
           diff_subln, w_branch, w_out, norm_ffn2, ffn2_gate, ffn2_up, ffn2_down):
    B, S, D = x_prompt.shape
    DB, DS, _ = x_sample.shape
    depth = w_in.shape[0]
    L = S + N_META
    Lp = -(-L // SEQ_BLOCK) * SEQ_BLOCK
    n_chunks = Lp // SEQ_BLOCK
    n_pool, page = cache_k.shape[1], cache_k.shape[2]
    past_len = page_table.shape[1] * page
    rows_p = B * Lp
    rows = rows_p + DB * DS
    assert rows % ROW_TILE == 0 and rows_p % LANE == 0 and (DB * DS) % LANE == 0 and DS % 8 == 0

    pieces = []
    for b in range(B):
        pieces += [meta_tokens.astype(F32), x_prompt[b], jnp.zeros((Lp - L, D), F32)]
    h = jnp.concatenate(pieces + [x_sample.reshape(DB * DS, D)], axis=0)

    cos_p, sin_p = _rope_tables(np.arange(Lp))
    cos_s, sin_s = _rope_tables(past_len + np.arange(DS))
    bt_prompt = _prompt_bucket_tiles()
    bt_sample = _sample_bucket_tiles(page, DS)
    ckt = jnp.transpose(cache_k, (0, 1, 3, 4, 2)).reshape(depth, n_pool, HEADS * 2 * DIFF_HD, page)
    cvt = jnp.transpose(cache_v, (0, 1, 3, 4, 2)).reshape(depth, n_pool, HEADS * DIFF_VD, page)
    zeros_conv = jnp.zeros((B, CONV_K - 1, CONV_C), F32)
    zeros_ret = jnp.zeros((B, HEADS, RET_DK, RET_DV), F32)
    zeros_gla = jnp.zeros((B, HEADS, GLA_DK, GLA_DV), F32)

    outs = {k: [] for k in ('kp', 'vp', 'ks', 'vs', 'rp', 'rs', 'gp', 'gs', 'cp', 'cs')}
    row2 = lambda a: a.reshape(1, -1).astype(F32)
    for l in range(depth):
        lam_init = 0.8 - 0.6 * math.exp(-0.3 * l)
        wi, bi = w_in[l], b_in[l]
        w = dict(
            g1=row2(norm_ffn1[l]), wg1=ffn1_gate[l].astype(BF16), wu1=ffn1_up[l].astype(BF16),
            wd1=ffn1_down[l].astype(BF16), gm=row2(norm_mix[l]),
            wlin=wi[:, :N_LIN].astype(BF16), blin=row2(bi[:N_LIN]),
            wlow=wi[:, OFF_GLOW:OFF_DIFF].astype(BF16), blow=row2(bi[OFF_GLOW:OFF_DIFF]),
            aw=gla_alpha_w[l].astype(BF16), ab=row2(gla_alpha_b[l]),
            wdf=wi[:, OFF_DIFF:OFF_GATES].astype(BF16), bdf=row2(bi[OFF_DIFF:OFF_GATES]),
            qg=row2(jnp.tile(q_norm[l], 2 * HEADS)), kg=row2(jnp.tile(k_norm[l], 2 * HEADS)),
            wgt=wi[:, OFF_GATES:].astype(BF16), bgt=row2(bi[OFF_GATES:]),
            wb=w_branch[l].astype(BF16), wo=w_out[l].astype(BF16), g2=row2(norm_ffn2[l]),
            wg2=ffn2_gate[l].astype(BF16), wu2=ffn2_up[l].astype(BF16), wd2=ffn2_down[l].astype(BF16),
            cw=conv_w[l], cb=row2(conv_b[l]), lng=row2(conv_ln_g[l]), lnb=row2(conv_ln_b[l]),
            rgn=row2(ret_gn[l]), ggn=row2(gla_gn[l]),
        )
        lamv = jnp.stack([lam_q1[l], lam_k1[l], lam_q2[l], lam_k2[l]]).astype(F32)
        sub = row2(diff_subln[l])

        h1, zlin, gdec, dq, dk, dv, dqt, dkt, dvt = _head_call(h, w)

        brl_p, rp, gp, cp = _mix_call(zlin, gdec, cos_p, sin_p, zeros_conv, zeros_ret, zeros_gla, w,
                                      n_seq=B, chunk=SEQ_BLOCK, n_chunks=n_chunks, seq_len=L, row_block0=0,
                                      name="mix_prompt")
        brl_s, rs, gs, cs = _mix_call(zlin, gdec, cos_s, sin_s, state_conv[l], state_ret[l], state_gla[l], w,
                                      n_seq=DB, chunk=DS, n_chunks=1, seq_len=DS, row_block0=rows_p // DS,
                                      name="mix_sample")
        top = _score_top(rel_bias, q_norm[l], k_norm[l])
        brd_p = _flash_call(bt_prompt, rel_bias, top, lamv, sub.reshape(-1, 1), dqt, dk, dvt,
                            n_seq=B, seq_rows=Lp, lam_init=lam_init)
        brd_s = _decode_call(page_table, bt_sample, rel_bias, lamv, sub, dq, dk, dv, ckt, cvt,
                             layer=l, n_seq=DB, dec_seq=DS, row0=rows_p, lam_init=lam_init)

        h = _tail_call(h1, brl_p, brl_s, brd_p, brd_s, w)

        seq_major = lambda t: jnp.transpose(t[:, :rows_p].reshape(HEADS, -1, B, Lp)[..., :L], (2, 3, 0, 1))
        outs['kp'].append(seq_major(dkt))
        outs['vp'].append(seq_major(dvt))
        outs['ks'].append(dk[rows_p:].reshape(DB, DS, HEADS, 2 * DIFF_HD))
        outs['vs'].append(dv[rows_p:].reshape(DB, DS, HEADS, DIFF_VD))
        outs['rp'].append(rp); outs['rs'].append(rs)
        outs['gp'].append(gp); outs['gs'].append(gs)
        outs['cp'].append(cp); outs['cs'].append(cs)

    y_prompt = jnp.stack([h[b * Lp + N_META:b * Lp + L] for b in range(B)], axis=0)
    y_sample = h[rows_p:].reshape(DB, DS, D)
    st = lambda k: jnp.stack(outs[k], axis=0)
    return (y_prompt, y_sample, st('kp'), st('vp'), st('ks'), st('vs'),
            st('rp'), st('rs'), st('gp'), st('gs'), st('cp'), st('cs'))
```

```python
import functools
import math

import numpy as np
import jax
import jax.numpy as jnp
from jax import lax
from jax.experimental import pallas as pl
from jax.experimental.pallas import tpu as pltpu

F32 = jnp.float32
BF16 = jnp.bfloat16

D_MODEL = 1024
N_META = 16
N_BRANCH = 4
BRANCH_W = 256
D_FF = 2816
CONV_C = 256
CONV_K = 31
HEADS = 4
RET_DK = 64
RET_DV = 64
GLA_DK = 32
GLA_DV = 64
GLA_RANK = 16
GLA_TAU = 16.0
DIFF_HD = 32
DIFF_VD = 64
REL_BUCKETS = 32
REL_MAX_DIST = 128
ROPE_BASE = 10000.0
EPS = 1e-6
NEG = -1e30
LOG2E = math.log2(math.e)
MAX_EXP2_SPAN = 100.0

N_LIN = 2304
OFF_GLOW = N_LIN
OFF_DIFF = OFF_GLOW + GLA_RANK
OFF_GATES = OFF_DIFF + 3 * 256
N_IN = OFF_GATES + N_BRANCH * D_MODEL

LANE = 128
ROW_TILE = 384
FFN_CHUNKS = 2
MXU_N = 256
SEQ_BLOCK = 128
FAR_BLOCK = 512
HIST = 32
VMEM_LIMIT = 60 * 1024 * 1024
PAGES_PER_STEP = 32


def _sigmoid(x):
    return 1.0 / (1.0 + jnp.exp(-x))


def _silu(x):
    return x * _sigmoid(x)


def _log_sigmoid(x):
    return jnp.minimum(x, 0.0) - jnp.log1p(jnp.exp(-jnp.abs(x)))


def _rms(x, g):
    return x * lax.rsqrt(jnp.mean(x * x, axis=-1, keepdims=True) + EPS) * g


def _dot(a, b):
    return jnp.dot(a, b, preferred_element_type=F32)


def _dot_nt(a, b):
    return lax.dot_general(a, b, (((1,), (1,)), ((), ())), preferred_element_type=F32)


def _dot_tn(a, b):
    return lax.dot_general(a, b, (((0,), (0,)), ((), ())), preferred_element_type=F32)


def _split_bf16(x):
    hi = x.astype(BF16)
    lo = (x - hi.astype(F32)).astype(BF16)
    return hi, lo


def _group_avg(n, group, low=True):
    r = lax.broadcasted_iota(jnp.int32, (n, n), 0) // group
    c = lax.broadcasted_iota(jnp.int32, (n, n), 1) // group
    return jnp.where(r == c, 1.0 / group, 0.0).astype(BF16 if low else F32)


def _group_mean(x, avg, low=True):
    if not low:
        return _dot(x, avg)
    hi, lo = _split_bf16(x)
    return _dot(hi, avg) + _dot(lo, avg)


def _const_spec(shape):
    nd = len(shape)
    return pl.BlockSpec(shape, lambda *_: (0,) * nd, pipeline_mode=pl.Buffered(1))


def _ffn(x, wg_ref, wu_ref, wd_ref):
    tiles = D_FF // MXU_N
    edges = [(tiles * c // FFN_CHUNKS) * MXU_N for c in range(FFN_CHUNKS + 1)]
    out = None
    for c in range(FFN_CHUNKS):
        cols = slice(edges[c], edges[c + 1])
        act = (_silu(_dot(x, wg_ref[:, cols])) * _dot(x, wu_ref[:, cols])).astype(BF16)
        part = _dot(act, wd_ref[cols, :])
        out = part if out is None else out + part
    return out


def _head_kernel(h_ref, g1_ref, wg_ref, wu_ref, wd_ref, gm_ref, wlin_ref, blin_ref, wlow_ref, blow_ref,
                 aw_ref, ab_ref, wdf_ref, bdf_ref, qg_ref, kg_ref,
                 h1_ref, zlin_ref, gdec_ref, dq_ref, dk_ref, dv_ref, qt_ref, kt_ref, vt_ref, *, prompt_tiles):
    h = h_ref[...]
    h1 = h + 0.5 * _ffn(_rms(h, g1_ref[...]).astype(BF16), wg_ref, wu_ref, wd_ref)
    h1_ref[...] = h1
    x = _rms(h1, gm_ref[...]).astype(BF16)
    zlin_ref[...] = _dot(x, wlin_ref[...]) + blin_ref[...]
    g_low = _dot(x, wlow_ref[...]) + blow_ref[...]
    g_pre = _dot(g_low.astype(BF16), aw_ref[...]) + ab_ref[...]
    gdec_ref[...] = _log_sigmoid(g_pre) * (1.0 / GLA_TAU)
    zd = _dot(x, wdf_ref[...]) + bdf_ref[...]
    d_q = zd[:, 0:256]
    d_k = zd[:, 256:512]
    d_v = zd[:, 512:768]
    avg = _group_avg(256, DIFF_HD)
    q_n = d_q * lax.rsqrt(_group_mean(d_q * d_q, avg) + EPS) * qg_ref[...]
    dq_ref[...] = q_n
    k_n = d_k * lax.rsqrt(_group_mean(d_k * d_k, avg) + EPS) * kg_ref[...]
    dk_ref[...] = k_n
    dv_ref[...] = d_v

    @pl.when(pl.program_id(0) < prompt_tiles)
    def _transposed():
        qt_ref[...] = q_n.T
        kt_ref[...] = k_n.T
        vt_ref[...] = d_v.T


def _head_call(h, w, *, n_seq, seq_rows):
    rows = h.shape[0]
    per_seq = seq_rows // ROW_TILE
    p_tiles = n_seq * per_seq
    row = lambda n: pl.BlockSpec((ROW_TILE, n), lambda i: (i, 0))

    def col_map(i):
        t = jnp.minimum(i, p_tiles - 1)
        return (t // per_seq, 0, t % per_seq)

    col = pl.BlockSpec((None, 256, ROW_TILE), col_map)
    consts = [w['g1'], w['wg1'], w['wu1'], w['wd1'], w['gm'], w['wlin'], w['blin'], w['wlow'], w['blow'],
              w['aw'], w['ab'], w['wdf'], w['bdf'], w['qg'], w['kg']]
    widths = (D_MODEL, N_LIN, LANE, 256, 256, 256)
    return pl.pallas_call(
        functools.partial(_head_kernel, prompt_tiles=p_tiles),
        grid=(rows // ROW_TILE,),
        in_specs=[row(D_MODEL)] + [_const_spec(c.shape) for c in consts],
        out_specs=[row(n) for n in widths] + [col, col, col],
        out_shape=([jax.ShapeDtypeStruct((rows, n), F32) for n in widths]
                   + [jax.ShapeDtypeStruct((n_seq, 256, seq_rows), F32)] * 3),
        compiler_params=pltpu.CompilerParams(dimension_semantics=("arbitrary",),
                                             vmem_limit_bytes=VMEM_LIMIT),
        name="head",
    )(h, *consts)


def _tail_kernel(h1_ref, brlp_ref, brls_ref, brdp_ref, brds_ref, gm_ref, wgt_ref, bgt_ref, wb_ref, wo_ref, g2_ref,
                 wg_ref, wu_ref, wd_ref, out_ref, *, prompt_tiles):
    h1 = h1_ref[...]
    x = _rms(h1, gm_ref[...]).astype(BF16)
    is_sample = pl.program_id(0) >= prompt_tiles
    merged = None
    for n in range(N_BRANCH):
        if n < 3:
            cols = slice(n * BRANCH_W, (n + 1) * BRANCH_W)
            br = jnp.where(is_sample, brls_ref[:, cols], brlp_ref[:, cols])
        else:
            br = jnp.where(is_sample, brds_ref[...], brdp_ref[...])
        gate = _dot(x, wgt_ref[:, n * D_MODEL:(n + 1) * D_MODEL]) + bgt_ref[:, n * D_MODEL:(n + 1) * D_MODEL]
        term = _dot(br.astype(BF16), wb_ref[n]) * _sigmoid(gate)
        merged = term if merged is None else merged + term
    h2 = h1 + _dot(merged.astype(BF16), wo_ref[...])
    out_ref[...] = h2 + 0.5 * _ffn(_rms(h2, g2_ref[...]).astype(BF16), wg_ref, wu_ref, wd_ref)


def _tail_call(h1, brl_p, brl_s, brd_p, brd_s, w):
    rows = h1.shape[0]
    p_tiles = brl_p.shape[0] // ROW_TILE
    row = lambda n: pl.BlockSpec((ROW_TILE, n), lambda i: (i, 0))
    row_p = lambda n: pl.BlockSpec((ROW_TILE, n), lambda i: (jnp.minimum(i, p_tiles - 1), 0))
    row_s = lambda n: pl.BlockSpec((ROW_TILE, n), lambda i: (jnp.maximum(i - p_tiles, 0), 0))
    consts = [w['gm'], w['wgt'], w['bgt'], w['wb'], w['wo'], w['g2'], w['wg2'], w['wu2'], w['wd2']]
    return pl.pallas_call(
        functools.partial(_tail_kernel, prompt_tiles=p_tiles),
        grid=(rows // ROW_TILE,),
        in_specs=[row(D_MODEL), row_p(3 * BRANCH_W), row_s(3 * BRANCH_W), row_p(BRANCH_W), row_s(BRANCH_W)]
                 + [_const_spec(c.shape) for c in consts],
        out_specs=row(D_MODEL),
        out_shape=jax.ShapeDtypeStruct((rows, D_MODEL), F32),
        compiler_params=pltpu.CompilerParams(dimension_semantics=("arbitrary",),
                                             vmem_limit_bytes=VMEM_LIMIT),
        name="tail",
    )(h1, brl_p, brl_s, brd_p, brd_s, *consts)


def _mix_kernel(z_ref, gd_ref, cos_ref, sin_ref, cpast_ref, sr0_ref, sg0_ref, cw_ref, cb_ref, lng_ref, lnb_ref,
                rgn_ref, ggn_ref, br_ref, sr_out, sg_out, cs_out, sr_s, sg_s, u_s, ush_s,
                *, chunk, n_chunks, seq_len):
    C = chunk
    c = pl.program_id(1)
    low = C >= 16
    cast = (lambda a: a.astype(BF16)) if low else (lambda a: a)

    t_col = lax.broadcasted_iota(jnp.int32, (C, 1), 0)
    s_row = lax.broadcasted_iota(jnp.int32, (1, C), 1)
    causal = t_col >= s_row
    tf = t_col.astype(F32)
    padded = n_chunks * C > seq_len
    if padded:
        valid = (c * C + t_col) < seq_len
        nvf = jnp.zeros((1, 1), F32) + jnp.minimum(seq_len - c * C, C).astype(F32)
        keep = lambda a: jnp.where(valid, a, 0.0)
    else:
        nvf = jnp.full((1, 1), float(C), F32)
        keep = lambda a: a

    @pl.when(c == 0)
    def _init():
        sr_s[...] = jnp.zeros(sr_s.shape, F32)
        sg_s[...] = jnp.zeros(sg_s.shape, F32)
        for h in range(HEADS):
            sr_s[h * RET_DK:(h + 1) * RET_DK, h * RET_DV:(h + 1) * RET_DV] = sr0_ref[0, h]
            sg_s[h * GLA_DK:(h + 1) * GLA_DK, h * GLA_DV:(h + 1) * GLA_DV] = sg0_ref[0, h]
        u_s[0:HIST - (CONV_K - 1), :] = jnp.zeros((HIST - (CONV_K - 1), CONV_C), F32)
        u_s[HIST - (CONV_K - 1):HIST, :] = cpast_ref[0]

    @pl.when(c > 0)
    def _shift():
        u_s[0:HIST, :] = u_s[C:C + HIST, :]

    u = z_ref[:, 0:256] * _sigmoid(z_ref[:, 256:512])
    u_s[HIST:HIST + C, :] = u
    span = C + HIST - 8
    for s in range(1, 8):
        ush_s[s - 1] = u_s[s:s + span, :]
    acc = jnp.zeros((C, CONV_C), F32)
    for j in range(CONV_K):
        lo = HIST - (CONV_K - 1) + j
        q, r = lo - lo % 8, lo % 8
        tap = u_s[q:q + C, :] if r == 0 else ush_s[r - 1, q:q + C, :]
        acc = acc + cw_ref[j:j + 1, :] * tap
    conv = acc + cb_ref[...]
    xc = conv - jnp.mean(conv, axis=-1, keepdims=True)
    ln = xc * lax.rsqrt(jnp.mean(xc * xc, axis=-1, keepdims=True) + EPS) * lng_ref[...] + lnb_ref[...]
    br_ref[:, 0:256] = _silu(ln)

    lane = lax.broadcasted_iota(jnp.int32, (1, 256), 1)
    first_half = (lane % RET_DK) < (RET_DK // 2)
    cos = cos_ref[...]
    sin = sin_ref[...]

    def rope(a):
        swapped = jnp.where(first_half, pltpu.roll(a, 256 - RET_DK // 2, 1), pltpu.roll(a, RET_DK // 2, 1))
        return a * cos + swapped * sin

    rq = rope(z_ref[:, 512:768])
    rk = keep(rope(z_ref[:, 768:1024]) * RET_DK ** -0.5)
    rv = keep(z_ref[:, 1024:1280])
    lane_head = lane // RET_DV
    row_head = lax.broadcasted_iota(jnp.int32, (HEADS * RET_DK, 1), 0) // RET_DK
    log_gamma = [math.log1p(-(2.0 ** (-5 - h))) for h in range(HEADS)]
    lg_lane = jnp.zeros((1, 256), F32)
    lg_row = jnp.zeros((HEADS * RET_DK, 1), F32)
    for h in range(HEADS):
        lg_lane = jnp.where(lane_head == h, log_gamma[h], lg_lane)
        lg_row = jnp.where(row_head == h, log_gamma[h], lg_row)
    avg = _group_avg(256, RET_DV, low)
    dts = (t_col - s_row).astype(F32)
    rk_c = cast(rk)
    o = _dot(cast(rq * jnp.exp((tf + 1.0) * lg_lane)), cast(sr_s[...]))
    for h in range(HEADS):
        own = lane_head == h
        decay = jnp.where(causal, jnp.exp(dts * log_gamma[h]), 0.0)
        a = _dot_nt(cast(jnp.where(own, rq, 0.0)), rk_c) * decay
        o = o + _dot(cast(a), cast(jnp.where(own, rv, 0.0)))
    update = _dot_tn(cast(rk * jnp.exp((nvf - 1.0 - tf) * lg_lane)), cast(rv))
    sr_s[...] = jnp.exp(nvf * lg_row) * sr_s[...] + jnp.where(row_head == lane_head, update, 0.0)
    oc = o - _group_mean(o, avg, low)
    y = oc * lax.rsqrt(_group_mean(oc * oc, avg, low) + EPS) * rgn_ref[...]
    br_ref[:, 256:512] = _silu(z_ref[:, 1280:1536]) * y

    gq = z_ref[:, 1536:1664] * GLA_DK ** -0.5
    gk = keep(z_ref[:, 1664:1792])
    gv = keep(z_ref[:, 1792:2048])
    g = keep(gd_ref[...])
    tri = jnp.where(causal, 1.0, 0.0)
    if low:
        g_hi, g_lo = _split_bf16(g)
        tri = tri.astype(BF16)
        bcum = _dot(tri, g_hi) + _dot(tri, g_lo)
    else:
        bcum = _dot(tri, g)
    mid = C // 2 - 1
    b_mid = bcum[mid:mid + 1, :]
    b_last = bcum[C - 1:C, :]
    q_intra = gq * jnp.exp(bcum - b_mid)
    k_intra = gk * jnp.exp(b_mid - bcum)
    q_inter = gq * jnp.exp(bcum)
    k_state = gk * jnp.exp(b_last - bcum)
    e_last = jnp.where(t_col == C - 1, jnp.exp(bcum), 0.0)
    ones = jnp.ones((C, HEADS * GLA_DV), BF16 if low else F32)
    if low:
        e_hi, e_lo = _split_bf16(e_last)
        e_rows = _dot_tn(e_hi, ones) + _dot_tn(e_lo, ones)
    else:
        e_rows = _dot_tn(e_last, ones)
    key_head = lax.broadcasted_iota(jnp.int32, (1, HEADS * GLA_DK), 1) // GLA_DK
    krow_head = lax.broadcasted_iota(jnp.int32, (HEADS * GLA_DK, 1), 0) // GLA_DK
    k_c = cast(k_intra)
    o = _dot(cast(q_inter), cast(sg_s[...]))
    for h in range(HEADS):
        a = jnp.where(causal, _dot_nt(cast(jnp.where(key_head == h, q_intra, 0.0)), k_c), 0.0)
        o = o + _dot(cast(a), cast(jnp.where(lane_head == h, gv, 0.0)))
    update = _dot_tn(cast(k_state), cast(gv))
    sg_s[...] = e_rows * sg_s[...] + jnp.where(krow_head == lane_head, update, 0.0)
    y = o * lax.rsqrt(_group_mean(o * o, avg, low) + EPS) * ggn_ref[...]
    br_ref[:, 512:768] = _silu(z_ref[:, 2048:2304]) * y

    @pl.when(c == n_chunks - 1)
    def _final():
        for h in range(HEADS):
            sr_out[0, h] = sr_s[h * RET_DK:(h + 1) * RET_DK, h * RET_DV:(h + 1) * RET_DV]
            sg_out[0, h] = sg_s[h * GLA_DK:(h + 1) * GLA_DK, h * GLA_DV:(h + 1) * GLA_DV]
        n_last = seq_len - (n_chunks - 1) * C
        cs_out[0] = u_s[HIST + n_last - (CONV_K - 1):HIST + n_last, :]


def _mix_call(zlin, gdec, cos, sin, conv_past, ret0, gla0, w, *, n_seq, chunk, n_chunks, seq_len,
              row_block0, name):
    blk = lambda n: pl.BlockSpec((chunk, n), lambda s, c: (row_block0 + s * n_chunks + c, 0))
    out_blk = pl.BlockSpec((chunk, 3 * BRANCH_W), lambda s, c: (s * n_chunks + c, 0))
    per_seq = lambda shp: pl.BlockSpec((1,) + shp, lambda s, c: (s,) + (0,) * len(shp))
    const = lambda a: pl.BlockSpec(a.shape, lambda s, c: (0,) * a.ndim)
    consts = [w['cw'], w['cb'], w['lng'], w['lnb'], w['rgn'], w['ggn']]
    in_specs = ([blk(N_LIN), blk(LANE),
                 pl.BlockSpec((chunk, 256), lambda s, c: (c, 0)), pl.BlockSpec((chunk, 256), lambda s, c: (c, 0)),
                 per_seq((CONV_K - 1, CONV_C)), per_seq((HEADS, RET_DK, RET_DV)), per_seq((HEADS, GLA_DK, GLA_DV))]
                + [const(a) for a in consts])
    args = [zlin, gdec, cos, sin, conv_past, ret0, gla0] + consts
    return pl.pallas_call(
        functools.partial(_mix_kernel, chunk=chunk, n_chunks=n_chunks, seq_len=seq_len),
        grid=(n_seq, n_chunks),
        in_specs=in_specs,
        out_specs=[out_blk, per_seq((HEADS, RET_DK, RET_DV)), per_seq((HEADS, GLA_DK, GLA_DV)),
                   per_seq((CONV_K - 1, CONV_C))],
        out_shape=[jax.ShapeDtypeStruct((n_seq * n_chunks * chunk, 3 * BRANCH_W), F32),
                   jax.ShapeDtypeStruct((n_seq, HEADS, RET_DK, RET_DV), F32),
                   jax.ShapeDtypeStruct((n_seq, HEADS, GLA_DK, GLA_DV), F32),
                   jax.ShapeDtypeStruct((n_seq, CONV_K - 1, CONV_C), F32)],
        scratch_shapes=[pltpu.VMEM((HEADS * RET_DK, HEADS * RET_DV), F32),
                        pltpu.VMEM((HEADS * GLA_DK, HEADS * GLA_DV), F32),
                        pltpu.VMEM((HIST + chunk, CONV_C), F32),
                        pltpu.VMEM((7, HIST + chunk - 8, CONV_C), F32)],
        compiler_params=pltpu.CompilerParams(dimension_semantics=("arbitrary", "arbitrary")),
        name=name,
    )(*args)


def _t5_bucket_np(rel):
    rel = np.asarray(rel)
    n = np.maximum(rel, 0)
    max_exact = REL_BUCKETS // 2
    nf = np.maximum(n, 1).astype(np.float64)
    large = max_exact + (np.log(nf / max_exact) / math.log(REL_MAX_DIST / max_exact)
                         * (REL_BUCKETS - max_exact)).astype(np.int64)
    large = np.minimum(large, REL_BUCKETS - 1)
    return np.where(n < max_exact, n, large).astype(np.int32)


def _lambda(lamv_ref, lam_init):
    a = jnp.sum(lamv_ref[0:1, :] * lamv_ref[1:2, :], axis=-1, keepdims=True)
    b = jnp.sum(lamv_ref[2:3, :] * lamv_ref[3:4, :], axis=-1, keepdims=True)
    return jnp.exp(a) - jnp.exp(b) + lam_init


def _flash_kernel(bt_ref, rb_ref, top_ref, lamv_ref, sub_ref, qt_ref, k_ref, vt_ref, o_ref,
                  bias_s, qbd_s, m_s, l_s, acc_s, sta_s, stb_s, *, lam_init, n_blocks):
    QB = SEQ_BLOCK
    SUB = FAR_BLOCK // QB
    b = pl.program_id(0)
    i = pl.program_id(1)

    @pl.when((b == 0) & (i == 0))
    def _tables():
        for kind, src in ((1, 1), (2, 0)):
            bt = bt_ref[src]
            tiles = [jnp.zeros((QB, QB), F32) for _ in range(HEADS)]
            for bk in range(REL_BUCKETS):
                hit = bt == bk
                for h in range(HEADS):
                    tiles[h] = jnp.where(hit, rb_ref[bk, h] * LOG2E, tiles[h])
            for h in range(HEADS):
                bias_s[kind, h] = jnp.where(bt < 0, NEG, tiles[h])
        for h in range(HEADS):
            bias_s[0, h] = jnp.zeros((QB, QB), F32) + rb_ref[REL_BUCKETS - 1, h] * LOG2E
            bias_s[3, h] = jnp.full((QB, QB), NEG, F32)

    qt = qt_ref[...] * (DIFF_HD ** -0.5 * LOG2E)
    row = lax.broadcasted_iota(jnp.int32, (256, 1), 0)
    for hj in range(2 * HEADS):
        own = (row >= hj * DIFF_HD) & (row < (hj + 1) * DIFF_HD)
        qbd_s[:, hj * QB:(hj + 1) * QB] = jnp.where(own, qt, 0.0).astype(BF16)
    m_s[...] = jnp.full(m_s.shape, NEG, F32)
    l_s[...] = jnp.zeros(l_s.shape, F32)
    acc_s[...] = jnp.zeros(acc_s.shape, F32)

    def online(h, st, shift, vt):
        m_old = m_s[h]
        m_new = jnp.maximum(m_old, jnp.max(st, axis=0, keepdims=True) + shift)
        alpha = jnp.exp2(m_old - m_new)
        pt = jnp.exp2(st - (m_new - shift))
        l_s[h] = alpha * l_s[h] + jnp.sum(pt, axis=0, keepdims=True)
        m_s[h] = m_new
        acc_s[h] = alpha * acc_s[h] + _dot(vt[h * DIFF_VD:(h + 1) * DIFF_VD, :], pt.astype(BF16))

    def fixed(h, st, shift, vt):
        pt = jnp.exp2(st - (top_ref[h] - shift))
        l_s[h] = l_s[h] + jnp.sum(pt, axis=0, keepdims=True)
        acc_s[h] = acc_s[h] + _dot(vt[h * DIFF_VD:(h + 1) * DIFF_VD, :], pt.astype(BF16))

    n_far = jnp.maximum(i - 1, 0) // SUB

    def sweep(accumulate):
        def far_body(kb, carry):
            off = pl.multiple_of(kb * FAR_BLOCK, FAR_BLOCK)
            k = k_ref[pl.ds(off, FAR_BLOCK), :].astype(BF16)
            vt = vt_ref[:, pl.ds(off, FAR_BLOCK)].astype(BF16)
            for h in range(HEADS):
                st = _dot(k, qbd_s[:, 2 * h * QB:(2 * h + 2) * QB])
                accumulate(h, st, rb_ref[REL_BUCKETS - 1, h] * LOG2E, vt)
            return carry

        lax.fori_loop(0, n_far, far_body, 0)

        def near_body(kb, carry):
            first = kb * SUB
            blk0 = jnp.minimum(first, n_blocks - SUB)
            off = pl.multiple_of(blk0 * QB, QB)
            k = k_ref[pl.ds(off, FAR_BLOCK), :].astype(BF16)
            vt = vt_ref[:, pl.ds(off, FAR_BLOCK)].astype(BF16)
            kinds = []
            for c in range(SUB):
                blk = blk0 + c
                kind = jnp.where(blk == i, 2, jnp.where(blk == i - 1, 1, 0))
                kinds.append(jnp.where((blk > i) | (blk < first), 3, kind))
            for h in range(HEADS):
                st = _dot(k, qbd_s[:, 2 * h * QB:(2 * h + 2) * QB])
                parts = []
                for c in range(SUB):
                    tile = bias_s[kinds[c], h]
                    parts.append(st[c * QB:(c + 1) * QB, :] + jnp.concatenate([tile, tile], axis=1))
                accumulate(h, jnp.concatenate(parts, axis=0), 0.0, vt)
            return carry

        lax.fori_loop(n_far, i // SUB + 1, near_body, 0)

    bounded = top_ref[HEADS] > 0.5

    def window(j):
        first = j * SUB
        return first, jnp.minimum(first, n_blocks - SUB)

    def score(j, buf):
        _, blk0 = window(j)
        k = k_ref[pl.ds(pl.multiple_of(blk0 * QB, QB), FAR_BLOCK), :].astype(BF16)
        for h in range(HEADS):
            buf[h] = _dot(k, qbd_s[:, 2 * h * QB:(2 * h + 2) * QB])

    def consume(j, buf):
        first, blk0 = window(j)
        vt = vt_ref[:, pl.ds(pl.multiple_of(blk0 * QB, QB), FAR_BLOCK)].astype(BF16)
        kinds = []
        for c in range(SUB):
            blk = blk0 + c
            kind = jnp.where(blk == i, 2, jnp.where(blk == i - 1, 1, 0))
            kinds.append(jnp.where((blk > i) | (blk < first), 3, kind))
        for h in range(HEADS):
            parts = []
            for c in range(SUB):
                tile = bias_s[kinds[c], h]
                parts.append(buf[h, c * QB:(c + 1) * QB, :] + jnp.concatenate([tile, tile], axis=1))
            fixed(h, jnp.concatenate(parts, axis=0), 0.0, vt)

    @pl.when(bounded)
    def _fixed():
        n_win = i // SUB + 1
        score(0, sta_s)

        def pair(m, carry):
            score(2 * m + 1, stb_s)
            consume(2 * m, sta_s)
            score(2 * m + 2, sta_s)
            consume(2 * m + 1, stb_s)
            return carry

        lax.fori_loop(0, n_win // 2, pair, 0)

        @pl.when(n_win % 2 == 1)
        def _tail():
            consume(n_win - 1, sta_s)

    @pl.when(jnp.logical_not(bounded))
    def _online():
        sweep(online)

    lam = _lambda(lamv_ref, lam_init)
    outs = []
    for h in range(HEADS):
        ot = acc_s[h] / l_s[h]
        d = ot[:, 0:QB] - lam * ot[:, QB:2 * QB]
        y = d * lax.rsqrt(jnp.mean(d * d, axis=0, keepdims=True) + EPS) * sub_ref[...] * (1.0 - lam_init)
        outs.append(y)
    o_ref[...] = jnp.concatenate(outs, axis=0).T


def _score_top(rel_bias, q_gain, k_gain):
    reach = DIFF_HD ** 0.5 * jnp.max(jnp.abs(q_gain * k_gain))
    top = (reach + jnp.max(rel_bias, axis=0)) * LOG2E
    low = (-reach + jnp.min(rel_bias, axis=0)) * LOG2E
    ok = jnp.all(top - low < MAX_EXP2_SPAN).astype(F32)
    return jnp.concatenate([top, ok[None]]).astype(F32)


def _flash_call(bt, rel_bias, top, lamv, sub_col, dqt, dk, dvt, *, n_seq, seq_rows, lam_init):
    rows = n_seq * seq_rows
    nq = seq_rows // SEQ_BLOCK
    assert seq_rows >= FAR_BLOCK
    const = lambda a: pl.BlockSpec(a.shape, lambda b, i: (0,) * a.ndim)
    smem = pl.BlockSpec(memory_space=pltpu.SMEM)
    return pl.pallas_call(
        functools.partial(_flash_kernel, lam_init=lam_init, n_blocks=nq),
        grid=(n_seq, nq),
        in_specs=[const(bt), smem, smem, const(lamv), const(sub_col),
                  pl.BlockSpec((None, 256, SEQ_BLOCK), lambda b, i: (b, 0, i)),
                  pl.BlockSpec((seq_rows, 256), lambda b, i: (b, 0)),
                  pl.BlockSpec((None, 256, seq_rows), lambda b, i: (b, 0, 0))],
        out_specs=pl.BlockSpec((SEQ_BLOCK, 256), lambda b, i: (b * nq + i, 0)),
        out_shape=jax.ShapeDtypeStruct((rows, 256), F32),
        scratch_shapes=[pltpu.VMEM((4, HEADS, SEQ_BLOCK, SEQ_BLOCK), F32),
                        pltpu.VMEM((256, 2 * HEADS * SEQ_BLOCK), BF16),
                        pltpu.VMEM((HEADS, 1, 2 * SEQ_BLOCK), F32), pltpu.VMEM((HEADS, 1, 2 * SEQ_BLOCK), F32),
                        pltpu.VMEM((HEADS, DIFF_VD, 2 * SEQ_BLOCK), F32),
                        pltpu.VMEM((HEADS, FAR_BLOCK, 2 * SEQ_BLOCK), F32),
                        pltpu.VMEM((HEADS, FAR_BLOCK, 2 * SEQ_BLOCK), F32)],
        compiler_params=pltpu.CompilerParams(dimension_semantics=("arbitrary", "arbitrary"),
                                             vmem_limit_bytes=VMEM_LIMIT),
        name="attn_prompt",
    )(bt, rel_bias, top, lamv, sub_col, dqt, dk, dvt)


def _decode_kernel(pt_ref, bt_ref, rb_ref, lamv_ref, sub_ref, q_ref, kn_ref, vn_ref, *refs,
                   pages_per_step, n_steps, dec_seq, lam_init):
    G = pages_per_step
    kt_refs = refs[:G]
    vt_refs = refs[G:2 * G]
    o_ref, bias_pg_s, bias_new_s, far_s, qs_s, m_s, l_s, acc_s = refs[2 * G:]
    del pt_ref
    b = pl.program_id(0)
    g = pl.program_id(1)
    last = g == n_steps - 1
    NQ = 2 * HEADS * dec_seq
    seqs_per_block = kn_ref.shape[0] // dec_seq

    row_head = lax.broadcasted_iota(jnp.int32, (NQ, 1), 0) // (2 * dec_seq)

    def bias_rows(bk):
        out = jnp.zeros((NQ, 1), F32)
        for h in range(HEADS):
            out = jnp.where(row_head == h, rb_ref[bk, h], out)
        return out

    @pl.when((b == 0) & (g == 0))
    def _tables():
        for src, dst in ((0, bias_pg_s), (1, bias_new_s)):
            bt = bt_ref[src]
            tile = jnp.zeros(bt.shape, F32)
            for bk in range(REL_BUCKETS):
                tile = jnp.where(bt == bk, bias_rows(bk), tile)
            dst[...] = jnp.where(bt < 0, NEG, tile)
        far_s[...] = jnp.zeros(far_s.shape, F32) + bias_rows(REL_BUCKETS - 1)

    @pl.when(g == 0)
    def _init():
        q = q_ref[...] * DIFF_HD ** -0.5
        lane = lax.broadcasted_iota(jnp.int32, (1, 256), 1)
        for hj in range(2 * HEADS):
            own = (lane >= hj * DIFF_HD) & (lane < (hj + 1) * DIFF_HD)
            qs_s[hj * dec_seq:(hj + 1) * dec_seq, :] = jnp.where(own, q, 0.0)
        m_s[...] = jnp.full(m_s.shape, NEG, F32)
        l_s[...] = jnp.zeros(l_s.shape, F32)
        acc_s[...] = jnp.zeros(acc_s.shape, F32)

    def update(scores, pv_fn):
        m_old = m_s[...]
        s_max = scores[0]
        for s in scores[1:]:
            s_max = jnp.maximum(s_max, s)
        m_new = jnp.maximum(m_old, jnp.max(s_max, axis=1, keepdims=True))
        alpha = jnp.exp(m_old - m_new)
        probs = [jnp.exp(s - m_new) for s in scores]
        p_sum = probs[0]
        for p in probs[1:]:
            p_sum = p_sum + p
        l_s[...] = alpha * l_s[...] + jnp.sum(p_sum, axis=1, keepdims=True)
        m_s[...] = m_new
        acc_s[...] = jnp.concatenate([alpha, alpha], axis=1) * acc_s[...] + pv_fn(probs)

    qs = qs_s[...].astype(BF16)
    scores = []
    for p in range(G):
        s = _dot(qs, kt_refs[p][...].astype(BF16))
        if p == G - 1:
            s = s + jnp.where(last, bias_pg_s[...], far_s[...])
        else:
            s = s + far_s[...]
        scores.append(s)

    def pv_pages(probs):
        out = None
        for p in range(G):
            t = _dot_nt(probs[p].astype(BF16), vt_refs[p][...].astype(BF16))
            out = t if out is None else out + t
        return out

    update(scores, pv_pages)

    @pl.when(last)
    def _finish():
        key_seq = lax.broadcasted_iota(jnp.int32, (1, kn_ref.shape[0]), 1) // dec_seq
        s_new = _dot_nt(qs, kn_ref[...].astype(BF16))
        s_new = jnp.where(key_seq == b % seqs_per_block, s_new + bias_new_s[...], NEG)
        update([s_new], lambda probs: _dot(probs[0].astype(BF16), vn_ref[...].astype(BF16)))
        l_all = l_s[...]
        o_all = acc_s[...] / jnp.concatenate([l_all, l_all], axis=1)
        lam = _lambda(lamv_ref, lam_init)
        for h in range(HEADS):
            r1 = (2 * h) * dec_seq
            r2 = (2 * h + 1) * dec_seq
            cols = slice(h * DIFF_VD, (h + 1) * DIFF_VD)
            d = o_all[r1:r1 + dec_seq, cols] - lam * o_all[r2:r2 + dec_seq, cols]
            y = d * lax.rsqrt(jnp.mean(d * d, axis=-1, keepdims=True) + EPS) * sub_ref[...] * (1.0 - lam_init)
            o_ref[:, cols] = y


def _decode_call(page_table, bt, rel_bias, lamv, sub, dq, dk, dv, cache_kt, cache_vt, *,
                 layer, n_seq, dec_seq, row0, lam_init):
    n_pages = page_table.shape[1]
    page = cache_kt.shape[3]
    G = min(PAGES_PER_STEP, n_pages)
    n_steps = n_pages // G
    nq = 2 * HEADS * dec_seq
    new_rows = LANE
    per_blk = new_rows // dec_seq
    const = lambda a: pl.BlockSpec(a.shape, lambda b, g, pt: (0,) * a.ndim)
    own = pl.BlockSpec((dec_seq, 256), lambda b, g, pt: (row0 // dec_seq + b, 0))
    new = pl.BlockSpec((new_rows, 256), lambda b, g, pt: (row0 // new_rows + b // per_blk, 0))

    def page_spec(p):
        return pl.BlockSpec((None, None, 256, page),
                            lambda b, g, pt: (layer, pt[b * n_pages + g * G + p], 0, 0))

    in_specs = ([const(bt), pl.BlockSpec(memory_space=pltpu.SMEM), const(lamv), const(sub), own, new, new]
                + [page_spec(p) for p in range(G)] + [page_spec(p) for p in range(G)])
    args = [bt, rel_bias, lamv, sub, dq, dk, dv] + [cache_kt] * G + [cache_vt] * G
    return pl.pallas_call(
        functools.partial(_decode_kernel, pages_per_step=G, n_steps=n_steps, dec_seq=dec_seq, lam_init=lam_init),
        grid_spec=pltpu.PrefetchScalarGridSpec(
            num_scalar_prefetch=1,
            grid=(n_seq, n_steps),
            in_specs=in_specs,
            out_specs=pl.BlockSpec((dec_seq, 256), lambda b, g, pt: (b, 0)),
            scratch_shapes=[pltpu.VMEM((nq, page), F32), pltpu.VMEM((nq, new_rows), F32),
                            pltpu.VMEM((nq, page), F32), pltpu.VMEM((nq, 256), F32),
                            pltpu.VMEM((nq, LANE), F32), pltpu.VMEM((nq, LANE), F32), pltpu.VMEM((nq, 256), F32)]),
        out_shape=jax.ShapeDtypeStruct((n_seq * dec_seq, 256), F32),
        compiler_params=pltpu.CompilerParams(dimension_semantics=("arbitrary", "arbitrary"),
                                             vmem_limit_bytes=VMEM_LIMIT),
        name="attn_sample",
    )(page_table.reshape(-1), *args)


def _rope_tables(pos):
    half = RET_DK // 2
    inv = ROPE_BASE ** (-np.arange(half, dtype=np.float64) / half)
    ang = np.asarray(pos, np.float64)[:, None] * inv[None, :]
    cos = np.concatenate([np.cos(ang), np.cos(ang)], axis=1)
    sin = np.concatenate([-np.sin(ang), np.sin(ang)], axis=1)
    return (jnp.asarray(np.tile(cos, (1, HEADS)), F32), jnp.asarray(np.tile(sin, (1, HEADS)), F32))


def _prompt_bucket_tiles():
    t = np.arange(SEQ_BLOCK)
    rel = t[None, :] - t[:, None]
    diag = np.where(rel >= 0, _t5_bucket_np(rel), -1)
    sub = _t5_bucket_np(rel + SEQ_BLOCK)
    return jnp.asarray(np.stack([diag, sub]).astype(np.int32))


def _sample_bucket_tiles(page, dec_seq):
    assert page == LANE
    iq = (np.arange(2 * HEADS * dec_seq) % dec_seq)[:, None]
    past = _t5_bucket_np(page + iq - np.arange(page)[None, :])
    rel_new = iq - (np.arange(LANE) % dec_seq)[None, :]
    new = np.where(rel_new >= 0, _t5_bucket_np(rel_new), -1)
    return jnp.asarray(np.stack([past, new]).astype(np.int32))


def kernel(x_prompt, x_sample, cache_k, cache_v, page_table, state_ret, state_gla, state_conv, meta_tokens,
           rel_bias, norm_ffn1, ffn1_gate, ffn1_up, ffn1_down, norm_mix, w_in, b_in, conv_w, conv_b, conv_ln_g,
           conv_ln_b, ret_gn, gla_alpha_w, gla_alpha_b, gla_gn, q_norm, k_norm, lam_q1, lam_k1, lam_q2, lam_k2,
           diff_subln, w_branch, w_out, norm_ffn2, ffn2_gate, ffn2_up, ffn2_down):
    B, S, D = x_prompt.shape
    DB, DS, _ = x_sample.shape
    depth = w_in.shape[0]
    L = S + N_META
    Lp = -(-L // SEQ_BLOCK) * SEQ_BLOCK
    n_chunks = Lp // SEQ_BLOCK
    n_pool, page = cache_k.shape[1], cache_k.shape[2]
    past_len = page_table.shape[1] * page
    rows_p = B * Lp
    rows_s = DB * DS
    rows_sp = -(-rows_s // ROW_TILE) * ROW_TILE
    assert Lp % ROW_TILE == 0 and rows_s % LANE == 0 and DS % 8 == 0

    pieces = []
    for b in range(B):
        pieces += [meta_tokens.astype(F32), x_prompt[b], jnp.zeros((Lp - L, D), F32)]
    pieces += [x_sample.reshape(rows_s, D), jnp.zeros((rows_sp - rows_s, D), F32)]
    h = jnp.concatenate(pieces, axis=0)
    pad_s = lambda a: jnp.pad(a, ((0, rows_sp - rows_s), (0, 0)))

    cos_p, sin_p = _rope_tables(np.arange(Lp))
    cos_s, sin_s = _rope_tables(past_len + np.arange(DS))
    bt_prompt = _prompt_bucket_tiles()
    bt_sample = _sample_bucket_tiles(page, DS)
    ckt = jnp.transpose(cache_k, (0, 1, 3, 4, 2)).reshape(depth, n_pool, HEADS * 2 * DIFF_HD, page)
    cvt = jnp.transpose(cache_v, (0, 1, 3, 4, 2)).reshape(depth, n_pool, HEADS * DIFF_VD, page)
    zeros_conv = jnp.zeros((B, CONV_K - 1, CONV_C), F32)
    zeros_ret = jnp.zeros((B, HEADS, RET_DK, RET_DV), F32)
    zeros_gla = jnp.zeros((B, HEADS, GLA_DK, GLA_DV), F32)

    outs = {k: [] for k in ('kp', 'vp', 'ks', 'vs', 'rp', 'rs', 'gp', 'gs', 'cp', 'cs')}
    row2 = lambda a: a.reshape(1, -1).astype(F32)
    for l in range(depth):
        lam_init = 0.8 - 0.6 * math.exp(-0.3 * l)
        wi, bi = w_in[l], b_in[l]
        w = dict(
            g1=row2(norm_ffn1[l]), wg1=ffn1_gate[l].astype(BF16), wu1=ffn1_up[l].astype(BF16),
            wd1=ffn1_down[l].astype(BF16), gm=row2(norm_mix[l]),
            wlin=wi[:, :N_LIN].astype(BF16), blin=row2(bi[:N_LIN]),
            wlow=wi[:, OFF_GLOW:OFF_DIFF].astype(BF16), blow=row2(bi[OFF_GLOW:OFF_DIFF]),
            aw=gla_alpha_w[l].astype(BF16), ab=row2(gla_alpha_b[l]),
            wdf=wi[:, OFF_DIFF:OFF_GATES].astype(BF16), bdf=row2(bi[OFF_DIFF:OFF_GATES]),
            qg=row2(jnp.tile(q_norm[l], 2 * HEADS)), kg=row2(jnp.tile(k_norm[l], 2 * HEADS)),
            wgt=wi[:, OFF_GATES:].astype(BF16), bgt=row2(bi[OFF_GATES:]),
            wb=w_branch[l].astype(BF16), wo=w_out[l].astype(BF16), g2=row2(norm_ffn2[l]),
            wg2=ffn2_gate[l].astype(BF16), wu2=ffn2_up[l].astype(BF16), wd2=ffn2_down[l].astype(BF16),
            cw=conv_w[l], cb=row2(conv_b[l]), lng=row2(conv_ln_g[l]), lnb=row2(conv_ln_b[l]),
            rgn=row2(ret_gn[l]), ggn=row2(gla_gn[l]),
        )
        lamv = jnp.stack([lam_q1[l], lam_k1[l], lam_q2[l], lam_k2[l]]).astype(F32)
        sub = row2(diff_subln[l])

        h1, zlin, gdec, dq, dk, dv, qt, kt, vt = _head_call(h, w, n_seq=B, seq_rows=Lp)

        brl_p, rp, gp, cp = _mix_call(zlin, gdec, cos_p, sin_p, zeros_conv, zeros_ret, zeros_gla, w,
                                      n_seq=B, chunk=SEQ_BLOCK, n_chunks=n_chunks, seq_len=L, row_block0=0,
                                      name="mix_prompt")
        brl_s, rs, gs, cs = _mix_call(zlin, gdec, cos_s, sin_s, state_conv[l], state_ret[l], state_gla[l], w,
                                      n_seq=DB, chunk=DS, n_chunks=1, seq_len=DS, row_block0=rows_p // DS,
                                      name="mix_sample")
        top = _score_top(rel_bias, q_norm[l], k_norm[l])
        brd_p = _flash_call(bt_prompt, rel_bias, top, lamv, sub.reshape(-1, 1), qt, dk, vt,
                            n_seq=B, seq_rows=Lp, lam_init=lam_init)
        brd_s = _decode_call(page_table, bt_sample, rel_bias, lamv, sub, dq, dk, dv, ckt, cvt,
                             layer=l, n_seq=DB, dec_seq=DS, row0=rows_p, lam_init=lam_init)

        h = _tail_call(h1, brl_p, pad_s(brl_s), brd_p, pad_s(brd_s), w)

        seq_major = lambda t: jnp.transpose(t.reshape(B, HEADS, -1, Lp)[..., :L], (0, 3, 1, 2))
        outs['kp'].append(seq_major(kt))
        outs['vp'].append(seq_major(vt))
        outs['ks'].append(dk[rows_p:rows_p + rows_s].reshape(DB, DS, HEADS, 2 * DIFF_HD))
        outs['vs'].append(dv[rows_p:rows_p + rows_s].reshape(DB, DS, HEADS, DIFF_VD))
        outs['rp'].append(rp); outs['rs'].append(rs)
        outs['gp'].append(gp); outs['gs'].append(gs)
        outs['cp'].append(cp); outs['cs'].append(cs)

    y_prompt = jnp.stack([h[b * Lp + N_META:b * Lp + L] for b in range(B)], axis=0)
    y_sample = h[rows_p:rows_p + rows_s].reshape(DB, DS, D)
    st = lambda k: jnp.stack(outs[k], axis=0)
    return (y_prompt, y_sample, st('kp'), st('vp'), st('ks'), st('vs'),
            st('rp'), st('rs'), st('gp'), st('gs'), st('cp'), st('cs'))
```

```python
import functools
import math

import numpy as np
import jax
import jax.numpy as jnp
from jax import lax
from jax.experimental import pallas as pl
from jax.experimental.pallas import tpu as pltpu

F32 = jnp.float32
BF16 = jnp.bfloat16

D_MODEL = 1024
N_META = 16
N_BRANCH = 4
BRANCH_W = 256
D_FF = 2816
CONV_C = 256
CONV_K = 31
HEADS = 4
RET_DK = 64
RET_DV = 64
GLA_DK = 32
GLA_DV = 64
GLA_RANK = 16
GLA_TAU = 16.0
DIFF_HD = 32
DIFF_VD = 64
REL_BUCKETS = 32
REL_MAX_DIST = 128
ROPE_BASE = 10000.0
EPS = 1e-6
NEG = -1e30
LOG2E = math.log2(math.e)
MAX_EXP2_SPAN = 100.0

N_LIN = 2304
OFF_GLOW = N_LIN
OFF_DIFF = OFF_GLOW + GLA_RANK
OFF_GATES = OFF_DIFF + 3 * 256
N_IN = OFF_GATES + N_BRANCH * D_MODEL

LANE = 128
ROW_TILE = 384
FFN_CHUNKS = 2
MXU_N = 256
SEQ_BLOCK = 128
FAR_BLOCK = 512
HIST = 32
VMEM_LIMIT = 60 * 1024 * 1024
PAGES_PER_STEP = 32


def _sigmoid(x):
    return 1.0 / (1.0 + jnp.exp(-x))


def _silu(x):
    return x * _sigmoid(x)


def _log_sigmoid(x):
    return jnp.minimum(x, 0.0) - jnp.log1p(jnp.exp(-jnp.abs(x)))


def _rms(x, g):
    return x * lax.rsqrt(jnp.mean(x * x, axis=-1, keepdims=True) + EPS) * g


def _dot(a, b):
    return jnp.dot(a, b, preferred_element_type=F32)


def _dot_nt(a, b):
    return lax.dot_general(a, b, (((1,), (1,)), ((), ())), preferred_element_type=F32)


def _dot_tn(a, b):
    return lax.dot_general(a, b, (((0,), (0,)), ((), ())), preferred_element_type=F32)


def _split_bf16(x):
    hi = x.astype(BF16)
    lo = (x - hi.astype(F32)).astype(BF16)
    return hi, lo


def _group_avg(n, group, low=True):
    r = lax.broadcasted_iota(jnp.int32, (n, n), 0) // group
    c = lax.broadcasted_iota(jnp.int32, (n, n), 1) // group
    return jnp.where(r == c, 1.0 / group, 0.0).astype(BF16 if low else F32)


def _group_mean(x, avg, low=True):
    if not low:
        return _dot(x, avg)
    hi, lo = _split_bf16(x)
    return _dot(hi, avg) + _dot(lo, avg)


def _layer_spec(a, layer, single_buffer=False):
    nd = a.ndim - 1
    kw = dict(pipeline_mode=pl.Buffered(1)) if single_buffer else {}
    return pl.BlockSpec((None,) + a.shape[1:], lambda *_: (layer,) + (0,) * nd, **kw)


def _whole_spec(a):
    return pl.BlockSpec(a.shape, lambda *_: (0,) * a.ndim)


def _ffn(x, wg_ref, wu_ref, wd_ref):
    tiles = D_FF // MXU_N
    edges = [(tiles * c // FFN_CHUNKS) * MXU_N for c in range(FFN_CHUNKS + 1)]
    out = None
    for c in range(FFN_CHUNKS):
        cols = slice(edges[c], edges[c + 1])
        act = (_silu(_dot(x, wg_ref[:, cols])) * _dot(x, wu_ref[:, cols])).astype(BF16)
        part = _dot(act, wd_ref[cols, :])
        out = part if out is None else out + part
    return out


def _head_kernel(h_ref, g1_ref, wg_ref, wu_ref, wd_ref, gm_ref, wlin_ref, blin_ref, wlow_ref, blow_ref,
                 aw_ref, ab_ref, wdf_ref, bdf_ref, qg_ref, kg_ref,
                 h1_ref, zlin_ref, gdec_ref, dq_ref, dk_ref, dv_ref, qt_ref, kt_ref, vt_ref, *, prompt_tiles):
    h = h_ref[...]
    h1 = h + 0.5 * _ffn(_rms(h, g1_ref[...]).astype(BF16), wg_ref, wu_ref, wd_ref)
    h1_ref[...] = h1
    x = _rms(h1, gm_ref[...]).astype(BF16)
    zlin_ref[...] = _dot(x, wlin_ref[...]) + blin_ref[...]
    g_low = _dot(x, wlow_ref[...]) + blow_ref[...]
    g_pre = _dot(g_low.astype(BF16), aw_ref[...]) + ab_ref[...]
    gdec_ref[...] = _log_sigmoid(g_pre) * (1.0 / GLA_TAU)
    zd = _dot(x, wdf_ref[...]) + bdf_ref[...]
    d_q = zd[:, 0:256]
    d_k = zd[:, 256:512]
    d_v = zd[:, 512:768]
    avg = _group_avg(256, DIFF_HD)
    q_n = d_q * lax.rsqrt(_group_mean(d_q * d_q, avg) + EPS) * qg_ref[...]
    dq_ref[...] = q_n
    k_n = d_k * lax.rsqrt(_group_mean(d_k * d_k, avg) + EPS) * kg_ref[...]
    dk_ref[...] = k_n
    dv_ref[...] = d_v

    @pl.when(pl.program_id(0) < prompt_tiles)
    def _transposed():
        qt_ref[...] = q_n.T
        kt_ref[...] = k_n.T
        vt_ref[...] = d_v.T


def _head_call(h, w, *, layer, n_seq, seq_rows):
    rows = h.shape[0]
    per_seq = seq_rows // ROW_TILE
    p_tiles = n_seq * per_seq
    row = lambda n: pl.BlockSpec((ROW_TILE, n), lambda i: (i, 0))

    def col_map(i):
        t = jnp.minimum(i, p_tiles - 1)
        return (t // per_seq, 0, t % per_seq)

    col = pl.BlockSpec((None, 256, ROW_TILE), col_map)
    consts = [w['g1'], w['wg1'], w['wu1'], w['wd1'], w['gm'], w['wlin'], w['blin'], w['wlow'], w['blow'],
              w['aw'], w['ab'], w['wdf'], w['bdf'], w['qg'], w['kg']]
    widths = (D_MODEL, N_LIN, LANE, 256, 256, 256)
    return pl.pallas_call(
        functools.partial(_head_kernel, prompt_tiles=p_tiles),
        grid=(rows // ROW_TILE,),
        in_specs=[row(D_MODEL)] + [_layer_spec(c, layer, single_buffer=True) for c in consts],
        out_specs=[row(n) for n in widths] + [col, col, col],
        out_shape=([jax.ShapeDtypeStruct((rows, n), F32) for n in widths]
                   + [jax.ShapeDtypeStruct((n_seq, 256, seq_rows), F32)] * 3),
        compiler_params=pltpu.CompilerParams(dimension_semantics=("arbitrary",),
                                             vmem_limit_bytes=VMEM_LIMIT),
        name="head",
    )(h, *consts)


def _tail_kernel(h1_ref, brlp_ref, brls_ref, brdp_ref, brds_ref, gm_ref, wgt_ref, bgt_ref, wb_ref, wo_ref, g2_ref,
                 wg_ref, wu_ref, wd_ref, out_ref, *, prompt_tiles):
    h1 = h1_ref[...]
    x = _rms(h1, gm_ref[...]).astype(BF16)
    is_sample = pl.program_id(0) >= prompt_tiles
    merged = None
    for n in range(N_BRANCH):
        if n < 3:
            cols = slice(n * BRANCH_W, (n + 1) * BRANCH_W)
            br = jnp.where(is_sample, brls_ref[:, cols], brlp_ref[:, cols])
        else:
            br = jnp.where(is_sample, brds_ref[...], brdp_ref[...])
        gate = _dot(x, wgt_ref[:, n * D_MODEL:(n + 1) * D_MODEL]) + bgt_ref[:, n * D_MODEL:(n + 1) * D_MODEL]
        term = _dot(br.astype(BF16), wb_ref[n]) * _sigmoid(gate)
        merged = term if merged is None else merged + term
    h2 = h1 + _dot(merged.astype(BF16), wo_ref[...])
    out_ref[...] = h2 + 0.5 * _ffn(_rms(h2, g2_ref[...]).astype(BF16), wg_ref, wu_ref, wd_ref)


def _tail_call(h1, brl_p, brl_s, brd_p, brd_s, w, *, layer):
    rows = h1.shape[0]
    p_tiles = brl_p.shape[0] // ROW_TILE
    row = lambda n: pl.BlockSpec((ROW_TILE, n), lambda i: (i, 0))
    row_p = lambda n: pl.BlockSpec((ROW_TILE, n), lambda i: (jnp.minimum(i, p_tiles - 1), 0))
    row_s = lambda n: pl.BlockSpec((ROW_TILE, n), lambda i: (jnp.maximum(i - p_tiles, 0), 0))
    consts = [w['gm'], w['wgt'], w['bgt'], w['wb'], w['wo'], w['g2'], w['wg2'], w['wu2'], w['wd2']]
    return pl.pallas_call(
        functools.partial(_tail_kernel, prompt_tiles=p_tiles),
        grid=(rows // ROW_TILE,),
        in_specs=[row(D_MODEL), row_p(3 * BRANCH_W), row_s(3 * BRANCH_W), row_p(BRANCH_W), row_s(BRANCH_W)]
                 + [_layer_spec(c, layer, single_buffer=True) for c in consts],
        out_specs=row(D_MODEL),
        out_shape=jax.ShapeDtypeStruct((rows, D_MODEL), F32),
        compiler_params=pltpu.CompilerParams(dimension_semantics=("arbitrary",),
                                             vmem_limit_bytes=VMEM_LIMIT),
        name="tail",
    )(h1, brl_p, brl_s, brd_p, brd_s, *consts)


def _mix_kernel(z_ref, gd_ref, cos_ref, sin_ref, cpast_ref, sr0_ref, sg0_ref, cw_ref, cb_ref, lng_ref, lnb_ref,
                rgn_ref, ggn_ref, br_ref, sr_out, sg_out, cs_out, sr_s, sg_s, u_s, ush_s,
                *, chunk, n_chunks, seq_len):
    C = chunk
    c = pl.program_id(1)
    low = C >= 16
    cast = (lambda a: a.astype(BF16)) if low else (lambda a: a)

    t_col = lax.broadcasted_iota(jnp.int32, (C, 1), 0)
    s_row = lax.broadcasted_iota(jnp.int32, (1, C), 1)
    causal = t_col >= s_row
    tf = t_col.astype(F32)
    padded = n_chunks * C > seq_len
    if padded:
        valid = (c * C + t_col) < seq_len
        nvf = jnp.zeros((1, 1), F32) + jnp.minimum(seq_len - c * C, C).astype(F32)
        keep = lambda a: jnp.where(valid, a, 0.0)
    else:
        nvf = jnp.full((1, 1), float(C), F32)
        keep = lambda a: a

    @pl.when(c == 0)
    def _init():
        sr_s[...] = jnp.zeros(sr_s.shape, F32)
        sg_s[...] = jnp.zeros(sg_s.shape, F32)
        for h in range(HEADS):
            sr_s[h * RET_DK:(h + 1) * RET_DK, h * RET_DV:(h + 1) * RET_DV] = sr0_ref[0, h]
            sg_s[h * GLA_DK:(h + 1) * GLA_DK, h * GLA_DV:(h + 1) * GLA_DV] = sg0_ref[0, h]
        u_s[0:HIST - (CONV_K - 1), :] = jnp.zeros((HIST - (CONV_K - 1), CONV_C), F32)
        u_s[HIST - (CONV_K - 1):HIST, :] = cpast_ref[0]

    @pl.when(c > 0)
    def _shift():
        u_s[0:HIST, :] = u_s[C:C + HIST, :]

    u = z_ref[:, 0:256] * _sigmoid(z_ref[:, 256:512])
    u_s[HIST:HIST + C, :] = u
    span = C + HIST - 8
    for s in range(1, 8):
        ush_s[s - 1] = u_s[s:s + span, :]
    acc = jnp.zeros((C, CONV_C), F32)
    for j in range(CONV_K):
        lo = HIST - (CONV_K - 1) + j
        q, r = lo - lo % 8, lo % 8
        tap = u_s[q:q + C, :] if r == 0 else ush_s[r - 1, q:q + C, :]
        acc = acc + cw_ref[j:j + 1, :] * tap
    conv = acc + cb_ref[...]
    xc = conv - jnp.mean(conv, axis=-1, keepdims=True)
    ln = xc * lax.rsqrt(jnp.mean(xc * xc, axis=-1, keepdims=True) + EPS) * lng_ref[...] + lnb_ref[...]
    br_ref[:, 0:256] = _silu(ln)

    lane = lax.broadcasted_iota(jnp.int32, (1, 256), 1)
    first_half = (lane % RET_DK) < (RET_DK // 2)
    cos = cos_ref[...]
    sin = sin_ref[...]

    def rope(a):
        swapped = jnp.where(first_half, pltpu.roll(a, 256 - RET_DK // 2, 1), pltpu.roll(a, RET_DK // 2, 1))
        return a * cos + swapped * sin

    rq = rope(z_ref[:, 512:768])
    rk = keep(rope(z_ref[:, 768:1024]) * RET_DK ** -0.5)
    rv = keep(z_ref[:, 1024:1280])
    lane_head = lane // RET_DV
    row_head = lax.broadcasted_iota(jnp.int32, (HEADS * RET_DK, 1), 0) // RET_DK
    log_gamma = [math.log1p(-(2.0 ** (-5 - h))) for h in range(HEADS)]
    lg_lane = jnp.zeros((1, 256), F32)
    lg_row = jnp.zeros((HEADS * RET_DK, 1), F32)
    for h in range(HEADS):
        lg_lane = jnp.where(lane_head == h, log_gamma[h], lg_lane)
        lg_row = jnp.where(row_head == h, log_gamma[h], lg_row)
    avg = _group_avg(256, RET_DV, low)
    dts = (t_col - s_row).astype(F32)
    rk_c = cast(rk)
    o = _dot(cast(rq * jnp.exp((tf + 1.0) * lg_lane)), cast(sr_s[...]))
    for h in range(HEADS):
        own = lane_head == h
        decay = jnp.where(causal, jnp.exp(dts * log_gamma[h]), 0.0)
        a = _dot_nt(cast(jnp.where(own, rq, 0.0)), rk_c) * decay
        o = o + _dot(cast(a), cast(jnp.where(own, rv, 0.0)))
    update = _dot_tn(cast(rk * jnp.exp((nvf - 1.0 - tf) * lg_lane)), cast(rv))
    sr_s[...] = jnp.exp(nvf * lg_row) * sr_s[...] + jnp.where(row_head == lane_head, update, 0.0)
    oc = o - _group_mean(o, avg, low)
    y = oc * lax.rsqrt(_group_mean(oc * oc, avg, low) + EPS) * rgn_ref[...]
    br_ref[:, 256:512] = _silu(z_ref[:, 1280:1536]) * y

    gq = z_ref[:, 1536:1664] * GLA_DK ** -0.5
    gk = keep(z_ref[:, 1664:1792])
    gv = keep(z_ref[:, 1792:2048])
    g = keep(gd_ref[...])
    tri = jnp.where(causal, 1.0, 0.0)
    if low:
        g_hi, g_lo = _split_bf16(g)
        tri = tri.astype(BF16)
        bcum = _dot(tri, g_hi) + _dot(tri, g_lo)
    else:
        bcum = _dot(tri, g)
    mid = C // 2 - 1
    b_mid = bcum[mid:mid + 1, :]
    b_last = bcum[C - 1:C, :]
    q_intra = gq * jnp.exp(bcum - b_mid)
    k_intra = gk * jnp.exp(b_mid - bcum)
    q_inter = gq * jnp.exp(bcum)
    k_state = gk * jnp.exp(b_last - bcum)
    e_last = jnp.where(t_col == C - 1, jnp.exp(bcum), 0.0)
    ones = jnp.ones((C, HEADS * GLA_DV), BF16 if low else F32)
    if low:
        e_hi, e_lo = _split_bf16(e_last)
        e_rows = _dot_tn(e_hi, ones) + _dot_tn(e_lo, ones)
    else:
        e_rows = _dot_tn(e_last, ones)
    key_head = lax.broadcasted_iota(jnp.int32, (1, HEADS * GLA_DK), 1) // GLA_DK
    krow_head = lax.broadcasted_iota(jnp.int32, (HEADS * GLA_DK, 1), 0) // GLA_DK
    k_c = cast(k_intra)
    o = _dot(cast(q_inter), cast(sg_s[...]))
    for h in range(HEADS):
        a = jnp.where(causal, _dot_nt(cast(jnp.where(key_head == h, q_intra, 0.0)), k_c), 0.0)
        o = o + _dot(cast(a), cast(jnp.where(lane_head == h, gv, 0.0)))
    update = _dot_tn(cast(k_state), cast(gv))
    sg_s[...] = e_rows * sg_s[...] + jnp.where(krow_head == lane_head, update, 0.0)
    y = o * lax.rsqrt(_group_mean(o * o, avg, low) + EPS) * ggn_ref[...]
    br_ref[:, 512:768] = _silu(z_ref[:, 2048:2304]) * y

    @pl.when(c == n_chunks - 1)
    def _final():
        for h in range(HEADS):
            sr_out[0, h] = sr_s[h * RET_DK:(h + 1) * RET_DK, h * RET_DV:(h + 1) * RET_DV]
            sg_out[0, h] = sg_s[h * GLA_DK:(h + 1) * GLA_DK, h * GLA_DV:(h + 1) * GLA_DV]
        n_last = seq_len - (n_chunks - 1) * C
        cs_out[0] = u_s[HIST + n_last - (CONV_K - 1):HIST + n_last, :]


def _mix_call(zlin, gdec, cos, sin, conv_past, ret0, gla0, w, *, layer, state_layer, n_seq, chunk, n_chunks,
              seq_len, row_block0, name):
    blk = lambda n: pl.BlockSpec((chunk, n), lambda s, c: (row_block0 + s * n_chunks + c, 0))
    out_blk = pl.BlockSpec((chunk, 3 * BRANCH_W), lambda s, c: (s * n_chunks + c, 0))
    per_seq = lambda shp: pl.BlockSpec((1,) + shp, lambda s, c: (s,) + (0,) * len(shp))
    state = lambda shp: pl.BlockSpec((None, 1) + shp, lambda s, c: (state_layer, s) + (0,) * len(shp))
    consts = [w['cw'], w['cb'], w['lng'], w['lnb'], w['rgn'], w['ggn']]
    in_specs = ([blk(N_LIN), blk(LANE),
                 pl.BlockSpec((chunk, 256), lambda s, c: (c, 0)), pl.BlockSpec((chunk, 256), lambda s, c: (c, 0)),
                 state((CONV_K - 1, CONV_C)), state((HEADS, RET_DK, RET_DV)), state((HEADS, GLA_DK, GLA_DV))]
                + [_layer_spec(a, layer) for a in consts])
    args = [zlin, gdec, cos, sin, conv_past, ret0, gla0] + consts
    return pl.pallas_call(
        functools.partial(_mix_kernel, chunk=chunk, n_chunks=n_chunks, seq_len=seq_len),
        grid=(n_seq, n_chunks),
        in_specs=in_specs,
        out_specs=[out_blk, per_seq((HEADS, RET_DK, RET_DV)), per_seq((HEADS, GLA_DK, GLA_DV)),
                   per_seq((CONV_K - 1, CONV_C))],
        out_shape=[jax.ShapeDtypeStruct((n_seq * n_chunks * chunk, 3 * BRANCH_W), F32),
                   jax.ShapeDtypeStruct((n_seq, HEADS, RET_DK, RET_DV), F32),
                   jax.ShapeDtypeStruct((n_seq, HEADS, GLA_DK, GLA_DV), F32),
                   jax.ShapeDtypeStruct((n_seq, CONV_K - 1, CONV_C), F32)],
        scratch_shapes=[pltpu.VMEM((HEADS * RET_DK, HEADS * RET_DV), F32),
                        pltpu.VMEM((HEADS * GLA_DK, HEADS * GLA_DV), F32),
                        pltpu.VMEM((HIST + chunk, CONV_C), F32),
                        pltpu.VMEM((7, HIST + chunk - 8, CONV_C), F32)],
        compiler_params=pltpu.CompilerParams(dimension_semantics=("arbitrary", "arbitrary")),
        name=name,
    )(*args)


def _t5_bucket_np(rel):
    rel = np.asarray(rel)
    n = np.maximum(rel, 0)
    max_exact = REL_BUCKETS // 2
    nf = np.maximum(n, 1).astype(np.float64)
    large = max_exact + (np.log(nf / max_exact) / math.log(REL_MAX_DIST / max_exact)
                         * (REL_BUCKETS - max_exact)).astype(np.int64)
    large = np.minimum(large, REL_BUCKETS - 1)
    return np.where(n < max_exact, n, large).astype(np.int32)


def _lambda(lamv_ref, lam_init):
    a = jnp.sum(lamv_ref[0:1, :] * lamv_ref[1:2, :], axis=-1, keepdims=True)
    b = jnp.sum(lamv_ref[2:3, :] * lamv_ref[3:4, :], axis=-1, keepdims=True)
    return jnp.exp(a) - jnp.exp(b) + lam_init


def _flash_kernel(bt_ref, rb_ref, top_ref, lamv_ref, sub_ref, qt_ref, k_ref, vt_ref, o_ref,
                  bias_s, qbd_s, m_s, l_s, acc_s, sta_s, stb_s, *, layer, lam_init, n_blocks):
    QB = SEQ_BLOCK
    SUB = FAR_BLOCK // QB
    b = pl.program_id(0)
    i = pl.program_id(1)

    @pl.when((b == 0) & (i == 0))
    def _tables():
        for kind, src in ((1, 1), (2, 0)):
            bt = bt_ref[src]
            tiles = [jnp.zeros((QB, QB), F32) for _ in range(HEADS)]
            for bk in range(REL_BUCKETS):
                hit = bt == bk
                for h in range(HEADS):
                    tiles[h] = jnp.where(hit, rb_ref[bk, h] * LOG2E, tiles[h])
            for h in range(HEADS):
                bias_s[kind, h] = jnp.where(bt < 0, NEG, tiles[h])
        for h in range(HEADS):
            bias_s[0, h] = jnp.zeros((QB, QB), F32) + rb_ref[REL_BUCKETS - 1, h] * LOG2E
            bias_s[3, h] = jnp.full((QB, QB), NEG, F32)
            for kind in range(4):
                bias_s[4 + kind, h] = bias_s[kind, h] - top_ref[layer, h]

    qt = qt_ref[...] * (DIFF_HD ** -0.5 * LOG2E)
    row = lax.broadcasted_iota(jnp.int32, (256, 1), 0)
    for hj in range(2 * HEADS):
        own = (row >= hj * DIFF_HD) & (row < (hj + 1) * DIFF_HD)
        qbd_s[:, hj * QB:(hj + 1) * QB] = jnp.where(own, qt, 0.0).astype(BF16)
    m_s[...] = jnp.full(m_s.shape, NEG, F32)
    l_s[...] = jnp.zeros(l_s.shape, F32)
    acc_s[...] = jnp.zeros(acc_s.shape, F32)

    def online(h, st, shift, vt):
        m_old = m_s[h]
        m_new = jnp.maximum(m_old, jnp.max(st, axis=0, keepdims=True) + shift)
        alpha = jnp.exp2(m_old - m_new)
        pt = jnp.exp2(st - (m_new - shift))
        l_s[h] = alpha * l_s[h] + jnp.sum(pt, axis=0, keepdims=True)
        m_s[h] = m_new
        acc_s[h] = alpha * acc_s[h] + _dot(vt[h * DIFF_VD:(h + 1) * DIFF_VD, :], pt.astype(BF16))

    def fixed(h, st, vt):
        pt = jnp.exp2(st)
        l_s[h] = l_s[h] + jnp.sum(pt, axis=0, keepdims=True)
        acc_s[h] = acc_s[h] + _dot(vt[h * DIFF_VD:(h + 1) * DIFF_VD, :], pt.astype(BF16))

    n_far = jnp.maximum(i - 1, 0) // SUB

    def sweep(accumulate):
        def far_body(kb, carry):
            off = pl.multiple_of(kb * FAR_BLOCK, FAR_BLOCK)
            k = k_ref[pl.ds(off, FAR_BLOCK), :].astype(BF16)
            vt = vt_ref[:, pl.ds(off, FAR_BLOCK)].astype(BF16)
            for h in range(HEADS):
                st = _dot(k, qbd_s[:, 2 * h * QB:(2 * h + 2) * QB])
                accumulate(h, st, rb_ref[REL_BUCKETS - 1, h] * LOG2E, vt)
            return carry

        lax.fori_loop(0, n_far, far_body, 0)

        def near_body(kb, carry):
            first = kb * SUB
            blk0 = jnp.minimum(first, n_blocks - SUB)
            off = pl.multiple_of(blk0 * QB, QB)
            k = k_ref[pl.ds(off, FAR_BLOCK), :].astype(BF16)
            vt = vt_ref[:, pl.ds(off, FAR_BLOCK)].astype(BF16)
            kinds = []
            for c in range(SUB):
                blk = blk0 + c
                kind = jnp.where(blk == i, 2, jnp.where(blk == i - 1, 1, 0))
                kinds.append(jnp.where((blk > i) | (blk < first), 3, kind))
            for h in range(HEADS):
                st = _dot(k, qbd_s[:, 2 * h * QB:(2 * h + 2) * QB])
                parts = []
                for c in range(SUB):
                    tile = bias_s[kinds[c], h]
                    parts.append(st[c * QB:(c + 1) * QB, :] + jnp.concatenate([tile, tile], axis=1))
                accumulate(h, jnp.concatenate(parts, axis=0), 0.0, vt)
            return carry

        lax.fori_loop(n_far, i // SUB + 1, near_body, 0)

    bounded = top_ref[layer, HEADS] > 0.5

    def window(j):
        first = j * SUB
        return first, jnp.minimum(first, n_blocks - SUB)

    def score(j, buf):
        _, blk0 = window(j)
        k = k_ref[pl.ds(pl.multiple_of(blk0 * QB, QB), FAR_BLOCK), :].astype(BF16)
        for h in range(HEADS):
            buf[h] = _dot(k, qbd_s[:, 2 * h * QB:(2 * h + 2) * QB])

    def consume(j, buf):
        first, blk0 = window(j)
        vt = vt_ref[:, pl.ds(pl.multiple_of(blk0 * QB, QB), FAR_BLOCK)].astype(BF16)
        kinds = []
        for c in range(SUB):
            blk = blk0 + c
            kind = jnp.where(blk == i, 2, jnp.where(blk == i - 1, 1, 0))
            kinds.append(jnp.where((blk > i) | (blk < first), 3, kind))
        for h in range(HEADS):
            parts = []
            for c in range(SUB):
                tile = bias_s[kinds[c] + 4, h]
                parts.append(buf[h, c * QB:(c + 1) * QB, :] + jnp.concatenate([tile, tile], axis=1))
            fixed(h, jnp.concatenate(parts, axis=0), vt)

    @pl.when(bounded)
    def _fixed():
        n_win = i // SUB + 1
        score(0, sta_s)

        def pair(m, carry):
            score(2 * m + 1, stb_s)
            consume(2 * m, sta_s)
            score(2 * m + 2, sta_s)
            consume(2 * m + 1, stb_s)
            return carry

        lax.fori_loop(0, n_win // 2, pair, 0)

        @pl.when(n_win % 2 == 1)
        def _tail():
            consume(n_win - 1, sta_s)

    @pl.when(jnp.logical_not(bounded))
    def _online():
        sweep(online)

    lam = _lambda(lamv_ref, lam_init)
    outs = []
    for h in range(HEADS):
        ot = acc_s[h] / l_s[h]
        d = ot[:, 0:QB] - lam * ot[:, QB:2 * QB]
        y = d * lax.rsqrt(jnp.mean(d * d, axis=0, keepdims=True) + EPS) * sub_ref[...] * (1.0 - lam_init)
        outs.append(y)
    o_ref[...] = jnp.concatenate(outs, axis=0).T


def _score_top(rel_bias, q_gain, k_gain):
    reach = DIFF_HD ** 0.5 * jnp.max(jnp.abs(q_gain * k_gain), axis=-1, keepdims=True)
    top = (reach + jnp.max(rel_bias, axis=0)[None, :]) * LOG2E
    low = (-reach + jnp.min(rel_bias, axis=0)[None, :]) * LOG2E
    ok = jnp.all(top - low < MAX_EXP2_SPAN, axis=-1, keepdims=True).astype(F32)
    return jnp.concatenate([top, ok], axis=-1).astype(F32)


def _flash_call(bt, rel_bias, top, lamv, sub_col, dqt, dk, dvt, *, layer, n_seq, seq_rows, lam_init):
    rows = n_seq * seq_rows
    nq = seq_rows // SEQ_BLOCK
    assert seq_rows >= FAR_BLOCK
    smem = pl.BlockSpec(memory_space=pltpu.SMEM)
    return pl.pallas_call(
        functools.partial(_flash_kernel, layer=layer, lam_init=lam_init, n_blocks=nq),
        grid=(n_seq, nq),
        in_specs=[_whole_spec(bt), smem, smem, _layer_spec(lamv, layer), _layer_spec(sub_col, layer),
                  pl.BlockSpec((None, 256, SEQ_BLOCK), lambda b, i: (b, 0, i)),
                  pl.BlockSpec((seq_rows, 256), lambda b, i: (b, 0)),
                  pl.BlockSpec((None, 256, seq_rows), lambda b, i: (b, 0, 0))],
        out_specs=pl.BlockSpec((SEQ_BLOCK, 256), lambda b, i: (b * nq + i, 0)),
        out_shape=jax.ShapeDtypeStruct((rows, 256), F32),
        scratch_shapes=[pltpu.VMEM((8, HEADS, SEQ_BLOCK, SEQ_BLOCK), F32),
                        pltpu.VMEM((256, 2 * HEADS * SEQ_BLOCK), BF16),
                        pltpu.VMEM((HEADS, 1, 2 * SEQ_BLOCK), F32), pltpu.VMEM((HEADS, 1, 2 * SEQ_BLOCK), F32),
                        pltpu.VMEM((HEADS, DIFF_VD, 2 * SEQ_BLOCK), F32),
                        pltpu.VMEM((HEADS, FAR_BLOCK, 2 * SEQ_BLOCK), F32),
                        pltpu.VMEM((HEADS, FAR_BLOCK, 2 * SEQ_BLOCK), F32)],
        compiler_params=pltpu.CompilerParams(dimension_semantics=("arbitrary", "arbitrary"),
                                             vmem_limit_bytes=VMEM_LIMIT),
        name="attn_prompt",
    )(bt, rel_bias, top, lamv, sub_col, dqt, dk, dvt)


def _decode_kernel(pt_ref, bt_ref, rb_ref, lamv_ref, sub_ref, q_ref, kn_ref, vn_ref, *refs,
                   pages_per_step, n_steps, dec_seq, lam_init):
    G = pages_per_step
    kt_refs = refs[:G]
    vt_refs = refs[G:2 * G]
    o_ref, bias_pg_s, bias_new_s, far_s, qs_s, m_s, l_s, acc_s = refs[2 * G:]
    del pt_ref
    b = pl.program_id(0)
    g = pl.program_id(1)
    last = g == n_steps - 1
    NQ = 2 * HEADS * dec_seq
    seqs_per_block = kn_ref.shape[0] // dec_seq

    row_head = lax.broadcasted_iota(jnp.int32, (NQ, 1), 0) // (2 * dec_seq)

    def bias_rows(bk):
        out = jnp.zeros((NQ, 1), F32)
        for h in range(HEADS):
            out = jnp.where(row_head == h, rb_ref[bk, h], out)
        return out

    @pl.when((b == 0) & (g == 0))
    def _tables():
        for src, dst in ((0, bias_pg_s), (1, bias_new_s)):
            bt = bt_ref[src]
            tile = jnp.zeros(bt.shape, F32)
            for bk in range(REL_BUCKETS):
                tile = jnp.where(bt == bk, bias_rows(bk), tile)
            dst[...] = jnp.where(bt < 0, NEG, tile)
        far_s[...] = jnp.zeros(far_s.shape, F32) + bias_rows(REL_BUCKETS - 1)

    @pl.when(g == 0)
    def _init():
        q = q_ref[...] * DIFF_HD ** -0.5
        lane = lax.broadcasted_iota(jnp.int32, (1, 256), 1)
        for hj in range(2 * HEADS):
            own = (lane >= hj * DIFF_HD) & (lane < (hj + 1) * DIFF_HD)
            qs_s[hj * dec_seq:(hj + 1) * dec_seq, :] = jnp.where(own, q, 0.0)
        m_s[...] = jnp.full(m_s.shape, NEG, F32)
        l_s[...] = jnp.zeros(l_s.shape, F32)
        acc_s[...] = jnp.zeros(acc_s.shape, F32)

    def update(scores, pv_fn):
        m_old = m_s[...]
        s_max = scores[0]
        for s in scores[1:]:
            s_max = jnp.maximum(s_max, s)
        m_new = jnp.maximum(m_old, jnp.max(s_max, axis=1, keepdims=True))
        alpha = jnp.exp(m_old - m_new)
        probs = [jnp.exp(s - m_new) for s in scores]
        p_sum = probs[0]
        for p in probs[1:]:
            p_sum = p_sum + p
        l_s[...] = alpha * l_s[...] + jnp.sum(p_sum, axis=1, keepdims=True)
        m_s[...] = m_new
        acc_s[...] = jnp.concatenate([alpha, alpha], axis=1) * acc_s[...] + pv_fn(probs)

    qs = qs_s[...].astype(BF16)
    scores = []
    for p in range(G):
        s = _dot(qs, kt_refs[p][...].astype(BF16))
        if p == G - 1:
            s = s + jnp.where(last, bias_pg_s[...], far_s[...])
        else:
            s = s + far_s[...]
        scores.append(s)

    def pv_pages(probs):
        out = None
        for p in range(G):
            t = _dot_nt(probs[p].astype(BF16), vt_refs[p][...].astype(BF16))
            out = t if out is None else out + t
        return out

    update(scores, pv_pages)

    @pl.when(last)
    def _finish():
        key_seq = lax.broadcasted_iota(jnp.int32, (1, kn_ref.shape[0]), 1) // dec_seq
        s_new = _dot_nt(qs, kn_ref[...].astype(BF16))
        s_new = jnp.where(key_seq == b % seqs_per_block, s_new + bias_new_s[...], NEG)
        update([s_new], lambda probs: _dot(probs[0].astype(BF16), vn_ref[...].astype(BF16)))
        l_all = l_s[...]
        o_all = acc_s[...] / jnp.concatenate([l_all, l_all], axis=1)
        lam = _lambda(lamv_ref, lam_init)
        for h in range(HEADS):
            r1 = (2 * h) * dec_seq
            r2 = (2 * h + 1) * dec_seq
            cols = slice(h * DIFF_VD, (h + 1) * DIFF_VD)
            d = o_all[r1:r1 + dec_seq, cols] - lam * o_all[r2:r2 + dec_seq, cols]
            y = d * lax.rsqrt(jnp.mean(d * d, axis=-1, keepdims=True) + EPS) * sub_ref[...] * (1.0 - lam_init)
            o_ref[:, cols] = y


def _decode_call(page_table, bt, rel_bias, lamv, sub, dq, dk, dv, cache_kt, cache_vt, *,
                 layer, n_seq, dec_seq, row0, lam_init):
    n_pages = page_table.shape[1]
    page = cache_kt.shape[3]
    G = min(PAGES_PER_STEP, n_pages)
    n_steps = n_pages // G
    nq = 2 * HEADS * dec_seq
    new_rows = LANE
    per_blk = new_rows // dec_seq
    own = pl.BlockSpec((dec_seq, 256), lambda b, g, pt: (row0 // dec_seq + b, 0))
    new = pl.BlockSpec((new_rows, 256), lambda b, g, pt: (row0 // new_rows + b // per_blk, 0))

    def page_spec(p):
        return pl.BlockSpec((None, None, 256, page),
                            lambda b, g, pt: (layer, pt[b * n_pages + g * G + p], 0, 0))

    in_specs = ([_whole_spec(bt), pl.BlockSpec(memory_space=pltpu.SMEM), _layer_spec(lamv, layer),
                 _layer_spec(sub, layer), own, new, new]
                + [page_spec(p) for p in range(G)] + [page_spec(p) for p in range(G)])
    args = [bt, rel_bias, lamv, sub, dq, dk, dv] + [cache_kt] * G + [cache_vt] * G
    return pl.pallas_call(
        functools.partial(_decode_kernel, pages_per_step=G, n_steps=n_steps, dec_seq=dec_seq, lam_init=lam_init),
        grid_spec=pltpu.PrefetchScalarGridSpec(
            num_scalar_prefetch=1,
            grid=(n_seq, n_steps),
            in_specs=in_specs,
            out_specs=pl.BlockSpec((dec_seq, 256), lambda b, g, pt: (b, 0)),
            scratch_shapes=[pltpu.VMEM((nq, page), F32), pltpu.VMEM((nq, new_rows), F32),
                            pltpu.VMEM((nq, page), F32), pltpu.VMEM((nq, 256), F32),
                            pltpu.VMEM((nq, LANE), F32), pltpu.VMEM((nq, LANE), F32), pltpu.VMEM((nq, 256), F32)]),
        out_shape=jax.ShapeDtypeStruct((n_seq * dec_seq, 256), F32),
        compiler_params=pltpu.CompilerParams(dimension_semantics=("arbitrary", "arbitrary"),
                                             vmem_limit_bytes=VMEM_LIMIT),
        name="attn_sample",
    )(page_table.reshape(-1), *args)


def _rope_tables(pos):
    half = RET_DK // 2
    inv = ROPE_BASE ** (-np.arange(half, dtype=np.float64) / half)
    ang = np.asarray(pos, np.float64)[:, None] * inv[None, :]
    cos = np.concatenate([np.cos(ang), np.cos(ang)], axis=1)
    sin = np.concatenate([-np.sin(ang), np.sin(ang)], axis=1)
    return (jnp.asarray(np.tile(cos, (1, HEADS)), F32), jnp.asarray(np.tile(sin, (1, HEADS)), F32))


def _prompt_bucket_tiles():
    t = np.arange(SEQ_BLOCK)
    rel = t[None, :] - t[:, None]
    diag = np.where(rel >= 0, _t5_bucket_np(rel), -1)
    sub = _t5_bucket_np(rel + SEQ_BLOCK)
    return jnp.asarray(np.stack([diag, sub]).astype(np.int32))


def _sample_bucket_tiles(page, dec_seq):
    assert page == LANE
    iq = (np.arange(2 * HEADS * dec_seq) % dec_seq)[:, None]
    past = _t5_bucket_np(page + iq - np.arange(page)[None, :])
    rel_new = iq - (np.arange(LANE) % dec_seq)[None, :]
    new = np.where(rel_new >= 0, _t5_bucket_np(rel_new), -1)
    return jnp.asarray(np.stack([past, new]).astype(np.int32))


def kernel(x_prompt, x_sample, cache_k, cache_v, page_table, state_ret, state_gla, state_conv, meta_tokens,
           rel_bias, norm_ffn1, ffn1_gate, ffn1_up, ffn1_down, norm_mix, w_in, b_in, conv_w, conv_b, conv_ln_g,
           conv_ln_b, ret_gn, gla_alpha_w, gla_alpha_b, gla_gn, q_norm, k_norm, lam_q1, lam_k1, lam_q2, lam_k2,
           diff_subln, w_branch, w_out, norm_ffn2, ffn2_gate, ffn2_up, ffn2_down):
    B, S, D = x_prompt.shape
    DB, DS, _ = x_sample.shape
    depth = w_in.shape[0]
    L = S + N_META
    Lp = -(-L // SEQ_BLOCK) * SEQ_BLOCK
    n_chunks = Lp // SEQ_BLOCK
    n_pool, page = cache_k.shape[1], cache_k.shape[2]
    past_len = page_table.shape[1] * page
    rows_p = B * Lp
    rows_s = DB * DS
    rows_sp = -(-rows_s // ROW_TILE) * ROW_TILE
    assert Lp % ROW_TILE == 0 and rows_s % LANE == 0 and DS % 8 == 0

    pieces = []
    for b in range(B):
        pieces += [meta_tokens.astype(F32), x_prompt[b], jnp.zeros((Lp - L, D), F32)]
    pieces += [x_sample.reshape(rows_s, D), jnp.zeros((rows_sp - rows_s, D), F32)]
    h = jnp.concatenate(pieces, axis=0)
    pad_s = lambda a: jnp.pad(a, ((0, rows_sp - rows_s), (0, 0)))

    cos_p, sin_p = _rope_tables(np.arange(Lp))
    cos_s, sin_s = _rope_tables(past_len + np.arange(DS))
    bt_prompt = _prompt_bucket_tiles()
    bt_sample = _sample_bucket_tiles(page, DS)
    ckt = jnp.transpose(cache_k, (0, 1, 3, 4, 2)).reshape(depth, n_pool, HEADS * 2 * DIFF_HD, page)
    cvt = jnp.transpose(cache_v, (0, 1, 3, 4, 2)).reshape(depth, n_pool, HEADS * DIFF_VD, page)
    zeros_conv = jnp.zeros((1, B, CONV_K - 1, CONV_C), F32)
    zeros_ret = jnp.zeros((1, B, HEADS, RET_DK, RET_DV), F32)
    zeros_gla = jnp.zeros((1, B, HEADS, GLA_DK, GLA_DV), F32)

    row3 = lambda a: a.reshape(depth, 1, -1).astype(F32)
    w = dict(
        g1=row3(norm_ffn1), wg1=ffn1_gate.astype(BF16), wu1=ffn1_up.astype(BF16), wd1=ffn1_down.astype(BF16),
        gm=row3(norm_mix),
        wlin=w_in[:, :, :N_LIN].astype(BF16), blin=row3(b_in[:, :N_LIN]),
        wlow=w_in[:, :, OFF_GLOW:OFF_DIFF].astype(BF16), blow=row3(b_in[:, OFF_GLOW:OFF_DIFF]),
        aw=gla_alpha_w.astype(BF16), ab=row3(gla_alpha_b),
        wdf=w_in[:, :, OFF_DIFF:OFF_GATES].astype(BF16), bdf=row3(b_in[:, OFF_DIFF:OFF_GATES]),
        qg=row3(jnp.tile(q_norm, (1, 2 * HEADS))), kg=row3(jnp.tile(k_norm, (1, 2 * HEADS))),
        wgt=w_in[:, :, OFF_GATES:].astype(BF16), bgt=row3(b_in[:, OFF_GATES:]),
        wb=w_branch.astype(BF16), wo=w_out.astype(BF16), g2=row3(norm_ffn2),
        wg2=ffn2_gate.astype(BF16), wu2=ffn2_up.astype(BF16), wd2=ffn2_down.astype(BF16),
        cw=conv_w, cb=row3(conv_b), lng=row3(conv_ln_g), lnb=row3(conv_ln_b),
        rgn=row3(ret_gn), ggn=row3(gla_gn),
    )
    lamv = jnp.stack([lam_q1, lam_k1, lam_q2, lam_k2], axis=1).astype(F32)
    sub = row3(diff_subln)
    sub_col = diff_subln.reshape(depth, -1, 1).astype(F32)
    top = _score_top(rel_bias, q_norm, k_norm)

    per_layer = []
    for l in range(depth):
        lam_init = 0.8 - 0.6 * math.exp(-0.3 * l)
        h1, zlin, gdec, dq, dk, dv, qt, kt, vt = _head_call(h, w, layer=l, n_seq=B, seq_rows=Lp)

        brl_p, rp, gp, cp = _mix_call(zlin, gdec, cos_p, sin_p, zeros_conv, zeros_ret, zeros_gla, w,
                                      layer=l, state_layer=0, n_seq=B, chunk=SEQ_BLOCK, n_chunks=n_chunks,
                                      seq_len=L, row_block0=0, name="mix_prompt")
        brl_s, rs, gs, cs = _mix_call(zlin, gdec, cos_s, sin_s, state_conv, state_ret, state_gla, w,
                                      layer=l, state_layer=l, n_seq=DB, chunk=DS, n_chunks=1, seq_len=DS,
                                      row_block0=rows_p // DS, name="mix_sample")
        brd_p = _flash_call(bt_prompt, rel_bias, top, lamv, sub_col, qt, dk, vt,
                            layer=l, n_seq=B, seq_rows=Lp, lam_init=lam_init)
        brd_s = _decode_call(page_table, bt_sample, rel_bias, lamv, sub, dq, dk, dv, ckt, cvt,
                             layer=l, n_seq=DB, dec_seq=DS, row0=rows_p, lam_init=lam_init)

        h = _tail_call(h1, brl_p, pad_s(brl_s), brd_p, pad_s(brd_s), w, layer=l)
        per_layer.append(dict(kt=kt, vt=vt, ks=dk[rows_p:rows_p + rows_s], vs=dv[rows_p:rows_p + rows_s],
                              rp=rp, rs=rs, gp=gp, gs=gs, cp=cp, cs=cs))

    st = lambda k: jnp.stack([p[k] for p in per_layer], axis=0)
    seq_major = lambda t: jnp.transpose(t.reshape(depth, B, HEADS, -1, Lp)[..., :L], (0, 1, 4, 2, 3))
    y_prompt = jnp.stack([h[b * Lp + N_META:b * Lp + L] for b in range(B)], axis=0)
    y_sample = h[rows_p:rows_p + rows_s].reshape(DB, DS, D)
    return (y_prompt, y_sample, seq_major(st('kt')), seq_major(st('vt')),
            st('ks').reshape(depth, DB, DS, HEADS, 2 * DIFF_HD), st('vs').reshape(depth, DB, DS, HEADS, DIFF_VD),
            st('rp'), st('rs'), st('gp'), st('gs'), st('cp'), st('cs'))
```

```python
import functools
import math

import numpy as np
import jax
import jax.numpy as jnp
from jax import lax
from jax.experimental import pallas as pl
from jax.experimental.pallas import tpu as pltpu

F32 = jnp.float32
BF16 = jnp.bfloat16

D_MODEL = 1024
N_META = 16
N_BRANCH = 4
BRANCH_W = 256
D_FF = 2816
CONV_C = 256
CONV_K = 31
HEADS = 4
RET_DK = 64
RET_DV = 64
GLA_DK = 32
GLA_DV = 64
GLA_RANK = 16
GLA_TAU = 16.0
DIFF_HD = 32
DIFF_VD = 64
REL_BUCKETS = 32
REL_MAX_DIST = 128
ROPE_BASE = 10000.0
EPS = 1e-6
NEG = -1e30
LOG2E = math.log2(math.e)
MAX_EXP2_SPAN = 100.0

N_LIN = 2304
OFF_GLOW = N_LIN
OFF_DIFF = OFF_GLOW + GLA_RANK
OFF_GATES = OFF_DIFF + 3 * 256
N_IN = OFF_GATES + N_BRANCH * D_MODEL

LANE = 128
ROW_TILE = 384
FFN_CHUNKS = 2
MXU_N = 256
SEQ_BLOCK = 128
FAR_BLOCK = 512
HIST = 32
VMEM_LIMIT = 60 * 1024 * 1024
PAGES_PER_STEP = 64


def _sigmoid(x):
    return 1.0 / (1.0 + jnp.exp(-x))


def _silu(x):
    return x * _sigmoid(x)


def _log_sigmoid(x):
    return jnp.minimum(x, 0.0) - jnp.log1p(jnp.exp(-jnp.abs(x)))


def _rms(x, g):
    return x * lax.rsqrt(jnp.mean(x * x, axis=-1, keepdims=True) + EPS) * g


def _dot(a, b):
    return jnp.dot(a, b, preferred_element_type=F32)


def _dot_nt(a, b):
    return lax.dot_general(a, b, (((1,), (1,)), ((), ())), preferred_element_type=F32)


def _dot_tn(a, b):
    return lax.dot_general(a, b, (((0,), (0,)), ((), ())), preferred_element_type=F32)


def _split_bf16(x):
    hi = x.astype(BF16)
    lo = (x - hi.astype(F32)).astype(BF16)
    return hi, lo


def _group_avg(n, group, low=True):
    r = lax.broadcasted_iota(jnp.int32, (n, n), 0) // group
    c = lax.broadcasted_iota(jnp.int32, (n, n), 1) // group
    return jnp.where(r == c, 1.0 / group, 0.0).astype(BF16 if low else F32)


def _group_mean(x, avg, low=True):
    if not low:
        return _dot(x, avg)
    hi, lo = _split_bf16(x)
    return _dot(hi, avg) + _dot(lo, avg)


def _layer_spec(a, layer, single_buffer=False):
    nd = a.ndim - 1
    kw = dict(pipeline_mode=pl.Buffered(1)) if single_buffer else {}
    return pl.BlockSpec((None,) + a.shape[1:], lambda *_: (layer,) + (0,) * nd, **kw)


def _whole_spec(a):
    return pl.BlockSpec(a.shape, lambda *_: (0,) * a.ndim)


def _ffn(x, wg_ref, wu_ref, wd_ref):
    tiles = D_FF // MXU_N
    edges = [(tiles * c // FFN_CHUNKS) * MXU_N for c in range(FFN_CHUNKS + 1)]
    out = None
    for c in range(FFN_CHUNKS):
        cols = slice(edges[c], edges[c + 1])
        act = (_silu(_dot(x, wg_ref[:, cols])) * _dot(x, wu_ref[:, cols])).astype(BF16)
        part = _dot(act, wd_ref[cols, :])
        out = part if out is None else out + part
    return out


def _head_kernel(h_ref, g1_ref, wg_ref, wu_ref, wd_ref, gm_ref, wlin_ref, blin_ref, wlow_ref, blow_ref,
                 aw_ref, ab_ref, wdf_ref, bdf_ref, qg_ref, kg_ref,
                 h1_ref, zlin_ref, gdec_ref, dq_ref, dk_ref, dv_ref, qt_ref, kt_ref, vt_ref, *, prompt_tiles):
    h = h_ref[...]
    h1 = h + 0.5 * _ffn(_rms(h, g1_ref[...]).astype(BF16), wg_ref, wu_ref, wd_ref)
    h1_ref[...] = h1
    x = _rms(h1, gm_ref[...]).astype(BF16)
    zlin_ref[...] = _dot(x, wlin_ref[...]) + blin_ref[...]
    g_low = _dot(x, wlow_ref[...]) + blow_ref[...]
    g_pre = _dot(g_low.astype(BF16), aw_ref[...]) + ab_ref[...]
    gdec_ref[...] = _log_sigmoid(g_pre) * (1.0 / GLA_TAU)
    zd = _dot(x, wdf_ref[...]) + bdf_ref[...]
    d_q = zd[:, 0:256]
    d_k = zd[:, 256:512]
    d_v = zd[:, 512:768]
    avg = _group_avg(256, DIFF_HD)
    q_n = d_q * lax.rsqrt(_group_mean(d_q * d_q, avg) + EPS) * qg_ref[...]
    dq_ref[...] = q_n
    k_n = d_k * lax.rsqrt(_group_mean(d_k * d_k, avg) + EPS) * kg_ref[...]
    dk_ref[...] = k_n
    dv_ref[...] = d_v

    @pl.when(pl.program_id(0) < prompt_tiles)
    def _transposed():
        qt_ref[...] = q_n.T
        kt_ref[...] = k_n.T
        vt_ref[...] = d_v.T


def _head_call(h, w, *, layer, n_seq, seq_rows):
    rows = h.shape[0]
    per_seq = seq_rows // ROW_TILE
    p_tiles = n_seq * per_seq
    row = lambda n: pl.BlockSpec((ROW_TILE, n), lambda i: (i, 0))

    def col_map(i):
        t = jnp.minimum(i, p_tiles - 1)
        return (t // per_seq, 0, t % per_seq)

    col = pl.BlockSpec((None, 256, ROW_TILE), col_map)
    consts = [w['g1'], w['wg1'], w['wu1'], w['wd1'], w['gm'], w['wlin'], w['blin'], w['wlow'], w['blow'],
              w['aw'], w['ab'], w['wdf'], w['bdf'], w['qg'], w['kg']]
    widths = (D_MODEL, N_LIN, LANE, 256, 256, 256)
    return pl.pallas_call(
        functools.partial(_head_kernel, prompt_tiles=p_tiles),
        grid=(rows // ROW_TILE,),
        in_specs=[row(D_MODEL)] + [_layer_spec(c, layer, single_buffer=True) for c in consts],
        out_specs=[row(n) for n in widths] + [col, col, col],
        out_shape=([jax.ShapeDtypeStruct((rows, n), F32) for n in widths]
                   + [jax.ShapeDtypeStruct((n_seq, 256, seq_rows), F32)] * 3),
        compiler_params=pltpu.CompilerParams(dimension_semantics=("arbitrary",),
                                             vmem_limit_bytes=VMEM_LIMIT),
        name="head",
    )(h, *consts)


def _tail_kernel(h1_ref, brlp_ref, brls_ref, brdp_ref, brds_ref, gm_ref, wgt_ref, bgt_ref, wb_ref, wo_ref, g2_ref,
                 wg_ref, wu_ref, wd_ref, out_ref, *, prompt_tiles):
    h1 = h1_ref[...]
    x = _rms(h1, gm_ref[...]).astype(BF16)
    is_sample = pl.program_id(0) >= prompt_tiles
    merged = None
    for n in range(N_BRANCH):
        if n < 3:
            cols = slice(n * BRANCH_W, (n + 1) * BRANCH_W)
            br = jnp.where(is_sample, brls_ref[:, cols], brlp_ref[:, cols])
        else:
            br = jnp.where(is_sample, brds_ref[...], brdp_ref[...])
        gate = _dot(x, wgt_ref[:, n * D_MODEL:(n + 1) * D_MODEL]) + bgt_ref[:, n * D_MODEL:(n + 1) * D_MODEL]
        term = _dot(br.astype(BF16), wb_ref[n]) * _sigmoid(gate)
        merged = term if merged is None else merged + term
    h2 = h1 + _dot(merged.astype(BF16), wo_ref[...])
    out_ref[...] = h2 + 0.5 * _ffn(_rms(h2, g2_ref[...]).astype(BF16), wg_ref, wu_ref, wd_ref)


def _tail_call(h1, brl_p, brl_s, brd_p, brd_s, w, *, layer):
    rows = h1.shape[0]
    p_tiles = brl_p.shape[0] // ROW_TILE
    row = lambda n: pl.BlockSpec((ROW_TILE, n), lambda i: (i, 0))
    row_p = lambda n: pl.BlockSpec((ROW_TILE, n), lambda i: (jnp.minimum(i, p_tiles - 1), 0))
    row_s = lambda n: pl.BlockSpec((ROW_TILE, n), lambda i: (jnp.maximum(i - p_tiles, 0), 0))
    consts = [w['gm'], w['wgt'], w['bgt'], w['wb'], w['wo'], w['g2'], w['wg2'], w['wu2'], w['wd2']]
    return pl.pallas_call(
        functools.partial(_tail_kernel, prompt_tiles=p_tiles),
        grid=(rows // ROW_TILE,),
        in_specs=[row(D_MODEL), row_p(3 * BRANCH_W), row_s(3 * BRANCH_W), row_p(BRANCH_W), row_s(BRANCH_W)]
                 + [_layer_spec(c, layer, single_buffer=True) for c in consts],
        out_specs=row(D_MODEL),
        out_shape=jax.ShapeDtypeStruct((rows, D_MODEL), F32),
        compiler_params=pltpu.CompilerParams(dimension_semantics=("arbitrary",),
                                             vmem_limit_bytes=VMEM_LIMIT),
        name="tail",
    )(h1, brl_p, brl_s, brd_p, brd_s, *consts)


def _mix_kernel(z_ref, gd_ref, cos_ref, sin_ref, cpast_ref, sr0_ref, sg0_ref, cw_ref, cb_ref, lng_ref, lnb_ref,
                rgn_ref, ggn_ref, br_ref, sr_out, sg_out, cs_out, sr_s, sg_s, u_s, ush_s,
                *, chunk, n_chunks, seq_len):
    C = chunk
    c = pl.program_id(1)
    low = C >= 16
    cast = (lambda a: a.astype(BF16)) if low else (lambda a: a)

    t_col = lax.broadcasted_iota(jnp.int32, (C, 1), 0)
    s_row = lax.broadcasted_iota(jnp.int32, (1, C), 1)
    causal = t_col >= s_row
    tf = t_col.astype(F32)
    padded = n_chunks * C > seq_len
    if padded:
        valid = (c * C + t_col) < seq_len
        nvf = jnp.zeros((1, 1), F32) + jnp.minimum(seq_len - c * C, C).astype(F32)
        keep = lambda a: jnp.where(valid, a, 0.0)
    else:
        nvf = jnp.full((1, 1), float(C), F32)
        keep = lambda a: a

    @pl.when(c == 0)
    def _init():
        sr_s[...] = jnp.zeros(sr_s.shape, F32)
        sg_s[...] = jnp.zeros(sg_s.shape, F32)
        for h in range(HEADS):
            sr_s[h * RET_DK:(h + 1) * RET_DK, h * RET_DV:(h + 1) * RET_DV] = sr0_ref[0, h]
            sg_s[h * GLA_DK:(h + 1) * GLA_DK, h * GLA_DV:(h + 1) * GLA_DV] = sg0_ref[0, h]
        u_s[0:HIST - (CONV_K - 1), :] = jnp.zeros((HIST - (CONV_K - 1), CONV_C), F32)
        u_s[HIST - (CONV_K - 1):HIST, :] = cpast_ref[0]

    @pl.when(c > 0)
    def _shift():
        u_s[0:HIST, :] = u_s[C:C + HIST, :]

    u = z_ref[:, 0:256] * _sigmoid(z_ref[:, 256:512])
    u_s[HIST:HIST + C, :] = u
    span = C + HIST - 8
    for s in range(1, 8):
        ush_s[s - 1] = u_s[s:s + span, :]
    acc = jnp.zeros((C, CONV_C), F32)
    for j in range(CONV_K):
        lo = HIST - (CONV_K - 1) + j
        q, r = lo - lo % 8, lo % 8
        tap = u_s[q:q + C, :] if r == 0 else ush_s[r - 1, q:q + C, :]
        acc = acc + cw_ref[j:j + 1, :] * tap
    conv = acc + cb_ref[...]
    xc = conv - jnp.mean(conv, axis=-1, keepdims=True)
    ln = xc * lax.rsqrt(jnp.mean(xc * xc, axis=-1, keepdims=True) + EPS) * lng_ref[...] + lnb_ref[...]
    br_ref[:, 0:256] = _silu(ln)

    lane = lax.broadcasted_iota(jnp.int32, (1, 256), 1)
    first_half = (lane % RET_DK) < (RET_DK // 2)
    cos = cos_ref[...]
    sin = sin_ref[...]

    def rope(a):
        swapped = jnp.where(first_half, pltpu.roll(a, 256 - RET_DK // 2, 1), pltpu.roll(a, RET_DK // 2, 1))
        return a * cos + swapped * sin

    rq = rope(z_ref[:, 512:768])
    rk = keep(rope(z_ref[:, 768:1024]) * RET_DK ** -0.5)
    rv = keep(z_ref[:, 1024:1280])
    lane_head = lane // RET_DV
    row_head = lax.broadcasted_iota(jnp.int32, (HEADS * RET_DK, 1), 0) // RET_DK
    log_gamma = [math.log1p(-(2.0 ** (-5 - h))) for h in range(HEADS)]
    lg_lane = jnp.zeros((1, 256), F32)
    lg_row = jnp.zeros((HEADS * RET_DK, 1), F32)
    for h in range(HEADS):
        lg_lane = jnp.where(lane_head == h, log_gamma[h], lg_lane)
        lg_row = jnp.where(row_head == h, log_gamma[h], lg_row)
    avg = _group_avg(256, RET_DV, low)
    dts = (t_col - s_row).astype(F32)
    rk_c = cast(rk)
    o = _dot(cast(rq * jnp.exp((tf + 1.0) * lg_lane)), cast(sr_s[...]))
    for h in range(HEADS):
        own = lane_head == h
        decay = jnp.where(causal, jnp.exp(dts * log_gamma[h]), 0.0)
        a = _dot_nt(cast(jnp.where(own, rq, 0.0)), rk_c) * decay
        o = o + _dot(cast(a), cast(jnp.where(own, rv, 0.0)))
    update = _dot_tn(cast(rk * jnp.exp((nvf - 1.0 - tf) * lg_lane)), cast(rv))
    sr_s[...] = jnp.exp(nvf * lg_row) * sr_s[...] + jnp.where(row_head == lane_head, update, 0.0)
    oc = o - _group_mean(o, avg, low)
    y = oc * lax.rsqrt(_group_mean(oc * oc, avg, low) + EPS) * rgn_ref[...]
    br_ref[:, 256:512] = _silu(z_ref[:, 1280:1536]) * y

    gq = z_ref[:, 1536:1664] * GLA_DK ** -0.5
    gk = keep(z_ref[:, 1664:1792])
    gv = keep(z_ref[:, 1792:2048])
    g = keep(gd_ref[...])
    tri = jnp.where(causal, 1.0, 0.0)
    if low:
        g_hi, g_lo = _split_bf16(g)
        tri = tri.astype(BF16)
        bcum = _dot(tri, g_hi) + _dot(tri, g_lo)
    else:
        bcum = _dot(tri, g)
    mid = C // 2 - 1
    b_mid = bcum[mid:mid + 1, :]
    b_last = bcum[C - 1:C, :]
    q_intra = gq * jnp.exp(bcum - b_mid)
    k_intra = gk * jnp.exp(b_mid - bcum)
    q_inter = gq * jnp.exp(bcum)
    k_state = gk * jnp.exp(b_last - bcum)
    e_last = jnp.where(t_col == C - 1, jnp.exp(bcum), 0.0)
    ones = jnp.ones((C, HEADS * GLA_DV), BF16 if low else F32)
    if low:
        e_hi, e_lo = _split_bf16(e_last)
        e_rows = _dot_tn(e_hi, ones) + _dot_tn(e_lo, ones)
    else:
        e_rows = _dot_tn(e_last, ones)
    key_head = lax.broadcasted_iota(jnp.int32, (1, HEADS * GLA_DK), 1) // GLA_DK
    krow_head = lax.broadcasted_iota(jnp.int32, (HEADS * GLA_DK, 1), 0) // GLA_DK
    k_c = cast(k_intra)
    o = _dot(cast(q_inter), cast(sg_s[...]))
    for h in range(HEADS):
        a = jnp.where(causal, _dot_nt(cast(jnp.where(key_head == h, q_intra, 0.0)), k_c), 0.0)
        o = o + _dot(cast(a), cast(jnp.where(lane_head == h, gv, 0.0)))
    update = _dot_tn(cast(k_state), cast(gv))
    sg_s[...] = e_rows * sg_s[...] + jnp.where(krow_head == lane_head, update, 0.0)
    y = o * lax.rsqrt(_group_mean(o * o, avg, low) + EPS) * ggn_ref[...]
    br_ref[:, 512:768] = _silu(z_ref[:, 2048:2304]) * y

    @pl.when(c == n_chunks - 1)
    def _final():
        for h in range(HEADS):
            sr_out[0, h] = sr_s[h * RET_DK:(h + 1) * RET_DK, h * RET_DV:(h + 1) * RET_DV]
            sg_out[0, h] = sg_s[h * GLA_DK:(h + 1) * GLA_DK, h * GLA_DV:(h + 1) * GLA_DV]
        n_last = seq_len - (n_chunks - 1) * C
        cs_out[0] = u_s[HIST + n_last - (CONV_K - 1):HIST + n_last, :]


def _mix_call(zlin, gdec, cos, sin, conv_past, ret0, gla0, w, *, layer, state_layer, n_seq, chunk, n_chunks,
              seq_len, row_block0, name):
    blk = lambda n: pl.BlockSpec((chunk, n), lambda s, c: (row_block0 + s * n_chunks + c, 0))
    out_blk = pl.BlockSpec((chunk, 3 * BRANCH_W), lambda s, c: (s * n_chunks + c, 0))
    per_seq = lambda shp: pl.BlockSpec((1,) + shp, lambda s, c: (s,) + (0,) * len(shp))
    state = lambda shp: pl.BlockSpec((None, 1) + shp, lambda s, c: (state_layer, s) + (0,) * len(shp))
    consts = [w['cw'], w['cb'], w['lng'], w['lnb'], w['rgn'], w['ggn']]
    in_specs = ([blk(N_LIN), blk(LANE),
                 pl.BlockSpec((chunk, 256), lambda s, c: (c, 0)), pl.BlockSpec((chunk, 256), lambda s, c: (c, 0)),
                 state((CONV_K - 1, CONV_C)), state((HEADS, RET_DK, RET_DV)), state((HEADS, GLA_DK, GLA_DV))]
                + [_layer_spec(a, layer) for a in consts])
    args = [zlin, gdec, cos, sin, conv_past, ret0, gla0] + consts
    return pl.pallas_call(
        functools.partial(_mix_kernel, chunk=chunk, n_chunks=n_chunks, seq_len=seq_len),
        grid=(n_seq, n_chunks),
        in_specs=in_specs,
        out_specs=[out_blk, per_seq((HEADS, RET_DK, RET_DV)), per_seq((HEADS, GLA_DK, GLA_DV)),
                   per_seq((CONV_K - 1, CONV_C))],
        out_shape=[jax.ShapeDtypeStruct((n_seq * n_chunks * chunk, 3 * BRANCH_W), F32),
                   jax.ShapeDtypeStruct((n_seq, HEADS, RET_DK, RET_DV), F32),
                   jax.ShapeDtypeStruct((n_seq, HEADS, GLA_DK, GLA_DV), F32),
                   jax.ShapeDtypeStruct((n_seq, CONV_K - 1, CONV_C), F32)],
        scratch_shapes=[pltpu.VMEM((HEADS * RET_DK, HEADS * RET_DV), F32),
                        pltpu.VMEM((HEADS * GLA_DK, HEADS * GLA_DV), F32),
                        pltpu.VMEM((HIST + chunk, CONV_C), F32),
                        pltpu.VMEM((7, HIST + chunk - 8, CONV_C), F32)],
        compiler_params=pltpu.CompilerParams(dimension_semantics=("arbitrary", "arbitrary")),
        name=name,
    )(*args)


def _t5_bucket_np(rel):
    rel = np.asarray(rel)
    n = np.maximum(rel, 0)
    max_exact = REL_BUCKETS // 2
    nf = np.maximum(n, 1).astype(np.float64)
    large = max_exact + (np.log(nf / max_exact) / math.log(REL_MAX_DIST / max_exact)
                         * (REL_BUCKETS - max_exact)).astype(np.int64)
    large = np.minimum(large, REL_BUCKETS - 1)
    return np.where(n < max_exact, n, large).astype(np.int32)


def _lambda(lamv_ref, lam_init):
    a = jnp.sum(lamv_ref[0:1, :] * lamv_ref[1:2, :], axis=-1, keepdims=True)
    b = jnp.sum(lamv_ref[2:3, :] * lamv_ref[3:4, :], axis=-1, keepdims=True)
    return jnp.exp(a) - jnp.exp(b) + lam_init


def _flash_kernel(bt_ref, rb_ref, top_ref, lamv_ref, sub_ref, qt_ref, k_ref, vt_ref, o_ref,
                  bias_s, qbd_s, m_s, l_s, acc_s, sta_s, stb_s, lf_s, accf_s, *, layer, lam_init, n_blocks):
    QB = SEQ_BLOCK
    SUB = FAR_BLOCK // QB
    b = pl.program_id(0)
    i = pl.program_id(1)

    @pl.when((b == 0) & (i == 0))
    def _tables():
        for kind, src in ((1, 1), (2, 0)):
            bt = bt_ref[src]
            tiles = [jnp.zeros((QB, QB), F32) for _ in range(HEADS)]
            for bk in range(REL_BUCKETS):
                hit = bt == bk
                for h in range(HEADS):
                    tiles[h] = jnp.where(hit, rb_ref[bk, h] * LOG2E, tiles[h])
            for h in range(HEADS):
                bias_s[kind, h] = jnp.where(bt < 0, NEG, tiles[h])
        for h in range(HEADS):
            bias_s[0, h] = jnp.zeros((QB, QB), F32) + rb_ref[REL_BUCKETS - 1, h] * LOG2E
            bias_s[3, h] = jnp.full((QB, QB), NEG, F32)
            for kind in range(4):
                bias_s[4 + kind, h] = bias_s[kind, h] - top_ref[layer, h]

    qt = qt_ref[...] * (DIFF_HD ** -0.5 * LOG2E)
    row = lax.broadcasted_iota(jnp.int32, (256, 1), 0)
    for hj in range(2 * HEADS):
        own = (row >= hj * DIFF_HD) & (row < (hj + 1) * DIFF_HD)
        qbd_s[:, hj * QB:(hj + 1) * QB] = jnp.where(own, qt, 0.0).astype(BF16)
    m_s[...] = jnp.full(m_s.shape, NEG, F32)
    l_s[...] = jnp.zeros(l_s.shape, F32)
    acc_s[...] = jnp.zeros(acc_s.shape, F32)

    def online(h, st, shift, vt):
        m_old = m_s[h]
        m_new = jnp.maximum(m_old, jnp.max(st, axis=0, keepdims=True) + shift)
        alpha = jnp.exp2(m_old - m_new)
        pt = jnp.exp2(st - (m_new - shift))
        l_s[h] = alpha * l_s[h] + jnp.sum(pt, axis=0, keepdims=True)
        m_s[h] = m_new
        acc_s[h] = alpha * acc_s[h] + _dot(vt[h * DIFF_VD:(h + 1) * DIFF_VD, :], pt.astype(BF16))

    def fixed(h, st, vt):
        pt = jnp.exp2(st)
        l_s[h] = l_s[h] + jnp.sum(pt, axis=0, keepdims=True)
        acc_s[h] = acc_s[h] + _dot(vt[h * DIFF_VD:(h + 1) * DIFF_VD, :], pt.astype(BF16))

    n_far = jnp.maximum(i - 1, 0) // SUB

    def sweep(accumulate):
        def far_body(kb, carry):
            off = pl.multiple_of(kb * FAR_BLOCK, FAR_BLOCK)
            k = k_ref[pl.ds(off, FAR_BLOCK), :].astype(BF16)
            vt = vt_ref[:, pl.ds(off, FAR_BLOCK)].astype(BF16)
            for h in range(HEADS):
                st = _dot(k, qbd_s[:, 2 * h * QB:(2 * h + 2) * QB])
                accumulate(h, st, rb_ref[REL_BUCKETS - 1, h] * LOG2E, vt)
            return carry

        lax.fori_loop(0, n_far, far_body, 0)

        def near_body(kb, carry):
            first = kb * SUB
            blk0 = jnp.minimum(first, n_blocks - SUB)
            off = pl.multiple_of(blk0 * QB, QB)
            k = k_ref[pl.ds(off, FAR_BLOCK), :].astype(BF16)
            vt = vt_ref[:, pl.ds(off, FAR_BLOCK)].astype(BF16)
            kinds = []
            for c in range(SUB):
                blk = blk0 + c
                kind = jnp.where(blk == i, 2, jnp.where(blk == i - 1, 1, 0))
                kinds.append(jnp.where((blk > i) | (blk < first), 3, kind))
            for h in range(HEADS):
                st = _dot(k, qbd_s[:, 2 * h * QB:(2 * h + 2) * QB])
                parts = []
                for c in range(SUB):
                    tile = bias_s[kinds[c], h]
                    parts.append(st[c * QB:(c + 1) * QB, :] + jnp.concatenate([tile, tile], axis=1))
                accumulate(h, jnp.concatenate(parts, axis=0), 0.0, vt)
            return carry

        lax.fori_loop(n_far, i // SUB + 1, near_body, 0)

    bounded = top_ref[layer, HEADS] > 0.5

    def window(j):
        first = j * SUB
        return first, jnp.minimum(first, n_blocks - SUB)

    def score(j, buf):
        _, blk0 = window(j)
        k = k_ref[pl.ds(pl.multiple_of(blk0 * QB, QB), FAR_BLOCK), :].astype(BF16)
        for h in range(HEADS):
            buf[h] = _dot(k, qbd_s[:, 2 * h * QB:(2 * h + 2) * QB])

    def consume(j, buf):
        first, blk0 = window(j)
        vt = vt_ref[:, pl.ds(pl.multiple_of(blk0 * QB, QB), FAR_BLOCK)].astype(BF16)
        kinds = []
        for c in range(SUB):
            blk = blk0 + c
            kind = jnp.where(blk == i, 2, jnp.where(blk == i - 1, 1, 0))
            kinds.append(jnp.where((blk > i) | (blk < first), 3, kind))
        for h in range(HEADS):
            parts = []
            for c in range(SUB):
                tile = bias_s[kinds[c] + 4, h]
                parts.append(buf[h, c * QB:(c + 1) * QB, :] + jnp.concatenate([tile, tile], axis=1))
            fixed(h, jnp.concatenate(parts, axis=0), vt)

    def consume_far(j, buf):
        vt = vt_ref[:, pl.ds(pl.multiple_of(j * FAR_BLOCK, FAR_BLOCK), FAR_BLOCK)].astype(BF16)
        for h in range(HEADS):
            pt = jnp.exp2(buf[h])
            lf_s[h] = lf_s[h] + jnp.sum(pt, axis=0, keepdims=True)
            accf_s[h] = accf_s[h] + _dot(vt[h * DIFF_VD:(h + 1) * DIFF_VD, :], pt.astype(BF16))

    @pl.when(bounded)
    def _fixed():
        n_win = i // SUB + 1
        far_pairs = n_far // 2
        lf_s[...] = jnp.zeros(lf_s.shape, F32)
        accf_s[...] = jnp.zeros(accf_s.shape, F32)
        score(0, sta_s)

        def far_pair(m, carry):
            score(2 * m + 1, stb_s)
            consume_far(2 * m, sta_s)
            score(2 * m + 2, sta_s)
            consume_far(2 * m + 1, stb_s)
            return carry

        lax.fori_loop(0, far_pairs, far_pair, 0)
        w0 = 2 * far_pairs
        rest = n_win - w0

        def pair(m, carry):
            score(w0 + 2 * m + 1, stb_s)
            consume(w0 + 2 * m, sta_s)
            score(w0 + 2 * m + 2, sta_s)
            consume(w0 + 2 * m + 1, stb_s)
            return carry

        lax.fori_loop(0, rest // 2, pair, 0)

        @pl.when(rest % 2 == 1)
        def _tail():
            consume(n_win - 1, sta_s)

        for h in range(HEADS):
            far_scale = jnp.exp2(jnp.zeros((1, 1), F32) + (rb_ref[REL_BUCKETS - 1, h] * LOG2E - top_ref[layer, h]))
            l_s[h] = l_s[h] + far_scale * lf_s[h]
            acc_s[h] = acc_s[h] + far_scale * accf_s[h]

    @pl.when(jnp.logical_not(bounded))
    def _online():
        sweep(online)

    lam = _lambda(lamv_ref, lam_init)
    outs = []
    for h in range(HEADS):
        ot = acc_s[h] / l_s[h]
        d = ot[:, 0:QB] - lam * ot[:, QB:2 * QB]
        y = d * lax.rsqrt(jnp.mean(d * d, axis=0, keepdims=True) + EPS) * sub_ref[...] * (1.0 - lam_init)
        outs.append(y)
    o_ref[...] = jnp.concatenate(outs, axis=0).T


def _score_top(rel_bias, q_gain, k_gain):
    reach = DIFF_HD ** 0.5 * jnp.max(jnp.abs(q_gain * k_gain), axis=-1, keepdims=True)
    top = (reach + jnp.max(rel_bias, axis=0)[None, :]) * LOG2E
    low = (-reach + jnp.min(rel_bias, axis=0)[None, :]) * LOG2E
    ok = jnp.all(top - low < MAX_EXP2_SPAN, axis=-1, keepdims=True).astype(F32)
    return jnp.concatenate([top, ok], axis=-1).astype(F32)


def _flash_call(bt, rel_bias, top, lamv, sub_col, dqt, dk, dvt, *, layer, n_seq, seq_rows, lam_init):
    rows = n_seq * seq_rows
    nq = seq_rows // SEQ_BLOCK
    assert seq_rows >= FAR_BLOCK
    smem = pl.BlockSpec(memory_space=pltpu.SMEM)
    return pl.pallas_call(
        functools.partial(_flash_kernel, layer=layer, lam_init=lam_init, n_blocks=nq),
        grid=(n_seq, nq),
        in_specs=[_whole_spec(bt), smem, smem, _layer_spec(lamv, layer), _layer_spec(sub_col, layer),
                  pl.BlockSpec((None, 256, SEQ_BLOCK), lambda b, i: (b, 0, i)),
                  pl.BlockSpec((seq_rows, 256), lambda b, i: (b, 0)),
                  pl.BlockSpec((None, 256, seq_rows), lambda b, i: (b, 0, 0))],
        out_specs=pl.BlockSpec((SEQ_BLOCK, 256), lambda b, i: (b * nq + i, 0)),
        out_shape=jax.ShapeDtypeStruct((rows, 256), F32),
        scratch_shapes=[pltpu.VMEM((8, HEADS, SEQ_BLOCK, SEQ_BLOCK), F32),
                        pltpu.VMEM((256, 2 * HEADS * SEQ_BLOCK), BF16),
                        pltpu.VMEM((HEADS, 1, 2 * SEQ_BLOCK), F32), pltpu.VMEM((HEADS, 1, 2 * SEQ_BLOCK), F32),
                        pltpu.VMEM((HEADS, DIFF_VD, 2 * SEQ_BLOCK), F32),
                        pltpu.VMEM((HEADS, FAR_BLOCK, 2 * SEQ_BLOCK), F32),
                        pltpu.VMEM((HEADS, FAR_BLOCK, 2 * SEQ_BLOCK), F32),
                        pltpu.VMEM((HEADS, 1, 2 * SEQ_BLOCK), F32),
                        pltpu.VMEM((HEADS, DIFF_VD, 2 * SEQ_BLOCK), F32)],
        compiler_params=pltpu.CompilerParams(dimension_semantics=("arbitrary", "arbitrary"),
                                             vmem_limit_bytes=VMEM_LIMIT),
        name="attn_prompt",
    )(bt, rel_bias, top, lamv, sub_col, dqt, dk, dvt)


def _decode_kernel(pt_ref, bt_ref, rb_ref, lamv_ref, sub_ref, q_ref, kn_ref, vn_ref, *refs,
                   pages_per_step, n_steps, dec_seq, lam_init):
    G = pages_per_step
    kt_refs = refs[:G]
    vt_refs = refs[G:2 * G]
    o_ref, bias_pg_s, bias_new_s, far_s, qs_s, m_s, l_s, acc_s = refs[2 * G:]
    del pt_ref
    b = pl.program_id(0)
    g = pl.program_id(1)
    last = g == n_steps - 1
    NQ = 2 * HEADS * dec_seq
    seqs_per_block = kn_ref.shape[0] // dec_seq

    row_head = lax.broadcasted_iota(jnp.int32, (NQ, 1), 0) // (2 * dec_seq)

    def bias_rows(bk):
        out = jnp.zeros((NQ, 1), F32)
        for h in range(HEADS):
            out = jnp.where(row_head == h, rb_ref[bk, h], out)
        return out

    @pl.when((b == 0) & (g == 0))
    def _tables():
        for src, dst in ((0, bias_pg_s), (1, bias_new_s)):
            bt = bt_ref[src]
            tile = jnp.zeros(bt.shape, F32)
            for bk in range(REL_BUCKETS):
                tile = jnp.where(bt == bk, bias_rows(bk), tile)
            dst[...] = jnp.where(bt < 0, NEG, tile)
        far_s[...] = jnp.zeros(far_s.shape, F32) + bias_rows(REL_BUCKETS - 1)

    @pl.when(g == 0)
    def _init():
        q = q_ref[...] * DIFF_HD ** -0.5
        lane = lax.broadcasted_iota(jnp.int32, (1, 256), 1)
        for hj in range(2 * HEADS):
            own = (lane >= hj * DIFF_HD) & (lane < (hj + 1) * DIFF_HD)
            qs_s[hj * dec_seq:(hj + 1) * dec_seq, :] = jnp.where(own, q, 0.0)
        m_s[...] = jnp.full(m_s.shape, NEG, F32)
        l_s[...] = jnp.zeros(l_s.shape, F32)
        acc_s[...] = jnp.zeros(acc_s.shape, F32)

    def update(scores, pv_fn):
        m_old = m_s[...]
        s_max = scores[0]
        for s in scores[1:]:
            s_max = jnp.maximum(s_max, s)
        m_new = jnp.maximum(m_old, jnp.max(s_max, axis=1, keepdims=True))
        alpha = jnp.exp(m_old - m_new)
        probs = [jnp.exp(s - m_new) for s in scores]
        p_sum = probs[0]
        for p in probs[1:]:
            p_sum = p_sum + p
        l_s[...] = alpha * l_s[...] + jnp.sum(p_sum, axis=1, keepdims=True)
        m_s[...] = m_new
        acc_s[...] = jnp.concatenate([alpha, alpha], axis=1) * acc_s[...] + pv_fn(probs)

    qs = qs_s[...].astype(BF16)
    scores = []
    for p in range(G):
        s = _dot(qs, kt_refs[p][...].astype(BF16))
        if p == G - 1:
            s = s + jnp.where(last, bias_pg_s[...], far_s[...])
        else:
            s = s + far_s[...]
        scores.append(s)

    def pv_pages(probs):
        out = None
        for p in range(G):
            t = _dot_nt(probs[p].astype(BF16), vt_refs[p][...].astype(BF16))
            out = t if out is None else out + t
        return out

    update(scores, pv_pages)

    @pl.when(last)
    def _finish():
        key_seq = lax.broadcasted_iota(jnp.int32, (1, kn_ref.shape[0]), 1) // dec_seq
        s_new = _dot_nt(qs, kn_ref[...].astype(BF16))
        s_new = jnp.where(key_seq == b % seqs_per_block, s_new + bias_new_s[...], NEG)
        update([s_new], lambda probs: _dot(probs[0].astype(BF16), vn_ref[...].astype(BF16)))
        l_all = l_s[...]
        o_all = acc_s[...] / jnp.concatenate([l_all, l_all], axis=1)
        lam = _lambda(lamv_ref, lam_init)
        for h in range(HEADS):
            r1 = (2 * h) * dec_seq
            r2 = (2 * h + 1) * dec_seq
            cols = slice(h * DIFF_VD, (h + 1) * DIFF_VD)
            d = o_all[r1:r1 + dec_seq, cols] - lam * o_all[r2:r2 + dec_seq, cols]
            y = d * lax.rsqrt(jnp.mean(d * d, axis=-1, keepdims=True) + EPS) * sub_ref[...] * (1.0 - lam_init)
            o_ref[:, cols] = y


def _decode_call(page_table, bt, rel_bias, lamv, sub, dq, dk, dv, cache_kt, cache_vt, *,
                 layer, n_seq, dec_seq, row0, lam_init):
    n_pages = page_table.shape[1]
    page = cache_kt.shape[3]
    G = min(PAGES_PER_STEP, n_pages)
    n_steps = n_pages // G
    nq = 2 * HEADS * dec_seq
    new_rows = LANE
    per_blk = new_rows // dec_seq
    own = pl.BlockSpec((dec_seq, 256), lambda b, g, pt: (row0 // dec_seq + b, 0))
    new = pl.BlockSpec((new_rows, 256), lambda b, g, pt: (row0 // new_rows + b // per_blk, 0))

    def page_spec(p):
        return pl.BlockSpec((None, None, 256, page),
                            lambda b, g, pt: (layer, pt[b * n_pages + g * G + p], 0, 0))

    in_specs = ([_whole_spec(bt), pl.BlockSpec(memory_space=pltpu.SMEM), _layer_spec(lamv, layer),
                 _layer_spec(sub, layer), own, new, new]
                + [page_spec(p) for p in range(G)] + [page_spec(p) for p in range(G)])
    args = [bt, rel_bias, lamv, sub, dq, dk, dv] + [cache_kt] * G + [cache_vt] * G
    return pl.pallas_call(
        functools.partial(_decode_kernel, pages_per_step=G, n_steps=n_steps, dec_seq=dec_seq, lam_init=lam_init),
        grid_spec=pltpu.PrefetchScalarGridSpec(
            num_scalar_prefetch=1,
            grid=(n_seq, n_steps),
            in_specs=in_specs,
            out_specs=pl.BlockSpec((dec_seq, 256), lambda b, g, pt: (b, 0)),
            scratch_shapes=[pltpu.VMEM((nq, page), F32), pltpu.VMEM((nq, new_rows), F32),
                            pltpu.VMEM((nq, page), F32), pltpu.VMEM((nq, 256), F32),
                            pltpu.VMEM((nq, LANE), F32), pltpu.VMEM((nq, LANE), F32), pltpu.VMEM((nq, 256), F32)]),
        out_shape=jax.ShapeDtypeStruct((n_seq * dec_seq, 256), F32),
        compiler_params=pltpu.CompilerParams(dimension_semantics=("arbitrary", "arbitrary"),
                                             vmem_limit_bytes=VMEM_LIMIT),
        name="attn_sample",
    )(page_table.reshape(-1), *args)


def _rope_tables(pos):
    half = RET_DK // 2
    inv = ROPE_BASE ** (-np.arange(half, dtype=np.float64) / half)
    ang = np.asarray(pos, np.float64)[:, None] * inv[None, :]
    cos = np.concatenate([np.cos(ang), np.cos(ang)], axis=1)
    sin = np.concatenate([-np.sin(ang), np.sin(ang)], axis=1)
    return (jnp.asarray(np.tile(cos, (1, HEADS)), F32), jnp.asarray(np.tile(sin, (1, HEADS)), F32))


def _prompt_bucket_tiles():
    t = np.arange(SEQ_BLOCK)
    rel = t[None, :] - t[:, None]
    diag = np.where(rel >= 0, _t5_bucket_np(rel), -1)
    sub = _t5_bucket_np(rel + SEQ_BLOCK)
    return jnp.asarray(np.stack([diag, sub]).astype(np.int32))


def _sample_bucket_tiles(page, dec_seq):
    assert page == LANE
    iq = (np.arange(2 * HEADS * dec_seq) % dec_seq)[:, None]
    past = _t5_bucket_np(page + iq - np.arange(page)[None, :])
    rel_new = iq - (np.arange(LANE) % dec_seq)[None, :]
    new = np.where(rel_new >= 0, _t5_bucket_np(rel_new), -1)
    return jnp.asarray(np.stack([past, new]).astype(np.int32))


def kernel(x_prompt, x_sample, cache_k, cache_v, page_table, state_ret, state_gla, state_conv, meta_tokens,
           rel_bias, norm_ffn1, ffn1_gate, ffn1_up, ffn1_down, norm_mix, w_in, b_in, conv_w, conv_b, conv_ln_g,
           conv_ln_b, ret_gn, gla_alpha_w, gla_alpha_b, gla_gn, q_norm, k_norm, lam_q1, lam_k1, lam_q2, lam_k2,
           diff_subln, w_branch, w_out, norm_ffn2, ffn2_gate, ffn2_up, ffn2_down):
    B, S, D = x_prompt.shape
    DB, DS, _ = x_sample.shape
    depth = w_in.shape[0]
    L = S + N_META
    Lp = -(-L // SEQ_BLOCK) * SEQ_BLOCK
    n_chunks = Lp // SEQ_BLOCK
    n_pool, page = cache_k.shape[1], cache_k.shape[2]
    past_len = page_table.shape[1] * page
    rows_p = B * Lp
    rows_s = DB * DS
    rows_sp = -(-rows_s // ROW_TILE) * ROW_TILE
    assert Lp % ROW_TILE == 0 and rows_s % LANE == 0 and DS % 8 == 0

    pieces = []
    for b in range(B):
        pieces += [meta_tokens.astype(F32), x_prompt[b], jnp.zeros((Lp - L, D), F32)]
    pieces += [x_sample.reshape(rows_s, D), jnp.zeros((rows_sp - rows_s, D), F32)]
    h = jnp.concatenate(pieces, axis=0)
    pad_s = lambda a: jnp.pad(a, ((0, rows_sp - rows_s), (0, 0)))

    cos_p, sin_p = _rope_tables(np.arange(Lp))
    cos_s, sin_s = _rope_tables(past_len + np.arange(DS))
    bt_prompt = _prompt_bucket_tiles()
    bt_sample = _sample_bucket_tiles(page, DS)
    ckt = jnp.transpose(cache_k, (0, 1, 3, 4, 2)).reshape(depth, n_pool, HEADS * 2 * DIFF_HD, page)
    cvt = jnp.transpose(cache_v, (0, 1, 3, 4, 2)).reshape(depth, n_pool, HEADS * DIFF_VD, page)
    zeros_conv = jnp.zeros((1, B, CONV_K - 1, CONV_C), F32)
    zeros_ret = jnp.zeros((1, B, HEADS, RET_DK, RET_DV), F32)
    zeros_gla = jnp.zeros((1, B, HEADS, GLA_DK, GLA_DV), F32)

    row3 = lambda a: a.reshape(depth, 1, -1).astype(F32)
    w = dict(
        g1=row3(norm_ffn1), wg1=ffn1_gate.astype(BF16), wu1=ffn1_up.astype(BF16), wd1=ffn1_down.astype(BF16),
        gm=row3(norm_mix),
        wlin=w_in[:, :, :N_LIN].astype(BF16), blin=row3(b_in[:, :N_LIN]),
        wlow=w_in[:, :, OFF_GLOW:OFF_DIFF].astype(BF16), blow=row3(b_in[:, OFF_GLOW:OFF_DIFF]),
        aw=gla_alpha_w.astype(BF16), ab=row3(gla_alpha_b),
        wdf=w_in[:, :, OFF_DIFF:OFF_GATES].astype(BF16), bdf=row3(b_in[:, OFF_DIFF:OFF_GATES]),
        qg=row3(jnp.tile(q_norm, (1, 2 * HEADS))), kg=row3(jnp.tile(k_norm, (1, 2 * HEADS))),
        wgt=w_in[:, :, OFF_GATES:].astype(BF16), bgt=row3(b_in[:, OFF_GATES:]),
        wb=w_branch.astype(BF16), wo=w_out.astype(BF16), g2=row3(norm_ffn2),
        wg2=ffn2_gate.astype(BF16), wu2=ffn2_up.astype(BF16), wd2=ffn2_down.astype(BF16),
        cw=conv_w, cb=row3(conv_b), lng=row3(conv_ln_g), lnb=row3(conv_ln_b),
        rgn=row3(ret_gn), ggn=row3(gla_gn),
    )
    lamv = jnp.stack([lam_q1, lam_k1, lam_q2, lam_k2], axis=1).astype(F32)
    sub = row3(diff_subln)
    sub_col = diff_subln.reshape(depth, -1, 1).astype(F32)
    top = _score_top(rel_bias, q_norm, k_norm)

    per_layer = []
    for l in range(depth):
        lam_init = 0.8 - 0.6 * math.exp(-0.3 * l)
        h1, zlin, gdec, dq, dk, dv, qt, kt, vt = _head_call(h, w, layer=l, n_seq=B, seq_rows=Lp)

        brl_p, rp, gp, cp = _mix_call(zlin, gdec, cos_p, sin_p, zeros_conv, zeros_ret, zeros_gla, w,
                                      layer=l, state_layer=0, n_seq=B, chunk=SEQ_BLOCK, n_chunks=n_chunks,
                                      seq_len=L, row_block0=0, name="mix_prompt")
        brl_s, rs, gs, cs = _mix_call(zlin, gdec, cos_s, sin_s, state_conv, state_ret, state_gla, w,
                                      layer=l, state_layer=l, n_seq=DB, chunk=DS, n_chunks=1, seq_len=DS,
                                      row_block0=rows_p // DS, name="mix_sample")
        brd_p = _flash_call(bt_prompt, rel_bias, top, lamv, sub_col, qt, dk, vt,
                            layer=l, n_seq=B, seq_rows=Lp, lam_init=lam_init)
        brd_s = _decode_call(page_table, bt_sample, rel_bias, lamv, sub, dq, dk, dv, ckt, cvt,
                             layer=l, n_seq=DB, dec_seq=DS, row0=rows_p, lam_init=lam_init)

        h = _tail_call(h1, brl_p, pad_s(brl_s), brd_p, pad_s(brd_s), w, layer=l)
        per_layer.append(dict(kt=kt, vt=vt, ks=dk[rows_p:rows_p + rows_s], vs=dv[rows_p:rows_p + rows_s],
                              rp=rp, rs=rs, gp=gp, gs=gs, cp=cp, cs=cs))

    st = lambda k: jnp.stack([p[k] for p in per_layer], axis=0)
    seq_major = lambda t: jnp.transpose(t.reshape(depth, B, HEADS, -1, Lp)[..., :L], (0, 1, 4, 2, 3))
    y_prompt = jnp.stack([h[b * Lp + N_META:b * Lp + L] for b in range(B)], axis=0)
    y_sample = h[rows_p:rows_p + rows_s].reshape(DB, DS, D)
    return (y_prompt, y_sample, seq_major(st('kt')), seq_major(st('vt')),
            st('ks').reshape(depth, DB, DS, HEADS, 2 * DIFF_HD), st('vs').reshape(depth, DB, DS, HEADS, DIFF_VD),
            st('rp'), st('rs'), st('gp'), st('gs'), st('cp'), st('cs'))
```

```python
import functools
import math

import numpy as np
import jax
import jax.numpy as jnp
from jax import lax
from jax.experimental import pallas as pl
from jax.experimental.pallas import tpu as pltpu

F32 = jnp.float32
BF16 = jnp.bfloat16

D_MODEL = 1024
N_META = 16
N_BRANCH = 4
BRANCH_W = 256
D_FF = 2816
CONV_C = 256
CONV_K = 31
HEADS = 4
RET_DK = 64
RET_DV = 64
GLA_DK = 32
GLA_DV = 64
GLA_RANK = 16
GLA_TAU = 16.0
DIFF_HD = 32
DIFF_VD = 64
REL_BUCKETS = 32
REL_MAX_DIST = 128
ROPE_BASE = 10000.0
EPS = 1e-6
NEG = -1e30
LOG2E = math.log2(math.e)
MAX_EXP2_SPAN = 100.0

N_LIN = 2304
OFF_GLOW = N_LIN
OFF_DIFF = OFF_GLOW + GLA_RANK
OFF_GATES = OFF_DIFF + 3 * 256
N_IN = OFF_GATES + N_BRANCH * D_MODEL

LANE = 128
ROW_TILE = 384
FFN_CHUNKS = 1
MXU_N = 256
SEQ_BLOCK = 128
FAR_BLOCK = 512
HIST = 32
VMEM_LIMIT = 60 * 1024 * 1024
MIX_GROUP = 2
PAGES_PER_STEP = 64


def _sigmoid(x):
    return 1.0 / (1.0 + jnp.exp(-x))


def _silu(x):
    return x * _sigmoid(x)


def _log_sigmoid(x):
    return jnp.minimum(x, 0.0) - jnp.log1p(jnp.exp(-jnp.abs(x)))


def _rms(x, g):
    return x * lax.rsqrt(jnp.mean(x * x, axis=-1, keepdims=True) + EPS) * g


def _dot(a, b):
    return jnp.dot(a, b, preferred_element_type=F32)


def _dot_nt(a, b):
    return lax.dot_general(a, b, (((1,), (1,)), ((), ())), preferred_element_type=F32)


def _dot_tn(a, b):
    return lax.dot_general(a, b, (((0,), (0,)), ((), ())), preferred_element_type=F32)


def _split_bf16(x):
    hi = x.astype(BF16)
    lo = (x - hi.astype(F32)).astype(BF16)
    return hi, lo


def _group_avg(n, group, low=True):
    r = lax.broadcasted_iota(jnp.int32, (n, n), 0) // group
    c = lax.broadcasted_iota(jnp.int32, (n, n), 1) // group
    return jnp.where(r == c, 1.0 / group, 0.0).astype(BF16 if low else F32)


def _group_mean(x, avg, low=True):
    if not low:
        return _dot(x, avg)
    hi, lo = _split_bf16(x)
    return _dot(hi, avg) + _dot(lo, avg)


def _layer_spec(a, layer, single_buffer=False):
    nd = a.ndim - 1
    kw = dict(pipeline_mode=pl.Buffered(1)) if single_buffer else {}
    return pl.BlockSpec((None,) + a.shape[1:], lambda *_: (layer,) + (0,) * nd, **kw)


def _whole_spec(a):
    return pl.BlockSpec(a.shape, lambda *_: (0,) * a.ndim)


def _ffn(x, wg_ref, wu_ref, wd_ref):
    tiles = D_FF // MXU_N
    edges = [(tiles * c // FFN_CHUNKS) * MXU_N for c in range(FFN_CHUNKS + 1)]
    out = None
    for c in range(FFN_CHUNKS):
        cols = slice(edges[c], edges[c + 1])
        act = (_silu(_dot(x, wg_ref[:, cols])) * _dot(x, wu_ref[:, cols])).astype(BF16)
        part = _dot(act, wd_ref[cols, :])
        out = part if out is None else out + part
    return out


def _head_kernel(h_ref, g1_ref, wg_ref, wu_ref, wd_ref, gm_ref, wlin_ref, blin_ref, wlow_ref, blow_ref,
                 aw_ref, ab_ref, wdf_ref, bdf_ref, qg_ref, kg_ref,
                 h1_ref, zlin_ref, gdec_ref, dq_ref, dk_ref, dv_ref, qt_ref, kt_ref, vt_ref, *, prompt_tiles):
    h = h_ref[...]
    h1 = h + 0.5 * _ffn(_rms(h, g1_ref[...]).astype(BF16), wg_ref, wu_ref, wd_ref)
    h1_ref[...] = h1
    x = _rms(h1, gm_ref[...]).astype(BF16)
    zlin_ref[...] = _dot(x, wlin_ref[...]) + blin_ref[...]
    g_low = _dot(x, wlow_ref[...]) + blow_ref[...]
    g_pre = _dot(g_low.astype(BF16), aw_ref[...]) + ab_ref[...]
    gdec_ref[...] = _log_sigmoid(g_pre) * (1.0 / GLA_TAU)
    zd = _dot(x, wdf_ref[...]) + bdf_ref[...]
    d_q = zd[:, 0:256]
    d_k = zd[:, 256:512]
    d_v = zd[:, 512:768]
    avg = _group_avg(256, DIFF_HD)
    q_n = d_q * lax.rsqrt(_group_mean(d_q * d_q, avg) + EPS) * qg_ref[...]
    dq_ref[...] = q_n
    k_n = d_k * lax.rsqrt(_group_mean(d_k * d_k, avg) + EPS) * kg_ref[...]
    dk_ref[...] = k_n
    dv_ref[...] = d_v

    @pl.when(pl.program_id(0) < prompt_tiles)
    def _transposed():
        qt_ref[...] = q_n.T
        kt_ref[...] = k_n.T
        vt_ref[...] = d_v.T


def _head_call(h, w, *, layer, n_seq, seq_rows):
    rows = h.shape[0]
    per_seq = seq_rows // ROW_TILE
    p_tiles = n_seq * per_seq
    row = lambda n: pl.BlockSpec((ROW_TILE, n), lambda i: (i, 0))

    def col_map(i):
        t = jnp.minimum(i, p_tiles - 1)
        return (t // per_seq, 0, t % per_seq)

    col = pl.BlockSpec((None, 256, ROW_TILE), col_map)
    consts = [w['g1'], w['wg1'], w['wu1'], w['wd1'], w['gm'], w['wlin'], w['blin'], w['wlow'], w['blow'],
              w['aw'], w['ab'], w['wdf'], w['bdf'], w['qg'], w['kg']]
    widths = (D_MODEL, N_LIN, LANE, 256, 256, 256)
    return pl.pallas_call(
        functools.partial(_head_kernel, prompt_tiles=p_tiles),
        grid=(rows // ROW_TILE,),
        in_specs=[row(D_MODEL)] + [_layer_spec(c, layer, single_buffer=True) for c in consts],
        out_specs=[row(n) for n in widths] + [col, col, col],
        out_shape=([jax.ShapeDtypeStruct((rows, n), F32) for n in widths]
                   + [jax.ShapeDtypeStruct((n_seq, 256, seq_rows), F32)] * 3),
        compiler_params=pltpu.CompilerParams(dimension_semantics=("arbitrary",),
                                             vmem_limit_bytes=VMEM_LIMIT),
        name="head",
    )(h, *consts)


def _tail_kernel(h1_ref, brlp_ref, brls_ref, brdp_ref, brds_ref, gm_ref, wgt_ref, bgt_ref, wb_ref, wo_ref, g2_ref,
                 wg_ref, wu_ref, wd_ref, out_ref, *, prompt_tiles):
    h1 = h1_ref[...]
    x = _rms(h1, gm_ref[...]).astype(BF16)
    is_sample = pl.program_id(0) >= prompt_tiles
    merged = None
    for n in range(N_BRANCH):
        if n < 3:
            cols = slice(n * BRANCH_W, (n + 1) * BRANCH_W)
            br_p = brlp_ref[:, :, cols].reshape(ROW_TILE, BRANCH_W)
            br = jnp.where(is_sample, brls_ref[:, cols], br_p)
        else:
            br = jnp.where(is_sample, brds_ref[...], brdp_ref[...])
        gate = _dot(x, wgt_ref[:, n * D_MODEL:(n + 1) * D_MODEL]) + bgt_ref[:, n * D_MODEL:(n + 1) * D_MODEL]
        term = _dot(br.astype(BF16), wb_ref[n]) * _sigmoid(gate)
        merged = term if merged is None else merged + term
    h2 = h1 + _dot(merged.astype(BF16), wo_ref[...])
    out_ref[...] = h2 + 0.5 * _ffn(_rms(h2, g2_ref[...]).astype(BF16), wg_ref, wu_ref, wd_ref)


def _tail_call(h1, brl_p, brl_s, brd_p, brd_s, w, *, layer):
    rows = h1.shape[0]
    p_tiles = brd_p.shape[0] // ROW_TILE
    group, chunk = brl_p.shape[2], brl_p.shape[3]
    per_tile = ROW_TILE // chunk
    per_seq = brl_p.shape[1] // per_tile
    row = lambda n: pl.BlockSpec((ROW_TILE, n), lambda i: (i, 0))
    row_p = lambda n: pl.BlockSpec((ROW_TILE, n), lambda i: (jnp.minimum(i, p_tiles - 1), 0))
    row_s = lambda n: pl.BlockSpec((ROW_TILE, n), lambda i: (jnp.maximum(i - p_tiles, 0), 0))

    def mixer_map(i):
        t = jnp.minimum(i, p_tiles - 1)
        seq = t // per_seq
        return (seq // group, t % per_seq, seq % group, 0, 0)

    mixer = pl.BlockSpec((None, per_tile, None, chunk, 3 * BRANCH_W), mixer_map)
    consts = [w['gm'], w['wgt'], w['bgt'], w['wb'], w['wo'], w['g2'], w['wg2'], w['wu2'], w['wd2']]
    return pl.pallas_call(
        functools.partial(_tail_kernel, prompt_tiles=p_tiles),
        grid=(rows // ROW_TILE,),
        in_specs=[row(D_MODEL), mixer, row_s(3 * BRANCH_W), row_p(BRANCH_W), row_s(BRANCH_W)]
                 + [_layer_spec(c, layer, single_buffer=True) for c in consts],
        out_specs=row(D_MODEL),
        out_shape=jax.ShapeDtypeStruct((rows, D_MODEL), F32),
        compiler_params=pltpu.CompilerParams(dimension_semantics=("arbitrary",),
                                             vmem_limit_bytes=VMEM_LIMIT),
        name="tail",
    )(h1, brl_p, brl_s, brd_p, brd_s, *consts)


def _mix_kernel(*refs, chunk, n_chunks, seq_len, group):
    G = group
    z_refs, gd_refs = refs[0:G], refs[G:2 * G]
    cos_ref, sin_ref = refs[2 * G:2 * G + 2]
    cpast_refs, sr0_refs, sg0_refs = (refs[2 + (2 + k) * G:2 + (3 + k) * G] for k in range(3))
    consts = refs[2 + 5 * G:8 + 5 * G]
    br_ref, sr_out, sg_out, cs_out, sr_s, sg_s, u_s, ush_s = refs[8 + 5 * G:]
    C = chunk
    c = pl.program_id(1)

    @pl.when(c == 0)
    def _init():
        sr_s[...] = jnp.zeros(sr_s.shape, F32)
        sg_s[...] = jnp.zeros(sg_s.shape, F32)
        for g in range(G):
            for h in range(HEADS):
                sr_s[g, h * RET_DK:(h + 1) * RET_DK, h * RET_DV:(h + 1) * RET_DV] = sr0_refs[g][0, h]
                sg_s[g, h * GLA_DK:(h + 1) * GLA_DK, h * GLA_DV:(h + 1) * GLA_DV] = sg0_refs[g][0, h]
            u_s[g, 0:HIST - (CONV_K - 1), :] = jnp.zeros((HIST - (CONV_K - 1), CONV_C), F32)
            u_s[g, HIST - (CONV_K - 1):HIST, :] = cpast_refs[g][0]

    @pl.when(c > 0)
    def _shift():
        for g in range(G):
            u_s[g, 0:HIST, :] = u_s[g, C:C + HIST, :]

    for g in range(G):
        _mix_member(z_refs[g], gd_refs[g], cos_ref, sin_ref, *consts, br_ref.at[g], sr_s.at[g], sg_s.at[g],
                    u_s.at[g], ush_s.at[g], c=c, chunk=chunk, n_chunks=n_chunks, seq_len=seq_len)

    @pl.when(c == n_chunks - 1)
    def _final():
        n_last = seq_len - (n_chunks - 1) * C
        for g in range(G):
            for h in range(HEADS):
                sr_out[g, h] = sr_s[g, h * RET_DK:(h + 1) * RET_DK, h * RET_DV:(h + 1) * RET_DV]
                sg_out[g, h] = sg_s[g, h * GLA_DK:(h + 1) * GLA_DK, h * GLA_DV:(h + 1) * GLA_DV]
            cs_out[g] = u_s[g, HIST + n_last - (CONV_K - 1):HIST + n_last, :]


def _mix_member(z_ref, gd_ref, cos_ref, sin_ref, cw_ref, cb_ref, lng_ref, lnb_ref, rgn_ref, ggn_ref,
                br_ref, sr_s, sg_s, u_s, ush_s, *, c, chunk, n_chunks, seq_len):
    C = chunk
    low = C >= 16
    cast = (lambda a: a.astype(BF16)) if low else (lambda a: a)

    t_col = lax.broadcasted_iota(jnp.int32, (C, 1), 0)
    s_row = lax.broadcasted_iota(jnp.int32, (1, C), 1)
    causal = t_col >= s_row
    tf = t_col.astype(F32)
    padded = n_chunks * C > seq_len
    if padded:
        valid = (c * C + t_col) < seq_len
        nvf = jnp.zeros((1, 1), F32) + jnp.minimum(seq_len - c * C, C).astype(F32)
        keep = lambda a: jnp.where(valid, a, 0.0)
    else:
        nvf = jnp.full((1, 1), float(C), F32)
        keep = lambda a: a

    u = z_ref[:, 0:256] * _sigmoid(z_ref[:, 256:512])
    u_s[HIST:HIST + C, :] = u
    span = C + HIST - 8
    for s in range(1, 8):
        ush_s[s - 1] = u_s[s:s + span, :]
    acc = jnp.zeros((C, CONV_C), F32)
    for j in range(CONV_K):
        lo = HIST - (CONV_K - 1) + j
        q, r = lo - lo % 8, lo % 8
        tap = u_s[q:q + C, :] if r == 0 else ush_s[r - 1, q:q + C, :]
        acc = acc + cw_ref[j:j + 1, :] * tap
    conv = acc + cb_ref[...]
    xc = conv - jnp.mean(conv, axis=-1, keepdims=True)
    ln = xc * lax.rsqrt(jnp.mean(xc * xc, axis=-1, keepdims=True) + EPS) * lng_ref[...] + lnb_ref[...]
    br_ref[:, 0:256] = _silu(ln)

    lane = lax.broadcasted_iota(jnp.int32, (1, 256), 1)
    first_half = (lane % RET_DK) < (RET_DK // 2)
    cos = cos_ref[...]
    sin = sin_ref[...]

    def rope(a):
        swapped = jnp.where(first_half, pltpu.roll(a, 256 - RET_DK // 2, 1), pltpu.roll(a, RET_DK // 2, 1))
        return a * cos + swapped * sin

    rq = rope(z_ref[:, 512:768])
    rk = keep(rope(z_ref[:, 768:1024]) * RET_DK ** -0.5)
    rv = keep(z_ref[:, 1024:1280])
    lane_head = lane // RET_DV
    row_head = lax.broadcasted_iota(jnp.int32, (HEADS * RET_DK, 1), 0) // RET_DK
    log_gamma = [math.log1p(-(2.0 ** (-5 - h))) for h in range(HEADS)]
    lg_lane = jnp.zeros((1, 256), F32)
    lg_row = jnp.zeros((HEADS * RET_DK, 1), F32)
    for h in range(HEADS):
        lg_lane = jnp.where(lane_head == h, log_gamma[h], lg_lane)
        lg_row = jnp.where(row_head == h, log_gamma[h], lg_row)
    avg = _group_avg(256, RET_DV, low)
    dts = (t_col - s_row).astype(F32)
    rk_c = cast(rk)
    o = _dot(cast(rq * jnp.exp((tf + 1.0) * lg_lane)), cast(sr_s[...]))
    for h in range(HEADS):
        own = lane_head == h
        decay = jnp.where(causal, jnp.exp(dts * log_gamma[h]), 0.0)
        a = _dot_nt(cast(jnp.where(own, rq, 0.0)), rk_c) * decay
        o = o + _dot(cast(a), cast(jnp.where(own, rv, 0.0)))
    update = _dot_tn(cast(rk * jnp.exp((nvf - 1.0 - tf) * lg_lane)), cast(rv))
    sr_s[...] = jnp.exp(nvf * lg_row) * sr_s[...] + jnp.where(row_head == lane_head, update, 0.0)
    oc = o - _group_mean(o, avg, low)
    y = oc * lax.rsqrt(_group_mean(oc * oc, avg, low) + EPS) * rgn_ref[...]
    br_ref[:, 256:512] = _silu(z_ref[:, 1280:1536]) * y

    gq = z_ref[:, 1536:1664] * GLA_DK ** -0.5
    gk = keep(z_ref[:, 1664:1792])
    gv = keep(z_ref[:, 1792:2048])
    g = keep(gd_ref[...])
    tri = jnp.where(causal, 1.0, 0.0)
    if low:
        g_hi, g_lo = _split_bf16(g)
        tri = tri.astype(BF16)
        bcum = _dot(tri, g_hi) + _dot(tri, g_lo)
    else:
        bcum = _dot(tri, g)
    mid = C // 2 - 1
    b_mid = bcum[mid:mid + 1, :]
    b_last = bcum[C - 1:C, :]
    q_intra = gq * jnp.exp(bcum - b_mid)
    k_intra = gk * jnp.exp(b_mid - bcum)
    q_inter = gq * jnp.exp(bcum)
    k_state = gk * jnp.exp(b_last - bcum)
    e_last = jnp.where(t_col == C - 1, jnp.exp(bcum), 0.0)
    ones = jnp.ones((C, HEADS * GLA_DV), BF16 if low else F32)
    if low:
        e_hi, e_lo = _split_bf16(e_last)
        e_rows = _dot_tn(e_hi, ones) + _dot_tn(e_lo, ones)
    else:
        e_rows = _dot_tn(e_last, ones)
    key_head = lax.broadcasted_iota(jnp.int32, (1, HEADS * GLA_DK), 1) // GLA_DK
    krow_head = lax.broadcasted_iota(jnp.int32, (HEADS * GLA_DK, 1), 0) // GLA_DK
    k_c = cast(k_intra)
    o = _dot(cast(q_inter), cast(sg_s[...]))
    for h in range(HEADS):
        a = jnp.where(causal, _dot_nt(cast(jnp.where(key_head == h, q_intra, 0.0)), k_c), 0.0)
        o = o + _dot(cast(a), cast(jnp.where(lane_head == h, gv, 0.0)))
    update = _dot_tn(cast(k_state), cast(gv))
    sg_s[...] = e_rows * sg_s[...] + jnp.where(krow_head == lane_head, update, 0.0)
    y = o * lax.rsqrt(_group_mean(o * o, avg, low) + EPS) * ggn_ref[...]
    br_ref[:, 512:768] = _silu(z_ref[:, 2048:2304]) * y

def _mix_call(zlin, gdec, cos, sin, conv_past, ret0, gla0, w, *, layer, state_layer, n_seq, chunk, n_chunks,
              seq_len, row_block0, name):
    G = MIX_GROUP
    assert n_seq % G == 0
    blk = lambda n, g: pl.BlockSpec((chunk, n), lambda s, c: (row_block0 + (s * G + g) * n_chunks + c, 0))
    grouped = lambda shp: pl.BlockSpec((None, G) + shp, lambda s, c: (s, 0) + (0,) * len(shp))
    state = lambda shp, g: pl.BlockSpec((None, 1) + shp,
                                        lambda s, c: (state_layer, s * G + g) + (0,) * len(shp))
    consts = [w['cw'], w['cb'], w['lng'], w['lnb'], w['rgn'], w['ggn']]
    members = range(G)
    in_specs = ([blk(N_LIN, g) for g in members] + [blk(LANE, g) for g in members]
                + [pl.BlockSpec((chunk, 256), lambda s, c: (c, 0)), pl.BlockSpec((chunk, 256), lambda s, c: (c, 0))]
                + [state((CONV_K - 1, CONV_C), g) for g in members]
                + [state((HEADS, RET_DK, RET_DV), g) for g in members]
                + [state((HEADS, GLA_DK, GLA_DV), g) for g in members]
                + [_layer_spec(a, layer) for a in consts])
    args = [zlin] * G + [gdec] * G + [cos, sin] + [conv_past] * G + [ret0] * G + [gla0] * G + consts
    br, sr, sg, cs = pl.pallas_call(
        functools.partial(_mix_kernel, chunk=chunk, n_chunks=n_chunks, seq_len=seq_len, group=G),
        grid=(n_seq // G, n_chunks),
        in_specs=in_specs,
        out_specs=[pl.BlockSpec((None, None, G, chunk, 3 * BRANCH_W), lambda s, c: (s, c, 0, 0, 0)),
                   grouped((HEADS, RET_DK, RET_DV)), grouped((HEADS, GLA_DK, GLA_DV)),
                   grouped((CONV_K - 1, CONV_C))],
        out_shape=[jax.ShapeDtypeStruct((n_seq // G, n_chunks, G, chunk, 3 * BRANCH_W), F32),
                   jax.ShapeDtypeStruct((n_seq // G, G, HEADS, RET_DK, RET_DV), F32),
                   jax.ShapeDtypeStruct((n_seq // G, G, HEADS, GLA_DK, GLA_DV), F32),
                   jax.ShapeDtypeStruct((n_seq // G, G, CONV_K - 1, CONV_C), F32)],
        scratch_shapes=[pltpu.VMEM((G, HEADS * RET_DK, HEADS * RET_DV), F32),
                        pltpu.VMEM((G, HEADS * GLA_DK, HEADS * GLA_DV), F32),
                        pltpu.VMEM((G, HIST + chunk, CONV_C), F32),
                        pltpu.VMEM((G, 7, HIST + chunk - 8, CONV_C), F32)],
        compiler_params=pltpu.CompilerParams(dimension_semantics=("arbitrary", "arbitrary")),
        name=name,
    )(*args)
    flat = lambda a: a.reshape((n_seq,) + a.shape[2:])
    return br, flat(sr), flat(sg), flat(cs)


def _t5_bucket_np(rel):
    rel = np.asarray(rel)
    n = np.maximum(rel, 0)
    max_exact = REL_BUCKETS // 2
    nf = np.maximum(n, 1).astype(np.float64)
    large = max_exact + (np.log(nf / max_exact) / math.log(REL_MAX_DIST / max_exact)
                         * (REL_BUCKETS - max_exact)).astype(np.int64)
    large = np.minimum(large, REL_BUCKETS - 1)
    return np.where(n < max_exact, n, large).astype(np.int32)


def _lambda(lamv_ref, lam_init):
    a = jnp.sum(lamv_ref[0:1, :] * lamv_ref[1:2, :], axis=-1, keepdims=True)
    b = jnp.sum(lamv_ref[2:3, :] * lamv_ref[3:4, :], axis=-1, keepdims=True)
    return jnp.exp(a) - jnp.exp(b) + lam_init


def _flash_kernel(bt_ref, rb_ref, top_ref, lamv_ref, sub_ref, qt_ref, k_ref, vt_ref, o_ref,
                  bias_s, qbd_s, m_s, l_s, acc_s, sta_s, stb_s, lf_s, accf_s, *, layer, lam_init, n_blocks):
    QB = SEQ_BLOCK
    SUB = FAR_BLOCK // QB
    b = pl.program_id(0)
    i = pl.program_id(1)

    @pl.when((b == 0) & (i == 0))
    def _tables():
        for kind, src in ((1, 1), (2, 0)):
            bt = bt_ref[src]
            tiles = [jnp.zeros((QB, QB), F32) for _ in range(HEADS)]
            for bk in range(REL_BUCKETS):
                hit = bt == bk
                for h in range(HEADS):
                    tiles[h] = jnp.where(hit, rb_ref[bk, h] * LOG2E, tiles[h])
            for h in range(HEADS):
                bias_s[kind, h] = jnp.where(bt < 0, NEG, tiles[h])
        for h in range(HEADS):
            bias_s[0, h] = jnp.zeros((QB, QB), F32) + rb_ref[REL_BUCKETS - 1, h] * LOG2E
            bias_s[3, h] = jnp.full((QB, QB), NEG, F32)
            for kind in range(4):
                bias_s[4 + kind, h] = bias_s[kind, h] - top_ref[layer, h]

    qt = qt_ref[...] * (DIFF_HD ** -0.5 * LOG2E)
    row = lax.broadcasted_iota(jnp.int32, (256, 1), 0)
    for hj in range(2 * HEADS):
        own = (row >= hj * DIFF_HD) & (row < (hj + 1) * DIFF_HD)
        qbd_s[:, hj * QB:(hj + 1) * QB] = jnp.where(own, qt, 0.0).astype(BF16)
    m_s[...] = jnp.full(m_s.shape, NEG, F32)
    l_s[...] = jnp.zeros(l_s.shape, F32)
    acc_s[...] = jnp.zeros(acc_s.shape, F32)

    def online(h, st, shift, vt):
        m_old = m_s[h]
        m_new = jnp.maximum(m_old, jnp.max(st, axis=0, keepdims=True) + shift)
        alpha = jnp.exp2(m_old - m_new)
        pt = jnp.exp2(st - (m_new - shift))
        l_s[h] = alpha * l_s[h] + jnp.sum(pt, axis=0, keepdims=True)
        m_s[h] = m_new
        acc_s[h] = alpha * acc_s[h] + _dot(vt[h * DIFF_VD:(h + 1) * DIFF_VD, :], pt.astype(BF16))

    def fixed(h, st, vt):
        pt = jnp.exp2(st)
        l_s[h] = l_s[h] + jnp.sum(pt, axis=0, keepdims=True)
        acc_s[h] = acc_s[h] + _dot(vt[h * DIFF_VD:(h + 1) * DIFF_VD, :], pt.astype(BF16))

    n_far = jnp.maximum(i - 1, 0) // SUB

    def sweep(accumulate):
        def far_body(kb, carry):
            off = pl.multiple_of(kb * FAR_BLOCK, FAR_BLOCK)
            k = k_ref[pl.ds(off, FAR_BLOCK), :].astype(BF16)
            vt = vt_ref[:, pl.ds(off, FAR_BLOCK)].astype(BF16)
            for h in range(HEADS):
                st = _dot(k, qbd_s[:, 2 * h * QB:(2 * h + 2) * QB])
                accumulate(h, st, rb_ref[REL_BUCKETS - 1, h] * LOG2E, vt)
            return carry

        lax.fori_loop(0, n_far, far_body, 0)

        def near_body(kb, carry):
            first = kb * SUB
            blk0 = jnp.minimum(first, n_blocks - SUB)
            off = pl.multiple_of(blk0 * QB, QB)
            k = k_ref[pl.ds(off, FAR_BLOCK), :].astype(BF16)
            vt = vt_ref[:, pl.ds(off, FAR_BLOCK)].astype(BF16)
            kinds = []
            for c in range(SUB):
                blk = blk0 + c
                kind = jnp.where(blk == i, 2, jnp.where(blk == i - 1, 1, 0))
                kinds.append(jnp.where((blk > i) | (blk < first), 3, kind))
            for h in range(HEADS):
                st = _dot(k, qbd_s[:, 2 * h * QB:(2 * h + 2) * QB])
                parts = []
                for c in range(SUB):
                    tile = bias_s[kinds[c], h]
                    parts.append(st[c * QB:(c + 1) * QB, :] + jnp.concatenate([tile, tile], axis=1))
                accumulate(h, jnp.concatenate(parts, axis=0), 0.0, vt)
            return carry

        lax.fori_loop(n_far, i // SUB + 1, near_body, 0)

    bounded = top_ref[layer, HEADS] > 0.5

    def window(j):
        first = j * SUB
        return first, jnp.minimum(first, n_blocks - SUB)

    def score(j, buf):
        _, blk0 = window(j)
        k = k_ref[pl.ds(pl.multiple_of(blk0 * QB, QB), FAR_BLOCK), :].astype(BF16)
        for h in range(HEADS):
            buf[h] = _dot(k, qbd_s[:, 2 * h * QB:(2 * h + 2) * QB])

    def consume(j, buf):
        first, blk0 = window(j)
        vt = vt_ref[:, pl.ds(pl.multiple_of(blk0 * QB, QB), FAR_BLOCK)].astype(BF16)
        kinds = []
        for c in range(SUB):
            blk = blk0 + c
            kind = jnp.where(blk == i, 2, jnp.where(blk == i - 1, 1, 0))
            kinds.append(jnp.where((blk > i) | (blk < first), 3, kind))
        for h in range(HEADS):
            parts = []
            for c in range(SUB):
                tile = bias_s[kinds[c] + 4, h]
                parts.append(buf[h, c * QB:(c + 1) * QB, :] + jnp.concatenate([tile, tile], axis=1))
            fixed(h, jnp.concatenate(parts, axis=0), vt)

    def consume_far(j, buf):
        vt = vt_ref[:, pl.ds(pl.multiple_of(j * FAR_BLOCK, FAR_BLOCK), FAR_BLOCK)].astype(BF16)
        for h in range(HEADS):
            pt = jnp.exp2(buf[h])
            lf_s[h] = lf_s[h] + jnp.sum(pt, axis=0, keepdims=True)
            accf_s[h] = accf_s[h] + _dot(vt[h * DIFF_VD:(h + 1) * DIFF_VD, :], pt.astype(BF16))

    @pl.when(bounded)
    def _fixed():
        n_win = i // SUB + 1
        far_pairs = n_far // 2
        lf_s[...] = jnp.zeros(lf_s.shape, F32)
        accf_s[...] = jnp.zeros(accf_s.shape, F32)
        score(0, sta_s)

        def far_pair(m, carry):
            score(2 * m + 1, stb_s)
            consume_far(2 * m, sta_s)
            score(2 * m + 2, sta_s)
            consume_far(2 * m + 1, stb_s)
            return carry

        lax.fori_loop(0, far_pairs, far_pair, 0)
        w0 = 2 * far_pairs
        rest = n_win - w0

        def pair(m, carry):
            score(w0 + 2 * m + 1, stb_s)
            consume(w0 + 2 * m, sta_s)
            score(w0 + 2 * m + 2, sta_s)
            consume(w0 + 2 * m + 1, stb_s)
            return carry

        lax.fori_loop(0, rest // 2, pair, 0)

        @pl.when(rest % 2 == 1)
        def _tail():
            consume(n_win - 1, sta_s)

        for h in range(HEADS):
            far_scale = jnp.exp2(jnp.zeros((1, 1), F32) + (rb_ref[REL_BUCKETS - 1, h] * LOG2E - top_ref[layer, h]))
            l_s[h] = l_s[h] + far_scale * lf_s[h]
            acc_s[h] = acc_s[h] + far_scale * accf_s[h]

    @pl.when(jnp.logical_not(bounded))
    def _online():
        sweep(online)

    lam = _lambda(lamv_ref, lam_init)
    outs = []
    for h in range(HEADS):
        ot = acc_s[h] / l_s[h]
        d = ot[:, 0:QB] - lam * ot[:, QB:2 * QB]
        y = d * lax.rsqrt(jnp.mean(d * d, axis=0, keepdims=True) + EPS) * sub_ref[...] * (1.0 - lam_init)
        outs.append(y)
    o_ref[...] = jnp.concatenate(outs, axis=0).T


def _score_top(rel_bias, q_gain, k_gain):
    reach = DIFF_HD ** 0.5 * jnp.max(jnp.abs(q_gain * k_gain), axis=-1, keepdims=True)
    top = (reach + jnp.max(rel_bias, axis=0)[None, :]) * LOG2E
    low = (-reach + jnp.min(rel_bias, axis=0)[None, :]) * LOG2E
    ok = jnp.all(top - low < MAX_EXP2_SPAN, axis=-1, keepdims=True).astype(F32)
    return jnp.concatenate([top, ok], axis=-1).astype(F32)


def _flash_call(bt, rel_bias, top, lamv, sub_col, dqt, dk, dvt, *, layer, n_seq, seq_rows, lam_init):
    rows = n_seq * seq_rows
    nq = seq_rows // SEQ_BLOCK
    assert seq_rows >= FAR_BLOCK
    smem = pl.BlockSpec(memory_space=pltpu.SMEM)
    return pl.pallas_call(
        functools.partial(_flash_kernel, layer=layer, lam_init=lam_init, n_blocks=nq),
        grid=(n_seq, nq),
        in_specs=[_whole_spec(bt), smem, smem, _layer_spec(lamv, layer), _layer_spec(sub_col, layer),
                  pl.BlockSpec((None, 256, SEQ_BLOCK), lambda b, i: (b, 0, i)),
                  pl.BlockSpec((seq_rows, 256), lambda b, i: (b, 0)),
                  pl.BlockSpec((None, 256, seq_rows), lambda b, i: (b, 0, 0))],
        out_specs=pl.BlockSpec((SEQ_BLOCK, 256), lambda b, i: (b * nq + i, 0)),
        out_shape=jax.ShapeDtypeStruct((rows, 256), F32),
        scratch_shapes=[pltpu.VMEM((8, HEADS, SEQ_BLOCK, SEQ_BLOCK), F32),
                        pltpu.VMEM((256, 2 * HEADS * SEQ_BLOCK), BF16),
                        pltpu.VMEM((HEADS, 1, 2 * SEQ_BLOCK), F32), pltpu.VMEM((HEADS, 1, 2 * SEQ_BLOCK), F32),
                        pltpu.VMEM((HEADS, DIFF_VD, 2 * SEQ_BLOCK), F32),
                        pltpu.VMEM((HEADS, FAR_BLOCK, 2 * SEQ_BLOCK), F32),
                        pltpu.VMEM((HEADS, FAR_BLOCK, 2 * SEQ_BLOCK), F32),
                        pltpu.VMEM((HEADS, 1, 2 * SEQ_BLOCK), F32),
                        pltpu.VMEM((HEADS, DIFF_VD, 2 * SEQ_BLOCK), F32)],
        compiler_params=pltpu.CompilerParams(dimension_semantics=("arbitrary", "arbitrary"),
                                             vmem_limit_bytes=VMEM_LIMIT),
        name="attn_prompt",
    )(bt, rel_bias, top, lamv, sub_col, dqt, dk, dvt)


def _decode_kernel(pt_ref, bt_ref, rb_ref, lamv_ref, sub_ref, q_ref, kn_ref, vn_ref, *refs,
                   pages_per_step, n_steps, dec_seq, lam_init):
    G = pages_per_step
    kt_refs = refs[:G]
    vt_refs = refs[G:2 * G]
    o_ref, bias_pg_s, bias_new_s, far_s, qs_s, m_s, l_s, acc_s = refs[2 * G:]
    del pt_ref
    b = pl.program_id(0)
    g = pl.program_id(1)
    last = g == n_steps - 1
    NQ = 2 * HEADS * dec_seq
    seqs_per_block = kn_ref.shape[0] // dec_seq

    row_head = lax.broadcasted_iota(jnp.int32, (NQ, 1), 0) // (2 * dec_seq)

    def bias_rows(bk):
        out = jnp.zeros((NQ, 1), F32)
        for h in range(HEADS):
            out = jnp.where(row_head == h, rb_ref[bk, h], out)
        return out

    @pl.when((b == 0) & (g == 0))
    def _tables():
        for src, dst in ((0, bias_pg_s), (1, bias_new_s)):
            bt = bt_ref[src]
            tile = jnp.zeros(bt.shape, F32)
            for bk in range(REL_BUCKETS):
                tile = jnp.where(bt == bk, bias_rows(bk), tile)
            dst[...] = jnp.where(bt < 0, NEG, tile)
        far_s[...] = jnp.zeros(far_s.shape, F32) + bias_rows(REL_BUCKETS - 1)

    @pl.when(g == 0)
    def _init():
        q = q_ref[...] * DIFF_HD ** -0.5
        lane = lax.broadcasted_iota(jnp.int32, (1, 256), 1)
        for hj in range(2 * HEADS):
            own = (lane >= hj * DIFF_HD) & (lane < (hj + 1) * DIFF_HD)
            qs_s[hj * dec_seq:(hj + 1) * dec_seq, :] = jnp.where(own, q, 0.0)
        m_s[...] = jnp.full(m_s.shape, NEG, F32)
        l_s[...] = jnp.zeros(l_s.shape, F32)
        acc_s[...] = jnp.zeros(acc_s.shape, F32)

    def update(scores, pv_fn):
        m_old = m_s[...]
        s_max = scores[0]
        for s in scores[1:]:
            s_max = jnp.maximum(s_max, s)
        m_new = jnp.maximum(m_old, jnp.max(s_max, axis=1, keepdims=True))
        alpha = jnp.exp(m_old - m_new)
        probs = [jnp.exp(s - m_new) for s in scores]
        p_sum = probs[0]
        for p in probs[1:]:
            p_sum = p_sum + p
        l_s[...] = alpha * l_s[...] + jnp.sum(p_sum, axis=1, keepdims=True)
        m_s[...] = m_new
        acc_s[...] = jnp.concatenate([alpha, alpha], axis=1) * acc_s[...] + pv_fn(probs)

    qs = qs_s[...].astype(BF16)
    scores = []
    for p in range(G):
        s = _dot(qs, kt_refs[p][...].astype(BF16))
        if p == G - 1:
            s = s + jnp.where(last, bias_pg_s[...], far_s[...])
        else:
            s = s + far_s[...]
        scores.append(s)

    def pv_pages(probs):
        out = None
        for p in range(G):
            t = _dot_nt(probs[p].astype(BF16), vt_refs[p][...].astype(BF16))
            out = t if out is None else out + t
        return out

    update(scores, pv_pages)

    @pl.when(last)
    def _finish():
        key_seq = lax.broadcasted_iota(jnp.int32, (1, kn_ref.shape[0]), 1) // dec_seq
        s_new = _dot_nt(qs, kn_ref[...].astype(BF16))
        s_new = jnp.where(key_seq == b % seqs_per_block, s_new + bias_new_s[...], NEG)
        update([s_new], lambda probs: _dot(probs[0].astype(BF16), vn_ref[...].astype(BF16)))
        l_all = l_s[...]
        o_all = acc_s[...] / jnp.concatenate([l_all, l_all], axis=1)
        lam = _lambda(lamv_ref, lam_init)
        for h in range(HEADS):
            r1 = (2 * h) * dec_seq
            r2 = (2 * h + 1) * dec_seq
            cols = slice(h * DIFF_VD, (h + 1) * DIFF_VD)
            d = o_all[r1:r1 + dec_seq, cols] - lam * o_all[r2:r2 + dec_seq, cols]
            y = d * lax.rsqrt(jnp.mean(d * d, axis=-1, keepdims=True) + EPS) * sub_ref[...] * (1.0 - lam_init)
            o_ref[:, cols] = y


def _decode_call(page_table, bt, rel_bias, lamv, sub, dq, dk, dv, cache_kt, cache_vt, *,
                 layer, n_seq, dec_seq, row0, lam_init):
    n_pages = page_table.shape[1]
    page = cache_kt.shape[3]
    G = min(PAGES_PER_STEP, n_pages)
    n_steps = n_pages // G
    nq = 2 * HEADS * dec_seq
    new_rows = LANE
    per_blk = new_rows // dec_seq
    own = pl.BlockSpec((dec_seq, 256), lambda b, g, pt: (row0 // dec_seq + b, 0))
    new = pl.BlockSpec((new_rows, 256), lambda b, g, pt: (row0 // new_rows + b // per_blk, 0))

    def page_spec(p):
        return pl.BlockSpec((None, None, 256, page),
                            lambda b, g, pt: (layer, pt[b * n_pages + g * G + p], 0, 0))

    in_specs = ([_whole_spec(bt), pl.BlockSpec(memory_space=pltpu.SMEM), _layer_spec(lamv, layer),
                 _layer_spec(sub, layer), own, new, new]
                + [page_spec(p) for p in range(G)] + [page_spec(p) for p in range(G)])
    args = [bt, rel_bias, lamv, sub, dq, dk, dv] + [cache_kt] * G + [cache_vt] * G
    return pl.pallas_call(
        functools.partial(_decode_kernel, pages_per_step=G, n_steps=n_steps, dec_seq=dec_seq, lam_init=lam_init),
        grid_spec=pltpu.PrefetchScalarGridSpec(
            num_scalar_prefetch=1,
            grid=(n_seq, n_steps),
            in_specs=in_specs,
            out_specs=pl.BlockSpec((dec_seq, 256), lambda b, g, pt: (b, 0)),
            scratch_shapes=[pltpu.VMEM((nq, page), F32), pltpu.VMEM((nq, new_rows), F32),
                            pltpu.VMEM((nq, page), F32), pltpu.VMEM((nq, 256), F32),
                            pltpu.VMEM((nq, LANE), F32), pltpu.VMEM((nq, LANE), F32), pltpu.VMEM((nq, 256), F32)]),
        out_shape=jax.ShapeDtypeStruct((n_seq * dec_seq, 256), F32),
        compiler_params=pltpu.CompilerParams(dimension_semantics=("arbitrary", "arbitrary"),
                                             vmem_limit_bytes=VMEM_LIMIT),
        name="attn_sample",
    )(page_table.reshape(-1), *args)


def _rope_tables(pos):
    half = RET_DK // 2
    inv = ROPE_BASE ** (-np.arange(half, dtype=np.float64) / half)
    ang = np.asarray(pos, np.float64)[:, None] * inv[None, :]
    cos = np.concatenate([np.cos(ang), np.cos(ang)], axis=1)
    sin = np.concatenate([-np.sin(ang), np.sin(ang)], axis=1)
    return (jnp.asarray(np.tile(cos, (1, HEADS)), F32), jnp.asarray(np.tile(sin, (1, HEADS)), F32))


def _prompt_bucket_tiles():
    t = np.arange(SEQ_BLOCK)
    rel = t[None, :] - t[:, None]
    diag = np.where(rel >= 0, _t5_bucket_np(rel), -1)
    sub = _t5_bucket_np(rel + SEQ_BLOCK)
    return jnp.asarray(np.stack([diag, sub]).astype(np.int32))


def _sample_bucket_tiles(page, dec_seq):
    assert page == LANE
    iq = (np.arange(2 * HEADS * dec_seq) % dec_seq)[:, None]
    past = _t5_bucket_np(page + iq - np.arange(page)[None, :])
    rel_new = iq - (np.arange(LANE) % dec_seq)[None, :]
    new = np.where(rel_new >= 0, _t5_bucket_np(rel_new), -1)
    return jnp.asarray(np.stack([past, new]).astype(np.int32))


def kernel(x_prompt, x_sample, cache_k, cache_v, page_table, state_ret, state_gla, state_conv, meta_tokens,
           rel_bias, norm_ffn1, ffn1_gate, ffn1_up, ffn1_down, norm_mix, w_in, b_in, conv_w, conv_b, conv_ln_g,
           conv_ln_b, ret_gn, gla_alpha_w, gla_alpha_b, gla_gn, q_norm, k_norm, lam_q1, lam_k1, lam_q2, lam_k2,
           diff_subln, w_branch, w_out, norm_ffn2, ffn2_gate, ffn2_up, ffn2_down):
    B, S, D = x_prompt.shape
    DB, DS, _ = x_sample.shape
    depth = w_in.shape[0]
    L = S + N_META
    Lp = -(-L // SEQ_BLOCK) * SEQ_BLOCK
    n_chunks = Lp // SEQ_BLOCK
    n_pool, page = cache_k.shape[1], cache_k.shape[2]
    past_len = page_table.shape[1] * page
    rows_p = B * Lp
    rows_s = DB * DS
    rows_sp = -(-rows_s // ROW_TILE) * ROW_TILE
    assert Lp % ROW_TILE == 0 and rows_s % LANE == 0 and DS % 8 == 0

    pieces = []
    for b in range(B):
        pieces += [meta_tokens.astype(F32), x_prompt[b], jnp.zeros((Lp - L, D), F32)]
    pieces += [x_sample.reshape(rows_s, D), jnp.zeros((rows_sp - rows_s, D), F32)]
    h = jnp.concatenate(pieces, axis=0)
    pad_s = lambda a: jnp.pad(a, ((0, rows_sp - rows_s), (0, 0)))

    cos_p, sin_p = _rope_tables(np.arange(Lp))
    cos_s, sin_s = _rope_tables(past_len + np.arange(DS))
    bt_prompt = _prompt_bucket_tiles()
    bt_sample = _sample_bucket_tiles(page, DS)
    ckt = jnp.transpose(cache_k, (0, 1, 3, 4, 2)).reshape(depth, n_pool, HEADS * 2 * DIFF_HD, page)
    cvt = jnp.transpose(cache_v, (0, 1, 3, 4, 2)).reshape(depth, n_pool, HEADS * DIFF_VD, page)
    zeros_conv = jnp.zeros((1, B, CONV_K - 1, CONV_C), F32)
    zeros_ret = jnp.zeros((1, B, HEADS, RET_DK, RET_DV), F32)
    zeros_gla = jnp.zeros((1, B, HEADS, GLA_DK, GLA_DV), F32)

    row3 = lambda a: a.reshape(depth, 1, -1).astype(F32)
    w = dict(
        g1=row3(norm_ffn1), wg1=ffn1_gate.astype(BF16), wu1=ffn1_up.astype(BF16), wd1=ffn1_down.astype(BF16),
        gm=row3(norm_mix),
        wlin=w_in[:, :, :N_LIN].astype(BF16), blin=row3(b_in[:, :N_LIN]),
        wlow=w_in[:, :, OFF_GLOW:OFF_DIFF].astype(BF16), blow=row3(b_in[:, OFF_GLOW:OFF_DIFF]),
        aw=gla_alpha_w.astype(BF16), ab=row3(gla_alpha_b),
        wdf=w_in[:, :, OFF_DIFF:OFF_GATES].astype(BF16), bdf=row3(b_in[:, OFF_DIFF:OFF_GATES]),
        qg=row3(jnp.tile(q_norm, (1, 2 * HEADS))), kg=row3(jnp.tile(k_norm, (1, 2 * HEADS))),
        wgt=w_in[:, :, OFF_GATES:].astype(BF16), bgt=row3(b_in[:, OFF_GATES:]),
        wb=w_branch.astype(BF16), wo=w_out.astype(BF16), g2=row3(norm_ffn2),
        wg2=ffn2_gate.astype(BF16), wu2=ffn2_up.astype(BF16), wd2=ffn2_down.astype(BF16),
        cw=conv_w, cb=row3(conv_b), lng=row3(conv_ln_g), lnb=row3(conv_ln_b),
        rgn=row3(ret_gn), ggn=row3(gla_gn),
    )
    lamv = jnp.stack([lam_q1, lam_k1, lam_q2, lam_k2], axis=1).astype(F32)
    sub = row3(diff_subln)
    sub_col = diff_subln.reshape(depth, -1, 1).astype(F32)
    top = _score_top(rel_bias, q_norm, k_norm)

    per_layer = []
    for l in range(depth):
        lam_init = 0.8 - 0.6 * math.exp(-0.3 * l)
        h1, zlin, gdec, dq, dk, dv, qt, kt, vt = _head_call(h, w, layer=l, n_seq=B, seq_rows=Lp)

        brl_p, rp, gp, cp = _mix_call(zlin, gdec, cos_p, sin_p, zeros_conv, zeros_ret, zeros_gla, w,
                                      layer=l, state_layer=0, n_seq=B, chunk=SEQ_BLOCK, n_chunks=n_chunks,
                                      seq_len=L, row_block0=0, name="mix_prompt")
        brl_s, rs, gs, cs = _mix_call(zlin, gdec, cos_s, sin_s, state_conv, state_ret, state_gla, w,
                                      layer=l, state_layer=l, n_seq=DB, chunk=DS, n_chunks=1, seq_len=DS,
                                      row_block0=rows_p // DS, name="mix_sample")
        brd_p = _flash_call(bt_prompt, rel_bias, top, lamv, sub_col, qt, dk, vt,
                            layer=l, n_seq=B, seq_rows=Lp, lam_init=lam_init)
        brd_s = _decode_call(page_table, bt_sample, rel_bias, lamv, sub, dq, dk, dv, ckt, cvt,
                             layer=l, n_seq=DB, dec_seq=DS, row0=rows_p, lam_init=lam_init)

        h = _tail_call(h1, brl_p, pad_s(brl_s.reshape(rows_s, -1)), brd_p, pad_s(brd_s), w, layer=l)
        per_layer.append(dict(kt=kt, vt=vt, ks=dk[rows_p:rows_p + rows_s], vs=dv[rows_p:rows_p + rows_s],
                              rp=rp, rs=rs, gp=gp, gs=gs, cp=cp, cs=cs))

    st = lambda k: jnp.stack([p[k] for p in per_layer], axis=0)
    seq_major = lambda t: jnp.transpose(t.reshape(depth, B, HEADS, -1, Lp)[..., :L], (0, 1, 4, 2, 3))
    y_prompt = jnp.stack([h[b * Lp + N_META:b * Lp + L] for b in range(B)], axis=0)
    y_sample = h[rows_p:rows_p + rows_s].reshape(DB, DS, D)
    return (y_prompt, y_sample, seq_major(st('kt')), seq_major(st('vt')),
            st('ks').reshape(depth, DB, DS, HEADS, 2 * DIFF_HD), st('vs').reshape(depth, DB, DS, HEADS, DIFF_VD),
            st('rp'), st('rs'), st('gp'), st('gs'), st('cp'), st('cs'))
```

```python
import functools
import math

import numpy as np
import jax
import jax.numpy as jnp
from jax import lax
from jax.experimental import pallas as pl
from jax.experimental.pallas import tpu as pltpu

F32 = jnp.float32
BF16 = jnp.bfloat16

D_MODEL = 1024
N_META = 16
N_BRANCH = 4
BRANCH_W = 256
D_FF = 2816
CONV_C = 256
CONV_K = 31
HEADS = 4
RET_DK = 64
RET_DV = 64
GLA_DK = 32
GLA_DV = 64
GLA_RANK = 16
GLA_TAU = 16.0
DIFF_HD = 32
DIFF_VD = 64
REL_BUCKETS = 32
REL_MAX_DIST = 128
ROPE_BASE = 10000.0
EPS = 1e-6
NEG = -1e30
LOG2E = math.log2(math.e)
MAX_EXP2_SPAN = 100.0

N_LIN = 2304
OFF_GLOW = N_LIN
OFF_DIFF = OFF_GLOW + GLA_RANK
OFF_GATES = OFF_DIFF + 3 * 256
N_IN = OFF_GATES + N_BRANCH * D_MODEL

LANE = 128
ROW_TILE = 384
FFN_CHUNKS = 1
MXU_N = 256
SEQ_BLOCK = 128
FAR_BLOCK = 512
HIST = 32
VMEM_LIMIT = 60 * 1024 * 1024
MIX_GROUP = 4
PAGES_PER_STEP = 64


def _sigmoid(x):
    return 1.0 / (1.0 + jnp.exp(-x))


def _silu(x):
    return x * _sigmoid(x)


def _log_sigmoid(x):
    return jnp.minimum(x, 0.0) - jnp.log1p(jnp.exp(-jnp.abs(x)))


def _rms(x, g):
    return x * lax.rsqrt(jnp.mean(x * x, axis=-1, keepdims=True) + EPS) * g


def _dot(a, b):
    return jnp.dot(a, b, preferred_element_type=F32)


def _dot_nt(a, b):
    return lax.dot_general(a, b, (((1,), (1,)), ((), ())), preferred_element_type=F32)


def _dot_tn(a, b):
    return lax.dot_general(a, b, (((0,), (0,)), ((), ())), preferred_element_type=F32)


def _split_bf16(x):
    hi = x.astype(BF16)
    lo = (x - hi.astype(F32)).astype(BF16)
    return hi, lo


def _group_avg(n, group, low=True):
    r = lax.broadcasted_iota(jnp.int32, (n, n), 0) // group
    c = lax.broadcasted_iota(jnp.int32, (n, n), 1) // group
    return jnp.where(r == c, 1.0 / group, 0.0).astype(BF16 if low else F32)


def _group_mean(x, avg, low=True):
    if not low:
        return _dot(x, avg)
    hi, lo = _split_bf16(x)
    return _dot(hi, avg) + _dot(lo, avg)


def _layer_spec(a, layer, single_buffer=False):
    nd = a.ndim - 1
    kw = dict(pipeline_mode=pl.Buffered(1)) if single_buffer else {}
    return pl.BlockSpec((None,) + a.shape[1:], lambda *_: (layer,) + (0,) * nd, **kw)


def _whole_spec(a):
    return pl.BlockSpec(a.shape, lambda *_: (0,) * a.ndim)


def _ffn(x, wg_ref, wu_ref, wd_ref):
    tiles = D_FF // MXU_N
    edges = [(tiles * c // FFN_CHUNKS) * MXU_N for c in range(FFN_CHUNKS + 1)]
    out = None
    for c in range(FFN_CHUNKS):
        cols = slice(edges[c], edges[c + 1])
        act = (_silu(_dot(x, wg_ref[:, cols])) * _dot(x, wu_ref[:, cols])).astype(BF16)
        part = _dot(act, wd_ref[cols, :])
        out = part if out is None else out + part
    return out


def _assemble_rows(xcur_ref, xprev_ref, xlast_ref, xs_ref, meta_ref, *, prompt_tiles, per_seq):
    i = pl.program_id(0)
    j = i % per_seq
    cut = ROW_TILE - N_META
    prev_tail = xprev_ref[cut:, :]
    cur_head = xcur_ref[0:cut, :]
    x_last = xlast_ref[...]
    pad = jnp.zeros((cut - x_last.shape[0], D_MODEL), F32)
    body = jnp.concatenate([prev_tail, cur_head], axis=0)
    first = jnp.concatenate([meta_ref[...], cur_head], axis=0)
    last = jnp.concatenate([prev_tail, x_last, pad], axis=0)
    xs = xs_ref[...]
    sample = jnp.concatenate([xs, jnp.zeros((ROW_TILE - xs.shape[0], D_MODEL), F32)], axis=0)
    rows = jnp.where(j == 0, first, jnp.where(j == per_seq - 1, last, body))
    return jnp.where(i >= prompt_tiles, sample, rows)


def _head_kernel(*refs, prompt_tiles, per_seq, first_layer):
    n_in = 5 if first_layer else 1
    (g1_ref, wg_ref, wu_ref, wd_ref, gm_ref, wlin_ref, blin_ref, wlow_ref, blow_ref,
     aw_ref, ab_ref, wdf_ref, bdf_ref, qg_ref, kg_ref,
     h1_ref, zlin_ref, gdec_ref, dq_ref, dk_ref, dv_ref, qt_ref, kt_ref, vt_ref) = refs[n_in:]
    if first_layer:
        h = _assemble_rows(*refs[:n_in], prompt_tiles=prompt_tiles, per_seq=per_seq)
    else:
        h = refs[0][...]
    h1 = h + 0.5 * _ffn(_rms(h, g1_ref[...]).astype(BF16), wg_ref, wu_ref, wd_ref)
    h1_ref[...] = h1
    x = _rms(h1, gm_ref[...]).astype(BF16)
    zlin_ref[...] = _dot(x, wlin_ref[...]) + blin_ref[...]
    g_low = _dot(x, wlow_ref[...]) + blow_ref[...]
    g_pre = _dot(g_low.astype(BF16), aw_ref[...]) + ab_ref[...]
    gdec_ref[...] = _log_sigmoid(g_pre) * (1.0 / GLA_TAU)
    zd = _dot(x, wdf_ref[...]) + bdf_ref[...]
    d_q = zd[:, 0:256]
    d_k = zd[:, 256:512]
    d_v = zd[:, 512:768]
    avg = _group_avg(256, DIFF_HD)
    q_n = d_q * lax.rsqrt(_group_mean(d_q * d_q, avg) + EPS) * qg_ref[...]
    dq_ref[...] = q_n
    k_n = d_k * lax.rsqrt(_group_mean(d_k * d_k, avg) + EPS) * kg_ref[...]
    dk_ref[...] = k_n
    dv_ref[...] = d_v

    @pl.when(pl.program_id(0) < prompt_tiles)
    def _transposed():
        qt_ref[...] = q_n.T
        kt_ref[...] = k_n.T
        vt_ref[...] = d_v.T


def _head_call(h, w, *, layer, n_seq, seq_rows, rows, model_inputs=None):
    per_seq = seq_rows // ROW_TILE
    p_tiles = n_seq * per_seq
    row = lambda n: pl.BlockSpec((ROW_TILE, n), lambda i: (i, 0))
    if h is not None:
        data, data_specs = [h], [row(D_MODEL)]
    else:
        x_prompt, x_rows, meta = model_inputs
        seq = x_prompt.shape[1]
        x_last = seq - ((per_seq - 1) * ROW_TILE)
        assert rows == (p_tiles + 1) * ROW_TILE and x_rows.shape[0] <= ROW_TILE and per_seq >= 2
        assert 0 < x_last <= ROW_TILE - N_META and x_last % 8 == 0 and ((per_seq - 1) * ROW_TILE) % x_last == 0

        def seq_tile(i):
            t = jnp.minimum(i, p_tiles - 1)
            return t // per_seq, t % per_seq

        cur = lambda i: (seq_tile(i)[0], jnp.minimum(seq_tile(i)[1], per_seq - 2), 0)
        prev = lambda i: (seq_tile(i)[0], jnp.maximum(seq_tile(i)[1] - 1, 0), 0)
        last = lambda i: (seq_tile(i)[0], (per_seq - 1) * ROW_TILE // x_last, 0)
        data = [x_prompt, x_prompt, x_prompt, x_rows, meta]
        data_specs = [pl.BlockSpec((None, ROW_TILE, D_MODEL), cur), pl.BlockSpec((None, ROW_TILE, D_MODEL), prev),
                      pl.BlockSpec((None, x_last, D_MODEL), last), _whole_spec(x_rows), _whole_spec(meta)]

    def col_map(i):
        t = jnp.minimum(i, p_tiles - 1)
        return (t // per_seq, 0, t % per_seq)

    col = pl.BlockSpec((None, 256, ROW_TILE), col_map)
    consts = [w['g1'], w['wg1'], w['wu1'], w['wd1'], w['gm'], w['wlin'], w['blin'], w['wlow'], w['blow'],
              w['aw'], w['ab'], w['wdf'], w['bdf'], w['qg'], w['kg']]
    widths = (D_MODEL, N_LIN, LANE, 256, 256, 256)
    return pl.pallas_call(
        functools.partial(_head_kernel, prompt_tiles=p_tiles, per_seq=per_seq, first_layer=h is None),
        grid=(rows // ROW_TILE,),
        in_specs=data_specs + [_layer_spec(c, layer, single_buffer=True) for c in consts],
        out_specs=[row(n) for n in widths] + [col, col, col],
        out_shape=([jax.ShapeDtypeStruct((rows, n), F32) for n in widths]
                   + [jax.ShapeDtypeStruct((n_seq, 256, seq_rows), F32)] * 3),
        compiler_params=pltpu.CompilerParams(dimension_semantics=("arbitrary",),
                                             vmem_limit_bytes=VMEM_LIMIT),
        name="head",
    )(*data, *consts)


def _tail_kernel(h1_ref, brlp_ref, brls_ref, brdp_ref, brds_ref, gm_ref, wgt_ref, bgt_ref, wb_ref, wo_ref, g2_ref,
                 wg_ref, wu_ref, wd_ref, *outs, prompt_tiles, last_layer):
    h1 = h1_ref[...]
    x = _rms(h1, gm_ref[...]).astype(BF16)
    is_sample = pl.program_id(0) >= prompt_tiles
    merged = None
    for n in range(N_BRANCH):
        if n < 3:
            cols = slice(n * BRANCH_W, (n + 1) * BRANCH_W)
            br_p = brlp_ref[:, :, cols].reshape(ROW_TILE, BRANCH_W)
            br = jnp.where(is_sample, brls_ref[:, cols], br_p)
        else:
            br = jnp.where(is_sample, brds_ref[...], brdp_ref[...])
        gate = _dot(x, wgt_ref[:, n * D_MODEL:(n + 1) * D_MODEL]) + bgt_ref[:, n * D_MODEL:(n + 1) * D_MODEL]
        term = _dot(br.astype(BF16), wb_ref[n]) * _sigmoid(gate)
        merged = term if merged is None else merged + term
    h2 = h1 + _dot(merged.astype(BF16), wo_ref[...])
    h3 = h2 + 0.5 * _ffn(_rms(h2, g2_ref[...]).astype(BF16), wg_ref, wu_ref, wd_ref)
    if not last_layer:
        outs[0][...] = h3
        return

    y_ref, ys_ref, carry_s = outs
    cut = ROW_TILE - N_META

    @pl.when(pl.program_id(0) == 0)
    def _no_previous_tile():
        carry_s[...] = jnp.zeros(carry_s.shape, F32)

    y_ref[0:cut, :] = carry_s[...]
    y_ref[cut:ROW_TILE, :] = h3[0:N_META, :]
    carry_s[...] = h3[N_META:, :]

    @pl.when(is_sample)
    def _sample_rows():
        ys_ref[...] = h3[0:ys_ref.shape[0], :]


def _tail_call(h1, brl_p, brl_s, brd_p, brd_s, w, *, layer, final=None):
    rows = h1.shape[0]
    p_tiles = brd_p.shape[0] // ROW_TILE
    group, chunk = brl_p.shape[2], brl_p.shape[3]
    per_tile = ROW_TILE // chunk
    per_seq = brl_p.shape[1] // per_tile
    row = lambda n: pl.BlockSpec((ROW_TILE, n), lambda i: (i, 0))
    row_p = lambda n: pl.BlockSpec((ROW_TILE, n), lambda i: (jnp.minimum(i, p_tiles - 1), 0))
    row_s = lambda n: pl.BlockSpec((ROW_TILE, n), lambda i: (jnp.maximum(i - p_tiles, 0), 0))

    def mixer_map(i):
        t = jnp.minimum(i, p_tiles - 1)
        seq = t // per_seq
        return (seq // group, t % per_seq, seq % group, 0, 0)

    mixer = pl.BlockSpec((None, per_tile, None, chunk, 3 * BRANCH_W), mixer_map)
    consts = [w['gm'], w['wgt'], w['bgt'], w['wb'], w['wo'], w['g2'], w['wg2'], w['wu2'], w['wd2']]
    if final is None:
        out_specs = row(D_MODEL)
        out_shape = jax.ShapeDtypeStruct((rows, D_MODEL), F32)
        scratch = []
    else:
        n_seq, seq, rows_s = final
        assert rows == (p_tiles + 1) * ROW_TILE and rows_s <= ROW_TILE

        def prev_tile(i):
            t = jnp.minimum(jnp.maximum(i - 1, 0), p_tiles - 1)
            return (t // per_seq, t % per_seq, 0)

        out_specs = [pl.BlockSpec((None, ROW_TILE, D_MODEL), prev_tile),
                     pl.BlockSpec((rows_s, D_MODEL), lambda i: (0, 0))]
        out_shape = [jax.ShapeDtypeStruct((n_seq, seq, D_MODEL), F32),
                     jax.ShapeDtypeStruct((rows_s, D_MODEL), F32)]
        scratch = [pltpu.VMEM((ROW_TILE - N_META, D_MODEL), F32)]
    return pl.pallas_call(
        functools.partial(_tail_kernel, prompt_tiles=p_tiles, last_layer=final is not None),
        grid=(rows // ROW_TILE,),
        in_specs=[row(D_MODEL), mixer, row_s(3 * BRANCH_W), row_p(BRANCH_W), row_s(BRANCH_W)]
                 + [_layer_spec(c, layer, single_buffer=True) for c in consts],
        out_specs=out_specs,
        out_shape=out_shape,
        scratch_shapes=scratch,
        compiler_params=pltpu.CompilerParams(dimension_semantics=("arbitrary",),
                                             vmem_limit_bytes=VMEM_LIMIT),
        name="tail",
    )(h1, brl_p, brl_s, brd_p, brd_s, *consts)


def _mix_kernel(*refs, chunk, n_chunks, seq_len, group):
    G = group
    z_refs, gd_refs = refs[0:G], refs[G:2 * G]
    cos_ref, sin_ref = refs[2 * G:2 * G + 2]
    cpast_refs, sr0_refs, sg0_refs = (refs[2 + (2 + k) * G:2 + (3 + k) * G] for k in range(3))
    consts = refs[2 + 5 * G:8 + 5 * G]
    br_ref, sr_out, sg_out, cs_out, sr_s, sg_s, u_s, ush_s = refs[8 + 5 * G:]
    C = chunk
    c = pl.program_id(1)

    @pl.when(c == 0)
    def _init():
        sr_s[...] = jnp.zeros(sr_s.shape, F32)
        sg_s[...] = jnp.zeros(sg_s.shape, F32)
        for g in range(G):
            for h in range(HEADS):
                sr_s[g, h * RET_DK:(h + 1) * RET_DK, h * RET_DV:(h + 1) * RET_DV] = sr0_refs[g][0, h]
                sg_s[g, h * GLA_DK:(h + 1) * GLA_DK, h * GLA_DV:(h + 1) * GLA_DV] = sg0_refs[g][0, h]
            u_s[g, 0:HIST - (CONV_K - 1), :] = jnp.zeros((HIST - (CONV_K - 1), CONV_C), F32)
            u_s[g, HIST - (CONV_K - 1):HIST, :] = cpast_refs[g][0]

    @pl.when(c > 0)
    def _shift():
        for g in range(G):
            u_s[g, 0:HIST, :] = u_s[g, C:C + HIST, :]

    for g in range(G):
        _mix_member(z_refs[g], gd_refs[g], cos_ref, sin_ref, *consts, br_ref.at[g], sr_s.at[g], sg_s.at[g],
                    u_s.at[g], ush_s.at[g], c=c, chunk=chunk, n_chunks=n_chunks, seq_len=seq_len)

    @pl.when(c == n_chunks - 1)
    def _final():
        n_last = seq_len - (n_chunks - 1) * C
        for g in range(G):
            for h in range(HEADS):
                sr_out[g, h] = sr_s[g, h * RET_DK:(h + 1) * RET_DK, h * RET_DV:(h + 1) * RET_DV]
                sg_out[g, h] = sg_s[g, h * GLA_DK:(h + 1) * GLA_DK, h * GLA_DV:(h + 1) * GLA_DV]
            cs_out[g] = u_s[g, HIST + n_last - (CONV_K - 1):HIST + n_last, :]


def _mix_member(z_ref, gd_ref, cos_ref, sin_ref, cw_ref, cb_ref, lng_ref, lnb_ref, rgn_ref, ggn_ref,
                br_ref, sr_s, sg_s, u_s, ush_s, *, c, chunk, n_chunks, seq_len):
    C = chunk
    low = C >= 16
    cast = (lambda a: a.astype(BF16)) if low else (lambda a: a)

    t_col = lax.broadcasted_iota(jnp.int32, (C, 1), 0)
    s_row = lax.broadcasted_iota(jnp.int32, (1, C), 1)
    causal = t_col >= s_row
    tf = t_col.astype(F32)
    padded = n_chunks * C > seq_len
    if padded:
        valid = (c * C + t_col) < seq_len
        nvf = jnp.zeros((1, 1), F32) + jnp.minimum(seq_len - c * C, C).astype(F32)
        keep = lambda a: jnp.where(valid, a, 0.0)
    else:
        nvf = jnp.full((1, 1), float(C), F32)
        keep = lambda a: a

    u = z_ref[:, 0:256] * _sigmoid(z_ref[:, 256:512])
    u_s[HIST:HIST + C, :] = u
    span = C + HIST - 8
    for s in range(1, 8):
        ush_s[s - 1] = u_s[s:s + span, :]
    acc = jnp.zeros((C, CONV_C), F32)
    for j in range(CONV_K):
        lo = HIST - (CONV_K - 1) + j
        q, r = lo - lo % 8, lo % 8
        tap = u_s[q:q + C, :] if r == 0 else ush_s[r - 1, q:q + C, :]
        acc = acc + cw_ref[j:j + 1, :] * tap
    conv = acc + cb_ref[...]
    xc = conv - jnp.mean(conv, axis=-1, keepdims=True)
    ln = xc * lax.rsqrt(jnp.mean(xc * xc, axis=-1, keepdims=True) + EPS) * lng_ref[...] + lnb_ref[...]
    br_ref[:, 0:256] = _silu(ln)

    lane = lax.broadcasted_iota(jnp.int32, (1, 256), 1)
    first_half = (lane % RET_DK) < (RET_DK // 2)
    cos = cos_ref[...]
    sin = sin_ref[...]

    def rope(a):
        swapped = jnp.where(first_half, pltpu.roll(a, 256 - RET_DK // 2, 1), pltpu.roll(a, RET_DK // 2, 1))
        return a * cos + swapped * sin

    rq = rope(z_ref[:, 512:768])
    rk = keep(rope(z_ref[:, 768:1024]) * RET_DK ** -0.5)
    rv = keep(z_ref[:, 1024:1280])
    lane_head = lane // RET_DV
    row_head = lax.broadcasted_iota(jnp.int32, (HEADS * RET_DK, 1), 0) // RET_DK
    log_gamma = [math.log1p(-(2.0 ** (-5 - h))) for h in range(HEADS)]
    lg_lane = jnp.zeros((1, 256), F32)
    lg_row = jnp.zeros((HEADS * RET_DK, 1), F32)
    for h in range(HEADS):
        lg_lane = jnp.where(lane_head == h, log_gamma[h], lg_lane)
        lg_row = jnp.where(row_head == h, log_gamma[h], lg_row)
    avg = _group_avg(256, RET_DV, low)
    dts = (t_col - s_row).astype(F32)
    rk_c = cast(rk)
    o = _dot(cast(rq * jnp.exp((tf + 1.0) * lg_lane)), cast(sr_s[...]))
    for h in range(HEADS):
        own = lane_head == h
        decay = jnp.where(causal, jnp.exp(dts * log_gamma[h]), 0.0)
        a = _dot_nt(cast(jnp.where(own, rq, 0.0)), rk_c) * decay
        o = o + _dot(cast(a), cast(jnp.where(own, rv, 0.0)))
    update = _dot_tn(cast(rk * jnp.exp((nvf - 1.0 - tf) * lg_lane)), cast(rv))
    sr_s[...] = jnp.exp(nvf * lg_row) * sr_s[...] + jnp.where(row_head == lane_head, update, 0.0)
    oc = o - _group_mean(o, avg, low)
    y = oc * lax.rsqrt(_group_mean(oc * oc, avg, low) + EPS) * rgn_ref[...]
    br_ref[:, 256:512] = _silu(z_ref[:, 1280:1536]) * y

    gq = z_ref[:, 1536:1664] * GLA_DK ** -0.5
    gk = keep(z_ref[:, 1664:1792])
    gv = keep(z_ref[:, 1792:2048])
    g = keep(gd_ref[...])
    tri = jnp.where(causal, 1.0, 0.0)
    if low:
        g_hi, g_lo = _split_bf16(g)
        tri = tri.astype(BF16)
        bcum = _dot(tri, g_hi) + _dot(tri, g_lo)
    else:
        bcum = _dot(tri, g)
    mid = C // 2 - 1
    b_mid = bcum[mid:mid + 1, :]
    b_last = bcum[C - 1:C, :]
    q_intra = gq * jnp.exp(bcum - b_mid)
    k_intra = gk * jnp.exp(b_mid - bcum)
    q_inter = gq * jnp.exp(bcum)
    k_state = gk * jnp.exp(b_last - bcum)
    e_last = jnp.where(t_col == C - 1, jnp.exp(bcum), 0.0)
    ones = jnp.ones((C, HEADS * GLA_DV), BF16 if low else F32)
    if low:
        e_hi, e_lo = _split_bf16(e_last)
        e_rows = _dot_tn(e_hi, ones) + _dot_tn(e_lo, ones)
    else:
        e_rows = _dot_tn(e_last, ones)
    key_head = lax.broadcasted_iota(jnp.int32, (1, HEADS * GLA_DK), 1) // GLA_DK
    krow_head = lax.broadcasted_iota(jnp.int32, (HEADS * GLA_DK, 1), 0) // GLA_DK
    k_c = cast(k_intra)
    o = _dot(cast(q_inter), cast(sg_s[...]))
    for h in range(HEADS):
        a = jnp.where(causal, _dot_nt(cast(jnp.where(key_head == h, q_intra, 0.0)), k_c), 0.0)
        o = o + _dot(cast(a), cast(jnp.where(lane_head == h, gv, 0.0)))
    update = _dot_tn(cast(k_state), cast(gv))
    sg_s[...] = e_rows * sg_s[...] + jnp.where(krow_head == lane_head, update, 0.0)
    y = o * lax.rsqrt(_group_mean(o * o, avg, low) + EPS) * ggn_ref[...]
    br_ref[:, 512:768] = _silu(z_ref[:, 2048:2304]) * y

def _mix_call(zlin, gdec, cos, sin, conv_past, ret0, gla0, w, *, layer, state_layer, n_seq, chunk, n_chunks,
              seq_len, row_block0, name):
    G = MIX_GROUP
    assert n_seq % G == 0
    blk = lambda n, g: pl.BlockSpec((chunk, n), lambda s, c: (row_block0 + (s * G + g) * n_chunks + c, 0))
    grouped = lambda shp: pl.BlockSpec((None, G) + shp, lambda s, c: (s, 0) + (0,) * len(shp))
    state = lambda shp, g: pl.BlockSpec((None, 1) + shp,
                                        lambda s, c: (state_layer, s * G + g) + (0,) * len(shp))
    consts = [w['cw'], w['cb'], w['lng'], w['lnb'], w['rgn'], w['ggn']]
    members = range(G)
    in_specs = ([blk(N_LIN, g) for g in members] + [blk(LANE, g) for g in members]
                + [pl.BlockSpec((chunk, 256), lambda s, c: (c, 0)), pl.BlockSpec((chunk, 256), lambda s, c: (c, 0))]
                + [state((CONV_K - 1, CONV_C), g) for g in members]
                + [state((HEADS, RET_DK, RET_DV), g) for g in members]
                + [state((HEADS, GLA_DK, GLA_DV), g) for g in members]
                + [_layer_spec(a, layer) for a in consts])
    args = [zlin] * G + [gdec] * G + [cos, sin] + [conv_past] * G + [ret0] * G + [gla0] * G + consts
    br, sr, sg, cs = pl.pallas_call(
        functools.partial(_mix_kernel, chunk=chunk, n_chunks=n_chunks, seq_len=seq_len, group=G),
        grid=(n_seq // G, n_chunks),
        in_specs=in_specs,
        out_specs=[pl.BlockSpec((None, None, G, chunk, 3 * BRANCH_W), lambda s, c: (s, c, 0, 0, 0)),
                   grouped((HEADS, RET_DK, RET_DV)), grouped((HEADS, GLA_DK, GLA_DV)),
                   grouped((CONV_K - 1, CONV_C))],
        out_shape=[jax.ShapeDtypeStruct((n_seq // G, n_chunks, G, chunk, 3 * BRANCH_W), F32),
                   jax.ShapeDtypeStruct((n_seq // G, G, HEADS, RET_DK, RET_DV), F32),
                   jax.ShapeDtypeStruct((n_seq // G, G, HEADS, GLA_DK, GLA_DV), F32),
                   jax.ShapeDtypeStruct((n_seq // G, G, CONV_K - 1, CONV_C), F32)],
        scratch_shapes=[pltpu.VMEM((G, HEADS * RET_DK, HEADS * RET_DV), F32),
                        pltpu.VMEM((G, HEADS * GLA_DK, HEADS * GLA_DV), F32),
                        pltpu.VMEM((G, HIST + chunk, CONV_C), F32),
                        pltpu.VMEM((G, 7, HIST + chunk - 8, CONV_C), F32)],
        compiler_params=pltpu.CompilerParams(dimension_semantics=("arbitrary", "arbitrary")),
        name=name,
    )(*args)
    flat = lambda a: a.reshape((n_seq,) + a.shape[2:])
    return br, flat(sr), flat(sg), flat(cs)


def _t5_bucket_np(rel):
    rel = np.asarray(rel)
    n = np.maximum(rel, 0)
    max_exact = REL_BUCKETS // 2
    nf = np.maximum(n, 1).astype(np.float64)
    large = max_exact + (np.log(nf / max_exact) / math.log(REL_MAX_DIST / max_exact)
                         * (REL_BUCKETS - max_exact)).astype(np.int64)
    large = np.minimum(large, REL_BUCKETS - 1)
    return np.where(n < max_exact, n, large).astype(np.int32)


def _lambda(lamv_ref, lam_init):
    a = jnp.sum(lamv_ref[0:1, :] * lamv_ref[1:2, :], axis=-1, keepdims=True)
    b = jnp.sum(lamv_ref[2:3, :] * lamv_ref[3:4, :], axis=-1, keepdims=True)
    return jnp.exp(a) - jnp.exp(b) + lam_init


def _flash_kernel(bt_ref, rb_ref, top_ref, lamv_ref, sub_ref, qt_ref, k_ref, vt_ref, o_ref,
                  bias_s, qbd_s, m_s, l_s, acc_s, sta_s, stb_s, lf_s, accf_s, *, layer, lam_init, n_blocks):
    QB = SEQ_BLOCK
    SUB = FAR_BLOCK // QB
    b = pl.program_id(0)
    i = pl.program_id(1)

    @pl.when((b == 0) & (i == 0))
    def _tables():
        for kind, src in ((1, 1), (2, 0)):
            bt = bt_ref[src]
            tiles = [jnp.zeros((QB, QB), F32) for _ in range(HEADS)]
            for bk in range(REL_BUCKETS):
                hit = bt == bk
                for h in range(HEADS):
                    tiles[h] = jnp.where(hit, rb_ref[bk, h] * LOG2E, tiles[h])
            for h in range(HEADS):
                bias_s[kind, h] = jnp.where(bt < 0, NEG, tiles[h])
        for h in range(HEADS):
            bias_s[0, h] = jnp.zeros((QB, QB), F32) + rb_ref[REL_BUCKETS - 1, h] * LOG2E
            bias_s[3, h] = jnp.full((QB, QB), NEG, F32)
            for kind in range(4):
                bias_s[4 + kind, h] = bias_s[kind, h] - top_ref[layer, h]

    qt = qt_ref[...] * (DIFF_HD ** -0.5 * LOG2E)
    row = lax.broadcasted_iota(jnp.int32, (256, 1), 0)
    for hj in range(2 * HEADS):
        own = (row >= hj * DIFF_HD) & (row < (hj + 1) * DIFF_HD)
        qbd_s[:, hj * QB:(hj + 1) * QB] = jnp.where(own, qt, 0.0).astype(BF16)
    m_s[...] = jnp.full(m_s.shape, NEG, F32)
    l_s[...] = jnp.zeros(l_s.shape, F32)
    acc_s[...] = jnp.zeros(acc_s.shape, F32)

    def online(h, st, shift, vt):
        m_old = m_s[h]
        m_new = jnp.maximum(m_old, jnp.max(st, axis=0, keepdims=True) + shift)
        alpha = jnp.exp2(m_old - m_new)
        pt = jnp.exp2(st - (m_new - shift))
        l_s[h] = alpha * l_s[h] + jnp.sum(pt, axis=0, keepdims=True)
        m_s[h] = m_new
        acc_s[h] = alpha * acc_s[h] + _dot(vt[h * DIFF_VD:(h + 1) * DIFF_VD, :], pt.astype(BF16))

    def fixed(h, st, vt):
        pt = jnp.exp2(st)
        l_s[h] = l_s[h] + jnp.sum(pt, axis=0, keepdims=True)
        acc_s[h] = acc_s[h] + _dot(vt[h * DIFF_VD:(h + 1) * DIFF_VD, :], pt.astype(BF16))

    n_far = jnp.maximum(i - 1, 0) // SUB

    def sweep(accumulate):
        def far_body(kb, carry):
            off = pl.multiple_of(kb * FAR_BLOCK, FAR_BLOCK)
            k = k_ref[pl.ds(off, FAR_BLOCK), :].astype(BF16)
            vt = vt_ref[:, pl.ds(off, FAR_BLOCK)].astype(BF16)
            for h in range(HEADS):
                st = _dot(k, qbd_s[:, 2 * h * QB:(2 * h + 2) * QB])
                accumulate(h, st, rb_ref[REL_BUCKETS - 1, h] * LOG2E, vt)
            return carry

        lax.fori_loop(0, n_far, far_body, 0)

        def near_body(kb, carry):
            first = kb * SUB
            blk0 = jnp.minimum(first, n_blocks - SUB)
            off = pl.multiple_of(blk0 * QB, QB)
            k = k_ref[pl.ds(off, FAR_BLOCK), :].astype(BF16)
            vt = vt_ref[:, pl.ds(off, FAR_BLOCK)].astype(BF16)
            kinds = []
            for c in range(SUB):
                blk = blk0 + c
                kind = jnp.where(blk == i, 2, jnp.where(blk == i - 1, 1, 0))
                kinds.append(jnp.where((blk > i) | (blk < first), 3, kind))
            for h in range(HEADS):
                st = _dot(k, qbd_s[:, 2 * h * QB:(2 * h + 2) * QB])
                parts = []
                for c in range(SUB):
                    tile = bias_s[kinds[c], h]
                    parts.append(st[c * QB:(c + 1) * QB, :] + jnp.concatenate([tile, tile], axis=1))
                accumulate(h, jnp.concatenate(parts, axis=0), 0.0, vt)
            return carry

        lax.fori_loop(n_far, i // SUB + 1, near_body, 0)

    bounded = top_ref[layer, HEADS] > 0.5

    def window(j):
        first = j * SUB
        return first, jnp.minimum(first, n_blocks - SUB)

    def score(j, buf):
        _, blk0 = window(j)
        k = k_ref[pl.ds(pl.multiple_of(blk0 * QB, QB), FAR_BLOCK), :].astype(BF16)
        for h in range(HEADS):
            buf[h] = _dot(k, qbd_s[:, 2 * h * QB:(2 * h + 2) * QB])

    def consume(j, buf):
        first, blk0 = window(j)
        vt = vt_ref[:, pl.ds(pl.multiple_of(blk0 * QB, QB), FAR_BLOCK)].astype(BF16)
        kinds = []
        for c in range(SUB):
            blk = blk0 + c
            kind = jnp.where(blk == i, 2, jnp.where(blk == i - 1, 1, 0))
            kinds.append(jnp.where((blk > i) | (blk < first), 3, kind))
        for h in range(HEADS):
            parts = []
            for c in range(SUB):
                tile = bias_s[kinds[c] + 4, h]
                parts.append(buf[h, c * QB:(c + 1) * QB, :] + jnp.concatenate([tile, tile], axis=1))
            fixed(h, jnp.concatenate(parts, axis=0), vt)

    def consume_far(j, buf):
        vt = vt_ref[:, pl.ds(pl.multiple_of(j * FAR_BLOCK, FAR_BLOCK), FAR_BLOCK)].astype(BF16)
        for h in range(HEADS):
            pt = jnp.exp2(buf[h])
            lf_s[h] = lf_s[h] + jnp.sum(pt, axis=0, keepdims=True)
            accf_s[h] = accf_s[h] + _dot(vt[h * DIFF_VD:(h + 1) * DIFF_VD, :], pt.astype(BF16))

    @pl.when(bounded)
    def _fixed():
        n_win = i // SUB + 1
        far_pairs = n_far // 2
        lf_s[...] = jnp.zeros(lf_s.shape, F32)
        accf_s[...] = jnp.zeros(accf_s.shape, F32)
        score(0, sta_s)

        def far_pair(m, carry):
            score(2 * m + 1, stb_s)
            consume_far(2 * m, sta_s)
            score(2 * m + 2, sta_s)
            consume_far(2 * m + 1, stb_s)
            return carry

        lax.fori_loop(0, far_pairs, far_pair, 0)
        w0 = 2 * far_pairs
        rest = n_win - w0

        def pair(m, carry):
            score(w0 + 2 * m + 1, stb_s)
            consume(w0 + 2 * m, sta_s)
            score(w0 + 2 * m + 2, sta_s)
            consume(w0 + 2 * m + 1, stb_s)
            return carry

        lax.fori_loop(0, rest // 2, pair, 0)

        @pl.when(rest % 2 == 1)
        def _tail():
            consume(n_win - 1, sta_s)

        for h in range(HEADS):
            far_scale = jnp.exp2(jnp.zeros((1, 1), F32) + (rb_ref[REL_BUCKETS - 1, h] * LOG2E - top_ref[layer, h]))
            l_s[h] = l_s[h] + far_scale * lf_s[h]
            acc_s[h] = acc_s[h] + far_scale * accf_s[h]

    @pl.when(jnp.logical_not(bounded))
    def _online():
        sweep(online)

    lam = _lambda(lamv_ref, lam_init)
    outs = []
    for h in range(HEADS):
        ot = acc_s[h] / l_s[h]
        d = ot[:, 0:QB] - lam * ot[:, QB:2 * QB]
        y = d * lax.rsqrt(jnp.mean(d * d, axis=0, keepdims=True) + EPS) * sub_ref[...] * (1.0 - lam_init)
        outs.append(y)
    o_ref[...] = jnp.concatenate(outs, axis=0).T


def _score_top(rel_bias, q_gain, k_gain):
    reach = DIFF_HD ** 0.5 * jnp.max(jnp.abs(q_gain * k_gain), axis=-1, keepdims=True)
    top = (reach + jnp.max(rel_bias, axis=0)[None, :]) * LOG2E
    low = (-reach + jnp.min(rel_bias, axis=0)[None, :]) * LOG2E
    ok = jnp.all(top - low < MAX_EXP2_SPAN, axis=-1, keepdims=True).astype(F32)
    return jnp.concatenate([top, ok], axis=-1).astype(F32)


def _flash_call(bt, rel_bias, top, lamv, sub_col, dqt, dk, dvt, *, layer, n_seq, seq_rows, lam_init):
    rows = n_seq * seq_rows
    nq = seq_rows // SEQ_BLOCK
    assert seq_rows >= FAR_BLOCK
    smem = pl.BlockSpec(memory_space=pltpu.SMEM)
    return pl.pallas_call(
        functools.partial(_flash_kernel, layer=layer, lam_init=lam_init, n_blocks=nq),
        grid=(n_seq, nq),
        in_specs=[_whole_spec(bt), smem, smem, _layer_spec(lamv, layer), _layer_spec(sub_col, layer),
                  pl.BlockSpec((None, 256, SEQ_BLOCK), lambda b, i: (b, 0, i)),
                  pl.BlockSpec((seq_rows, 256), lambda b, i: (b, 0)),
                  pl.BlockSpec((None, 256, seq_rows), lambda b, i: (b, 0, 0))],
        out_specs=pl.BlockSpec((SEQ_BLOCK, 256), lambda b, i: (b * nq + i, 0)),
        out_shape=jax.ShapeDtypeStruct((rows, 256), F32),
        scratch_shapes=[pltpu.VMEM((8, HEADS, SEQ_BLOCK, SEQ_BLOCK), F32),
                        pltpu.VMEM((256, 2 * HEADS * SEQ_BLOCK), BF16),
                        pltpu.VMEM((HEADS, 1, 2 * SEQ_BLOCK), F32), pltpu.VMEM((HEADS, 1, 2 * SEQ_BLOCK), F32),
                        pltpu.VMEM((HEADS, DIFF_VD, 2 * SEQ_BLOCK), F32),
                        pltpu.VMEM((HEADS, FAR_BLOCK, 2 * SEQ_BLOCK), F32),
                        pltpu.VMEM((HEADS, FAR_BLOCK, 2 * SEQ_BLOCK), F32),
                        pltpu.VMEM((HEADS, 1, 2 * SEQ_BLOCK), F32),
                        pltpu.VMEM((HEADS, DIFF_VD, 2 * SEQ_BLOCK), F32)],
        compiler_params=pltpu.CompilerParams(dimension_semantics=("arbitrary", "arbitrary"),
                                             vmem_limit_bytes=VMEM_LIMIT),
        name="attn_prompt",
    )(bt, rel_bias, top, lamv, sub_col, dqt, dk, dvt)


def _decode_kernel(pt_ref, bt_ref, rb_ref, lamv_ref, sub_ref, q_ref, kn_ref, vn_ref, *refs,
                   pages_per_step, n_steps, dec_seq, lam_init):
    G = pages_per_step
    kt_refs = refs[:G]
    vt_refs = refs[G:2 * G]
    o_ref, bias_pg_s, bias_new_s, far_s, qs_s, m_s, l_s, acc_s = refs[2 * G:]
    del pt_ref
    b = pl.program_id(0)
    g = pl.program_id(1)
    last = g == n_steps - 1
    NQ = 2 * HEADS * dec_seq
    seqs_per_block = kn_ref.shape[0] // dec_seq

    row_head = lax.broadcasted_iota(jnp.int32, (NQ, 1), 0) // (2 * dec_seq)

    def bias_rows(bk):
        out = jnp.zeros((NQ, 1), F32)
        for h in range(HEADS):
            out = jnp.where(row_head == h, rb_ref[bk, h], out)
        return out

    @pl.when((b == 0) & (g == 0))
    def _tables():
        for src, dst in ((0, bias_pg_s), (1, bias_new_s)):
            bt = bt_ref[src]
            tile = jnp.zeros(bt.shape, F32)
            for bk in range(REL_BUCKETS):
                tile = jnp.where(bt == bk, bias_rows(bk), tile)
            dst[...] = jnp.where(bt < 0, NEG, tile)
        far_s[...] = jnp.zeros(far_s.shape, F32) + bias_rows(REL_BUCKETS - 1)

    @pl.when(g == 0)
    def _init():
        q = q_ref[...] * DIFF_HD ** -0.5
        lane = lax.broadcasted_iota(jnp.int32, (1, 256), 1)
        for hj in range(2 * HEADS):
            own = (lane >= hj * DIFF_HD) & (lane < (hj + 1) * DIFF_HD)
            qs_s[hj * dec_seq:(hj + 1) * dec_seq, :] = jnp.where(own, q, 0.0)
        m_s[...] = jnp.full(m_s.shape, NEG, F32)
        l_s[...] = jnp.zeros(l_s.shape, F32)
        acc_s[...] = jnp.zeros(acc_s.shape, F32)

    def update(scores, pv_fn):
        m_old = m_s[...]
        s_max = scores[0]
        for s in scores[1:]:
            s_max = jnp.maximum(s_max, s)
        m_new = jnp.maximum(m_old, jnp.max(s_max, axis=1, keepdims=True))
        alpha = jnp.exp(m_old - m_new)
        probs = [jnp.exp(s - m_new) for s in scores]
        p_sum = probs[0]
        for p in probs[1:]:
            p_sum = p_sum + p
        l_s[...] = alpha * l_s[...] + jnp.sum(p_sum, axis=1, keepdims=True)
        m_s[...] = m_new
        acc_s[...] = jnp.concatenate([alpha, alpha], axis=1) * acc_s[...] + pv_fn(probs)

    qs = qs_s[...].astype(BF16)
    scores = []
    for p in range(G):
        s = _dot(qs, kt_refs[p][...].astype(BF16))
        if p == G - 1:
            s = s + jnp.where(last, bias_pg_s[...], far_s[...])
        else:
            s = s + far_s[...]
        scores.append(s)

    def pv_pages(probs):
        out = None
        for p in range(G):
            t = _dot_nt(probs[p].astype(BF16), vt_refs[p][...].astype(BF16))
            out = t if out is None else out + t
        return out

    update(scores, pv_pages)

    @pl.when(last)
    def _finish():
        key_seq = lax.broadcasted_iota(jnp.int32, (1, kn_ref.shape[0]), 1) // dec_seq
        s_new = _dot_nt(qs, kn_ref[...].astype(BF16))
        s_new = jnp.where(key_seq == b % seqs_per_block, s_new + bias_new_s[...], NEG)
        update([s_new], lambda probs: _dot(probs[0].astype(BF16), vn_ref[...].astype(BF16)))
        l_all = l_s[...]
        o_all = acc_s[...] / jnp.concatenate([l_all, l_all], axis=1)
        lam = _lambda(lamv_ref, lam_init)
        for h in range(HEADS):
            r1 = (2 * h) * dec_seq
            r2 = (2 * h + 1) * dec_seq
            cols = slice(h * DIFF_VD, (h + 1) * DIFF_VD)
            d = o_all[r1:r1 + dec_seq, cols] - lam * o_all[r2:r2 + dec_seq, cols]
            y = d * lax.rsqrt(jnp.mean(d * d, axis=-1, keepdims=True) + EPS) * sub_ref[...] * (1.0 - lam_init)
            o_ref[:, cols] = y


def _decode_call(page_table, bt, rel_bias, lamv, sub, dq, dk, dv, cache_kt, cache_vt, *,
                 layer, n_seq, dec_seq, row0, lam_init):
    n_pages = page_table.shape[1]
    page = cache_kt.shape[3]
    G = min(PAGES_PER_STEP, n_pages)
    n_steps = n_pages // G
    nq = 2 * HEADS * dec_seq
    new_rows = LANE
    per_blk = new_rows // dec_seq
    own = pl.BlockSpec((dec_seq, 256), lambda b, g, pt: (row0 // dec_seq + b, 0))
    new = pl.BlockSpec((new_rows, 256), lambda b, g, pt: (row0 // new_rows + b // per_blk, 0))

    def page_spec(p):
        return pl.BlockSpec((None, None, 256, page),
                            lambda b, g, pt: (layer, pt[b * n_pages + g * G + p], 0, 0))

    in_specs = ([_whole_spec(bt), pl.BlockSpec(memory_space=pltpu.SMEM), _layer_spec(lamv, layer),
                 _layer_spec(sub, layer), own, new, new]
                + [page_spec(p) for p in range(G)] + [page_spec(p) for p in range(G)])
    args = [bt, rel_bias, lamv, sub, dq, dk, dv] + [cache_kt] * G + [cache_vt] * G
    return pl.pallas_call(
        functools.partial(_decode_kernel, pages_per_step=G, n_steps=n_steps, dec_seq=dec_seq, lam_init=lam_init),
        grid_spec=pltpu.PrefetchScalarGridSpec(
            num_scalar_prefetch=1,
            grid=(n_seq, n_steps),
            in_specs=in_specs,
            out_specs=pl.BlockSpec((dec_seq, 256), lambda b, g, pt: (b, 0)),
            scratch_shapes=[pltpu.VMEM((nq, page), F32), pltpu.VMEM((nq, new_rows), F32),
                            pltpu.VMEM((nq, page), F32), pltpu.VMEM((nq, 256), F32),
                            pltpu.VMEM((nq, LANE), F32), pltpu.VMEM((nq, LANE), F32), pltpu.VMEM((nq, 256), F32)]),
        out_shape=jax.ShapeDtypeStruct((n_seq * dec_seq, 256), F32),
        compiler_params=pltpu.CompilerParams(dimension_semantics=("arbitrary", "arbitrary"),
                                             vmem_limit_bytes=VMEM_LIMIT),
        name="attn_sample",
    )(page_table.reshape(-1), *args)


def _rope_tables(pos):
    half = RET_DK // 2
    inv = ROPE_BASE ** (-np.arange(half, dtype=np.float64) / half)
    ang = np.asarray(pos, np.float64)[:, None] * inv[None, :]
    cos = np.concatenate([np.cos(ang), np.cos(ang)], axis=1)
    sin = np.concatenate([-np.sin(ang), np.sin(ang)], axis=1)
    return (jnp.asarray(np.tile(cos, (1, HEADS)), F32), jnp.asarray(np.tile(sin, (1, HEADS)), F32))


def _prompt_bucket_tiles():
    t = np.arange(SEQ_BLOCK)
    rel = t[None, :] - t[:, None]
    diag = np.where(rel >= 0, _t5_bucket_np(rel), -1)
    sub = _t5_bucket_np(rel + SEQ_BLOCK)
    return jnp.asarray(np.stack([diag, sub]).astype(np.int32))


def _sample_bucket_tiles(page, dec_seq):
    assert page == LANE
    iq = (np.arange(2 * HEADS * dec_seq) % dec_seq)[:, None]
    past = _t5_bucket_np(page + iq - np.arange(page)[None, :])
    rel_new = iq - (np.arange(LANE) % dec_seq)[None, :]
    new = np.where(rel_new >= 0, _t5_bucket_np(rel_new), -1)
    return jnp.asarray(np.stack([past, new]).astype(np.int32))


def kernel(x_prompt, x_sample, cache_k, cache_v, page_table, state_ret, state_gla, state_conv, meta_tokens,
           rel_bias, norm_ffn1, ffn1_gate, ffn1_up, ffn1_down, norm_mix, w_in, b_in, conv_w, conv_b, conv_ln_g,
           conv_ln_b, ret_gn, gla_alpha_w, gla_alpha_b, gla_gn, q_norm, k_norm, lam_q1, lam_k1, lam_q2, lam_k2,
           diff_subln, w_branch, w_out, norm_ffn2, ffn2_gate, ffn2_up, ffn2_down):
    B, S, D = x_prompt.shape
    DB, DS, _ = x_sample.shape
    depth = w_in.shape[0]
    L = S + N_META
    Lp = -(-L // SEQ_BLOCK) * SEQ_BLOCK
    n_chunks = Lp // SEQ_BLOCK
    n_pool, page = cache_k.shape[1], cache_k.shape[2]
    past_len = page_table.shape[1] * page
    rows_p = B * Lp
    rows_s = DB * DS
    rows_sp = -(-rows_s // ROW_TILE) * ROW_TILE
    assert Lp % ROW_TILE == 0 and rows_s % LANE == 0 and DS % 8 == 0

    rows = rows_p + rows_sp
    h = None
    pad_s = lambda a: jnp.pad(a, ((0, rows_sp - rows_s), (0, 0)))

    cos_p, sin_p = _rope_tables(np.arange(Lp))
    cos_s, sin_s = _rope_tables(past_len + np.arange(DS))
    bt_prompt = _prompt_bucket_tiles()
    bt_sample = _sample_bucket_tiles(page, DS)
    ckt = jnp.transpose(cache_k, (0, 1, 3, 4, 2)).reshape(depth, n_pool, HEADS * 2 * DIFF_HD, page)
    cvt = jnp.transpose(cache_v, (0, 1, 3, 4, 2)).reshape(depth, n_pool, HEADS * DIFF_VD, page)
    zeros_conv = jnp.zeros((1, B, CONV_K - 1, CONV_C), F32)
    zeros_ret = jnp.zeros((1, B, HEADS, RET_DK, RET_DV), F32)
    zeros_gla = jnp.zeros((1, B, HEADS, GLA_DK, GLA_DV), F32)

    row3 = lambda a: a.reshape(depth, 1, -1).astype(F32)
    w = dict(
        g1=row3(norm_ffn1), wg1=ffn1_gate.astype(BF16), wu1=ffn1_up.astype(BF16), wd1=ffn1_down.astype(BF16),
        gm=row3(norm_mix),
        wlin=w_in[:, :, :N_LIN].astype(BF16), blin=row3(b_in[:, :N_LIN]),
        wlow=w_in[:, :, OFF_GLOW:OFF_DIFF].astype(BF16), blow=row3(b_in[:, OFF_GLOW:OFF_DIFF]),
        aw=gla_alpha_w.astype(BF16), ab=row3(gla_alpha_b),
        wdf=w_in[:, :, OFF_DIFF:OFF_GATES].astype(BF16), bdf=row3(b_in[:, OFF_DIFF:OFF_GATES]),
        qg=row3(jnp.tile(q_norm, (1, 2 * HEADS))), kg=row3(jnp.tile(k_norm, (1, 2 * HEADS))),
        wgt=w_in[:, :, OFF_GATES:].astype(BF16), bgt=row3(b_in[:, OFF_GATES:]),
        wb=w_branch.astype(BF16), wo=w_out.astype(BF16), g2=row3(norm_ffn2),
        wg2=ffn2_gate.astype(BF16), wu2=ffn2_up.astype(BF16), wd2=ffn2_down.astype(BF16),
        cw=conv_w, cb=row3(conv_b), lng=row3(conv_ln_g), lnb=row3(conv_ln_b),
        rgn=row3(ret_gn), ggn=row3(gla_gn),
    )
    lamv = jnp.stack([lam_q1, lam_k1, lam_q2, lam_k2], axis=1).astype(F32)
    sub = row3(diff_subln)
    sub_col = diff_subln.reshape(depth, -1, 1).astype(F32)
    top = _score_top(rel_bias, q_norm, k_norm)

    per_layer = []
    for l in range(depth):
        lam_init = 0.8 - 0.6 * math.exp(-0.3 * l)
        h1, zlin, gdec, dq, dk, dv, qt, kt, vt = _head_call(
            h, w, layer=l, n_seq=B, seq_rows=Lp, rows=rows,
            model_inputs=(x_prompt, x_sample.reshape(rows_s, D), meta_tokens.astype(F32)))

        brl_p, rp, gp, cp = _mix_call(zlin, gdec, cos_p, sin_p, zeros_conv, zeros_ret, zeros_gla, w,
                                      layer=l, state_layer=0, n_seq=B, chunk=SEQ_BLOCK, n_chunks=n_chunks,
                                      seq_len=L, row_block0=0, name="mix_prompt")
        brl_s, rs, gs, cs = _mix_call(zlin, gdec, cos_s, sin_s, state_conv, state_ret, state_gla, w,
                                      layer=l, state_layer=l, n_seq=DB, chunk=DS, n_chunks=1, seq_len=DS,
                                      row_block0=rows_p // DS, name="mix_sample")
        brd_p = _flash_call(bt_prompt, rel_bias, top, lamv, sub_col, qt, dk, vt,
                            layer=l, n_seq=B, seq_rows=Lp, lam_init=lam_init)
        brd_s = _decode_call(page_table, bt_sample, rel_bias, lamv, sub, dq, dk, dv, ckt, cvt,
                             layer=l, n_seq=DB, dec_seq=DS, row0=rows_p, lam_init=lam_init)

        h = _tail_call(h1, brl_p, pad_s(brl_s.reshape(rows_s, -1)), brd_p, pad_s(brd_s), w, layer=l,
                       final=(B, S, rows_s) if l == depth - 1 else None)
        per_layer.append(dict(kt=kt, vt=vt, ks=dk[rows_p:rows_p + rows_s], vs=dv[rows_p:rows_p + rows_s],
                              rp=rp, rs=rs, gp=gp, gs=gs, cp=cp, cs=cs))

    st = lambda k: jnp.stack([p[k] for p in per_layer], axis=0)
    seq_major = lambda t: jnp.transpose(t.reshape(depth, B, HEADS, -1, Lp)[..., :L], (0, 1, 4, 2, 3))
    y_prompt, y_rows = h
    y_sample = y_rows.reshape(DB, DS, D)
    return (y_prompt, y_sample, seq_major(st('kt')), seq_major(st('vt')),
            st('ks').reshape(depth, DB, DS, HEADS, 2 * DIFF_HD), st('vs').reshape(depth, DB, DS, HEADS, DIFF_VD),
            st('rp'), st('rs'), st('gp'), st('gs'), st('cp'), st('cs'))
```

```python
import functools
import math

import numpy as np
import jax
import jax.numpy as jnp
from jax import lax
from jax.experimental import pallas as pl
from jax.experimental.pallas import tpu as pltpu

F32 = jnp.float32
BF16 = jnp.bfloat16

D_MODEL = 1024
N_META = 16
N_BRANCH = 4
BRANCH_W = 256
D_FF = 2816
CONV_C = 256
CONV_K = 31
HEADS = 4
RET_DK = 64
RET_DV = 64
GLA_DK = 32
GLA_DV = 64
GLA_RANK = 16
GLA_TAU = 16.0
DIFF_HD = 32
DIFF_VD = 64
REL_BUCKETS = 32
REL_MAX_DIST = 128
ROPE_BASE = 10000.0
EPS = 1e-6
NEG = -1e30
LOG2E = math.log2(math.e)
MAX_EXP2_SPAN = 100.0

N_LIN = 2304
OFF_GLOW = N_LIN
OFF_DIFF = OFF_GLOW + GLA_RANK
OFF_GATES = OFF_DIFF + 3 * 256

LANE = 128
ROW_TILE = 384
FFN_CHUNKS = 1
MXU_N = 256
SEQ_BLOCK = 128
FAR_BLOCK = 512
HIST = 32
VMEM_LIMIT = 60 * 1024 * 1024
MIX_GROUP = 4
PAGES_PER_STEP = 64


def _sigmoid(x):
    return 1.0 / (1.0 + jnp.exp(-x))


def _silu(x):
    return x * _sigmoid(x)


def _log_sigmoid(x):
    return jnp.minimum(x, 0.0) - jnp.log1p(jnp.exp(-jnp.abs(x)))


def _rms(x, g):
    return x * lax.rsqrt(jnp.mean(x * x, axis=-1, keepdims=True) + EPS) * g


def _dot(a, b):
    return jnp.dot(a, b, preferred_element_type=F32)


def _dot_nt(a, b):
    return lax.dot_general(a, b, (((1,), (1,)), ((), ())), preferred_element_type=F32)


def _dot_tn(a, b):
    return lax.dot_general(a, b, (((0,), (0,)), ((), ())), preferred_element_type=F32)


def _split_bf16(x):
    hi = x.astype(BF16)
    lo = (x - hi.astype(F32)).astype(BF16)
    return hi, lo


def _group_avg(n, group, low=True):
    r = lax.broadcasted_iota(jnp.int32, (n, n), 0) // group
    c = lax.broadcasted_iota(jnp.int32, (n, n), 1) // group
    return jnp.where(r == c, 1.0 / group, 0.0).astype(BF16 if low else F32)


def _group_mean(x, avg, low=True):
    if not low:
        return _dot(x, avg)
    hi, lo = _split_bf16(x)
    return _dot(hi, avg) + _dot(lo, avg)


def _layer_spec(a, layer, single_buffer=False):
    nd = a.ndim - 1
    kw = dict(pipeline_mode=pl.Buffered(1)) if single_buffer else {}
    return pl.BlockSpec((None,) + a.shape[1:], lambda *_: (layer,) + (0,) * nd, **kw)


def _whole_spec(a):
    return pl.BlockSpec(a.shape, lambda *_: (0,) * a.ndim)


def _ffn(x, wg_ref, wu_ref, wd_ref):
    tiles = D_FF // MXU_N
    edges = [(tiles * c // FFN_CHUNKS) * MXU_N for c in range(FFN_CHUNKS + 1)]
    out = None
    for c in range(FFN_CHUNKS):
        cols = slice(edges[c], edges[c + 1])
        act = (_silu(_dot(x, wg_ref[:, cols])) * _dot(x, wu_ref[:, cols])).astype(BF16)
        part = _dot(act, wd_ref[cols, :])
        out = part if out is None else out + part
    return out


def _assemble_rows(xcur_ref, xprev_ref, xlast_ref, xs_ref, meta_ref, *, prompt_tiles, per_seq):
    i = pl.program_id(0)
    j = i % per_seq
    cut = ROW_TILE - N_META
    prev_tail = xprev_ref[cut:, :]
    cur_head = xcur_ref[0:cut, :]
    x_last = xlast_ref[...]
    pad = jnp.zeros((cut - x_last.shape[0], D_MODEL), F32)
    body = jnp.concatenate([prev_tail, cur_head], axis=0)
    first = jnp.concatenate([meta_ref[...], cur_head], axis=0)
    last = jnp.concatenate([prev_tail, x_last, pad], axis=0)
    xs = xs_ref[...]
    sample = jnp.concatenate([xs, jnp.zeros((ROW_TILE - xs.shape[0], D_MODEL), F32)], axis=0)
    rows = jnp.where(j == 0, first, jnp.where(j == per_seq - 1, last, body))
    return jnp.where(i >= prompt_tiles, sample, rows)


def _head_kernel(*refs, prompt_tiles, per_seq, first_layer):
    n_in = 5 if first_layer else 1
    (g1_ref, wg_ref, wu_ref, wd_ref, gm_ref, wlin_ref, blin_ref, wlow_ref, blow_ref,
     aw_ref, ab_ref, wdf_ref, bdf_ref, qg_ref, kg_ref,
     h1_ref, zlin_ref, gdec_ref, dq_ref, dk_ref, dv_ref, qt_ref, kt_ref, vt_ref, kh_ref) = refs[n_in:]
    if first_layer:
        h = _assemble_rows(*refs[:n_in], prompt_tiles=prompt_tiles, per_seq=per_seq)
    else:
        h = refs[0][...]
    h1 = h + 0.5 * _ffn(_rms(h, g1_ref[...]).astype(BF16), wg_ref, wu_ref, wd_ref)
    h1_ref[...] = h1
    x = _rms(h1, gm_ref[...]).astype(BF16)
    zlin_ref[...] = _dot(x, wlin_ref[...]) + blin_ref[...]
    g_low = _dot(x, wlow_ref[...]) + blow_ref[...]
    g_pre = _dot(g_low.astype(BF16), aw_ref[...]) + ab_ref[...]
    gdec_ref[...] = _log_sigmoid(g_pre) * (1.0 / GLA_TAU)
    zd = _dot(x, wdf_ref[...]) + bdf_ref[...]
    d_q = zd[:, 0:256]
    d_k = zd[:, 256:512]
    d_v = zd[:, 512:768]
    avg = _group_avg(256, DIFF_HD)
    q_n = d_q * lax.rsqrt(_group_mean(d_q * d_q, avg) + EPS) * qg_ref[...]
    dq_ref[...] = q_n
    k_n = d_k * lax.rsqrt(_group_mean(d_k * d_k, avg) + EPS) * kg_ref[...]
    dk_ref[...] = k_n
    dv_ref[...] = d_v

    @pl.when(pl.program_id(0) < prompt_tiles)
    def _transposed():
        qt_ref[...] = q_n.T
        kt_ref[...] = k_n.T
        vt_ref[...] = d_v.T
        for h in range(HEADS):
            kh_ref[h] = k_n[:, h * 2 * DIFF_HD:(h + 1) * 2 * DIFF_HD]


def _head_call(h, w, *, layer, n_seq, seq_rows, rows, model_inputs=None):
    per_seq = seq_rows // ROW_TILE
    p_tiles = n_seq * per_seq
    row = lambda n: pl.BlockSpec((ROW_TILE, n), lambda i: (i, 0))
    if h is not None:
        data, data_specs = [h], [row(D_MODEL)]
    else:
        x_prompt, x_rows, meta = model_inputs
        seq = x_prompt.shape[1]
        x_last = seq - ((per_seq - 1) * ROW_TILE)
        assert rows == (p_tiles + 1) * ROW_TILE and x_rows.shape[0] <= ROW_TILE and per_seq >= 2
        assert 0 < x_last <= ROW_TILE - N_META and x_last % 8 == 0 and ((per_seq - 1) * ROW_TILE) % x_last == 0

        def seq_tile(i):
            t = jnp.minimum(i, p_tiles - 1)
            return t // per_seq, t % per_seq

        cur = lambda i: (seq_tile(i)[0], jnp.minimum(seq_tile(i)[1], per_seq - 2), 0)
        prev = lambda i: (seq_tile(i)[0], jnp.maximum(seq_tile(i)[1] - 1, 0), 0)
        last = lambda i: (seq_tile(i)[0], (per_seq - 1) * ROW_TILE // x_last, 0)
        data = [x_prompt, x_prompt, x_prompt, x_rows, meta]
        data_specs = [pl.BlockSpec((None, ROW_TILE, D_MODEL), cur), pl.BlockSpec((None, ROW_TILE, D_MODEL), prev),
                      pl.BlockSpec((None, x_last, D_MODEL), last), _whole_spec(x_rows), _whole_spec(meta)]

    def col_map(i):
        t = jnp.minimum(i, p_tiles - 1)
        return (t // per_seq, 0, t % per_seq)

    col = pl.BlockSpec((None, 256, ROW_TILE), col_map)
    heads = pl.BlockSpec((None, HEADS, ROW_TILE, 2 * DIFF_HD),
                         lambda i: (col_map(i)[0], 0, col_map(i)[2], 0))
    consts = [w['g1'], w['wg1'], w['wu1'], w['wd1'], w['gm'], w['wlin'], w['blin'], w['wlow'], w['blow'],
              w['aw'], w['ab'], w['wdf'], w['bdf'], w['qg'], w['kg']]
    widths = (D_MODEL, N_LIN, LANE, 256, 256, 256)
    return pl.pallas_call(
        functools.partial(_head_kernel, prompt_tiles=p_tiles, per_seq=per_seq, first_layer=h is None),
        grid=(rows // ROW_TILE,),
        in_specs=data_specs + [_layer_spec(c, layer, single_buffer=True) for c in consts],
        out_specs=[row(n) for n in widths] + [col, col, col, heads],
        out_shape=([jax.ShapeDtypeStruct((rows, n), F32) for n in widths]
                   + [jax.ShapeDtypeStruct((n_seq, 256, seq_rows), F32)] * 3
                   + [jax.ShapeDtypeStruct((n_seq, HEADS, seq_rows, 2 * DIFF_HD), F32)]),
        compiler_params=pltpu.CompilerParams(dimension_semantics=("arbitrary",),
                                             vmem_limit_bytes=VMEM_LIMIT),
        name="head",
    )(*data, *consts)


def _tail_kernel(h1_ref, brlp_ref, brls_ref, brdp_ref, brds_ref, gm_ref, wgt_ref, bgt_ref, wb_ref, wo_ref, g2_ref,
                 wg_ref, wu_ref, wd_ref, *outs, prompt_tiles, last_layer):
    h1 = h1_ref[...]
    x = _rms(h1, gm_ref[...]).astype(BF16)
    is_sample = pl.program_id(0) >= prompt_tiles
    merged = None
    for n in range(N_BRANCH):
        if n < 3:
            cols = slice(n * BRANCH_W, (n + 1) * BRANCH_W)
            br_p = brlp_ref[:, :, cols].reshape(ROW_TILE, BRANCH_W)
            br = jnp.where(is_sample, brls_ref[:, cols], br_p)
        else:
            br = jnp.where(is_sample, brds_ref[...], brdp_ref[...])
        gate = _dot(x, wgt_ref[:, n * D_MODEL:(n + 1) * D_MODEL]) + bgt_ref[:, n * D_MODEL:(n + 1) * D_MODEL]
        term = _dot(br.astype(BF16), wb_ref[n]) * _sigmoid(gate)
        merged = term if merged is None else merged + term
    h2 = h1 + _dot(merged.astype(BF16), wo_ref[...])
    h3 = h2 + 0.5 * _ffn(_rms(h2, g2_ref[...]).astype(BF16), wg_ref, wu_ref, wd_ref)
    if not last_layer:
        outs[0][...] = h3
        return

    y_ref, ys_ref, carry_s = outs
    cut = ROW_TILE - N_META

    @pl.when(pl.program_id(0) == 0)
    def _no_previous_tile():
        carry_s[...] = jnp.zeros(carry_s.shape, F32)

    y_ref[0:cut, :] = carry_s[...]
    y_ref[cut:ROW_TILE, :] = h3[0:N_META, :]
    carry_s[...] = h3[N_META:, :]

    @pl.when(is_sample)
    def _sample_rows():
        ys_ref[...] = h3[0:ys_ref.shape[0], :]


def _tail_call(h1, brl_p, brl_s, brd_p, brd_s, w, *, layer, final=None):
    rows = h1.shape[0]
    p_tiles = brd_p.shape[0] // ROW_TILE
    group, chunk = brl_p.shape[2], brl_p.shape[3]
    per_tile = ROW_TILE // chunk
    per_seq = brl_p.shape[1] // per_tile
    row = lambda n: pl.BlockSpec((ROW_TILE, n), lambda i: (i, 0))
    row_p = lambda n: pl.BlockSpec((ROW_TILE, n), lambda i: (jnp.minimum(i, p_tiles - 1), 0))
    row_s = lambda n: pl.BlockSpec((ROW_TILE, n), lambda i: (jnp.maximum(i - p_tiles, 0), 0))

    def mixer_map(i):
        t = jnp.minimum(i, p_tiles - 1)
        seq = t // per_seq
        return (seq // group, t % per_seq, seq % group, 0, 0)

    mixer = pl.BlockSpec((None, per_tile, None, chunk, 3 * BRANCH_W), mixer_map)
    consts = [w['gm'], w['wgt'], w['bgt'], w['wb'], w['wo'], w['g2'], w['wg2'], w['wu2'], w['wd2']]
    if final is None:
        out_specs = row(D_MODEL)
        out_shape = jax.ShapeDtypeStruct((rows, D_MODEL), F32)
        scratch = []
    else:
        n_seq, seq, rows_s = final
        assert rows == (p_tiles + 1) * ROW_TILE and rows_s <= ROW_TILE

        def prev_tile(i):
            t = jnp.minimum(jnp.maximum(i - 1, 0), p_tiles - 1)
            return (t // per_seq, t % per_seq, 0)

        out_specs = [pl.BlockSpec((None, ROW_TILE, D_MODEL), prev_tile),
                     pl.BlockSpec((rows_s, D_MODEL), lambda i: (0, 0))]
        out_shape = [jax.ShapeDtypeStruct((n_seq, seq, D_MODEL), F32),
                     jax.ShapeDtypeStruct((rows_s, D_MODEL), F32)]
        scratch = [pltpu.VMEM((ROW_TILE - N_META, D_MODEL), F32)]
    return pl.pallas_call(
        functools.partial(_tail_kernel, prompt_tiles=p_tiles, last_layer=final is not None),
        grid=(rows // ROW_TILE,),
        in_specs=[row(D_MODEL), mixer, row_s(3 * BRANCH_W), row_p(BRANCH_W), row_s(BRANCH_W)]
                 + [_layer_spec(c, layer, single_buffer=True) for c in consts],
        out_specs=out_specs,
        out_shape=out_shape,
        scratch_shapes=scratch,
        compiler_params=pltpu.CompilerParams(dimension_semantics=("arbitrary",),
                                             vmem_limit_bytes=VMEM_LIMIT),
        name="tail",
    )(h1, brl_p, brl_s, brd_p, brd_s, *consts)


def _mix_kernel(*refs, chunk, n_chunks, seq_len, group):
    G = group
    z_refs, gd_refs = refs[0:G], refs[G:2 * G]
    cos_ref, sin_ref = refs[2 * G:2 * G + 2]
    cpast_refs, sr0_refs, sg0_refs = (refs[2 + (2 + k) * G:2 + (3 + k) * G] for k in range(3))
    consts = refs[2 + 5 * G:8 + 5 * G]
    br_ref, sr_out, sg_out, cs_out, sr_s, sg_s, u_s, ush_s = refs[8 + 5 * G:]
    C = chunk
    c = pl.program_id(1)

    @pl.when(c == 0)
    def _init():
        sr_s[...] = jnp.zeros(sr_s.shape, F32)
        sg_s[...] = jnp.zeros(sg_s.shape, F32)
        for g in range(G):
            for h in range(HEADS):
                sr_s[g, h * RET_DK:(h + 1) * RET_DK, h * RET_DV:(h + 1) * RET_DV] = sr0_refs[g][0, h]
                sg_s[g, h * GLA_DK:(h + 1) * GLA_DK, h * GLA_DV:(h + 1) * GLA_DV] = sg0_refs[g][0, h]
            u_s[g, 0:HIST - (CONV_K - 1), :] = jnp.zeros((HIST - (CONV_K - 1), CONV_C), F32)
            u_s[g, HIST - (CONV_K - 1):HIST, :] = cpast_refs[g][0]

    @pl.when(c > 0)
    def _shift():
        for g in range(G):
            u_s[g, 0:HIST, :] = u_s[g, C:C + HIST, :]

    for g in range(G):
        _mix_member(z_refs[g], gd_refs[g], cos_ref, sin_ref, *consts, br_ref.at[g], sr_s.at[g], sg_s.at[g],
                    u_s.at[g], ush_s.at[g], c=c, chunk=chunk, n_chunks=n_chunks, seq_len=seq_len)

    @pl.when(c == n_chunks - 1)
    def _final():
        n_last = seq_len - (n_chunks - 1) * C
        for g in range(G):
            for h in range(HEADS):
                sr_out[g, h] = sr_s[g, h * RET_DK:(h + 1) * RET_DK, h * RET_DV:(h + 1) * RET_DV]
                sg_out[g, h] = sg_s[g, h * GLA_DK:(h + 1) * GLA_DK, h * GLA_DV:(h + 1) * GLA_DV]
            cs_out[g] = u_s[g, HIST + n_last - (CONV_K - 1):HIST + n_last, :]


def _mix_member(z_ref, gd_ref, cos_ref, sin_ref, cw_ref, cb_ref, lng_ref, lnb_ref, rgn_ref, ggn_ref,
                br_ref, sr_s, sg_s, u_s, ush_s, *, c, chunk, n_chunks, seq_len):
    C = chunk
    low = C >= 16
    cast = (lambda a: a.astype(BF16)) if low else (lambda a: a)

    t_col = lax.broadcasted_iota(jnp.int32, (C, 1), 0)
    s_row = lax.broadcasted_iota(jnp.int32, (1, C), 1)
    causal = t_col >= s_row
    tf = t_col.astype(F32)
    padded = n_chunks * C > seq_len
    if padded:
        valid = (c * C + t_col) < seq_len
        nvf = jnp.zeros((1, 1), F32) + jnp.minimum(seq_len - c * C, C).astype(F32)
        keep = lambda a: jnp.where(valid, a, 0.0)
    else:
        nvf = jnp.full((1, 1), float(C), F32)
        keep = lambda a: a

    u = z_ref[:, 0:256] * _sigmoid(z_ref[:, 256:512])
    u_s[HIST:HIST + C, :] = u
    span = C + HIST - 8
    for s in range(1, 8):
        ush_s[s - 1] = u_s[s:s + span, :]
    acc = jnp.zeros((C, CONV_C), F32)
    for j in range(CONV_K):
        lo = HIST - (CONV_K - 1) + j
        q, r = lo - lo % 8, lo % 8
        tap = u_s[q:q + C, :] if r == 0 else ush_s[r - 1, q:q + C, :]
        acc = acc + cw_ref[j:j + 1, :] * tap
    conv = acc + cb_ref[...]
    xc = conv - jnp.mean(conv, axis=-1, keepdims=True)
    ln = xc * lax.rsqrt(jnp.mean(xc * xc, axis=-1, keepdims=True) + EPS) * lng_ref[...] + lnb_ref[...]
    br_ref[:, 0:256] = _silu(ln)

    lane = lax.broadcasted_iota(jnp.int32, (1, 256), 1)
    first_half = (lane % RET_DK) < (RET_DK // 2)
    cos = cos_ref[...]
    sin = sin_ref[...]

    def rope(a):
        swapped = jnp.where(first_half, pltpu.roll(a, 256 - RET_DK // 2, 1), pltpu.roll(a, RET_DK // 2, 1))
        return a * cos + swapped * sin

    rq = rope(z_ref[:, 512:768])
    rk = keep(rope(z_ref[:, 768:1024]) * RET_DK ** -0.5)
    rv = keep(z_ref[:, 1024:1280])
    lane_head = lane // RET_DV
    row_head = lax.broadcasted_iota(jnp.int32, (HEADS * RET_DK, 1), 0) // RET_DK
    log_gamma = [math.log1p(-(2.0 ** (-5 - h))) for h in range(HEADS)]
    lg_lane = jnp.zeros((1, 256), F32)
    lg_row = jnp.zeros((HEADS * RET_DK, 1), F32)
    for h in range(HEADS):
        lg_lane = jnp.where(lane_head == h, log_gamma[h], lg_lane)
        lg_row = jnp.where(row_head == h, log_gamma[h], lg_row)
    avg = _group_avg(256, RET_DV, low)
    dts = (t_col - s_row).astype(F32)
    rk_c = cast(rk)
    o = _dot(cast(rq * jnp.exp((tf + 1.0) * lg_lane)), cast(sr_s[...]))
    for h in range(HEADS):
        own = lane_head == h
        decay = jnp.where(causal, jnp.exp(dts * log_gamma[h]), 0.0)
        a = _dot_nt(cast(jnp.where(own, rq, 0.0)), rk_c) * decay
        o = o + _dot(cast(a), cast(jnp.where(own, rv, 0.0)))
    update = _dot_tn(cast(rk * jnp.exp((nvf - 1.0 - tf) * lg_lane)), cast(rv))
    sr_s[...] = jnp.exp(nvf * lg_row) * sr_s[...] + jnp.where(row_head == lane_head, update, 0.0)
    oc = o - _group_mean(o, avg, low)
    y = oc * lax.rsqrt(_group_mean(oc * oc, avg, low) + EPS) * rgn_ref[...]
    br_ref[:, 256:512] = _silu(z_ref[:, 1280:1536]) * y

    gq = z_ref[:, 1536:1664] * GLA_DK ** -0.5
    gk = keep(z_ref[:, 1664:1792])
    gv = keep(z_ref[:, 1792:2048])
    g = keep(gd_ref[...])
    tri = jnp.where(causal, 1.0, 0.0)
    if low:
        g_hi, g_lo = _split_bf16(g)
        tri = tri.astype(BF16)
        bcum = _dot(tri, g_hi) + _dot(tri, g_lo)
    else:
        bcum = _dot(tri, g)
    mid = C // 2 - 1
    b_mid = bcum[mid:mid + 1, :]
    b_last = bcum[C - 1:C, :]
    q_intra = gq * jnp.exp(bcum - b_mid)
    k_intra = gk * jnp.exp(b_mid - bcum)
    q_inter = gq * jnp.exp(bcum)
    k_state = gk * jnp.exp(b_last - bcum)
    e_last = jnp.where(t_col == C - 1, jnp.exp(bcum), 0.0)
    ones = jnp.ones((C, HEADS * GLA_DV), BF16 if low else F32)
    if low:
        e_hi, e_lo = _split_bf16(e_last)
        e_rows = _dot_tn(e_hi, ones) + _dot_tn(e_lo, ones)
    else:
        e_rows = _dot_tn(e_last, ones)
    key_head = lax.broadcasted_iota(jnp.int32, (1, HEADS * GLA_DK), 1) // GLA_DK
    krow_head = lax.broadcasted_iota(jnp.int32, (HEADS * GLA_DK, 1), 0) // GLA_DK
    k_c = cast(k_intra)
    o = _dot(cast(q_inter), cast(sg_s[...]))
    for h in range(HEADS):
        a = jnp.where(causal, _dot_nt(cast(jnp.where(key_head == h, q_intra, 0.0)), k_c), 0.0)
        o = o + _dot(cast(a), cast(jnp.where(lane_head == h, gv, 0.0)))
    update = _dot_tn(cast(k_state), cast(gv))
    sg_s[...] = e_rows * sg_s[...] + jnp.where(krow_head == lane_head, update, 0.0)
    y = o * lax.rsqrt(_group_mean(o * o, avg, low) + EPS) * ggn_ref[...]
    br_ref[:, 512:768] = _silu(z_ref[:, 2048:2304]) * y

def _mix_call(zlin, gdec, cos, sin, conv_past, ret0, gla0, w, *, layer, state_layer, n_seq, chunk, n_chunks,
              seq_len, row_block0, name):
    G = MIX_GROUP
    assert n_seq % G == 0
    blk = lambda n, g: pl.BlockSpec((chunk, n), lambda s, c: (row_block0 + (s * G + g) * n_chunks + c, 0))
    grouped = lambda shp: pl.BlockSpec((None, G) + shp, lambda s, c: (s, 0) + (0,) * len(shp))
    state = lambda shp, g: pl.BlockSpec((None, 1) + shp,
                                        lambda s, c: (state_layer, s * G + g) + (0,) * len(shp))
    consts = [w['cw'], w['cb'], w['lng'], w['lnb'], w['rgn'], w['ggn']]
    members = range(G)
    in_specs = ([blk(N_LIN, g) for g in members] + [blk(LANE, g) for g in members]
                + [pl.BlockSpec((chunk, 256), lambda s, c: (c, 0)), pl.BlockSpec((chunk, 256), lambda s, c: (c, 0))]
                + [state((CONV_K - 1, CONV_C), g) for g in members]
                + [state((HEADS, RET_DK, RET_DV), g) for g in members]
                + [state((HEADS, GLA_DK, GLA_DV), g) for g in members]
                + [_layer_spec(a, layer) for a in consts])
    args = [zlin] * G + [gdec] * G + [cos, sin] + [conv_past] * G + [ret0] * G + [gla0] * G + consts
    br, sr, sg, cs = pl.pallas_call(
        functools.partial(_mix_kernel, chunk=chunk, n_chunks=n_chunks, seq_len=seq_len, group=G),
        grid=(n_seq // G, n_chunks),
        in_specs=in_specs,
        out_specs=[pl.BlockSpec((None, None, G, chunk, 3 * BRANCH_W), lambda s, c: (s, c, 0, 0, 0)),
                   grouped((HEADS, RET_DK, RET_DV)), grouped((HEADS, GLA_DK, GLA_DV)),
                   grouped((CONV_K - 1, CONV_C))],
        out_shape=[jax.ShapeDtypeStruct((n_seq // G, n_chunks, G, chunk, 3 * BRANCH_W), F32),
                   jax.ShapeDtypeStruct((n_seq // G, G, HEADS, RET_DK, RET_DV), F32),
                   jax.ShapeDtypeStruct((n_seq // G, G, HEADS, GLA_DK, GLA_DV), F32),
                   jax.ShapeDtypeStruct((n_seq // G, G, CONV_K - 1, CONV_C), F32)],
        scratch_shapes=[pltpu.VMEM((G, HEADS * RET_DK, HEADS * RET_DV), F32),
                        pltpu.VMEM((G, HEADS * GLA_DK, HEADS * GLA_DV), F32),
                        pltpu.VMEM((G, HIST + chunk, CONV_C), F32),
                        pltpu.VMEM((G, 7, HIST + chunk - 8, CONV_C), F32)],
        compiler_params=pltpu.CompilerParams(dimension_semantics=("arbitrary", "arbitrary")),
        name=name,
    )(*args)
    flat = lambda a: a.reshape((n_seq,) + a.shape[2:])
    return br, flat(sr), flat(sg), flat(cs)


def _t5_bucket_np(rel):
    rel = np.asarray(rel)
    n = np.maximum(rel, 0)
    max_exact = REL_BUCKETS // 2
    nf = np.maximum(n, 1).astype(np.float64)
    large = max_exact + (np.log(nf / max_exact) / math.log(REL_MAX_DIST / max_exact)
                         * (REL_BUCKETS - max_exact)).astype(np.int64)
    large = np.minimum(large, REL_BUCKETS - 1)
    return np.where(n < max_exact, n, large).astype(np.int32)


def _lambda(lamv_ref, lam_init):
    a = jnp.sum(lamv_ref[0:1, :] * lamv_ref[1:2, :], axis=-1, keepdims=True)
    b = jnp.sum(lamv_ref[2:3, :] * lamv_ref[3:4, :], axis=-1, keepdims=True)
    return jnp.exp(a) - jnp.exp(b) + lam_init


def _flash_kernel(bt_ref, rb_ref, top_ref, lamv_ref, sub_ref, qt_ref, k_ref, vt_ref, o_ref,
                  bias_s, qbd_s, m_s, l_s, acc_s, sta_s, stb_s, lf_s, accf_s, *, layer, lam_init, n_blocks):
    QB = SEQ_BLOCK
    SUB = FAR_BLOCK // QB
    b = pl.program_id(0)
    i = pl.program_id(1)

    @pl.when((b == 0) & (i == 0))
    def _tables():
        for kind, src in ((1, 1), (2, 0)):
            bt = bt_ref[src]
            tiles = [jnp.zeros((QB, QB), F32) for _ in range(HEADS)]
            for bk in range(REL_BUCKETS):
                hit = bt == bk
                for h in range(HEADS):
                    tiles[h] = jnp.where(hit, rb_ref[bk, h] * LOG2E, tiles[h])
            for h in range(HEADS):
                bias_s[kind, h] = jnp.where(bt < 0, NEG, tiles[h])
        for h in range(HEADS):
            bias_s[0, h] = jnp.zeros((QB, QB), F32) + rb_ref[REL_BUCKETS - 1, h] * LOG2E
            bias_s[3, h] = jnp.full((QB, QB), NEG, F32)
            for kind in range(4):
                bias_s[4 + kind, h] = bias_s[kind, h] - top_ref[layer, h]

    qt = qt_ref[...] * (DIFF_HD ** -0.5 * LOG2E)
    first_branch = lax.broadcasted_iota(jnp.int32, (2 * DIFF_HD, 1), 0) < DIFF_HD
    for h in range(HEADS):
        qh = qt[h * 2 * DIFF_HD:(h + 1) * 2 * DIFF_HD, :]
        qbd_s[h, :, 0:QB] = jnp.where(first_branch, qh, 0.0).astype(BF16)
        qbd_s[h, :, QB:2 * QB] = jnp.where(first_branch, 0.0, qh).astype(BF16)
    m_s[...] = jnp.full(m_s.shape, NEG, F32)
    l_s[...] = jnp.zeros(l_s.shape, F32)
    acc_s[...] = jnp.zeros(acc_s.shape, F32)

    def online(h, st, shift, vt):
        m_old = m_s[h]
        m_new = jnp.maximum(m_old, jnp.max(st, axis=0, keepdims=True) + shift)
        alpha = jnp.exp2(m_old - m_new)
        pt = jnp.exp2(st - (m_new - shift))
        l_s[h] = alpha * l_s[h] + jnp.sum(pt, axis=0, keepdims=True)
        m_s[h] = m_new
        acc_s[h] = alpha * acc_s[h] + _dot(vt[h * DIFF_VD:(h + 1) * DIFF_VD, :], pt.astype(BF16))

    def fixed(h, st, vt):
        pt = jnp.exp2(st)
        l_s[h] = l_s[h] + jnp.sum(pt, axis=0, keepdims=True)
        acc_s[h] = acc_s[h] + _dot(vt[h * DIFF_VD:(h + 1) * DIFF_VD, :], pt.astype(BF16))

    n_far = jnp.maximum(i - 1, 0) // SUB

    def sweep(accumulate):
        def far_body(kb, carry):
            off = pl.multiple_of(kb * FAR_BLOCK, FAR_BLOCK)
            vt = vt_ref[:, pl.ds(off, FAR_BLOCK)].astype(BF16)
            for h in range(HEADS):
                st = _dot(k_ref[h, pl.ds(off, FAR_BLOCK), :].astype(BF16), qbd_s[h])
                accumulate(h, st, rb_ref[REL_BUCKETS - 1, h] * LOG2E, vt)
            return carry

        lax.fori_loop(0, n_far, far_body, 0)

        def near_body(kb, carry):
            first = kb * SUB
            blk0 = jnp.minimum(first, n_blocks - SUB)
            off = pl.multiple_of(blk0 * QB, QB)
            vt = vt_ref[:, pl.ds(off, FAR_BLOCK)].astype(BF16)
            kinds = []
            for c in range(SUB):
                blk = blk0 + c
                kind = jnp.where(blk == i, 2, jnp.where(blk == i - 1, 1, 0))
                kinds.append(jnp.where((blk > i) | (blk < first), 3, kind))
            for h in range(HEADS):
                st = _dot(k_ref[h, pl.ds(off, FAR_BLOCK), :].astype(BF16), qbd_s[h])
                parts = []
                for c in range(SUB):
                    tile = bias_s[kinds[c], h]
                    parts.append(st[c * QB:(c + 1) * QB, :] + jnp.concatenate([tile, tile], axis=1))
                accumulate(h, jnp.concatenate(parts, axis=0), 0.0, vt)
            return carry

        lax.fori_loop(n_far, i // SUB + 1, near_body, 0)

    bounded = top_ref[layer, HEADS] > 0.5

    def window(j):
        first = j * SUB
        return first, jnp.minimum(first, n_blocks - SUB)

    def score(j, buf):
        _, blk0 = window(j)
        off = pl.multiple_of(blk0 * QB, QB)
        for h in range(HEADS):
            buf[h] = _dot(k_ref[h, pl.ds(off, FAR_BLOCK), :].astype(BF16), qbd_s[h])

    def consume(j, buf):
        first, blk0 = window(j)
        vt = vt_ref[:, pl.ds(pl.multiple_of(blk0 * QB, QB), FAR_BLOCK)].astype(BF16)
        kinds = []
        for c in range(SUB):
            blk = blk0 + c
            kind = jnp.where(blk == i, 2, jnp.where(blk == i - 1, 1, 0))
            kinds.append(jnp.where((blk > i) | (blk < first), 3, kind))
        for h in range(HEADS):
            parts = []
            for c in range(SUB):
                tile = bias_s[kinds[c] + 4, h]
                parts.append(buf[h, c * QB:(c + 1) * QB, :] + jnp.concatenate([tile, tile], axis=1))
            fixed(h, jnp.concatenate(parts, axis=0), vt)

    def consume_far(j, buf):
        vt = vt_ref[:, pl.ds(pl.multiple_of(j * FAR_BLOCK, FAR_BLOCK), FAR_BLOCK)].astype(BF16)
        for h in range(HEADS):
            pt = jnp.exp2(buf[h])
            lf_s[h] = lf_s[h] + jnp.sum(pt, axis=0, keepdims=True)
            accf_s[h] = accf_s[h] + _dot(vt[h * DIFF_VD:(h + 1) * DIFF_VD, :], pt.astype(BF16))

    @pl.when(bounded)
    def _fixed():
        n_win = i // SUB + 1
        far_pairs = n_far // 2
        lf_s[...] = jnp.zeros(lf_s.shape, F32)
        accf_s[...] = jnp.zeros(accf_s.shape, F32)
        score(0, sta_s)

        def far_pair(m, carry):
            score(2 * m + 1, stb_s)
            consume_far(2 * m, sta_s)
            score(2 * m + 2, sta_s)
            consume_far(2 * m + 1, stb_s)
            return carry

        lax.fori_loop(0, far_pairs, far_pair, 0)
        w0 = 2 * far_pairs
        rest = n_win - w0

        def pair(m, carry):
            score(w0 + 2 * m + 1, stb_s)
            consume(w0 + 2 * m, sta_s)
            score(w0 + 2 * m + 2, sta_s)
            consume(w0 + 2 * m + 1, stb_s)
            return carry

        lax.fori_loop(0, rest // 2, pair, 0)

        @pl.when(rest % 2 == 1)
        def _tail():
            consume(n_win - 1, sta_s)

        for h in range(HEADS):
            far_scale = jnp.exp2(jnp.zeros((1, 1), F32) + (rb_ref[REL_BUCKETS - 1, h] * LOG2E - top_ref[layer, h]))
            l_s[h] = l_s[h] + far_scale * lf_s[h]
            acc_s[h] = acc_s[h] + far_scale * accf_s[h]

    @pl.when(jnp.logical_not(bounded))
    def _online():
        sweep(online)

    lam = _lambda(lamv_ref, lam_init)
    outs = []
    for h in range(HEADS):
        ot = acc_s[h] / l_s[h]
        d = ot[:, 0:QB] - lam * ot[:, QB:2 * QB]
        y = d * lax.rsqrt(jnp.mean(d * d, axis=0, keepdims=True) + EPS) * sub_ref[...] * (1.0 - lam_init)
        outs.append(y)
    o_ref[...] = jnp.concatenate(outs, axis=0).T


def _score_top(rel_bias, q_gain, k_gain):
    reach = DIFF_HD ** 0.5 * jnp.max(jnp.abs(q_gain * k_gain), axis=-1, keepdims=True)
    top = (reach + jnp.max(rel_bias, axis=0)[None, :]) * LOG2E
    low = (-reach + jnp.min(rel_bias, axis=0)[None, :]) * LOG2E
    ok = jnp.all(top - low < MAX_EXP2_SPAN, axis=-1, keepdims=True).astype(F32)
    return jnp.concatenate([top, ok], axis=-1).astype(F32)


def _flash_call(bt, rel_bias, top, lamv, sub_col, dqt, dk, dvt, *, layer, n_seq, seq_rows, lam_init):
    rows = n_seq * seq_rows
    nq = seq_rows // SEQ_BLOCK
    assert seq_rows >= FAR_BLOCK
    smem = pl.BlockSpec(memory_space=pltpu.SMEM)
    return pl.pallas_call(
        functools.partial(_flash_kernel, layer=layer, lam_init=lam_init, n_blocks=nq),
        grid=(n_seq, nq),
        in_specs=[_whole_spec(bt), smem, smem, _layer_spec(lamv, layer), _layer_spec(sub_col, layer),
                  pl.BlockSpec((None, 256, SEQ_BLOCK), lambda b, i: (b, 0, i)),
                  pl.BlockSpec((None, HEADS, seq_rows, 2 * DIFF_HD), lambda b, i: (b, 0, 0, 0)),
                  pl.BlockSpec((None, 256, seq_rows), lambda b, i: (b, 0, 0))],
        out_specs=pl.BlockSpec((SEQ_BLOCK, 256), lambda b, i: (b * nq + i, 0)),
        out_shape=jax.ShapeDtypeStruct((rows, 256), F32),
        scratch_shapes=[pltpu.VMEM((8, HEADS, SEQ_BLOCK, SEQ_BLOCK), F32),
                        pltpu.VMEM((HEADS, 2 * DIFF_HD, 2 * SEQ_BLOCK), BF16),
                        pltpu.VMEM((HEADS, 1, 2 * SEQ_BLOCK), F32), pltpu.VMEM((HEADS, 1, 2 * SEQ_BLOCK), F32),
                        pltpu.VMEM((HEADS, DIFF_VD, 2 * SEQ_BLOCK), F32),
                        pltpu.VMEM((HEADS, FAR_BLOCK, 2 * SEQ_BLOCK), F32),
                        pltpu.VMEM((HEADS, FAR_BLOCK, 2 * SEQ_BLOCK), F32),
                        pltpu.VMEM((HEADS, 1, 2 * SEQ_BLOCK), F32),
                        pltpu.VMEM((HEADS, DIFF_VD, 2 * SEQ_BLOCK), F32)],
        compiler_params=pltpu.CompilerParams(dimension_semantics=("arbitrary", "arbitrary"),
                                             vmem_limit_bytes=VMEM_LIMIT),
        name="attn_prompt",
    )(bt, rel_bias, top, lamv, sub_col, dqt, dk, dvt)


def _decode_kernel(pt_ref, bt_ref, rb_ref, lamv_ref, sub_ref, q_ref, kn_ref, vn_ref, *refs,
                   pages_per_step, n_steps, dec_seq, lam_init):
    G = pages_per_step
    kt_refs = refs[:G]
    vt_refs = refs[G:2 * G]
    o_ref, bias_pg_s, bias_new_s, far_s, qs_s, m_s, l_s, acc_s = refs[2 * G:]
    del pt_ref
    b = pl.program_id(0)
    g = pl.program_id(1)
    last = g == n_steps - 1
    NQ = 2 * HEADS * dec_seq
    seqs_per_block = kn_ref.shape[0] // dec_seq

    row_head = lax.broadcasted_iota(jnp.int32, (NQ, 1), 0) // (2 * dec_seq)

    def bias_rows(bk):
        out = jnp.zeros((NQ, 1), F32)
        for h in range(HEADS):
            out = jnp.where(row_head == h, rb_ref[bk, h], out)
        return out

    @pl.when((b == 0) & (g == 0))
    def _tables():
        for src, dst in ((0, bias_pg_s), (1, bias_new_s)):
            bt = bt_ref[src]
            tile = jnp.zeros(bt.shape, F32)
            for bk in range(REL_BUCKETS):
                tile = jnp.where(bt == bk, bias_rows(bk), tile)
            dst[...] = jnp.where(bt < 0, NEG, tile)
        far_s[...] = jnp.zeros(far_s.shape, F32) + bias_rows(REL_BUCKETS - 1)

    @pl.when(g == 0)
    def _init():
        q = q_ref[...] * DIFF_HD ** -0.5
        lane = lax.broadcasted_iota(jnp.int32, (1, 256), 1)
        for hj in range(2 * HEADS):
            own = (lane >= hj * DIFF_HD) & (lane < (hj + 1) * DIFF_HD)
            qs_s[hj * dec_seq:(hj + 1) * dec_seq, :] = jnp.where(own, q, 0.0)
        m_s[...] = jnp.full(m_s.shape, NEG, F32)
        l_s[...] = jnp.zeros(l_s.shape, F32)
        acc_s[...] = jnp.zeros(acc_s.shape, F32)

    def update(scores, pv_fn):
        m_old = m_s[...]
        s_max = scores[0]
        for s in scores[1:]:
            s_max = jnp.maximum(s_max, s)
        m_new = jnp.maximum(m_old, jnp.max(s_max, axis=1, keepdims=True))
        alpha = jnp.exp(m_old - m_new)
        probs = [jnp.exp(s - m_new) for s in scores]
        p_sum = probs[0]
        for p in probs[1:]:
            p_sum = p_sum + p
        l_s[...] = alpha * l_s[...] + jnp.sum(p_sum, axis=1, keepdims=True)
        m_s[...] = m_new
        acc_s[...] = jnp.concatenate([alpha, alpha], axis=1) * acc_s[...] + pv_fn(probs)

    qs = qs_s[...].astype(BF16)
    scores = []
    for p in range(G):
        s = _dot(qs, kt_refs[p][...].astype(BF16))
        if p == G - 1:
            s = s + jnp.where(last, bias_pg_s[...], far_s[...])
        else:
            s = s + far_s[...]
        scores.append(s)

    def pv_pages(probs):
        out = None
        for p in range(G):
            t = _dot_nt(probs[p].astype(BF16), vt_refs[p][...].astype(BF16))
            out = t if out is None else out + t
        return out

    update(scores, pv_pages)

    @pl.when(last)
    def _finish():
        key_seq = lax.broadcasted_iota(jnp.int32, (1, kn_ref.shape[0]), 1) // dec_seq
        s_new = _dot_nt(qs, kn_ref[...].astype(BF16))
        s_new = jnp.where(key_seq == b % seqs_per_block, s_new + bias_new_s[...], NEG)
        update([s_new], lambda probs: _dot(probs[0].astype(BF16), vn_ref[...].astype(BF16)))
        l_all = l_s[...]
        o_all = acc_s[...] / jnp.concatenate([l_all, l_all], axis=1)
        lam = _lambda(lamv_ref, lam_init)
        for h in range(HEADS):
            r1 = (2 * h) * dec_seq
            r2 = (2 * h + 1) * dec_seq
            cols = slice(h * DIFF_VD, (h + 1) * DIFF_VD)
            d = o_all[r1:r1 + dec_seq, cols] - lam * o_all[r2:r2 + dec_seq, cols]
            y = d * lax.rsqrt(jnp.mean(d * d, axis=-1, keepdims=True) + EPS) * sub_ref[...] * (1.0 - lam_init)
            o_ref[:, cols] = y


def _decode_call(page_table, bt, rel_bias, lamv, sub, dq, dk, dv, cache_kt, cache_vt, *,
                 layer, n_seq, dec_seq, row0, lam_init):
    n_pages = page_table.shape[1]
    page = cache_kt.shape[3]
    G = min(PAGES_PER_STEP, n_pages)
    n_steps = n_pages // G
    nq = 2 * HEADS * dec_seq
    new_rows = LANE
    per_blk = new_rows // dec_seq
    own = pl.BlockSpec((dec_seq, 256), lambda b, g, pt: (row0 // dec_seq + b, 0))
    new = pl.BlockSpec((new_rows, 256), lambda b, g, pt: (row0 // new_rows + b // per_blk, 0))

    def page_spec(p):
        return pl.BlockSpec((None, None, 256, page),
                            lambda b, g, pt: (layer, pt[b * n_pages + g * G + p], 0, 0))

    in_specs = ([_whole_spec(bt), pl.BlockSpec(memory_space=pltpu.SMEM), _layer_spec(lamv, layer),
                 _layer_spec(sub, layer), own, new, new]
                + [page_spec(p) for p in range(G)] + [page_spec(p) for p in range(G)])
    args = [bt, rel_bias, lamv, sub, dq, dk, dv] + [cache_kt] * G + [cache_vt] * G
    return pl.pallas_call(
        functools.partial(_decode_kernel, pages_per_step=G, n_steps=n_steps, dec_seq=dec_seq, lam_init=lam_init),
        grid_spec=pltpu.PrefetchScalarGridSpec(
            num_scalar_prefetch=1,
            grid=(n_seq, n_steps),
            in_specs=in_specs,
            out_specs=pl.BlockSpec((dec_seq, 256), lambda b, g, pt: (b, 0)),
            scratch_shapes=[pltpu.VMEM((nq, page), F32), pltpu.VMEM((nq, new_rows), F32),
                            pltpu.VMEM((nq, page), F32), pltpu.VMEM((nq, 256), F32),
                            pltpu.VMEM((nq, LANE), F32), pltpu.VMEM((nq, LANE), F32), pltpu.VMEM((nq, 256), F32)]),
        out_shape=jax.ShapeDtypeStruct((n_seq * dec_seq, 256), F32),
        compiler_params=pltpu.CompilerParams(dimension_semantics=("arbitrary", "arbitrary"),
                                             vmem_limit_bytes=VMEM_LIMIT),
        name="attn_sample",
    )(page_table.reshape(-1), *args)


def _rope_tables(pos):
    half = RET_DK // 2
    inv = ROPE_BASE ** (-np.arange(half, dtype=np.float64) / half)
    ang = np.asarray(pos, np.float64)[:, None] * inv[None, :]
    cos = np.concatenate([np.cos(ang), np.cos(ang)], axis=1)
    sin = np.concatenate([-np.sin(ang), np.sin(ang)], axis=1)
    return (jnp.asarray(np.tile(cos, (1, HEADS)), F32), jnp.asarray(np.tile(sin, (1, HEADS)), F32))


def _prompt_bucket_tiles():
    t = np.arange(SEQ_BLOCK)
    rel = t[None, :] - t[:, None]
    diag = np.where(rel >= 0, _t5_bucket_np(rel), -1)
    sub = _t5_bucket_np(rel + SEQ_BLOCK)
    return jnp.asarray(np.stack([diag, sub]).astype(np.int32))


def _sample_bucket_tiles(page, dec_seq):
    assert page == LANE
    iq = (np.arange(2 * HEADS * dec_seq) % dec_seq)[:, None]
    past = _t5_bucket_np(page + iq - np.arange(page)[None, :])
    rel_new = iq - (np.arange(LANE) % dec_seq)[None, :]
    new = np.where(rel_new >= 0, _t5_bucket_np(rel_new), -1)
    return jnp.asarray(np.stack([past, new]).astype(np.int32))


def kernel(x_prompt, x_sample, cache_k, cache_v, page_table, state_ret, state_gla, state_conv, meta_tokens,
           rel_bias, norm_ffn1, ffn1_gate, ffn1_up, ffn1_down, norm_mix, w_in, b_in, conv_w, conv_b, conv_ln_g,
           conv_ln_b, ret_gn, gla_alpha_w, gla_alpha_b, gla_gn, q_norm, k_norm, lam_q1, lam_k1, lam_q2, lam_k2,
           diff_subln, w_branch, w_out, norm_ffn2, ffn2_gate, ffn2_up, ffn2_down):
    B, S, D = x_prompt.shape
    DB, DS, _ = x_sample.shape
    depth = w_in.shape[0]
    L = S + N_META
    Lp = -(-L // SEQ_BLOCK) * SEQ_BLOCK
    n_chunks = Lp // SEQ_BLOCK
    n_pool, page = cache_k.shape[1], cache_k.shape[2]
    past_len = page_table.shape[1] * page
    rows_p = B * Lp
    rows_s = DB * DS
    rows_sp = -(-rows_s // ROW_TILE) * ROW_TILE
    assert Lp % ROW_TILE == 0 and rows_s % LANE == 0 and DS % 8 == 0

    rows = rows_p + rows_sp
    h = None
    pad_s = lambda a: jnp.pad(a, ((0, rows_sp - rows_s), (0, 0)))

    cos_p, sin_p = _rope_tables(np.arange(Lp))
    cos_s, sin_s = _rope_tables(past_len + np.arange(DS))
    bt_prompt = _prompt_bucket_tiles()
    bt_sample = _sample_bucket_tiles(page, DS)
    ckt = jnp.transpose(cache_k, (0, 1, 3, 4, 2)).reshape(depth, n_pool, HEADS * 2 * DIFF_HD, page)
    cvt = jnp.transpose(cache_v, (0, 1, 3, 4, 2)).reshape(depth, n_pool, HEADS * DIFF_VD, page)
    zeros_conv = jnp.zeros((1, B, CONV_K - 1, CONV_C), F32)
    zeros_ret = jnp.zeros((1, B, HEADS, RET_DK, RET_DV), F32)
    zeros_gla = jnp.zeros((1, B, HEADS, GLA_DK, GLA_DV), F32)

    row3 = lambda a: a.reshape(depth, 1, -1).astype(F32)
    w = dict(
        g1=row3(norm_ffn1), wg1=ffn1_gate.astype(BF16), wu1=ffn1_up.astype(BF16), wd1=ffn1_down.astype(BF16),
        gm=row3(norm_mix),
        wlin=w_in[:, :, :N_LIN].astype(BF16), blin=row3(b_in[:, :N_LIN]),
        wlow=w_in[:, :, OFF_GLOW:OFF_DIFF].astype(BF16), blow=row3(b_in[:, OFF_GLOW:OFF_DIFF]),
        aw=gla_alpha_w.astype(BF16), ab=row3(gla_alpha_b),
        wdf=w_in[:, :, OFF_DIFF:OFF_GATES].astype(BF16), bdf=row3(b_in[:, OFF_DIFF:OFF_GATES]),
        qg=row3(jnp.tile(q_norm, (1, 2 * HEADS))), kg=row3(jnp.tile(k_norm, (1, 2 * HEADS))),
        wgt=w_in[:, :, OFF_GATES:].astype(BF16), bgt=row3(b_in[:, OFF_GATES:]),
        wb=w_branch.astype(BF16), wo=w_out.astype(BF16), g2=row3(norm_ffn2),
        wg2=ffn2_gate.astype(BF16), wu2=ffn2_up.astype(BF16), wd2=ffn2_down.astype(BF16),
        cw=conv_w, cb=row3(conv_b), lng=row3(conv_ln_g), lnb=row3(conv_ln_b),
        rgn=row3(ret_gn), ggn=row3(gla_gn),
    )
    lamv = jnp.stack([lam_q1, lam_k1, lam_q2, lam_k2], axis=1).astype(F32)
    sub = row3(diff_subln)
    sub_col = diff_subln.reshape(depth, -1, 1).astype(F32)
    top = _score_top(rel_bias, q_norm, k_norm)

    per_layer = []
    for l in range(depth):
        lam_init = 0.8 - 0.6 * math.exp(-0.3 * l)
        h1, zlin, gdec, dq, dk, dv, qt, kt, vt, kh = _head_call(
            h, w, layer=l, n_seq=B, seq_rows=Lp, rows=rows,
            model_inputs=(x_prompt, x_sample.reshape(rows_s, D), meta_tokens.astype(F32)))

        brl_p, rp, gp, cp = _mix_call(zlin, gdec, cos_p, sin_p, zeros_conv, zeros_ret, zeros_gla, w,
                                      layer=l, state_layer=0, n_seq=B, chunk=SEQ_BLOCK, n_chunks=n_chunks,
                                      seq_len=L, row_block0=0, name="mix_prompt")
        brl_s, rs, gs, cs = _mix_call(zlin, gdec, cos_s, sin_s, state_conv, state_ret, state_gla, w,
                                      layer=l, state_layer=l, n_seq=DB, chunk=DS, n_chunks=1, seq_len=DS,
                                      row_block0=rows_p // DS, name="mix_sample")
        brd_p = _flash_call(bt_prompt, rel_bias, top, lamv, sub_col, qt, kh, vt,
                            layer=l, n_seq=B, seq_rows=Lp, lam_init=lam_init)
        brd_s = _decode_call(page_table, bt_sample, rel_bias, lamv, sub, dq, dk, dv, ckt, cvt,
                             layer=l, n_seq=DB, dec_seq=DS, row0=rows_p, lam_init=lam_init)

        h = _tail_call(h1, brl_p, pad_s(brl_s.reshape(rows_s, -1)), brd_p, pad_s(brd_s), w, layer=l,
                       final=(B, S, rows_s) if l == depth - 1 else None)
        per_layer.append(dict(kt=kt, vt=vt, ks=dk[rows_p:rows_p + rows_s], vs=dv[rows_p:rows_p + rows_s],
                              rp=rp, rs=rs, gp=gp, gs=gs, cp=cp, cs=cs))

    st = lambda k: jnp.stack([p[k] for p in per_layer], axis=0)
    seq_major = lambda t: jnp.transpose(t.reshape(depth, B, HEADS, -1, Lp)[..., :L], (0, 1, 4, 2, 3))
    y_prompt, y_rows = h
    y_sample = y_rows.reshape(DB, DS, D)
    return (y_prompt, y_sample, seq_major(st('kt')), seq_major(st('vt')),
            st('ks').reshape(depth, DB, DS, HEADS, 2 * DIFF_HD), st('vs').reshape(depth, DB, DS, HEADS, DIFF_VD),
            st('rp'), st('rs'), st('gp'), st('gs'), st('cp'), st('cs'))
```

```python
import functools
import math

import numpy as np
import jax
import jax.numpy as jnp
from jax import lax
from jax.experimental import pallas as pl
from jax.experimental.pallas import tpu as pltpu

F32 = jnp.float32
BF16 = jnp.bfloat16

D_MODEL = 1024
N_META = 16
N_BRANCH = 4
BRANCH_W = 256
D_FF = 2816
CONV_C = 256
CONV_K = 31
HEADS = 4
RET_DK = 64
RET_DV = 64
GLA_DK = 32
GLA_DV = 64
GLA_RANK = 16
GLA_TAU = 16.0
DIFF_HD = 32
DIFF_VD = 64
REL_BUCKETS = 32
REL_MAX_DIST = 128
ROPE_BASE = 10000.0
EPS = 1e-6
NEG = -1e30
LOG2E = math.log2(math.e)
MAX_EXP2_SPAN = 100.0

N_LIN = 2304
OFF_GLOW = N_LIN
OFF_DIFF = OFF_GLOW + GLA_RANK
OFF_GATES = OFF_DIFF + 3 * 256

LANE = 128
ROW_TILE = 384
FFN_CHUNKS = 1
MXU_N = 256
SEQ_BLOCK = 128
FAR_BLOCK = 512
HIST = 32
VMEM_LIMIT = 60 * 1024 * 1024
MIX_GROUP = 4
MIX_GROUP_SAMPLE = 4
PAGES_PER_STEP = 64


def _sigmoid(x):
    return 1.0 / (1.0 + jnp.exp(-x))


def _silu(x):
    return x * _sigmoid(x)


def _log_sigmoid(x):
    return jnp.minimum(x, 0.0) - jnp.log1p(jnp.exp(-jnp.abs(x)))


def _rms(x, g):
    return x * lax.rsqrt(jnp.mean(x * x, axis=-1, keepdims=True) + EPS) * g


def _dot(a, b):
    return jnp.dot(a, b, preferred_element_type=F32)


def _dot_nt(a, b):
    return lax.dot_general(a, b, (((1,), (1,)), ((), ())), preferred_element_type=F32)


def _dot_tn(a, b):
    return lax.dot_general(a, b, (((0,), (0,)), ((), ())), preferred_element_type=F32)


def _split_bf16(x):
    hi = x.astype(BF16)
    lo = (x - hi.astype(F32)).astype(BF16)
    return hi, lo


def _group_avg(n, group, low=True):
    r = lax.broadcasted_iota(jnp.int32, (n, n), 0) // group
    c = lax.broadcasted_iota(jnp.int32, (n, n), 1) // group
    return jnp.where(r == c, 1.0 / group, 0.0).astype(BF16 if low else F32)


def _group_mean(x, avg, low=True):
    if not low:
        return _dot(x, avg)
    hi, lo = _split_bf16(x)
    return _dot(hi, avg) + _dot(lo, avg)


def _layer_spec(a, layer, single_buffer=False):
    nd = a.ndim - 1
    kw = dict(pipeline_mode=pl.Buffered(1)) if single_buffer else {}
    return pl.BlockSpec((None,) + a.shape[1:], lambda *_: (layer,) + (0,) * nd, **kw)


def _whole_spec(a):
    return pl.BlockSpec(a.shape, lambda *_: (0,) * a.ndim)


def _ffn(x, wg_ref, wu_ref, wd_ref):
    tiles = D_FF // MXU_N
    edges = [(tiles * c // FFN_CHUNKS) * MXU_N for c in range(FFN_CHUNKS + 1)]
    out = None
    for c in range(FFN_CHUNKS):
        cols = slice(edges[c], edges[c + 1])
        act = (_silu(_dot(x, wg_ref[:, cols])) * _dot(x, wu_ref[:, cols])).astype(BF16)
        part = _dot(act, wd_ref[cols, :])
        out = part if out is None else out + part
    return out


def _assemble_rows(xcur_ref, xprev_ref, xlast_ref, xs_ref, meta_ref, *, prompt_tiles, per_seq):
    i = pl.program_id(0)
    j = i % per_seq
    cut = ROW_TILE - N_META
    prev_tail = xprev_ref[cut:, :]
    cur_head = xcur_ref[0:cut, :]
    x_last = xlast_ref[...]
    pad = jnp.zeros((cut - x_last.shape[0], D_MODEL), F32)
    body = jnp.concatenate([prev_tail, cur_head], axis=0)
    first = jnp.concatenate([meta_ref[...], cur_head], axis=0)
    last = jnp.concatenate([prev_tail, x_last, pad], axis=0)
    xs = xs_ref[...]
    sample = jnp.concatenate([xs, jnp.zeros((ROW_TILE - xs.shape[0], D_MODEL), F32)], axis=0)
    rows = jnp.where(j == 0, first, jnp.where(j == per_seq - 1, last, body))
    return jnp.where(i >= prompt_tiles, sample, rows)


def _head_kernel(*refs, prompt_tiles, per_seq, first_layer):
    n_in = 5 if first_layer else 1
    (g1_ref, wg_ref, wu_ref, wd_ref, gm_ref, wlin_ref, blin_ref, wlow_ref, blow_ref,
     aw_ref, ab_ref, wdf_ref, bdf_ref, qg_ref, kg_ref,
     h1_ref, zlin_ref, gdec_ref, dq_ref, dk_ref, dv_ref, qt_ref, kt_ref, vt_ref, kh_ref) = refs[n_in:]
    if first_layer:
        h = _assemble_rows(*refs[:n_in], prompt_tiles=prompt_tiles, per_seq=per_seq)
    else:
        h = refs[0][...]
    h1 = h + 0.5 * _ffn(_rms(h, g1_ref[...]).astype(BF16), wg_ref, wu_ref, wd_ref)
    h1_ref[...] = h1
    x = _rms(h1, gm_ref[...]).astype(BF16)
    zlin_ref[...] = _dot(x, wlin_ref[...]) + blin_ref[...]
    g_low = _dot(x, wlow_ref[...]) + blow_ref[...]
    g_pre = _dot(g_low.astype(BF16), aw_ref[...]) + ab_ref[...]
    gdec_ref[...] = _log_sigmoid(g_pre) * (1.0 / GLA_TAU)
    zd = _dot(x, wdf_ref[...]) + bdf_ref[...]
    d_q = zd[:, 0:256]
    d_k = zd[:, 256:512]
    d_v = zd[:, 512:768]
    avg = _group_avg(256, DIFF_HD)
    q_n = d_q * lax.rsqrt(_group_mean(d_q * d_q, avg) + EPS) * qg_ref[...]
    dq_ref[...] = q_n
    k_n = d_k * lax.rsqrt(_group_mean(d_k * d_k, avg) + EPS) * kg_ref[...]
    dk_ref[...] = k_n
    dv_ref[...] = d_v

    @pl.when(pl.program_id(0) < prompt_tiles)
    def _transposed():
        qt_ref[...] = q_n.T
        kt_ref[...] = k_n.T
        vt_ref[...] = d_v.T
        for h in range(HEADS):
            kh_ref[h] = k_n[:, h * 2 * DIFF_HD:(h + 1) * 2 * DIFF_HD]


def _head_call(h, w, *, layer, n_seq, seq_rows, rows, model_inputs=None):
    per_seq = seq_rows // ROW_TILE
    p_tiles = n_seq * per_seq
    row = lambda n: pl.BlockSpec((ROW_TILE, n), lambda i: (i, 0))
    if h is not None:
        data, data_specs = [h], [row(D_MODEL)]
    else:
        x_prompt, x_rows, meta = model_inputs
        seq = x_prompt.shape[1]
        x_last = seq - ((per_seq - 1) * ROW_TILE)
        assert rows == (p_tiles + 1) * ROW_TILE and x_rows.shape[0] <= ROW_TILE and per_seq >= 2
        assert 0 < x_last <= ROW_TILE - N_META and x_last % 8 == 0 and ((per_seq - 1) * ROW_TILE) % x_last == 0

        def seq_tile(i):
            t = jnp.minimum(i, p_tiles - 1)
            return t // per_seq, t % per_seq

        cur = lambda i: (seq_tile(i)[0], jnp.minimum(seq_tile(i)[1], per_seq - 2), 0)
        prev = lambda i: (seq_tile(i)[0], jnp.maximum(seq_tile(i)[1] - 1, 0), 0)
        last = lambda i: (seq_tile(i)[0], (per_seq - 1) * ROW_TILE // x_last, 0)
        data = [x_prompt, x_prompt, x_prompt, x_rows, meta]
        data_specs = [pl.BlockSpec((None, ROW_TILE, D_MODEL), cur), pl.BlockSpec((None, ROW_TILE, D_MODEL), prev),
                      pl.BlockSpec((None, x_last, D_MODEL), last), _whole_spec(x_rows), _whole_spec(meta)]

    def col_map(i):
        t = jnp.minimum(i, p_tiles - 1)
        return (t // per_seq, 0, t % per_seq)

    col = pl.BlockSpec((None, 256, ROW_TILE), col_map)
    heads = pl.BlockSpec((None, HEADS, ROW_TILE, 2 * DIFF_HD),
                         lambda i: (col_map(i)[0], 0, col_map(i)[2], 0))
    consts = [w['g1'], w['wg1'], w['wu1'], w['wd1'], w['gm'], w['wlin'], w['blin'], w['wlow'], w['blow'],
              w['aw'], w['ab'], w['wdf'], w['bdf'], w['qg'], w['kg']]
    widths = (D_MODEL, N_LIN, LANE, 256, 256, 256)
    return pl.pallas_call(
        functools.partial(_head_kernel, prompt_tiles=p_tiles, per_seq=per_seq, first_layer=h is None),
        grid=(rows // ROW_TILE,),
        in_specs=data_specs + [_layer_spec(c, layer, single_buffer=True) for c in consts],
        out_specs=[row(n) for n in widths] + [col, col, col, heads],
        out_shape=([jax.ShapeDtypeStruct((rows, n), F32) for n in widths]
                   + [jax.ShapeDtypeStruct((n_seq, 256, seq_rows), F32)] * 3
                   + [jax.ShapeDtypeStruct((n_seq, HEADS, seq_rows, 2 * DIFF_HD), F32)]),
        compiler_params=pltpu.CompilerParams(dimension_semantics=("arbitrary",),
                                             vmem_limit_bytes=VMEM_LIMIT),
        name="head",
    )(*data, *consts)


def _tail_kernel(h1_ref, brlp_ref, brls_ref, brdp_ref, brds_ref, gm_ref, wgt_ref, bgt_ref, wb_ref, wo_ref, g2_ref,
                 wg_ref, wu_ref, wd_ref, *outs, prompt_tiles, last_layer):
    h1 = h1_ref[...]
    x = _rms(h1, gm_ref[...]).astype(BF16)
    is_sample = pl.program_id(0) >= prompt_tiles
    merged = None
    for n in range(N_BRANCH):
        if n < 3:
            cols = slice(n * BRANCH_W, (n + 1) * BRANCH_W)
            br_p = brlp_ref[:, :, cols].reshape(ROW_TILE, BRANCH_W)
            br = jnp.where(is_sample, brls_ref[:, cols], br_p)
        else:
            br = jnp.where(is_sample, brds_ref[...], brdp_ref[...])
        gate = _dot(x, wgt_ref[:, n * D_MODEL:(n + 1) * D_MODEL]) + bgt_ref[:, n * D_MODEL:(n + 1) * D_MODEL]
        term = _dot(br.astype(BF16), wb_ref[n]) * _sigmoid(gate)
        merged = term if merged is None else merged + term
    h2 = h1 + _dot(merged.astype(BF16), wo_ref[...])
    h3 = h2 + 0.5 * _ffn(_rms(h2, g2_ref[...]).astype(BF16), wg_ref, wu_ref, wd_ref)
    if not last_layer:
        outs[0][...] = h3
        return

    y_ref, ys_ref, carry_s = outs
    cut = ROW_TILE - N_META

    @pl.when(pl.program_id(0) == 0)
    def _no_previous_tile():
        carry_s[...] = jnp.zeros(carry_s.shape, F32)

    y_ref[0:cut, :] = carry_s[...]
    y_ref[cut:ROW_TILE, :] = h3[0:N_META, :]
    carry_s[...] = h3[N_META:, :]

    @pl.when(is_sample)
    def _sample_rows():
        ys_ref[...] = h3[0:ys_ref.shape[0], :]


def _tail_call(h1, brl_p, brl_s, brd_p, brd_s, w, *, layer, final=None):
    rows = h1.shape[0]
    p_tiles = brd_p.shape[0] // ROW_TILE
    group, chunk = brl_p.shape[2], brl_p.shape[3]
    per_tile = ROW_TILE // chunk
    per_seq = brl_p.shape[1] // per_tile
    row = lambda n: pl.BlockSpec((ROW_TILE, n), lambda i: (i, 0))
    row_p = lambda n: pl.BlockSpec((ROW_TILE, n), lambda i: (jnp.minimum(i, p_tiles - 1), 0))
    row_s = lambda n: pl.BlockSpec((ROW_TILE, n), lambda i: (jnp.maximum(i - p_tiles, 0), 0))

    def mixer_map(i):
        t = jnp.minimum(i, p_tiles - 1)
        seq = t // per_seq
        return (seq // group, t % per_seq, seq % group, 0, 0)

    mixer = pl.BlockSpec((None, per_tile, None, chunk, 3 * BRANCH_W), mixer_map)
    consts = [w['gm'], w['wgt'], w['bgt'], w['wb'], w['wo'], w['g2'], w['wg2'], w['wu2'], w['wd2']]
    if final is None:
        out_specs = row(D_MODEL)
        out_shape = jax.ShapeDtypeStruct((rows, D_MODEL), F32)
        scratch = []
    else:
        n_seq, seq, rows_s = final
        assert rows == (p_tiles + 1) * ROW_TILE and rows_s <= ROW_TILE

        def prev_tile(i):
            t = jnp.minimum(jnp.maximum(i - 1, 0), p_tiles - 1)
            return (t // per_seq, t % per_seq, 0)

        out_specs = [pl.BlockSpec((None, ROW_TILE, D_MODEL), prev_tile),
                     pl.BlockSpec((rows_s, D_MODEL), lambda i: (0, 0))]
        out_shape = [jax.ShapeDtypeStruct((n_seq, seq, D_MODEL), F32),
                     jax.ShapeDtypeStruct((rows_s, D_MODEL), F32)]
        scratch = [pltpu.VMEM((ROW_TILE - N_META, D_MODEL), F32)]
    return pl.pallas_call(
        functools.partial(_tail_kernel, prompt_tiles=p_tiles, last_layer=final is not None),
        grid=(rows // ROW_TILE,),
        in_specs=[row(D_MODEL), mixer, row_s(3 * BRANCH_W), row_p(BRANCH_W), row_s(BRANCH_W)]
                 + [_layer_spec(c, layer, single_buffer=True) for c in consts],
        out_specs=out_specs,
        out_shape=out_shape,
        scratch_shapes=scratch,
        compiler_params=pltpu.CompilerParams(dimension_semantics=("arbitrary",),
                                             vmem_limit_bytes=VMEM_LIMIT),
        name="tail",
    )(h1, brl_p, brl_s, brd_p, brd_s, *consts)


def _mix_kernel(*refs, chunk, n_chunks, seq_len, group):
    G = group
    z_refs, gd_refs = refs[0:G], refs[G:2 * G]
    cos_ref, sin_ref = refs[2 * G:2 * G + 2]
    cpast_refs, sr0_refs, sg0_refs = (refs[2 + (2 + k) * G:2 + (3 + k) * G] for k in range(3))
    consts = refs[2 + 5 * G:8 + 5 * G]
    br_ref, sr_out, sg_out, cs_out, sr_s, sg_s, u_s, ush_s = refs[8 + 5 * G:]
    C = chunk
    c = pl.program_id(1)

    @pl.when(c == 0)
    def _init():
        sr_s[...] = jnp.zeros(sr_s.shape, F32)
        sg_s[...] = jnp.zeros(sg_s.shape, F32)
        for g in range(G):
            for h in range(HEADS):
                sr_s[g, h * RET_DK:(h + 1) * RET_DK, h * RET_DV:(h + 1) * RET_DV] = sr0_refs[g][0, h]
                sg_s[g, h * GLA_DK:(h + 1) * GLA_DK, h * GLA_DV:(h + 1) * GLA_DV] = sg0_refs[g][0, h]
            u_s[g, 0:HIST - (CONV_K - 1), :] = jnp.zeros((HIST - (CONV_K - 1), CONV_C), F32)
            u_s[g, HIST - (CONV_K - 1):HIST, :] = cpast_refs[g][0]

    @pl.when(c > 0)
    def _shift():
        for g in range(G):
            u_s[g, 0:HIST, :] = u_s[g, C:C + HIST, :]

    for g in range(G):
        _mix_member(z_refs[g], gd_refs[g], cos_ref, sin_ref, *consts, br_ref.at[g], sr_s.at[g], sg_s.at[g],
                    u_s.at[g], ush_s.at[g], c=c, chunk=chunk, n_chunks=n_chunks, seq_len=seq_len)

    @pl.when(c == n_chunks - 1)
    def _final():
        n_last = seq_len - (n_chunks - 1) * C
        for g in range(G):
            for h in range(HEADS):
                sr_out[g, h] = sr_s[g, h * RET_DK:(h + 1) * RET_DK, h * RET_DV:(h + 1) * RET_DV]
                sg_out[g, h] = sg_s[g, h * GLA_DK:(h + 1) * GLA_DK, h * GLA_DV:(h + 1) * GLA_DV]
            cs_out[g] = u_s[g, HIST + n_last - (CONV_K - 1):HIST + n_last, :]


def _mix_member(z_ref, gd_ref, cos_ref, sin_ref, cw_ref, cb_ref, lng_ref, lnb_ref, rgn_ref, ggn_ref,
                br_ref, sr_s, sg_s, u_s, ush_s, *, c, chunk, n_chunks, seq_len):
    C = chunk
    low = C >= 16
    cast = (lambda a: a.astype(BF16)) if low else (lambda a: a)

    t_col = lax.broadcasted_iota(jnp.int32, (C, 1), 0)
    s_row = lax.broadcasted_iota(jnp.int32, (1, C), 1)
    causal = t_col >= s_row
    tf = t_col.astype(F32)
    padded = n_chunks * C > seq_len
    if padded:
        valid = (c * C + t_col) < seq_len
        nvf = jnp.zeros((1, 1), F32) + jnp.minimum(seq_len - c * C, C).astype(F32)
        keep = lambda a: jnp.where(valid, a, 0.0)
    else:
        nvf = jnp.full((1, 1), float(C), F32)
        keep = lambda a: a

    u = z_ref[:, 0:256] * _sigmoid(z_ref[:, 256:512])
    u_s[HIST:HIST + C, :] = u
    span = C + HIST - 8
    for s in range(1, 8):
        ush_s[s - 1] = u_s[s:s + span, :]
    acc = jnp.zeros((C, CONV_C), F32)
    for j in range(CONV_K):
        lo = HIST - (CONV_K - 1) + j
        q, r = lo - lo % 8, lo % 8
        tap = u_s[q:q + C, :] if r == 0 else ush_s[r - 1, q:q + C, :]
        acc = acc + cw_ref[j:j + 1, :] * tap
    conv = acc + cb_ref[...]
    xc = conv - jnp.mean(conv, axis=-1, keepdims=True)
    ln = xc * lax.rsqrt(jnp.mean(xc * xc, axis=-1, keepdims=True) + EPS) * lng_ref[...] + lnb_ref[...]
    br_ref[:, 0:256] = _silu(ln)

    lane = lax.broadcasted_iota(jnp.int32, (1, 256), 1)
    first_half = (lane % RET_DK) < (RET_DK // 2)
    cos = cos_ref[...]
    sin = sin_ref[...]

    def rope(a):
        swapped = jnp.where(first_half, pltpu.roll(a, 256 - RET_DK // 2, 1), pltpu.roll(a, RET_DK // 2, 1))
        return a * cos + swapped * sin

    rq = rope(z_ref[:, 512:768])
    rk = keep(rope(z_ref[:, 768:1024]) * RET_DK ** -0.5)
    rv = keep(z_ref[:, 1024:1280])
    lane_head = lane // RET_DV
    row_head = lax.broadcasted_iota(jnp.int32, (HEADS * RET_DK, 1), 0) // RET_DK
    log_gamma = [math.log1p(-(2.0 ** (-5 - h))) for h in range(HEADS)]
    lg_lane = jnp.zeros((1, 256), F32)
    lg_row = jnp.zeros((HEADS * RET_DK, 1), F32)
    for h in range(HEADS):
        lg_lane = jnp.where(lane_head == h, log_gamma[h], lg_lane)
        lg_row = jnp.where(row_head == h, log_gamma[h], lg_row)
    avg = _group_avg(256, RET_DV, low)
    dts = (t_col - s_row).astype(F32)
    rk_c = cast(rk)
    o = _dot(cast(rq * jnp.exp((tf + 1.0) * lg_lane)), cast(sr_s[...]))
    for h in range(HEADS):
        own = lane_head == h
        decay = jnp.where(causal, jnp.exp(dts * log_gamma[h]), 0.0)
        a = _dot_nt(cast(jnp.where(own, rq, 0.0)), rk_c) * decay
        o = o + _dot(cast(a), cast(jnp.where(own, rv, 0.0)))
    update = _dot_tn(cast(rk * jnp.exp((nvf - 1.0 - tf) * lg_lane)), cast(rv))
    sr_s[...] = jnp.exp(nvf * lg_row) * sr_s[...] + jnp.where(row_head == lane_head, update, 0.0)
    oc = o - _group_mean(o, avg, low)
    y = oc * lax.rsqrt(_group_mean(oc * oc, avg, low) + EPS) * rgn_ref[...]
    br_ref[:, 256:512] = _silu(z_ref[:, 1280:1536]) * y

    gq = z_ref[:, 1536:1664] * GLA_DK ** -0.5
    gk = keep(z_ref[:, 1664:1792])
    gv = keep(z_ref[:, 1792:2048])
    g = keep(gd_ref[...])
    tri = jnp.where(causal, 1.0, 0.0)
    if low:
        g_hi, g_lo = _split_bf16(g)
        tri = tri.astype(BF16)
        bcum = _dot(tri, g_hi) + _dot(tri, g_lo)
    else:
        bcum = _dot(tri, g)
    mid = C // 2 - 1
    b_mid = bcum[mid:mid + 1, :]
    b_last = bcum[C - 1:C, :]
    q_intra = gq * jnp.exp(bcum - b_mid)
    k_intra = gk * jnp.exp(b_mid - bcum)
    q_inter = gq * jnp.exp(bcum)
    k_state = gk * jnp.exp(b_last - bcum)
    e_last = jnp.where(t_col == C - 1, jnp.exp(bcum), 0.0)
    ones = jnp.ones((C, HEADS * GLA_DV), BF16 if low else F32)
    if low:
        e_hi, e_lo = _split_bf16(e_last)
        e_rows = _dot_tn(e_hi, ones) + _dot_tn(e_lo, ones)
    else:
        e_rows = _dot_tn(e_last, ones)
    key_head = lax.broadcasted_iota(jnp.int32, (1, HEADS * GLA_DK), 1) // GLA_DK
    krow_head = lax.broadcasted_iota(jnp.int32, (HEADS * GLA_DK, 1), 0) // GLA_DK
    k_c = cast(k_intra)
    o = _dot(cast(q_inter), cast(sg_s[...]))
    for h in range(HEADS):
        a = jnp.where(causal, _dot_nt(cast(jnp.where(key_head == h, q_intra, 0.0)), k_c), 0.0)
        o = o + _dot(cast(a), cast(jnp.where(lane_head == h, gv, 0.0)))
    update = _dot_tn(cast(k_state), cast(gv))
    sg_s[...] = e_rows * sg_s[...] + jnp.where(krow_head == lane_head, update, 0.0)
    y = o * lax.rsqrt(_group_mean(o * o, avg, low) + EPS) * ggn_ref[...]
    br_ref[:, 512:768] = _silu(z_ref[:, 2048:2304]) * y

def _mix_call(zlin, gdec, cos, sin, conv_past, ret0, gla0, w, *, layer, state_layer, n_seq, chunk, n_chunks,
              seq_len, row_block0, group, name):
    G = group
    assert n_seq % G == 0
    blk = lambda n, g: pl.BlockSpec((chunk, n), lambda s, c: (row_block0 + (s * G + g) * n_chunks + c, 0))
    grouped = lambda shp: pl.BlockSpec((None, G) + shp, lambda s, c: (s, 0) + (0,) * len(shp))
    state = lambda shp, g: pl.BlockSpec((None, 1) + shp,
                                        lambda s, c: (state_layer, s * G + g) + (0,) * len(shp))
    consts = [w['cw'], w['cb'], w['lng'], w['lnb'], w['rgn'], w['ggn']]
    members = range(G)
    in_specs = ([blk(N_LIN, g) for g in members] + [blk(LANE, g) for g in members]
                + [pl.BlockSpec((chunk, 256), lambda s, c: (c, 0)), pl.BlockSpec((chunk, 256), lambda s, c: (c, 0))]
                + [state((CONV_K - 1, CONV_C), g) for g in members]
                + [state((HEADS, RET_DK, RET_DV), g) for g in members]
                + [state((HEADS, GLA_DK, GLA_DV), g) for g in members]
                + [_layer_spec(a, layer) for a in consts])
    args = [zlin] * G + [gdec] * G + [cos, sin] + [conv_past] * G + [ret0] * G + [gla0] * G + consts
    br, sr, sg, cs = pl.pallas_call(
        functools.partial(_mix_kernel, chunk=chunk, n_chunks=n_chunks, seq_len=seq_len, group=G),
        grid=(n_seq // G, n_chunks),
        in_specs=in_specs,
        out_specs=[pl.BlockSpec((None, None, G, chunk, 3 * BRANCH_W), lambda s, c: (s, c, 0, 0, 0)),
                   grouped((HEADS, RET_DK, RET_DV)), grouped((HEADS, GLA_DK, GLA_DV)),
                   grouped((CONV_K - 1, CONV_C))],
        out_shape=[jax.ShapeDtypeStruct((n_seq // G, n_chunks, G, chunk, 3 * BRANCH_W), F32),
                   jax.ShapeDtypeStruct((n_seq // G, G, HEADS, RET_DK, RET_DV), F32),
                   jax.ShapeDtypeStruct((n_seq // G, G, HEADS, GLA_DK, GLA_DV), F32),
                   jax.ShapeDtypeStruct((n_seq // G, G, CONV_K - 1, CONV_C), F32)],
        scratch_shapes=[pltpu.VMEM((G, HEADS * RET_DK, HEADS * RET_DV), F32),
                        pltpu.VMEM((G, HEADS * GLA_DK, HEADS * GLA_DV), F32),
                        pltpu.VMEM((G, HIST + chunk, CONV_C), F32),
                        pltpu.VMEM((G, 7, HIST + chunk - 8, CONV_C), F32)],
        compiler_params=pltpu.CompilerParams(dimension_semantics=("arbitrary", "arbitrary")),
        name=name,
    )(*args)
    flat = lambda a: a.reshape((n_seq,) + a.shape[2:])
    return br, flat(sr), flat(sg), flat(cs)


def _t5_bucket_np(rel):
    rel = np.asarray(rel)
    n = np.maximum(rel, 0)
    max_exact = REL_BUCKETS // 2
    nf = np.maximum(n, 1).astype(np.float64)
    large = max_exact + (np.log(nf / max_exact) / math.log(REL_MAX_DIST / max_exact)
                         * (REL_BUCKETS - max_exact)).astype(np.int64)
    large = np.minimum(large, REL_BUCKETS - 1)
    return np.where(n < max_exact, n, large).astype(np.int32)


def _lambda(lamv_ref, lam_init):
    a = jnp.sum(lamv_ref[0:1, :] * lamv_ref[1:2, :], axis=-1, keepdims=True)
    b = jnp.sum(lamv_ref[2:3, :] * lamv_ref[3:4, :], axis=-1, keepdims=True)
    return jnp.exp(a) - jnp.exp(b) + lam_init


def _flash_kernel(bt_ref, rb_ref, top_ref, lamv_ref, sub_ref, qt_ref, k_ref, vt_ref, o_ref,
                  bias_s, qbd_s, m_s, l_s, acc_s, sta_s, stb_s, lf_s, accf_s, *, layer, lam_init, n_blocks):
    QB = SEQ_BLOCK
    SUB = FAR_BLOCK // QB
    b = pl.program_id(0)
    i = pl.program_id(1)

    @pl.when((b == 0) & (i == 0))
    def _tables():
        for kind, src in ((1, 1), (2, 0)):
            bt = bt_ref[src]
            tiles = [jnp.zeros((QB, QB), F32) for _ in range(HEADS)]
            for bk in range(REL_BUCKETS):
                hit = bt == bk
                for h in range(HEADS):
                    tiles[h] = jnp.where(hit, rb_ref[bk, h] * LOG2E, tiles[h])
            for h in range(HEADS):
                bias_s[kind, h] = jnp.where(bt < 0, NEG, tiles[h])
        for h in range(HEADS):
            bias_s[0, h] = jnp.zeros((QB, QB), F32) + rb_ref[REL_BUCKETS - 1, h] * LOG2E
            bias_s[3, h] = jnp.full((QB, QB), NEG, F32)
            for kind in range(4):
                bias_s[4 + kind, h] = bias_s[kind, h] - top_ref[layer, h]

    qt = qt_ref[...] * (DIFF_HD ** -0.5 * LOG2E)
    first_branch = lax.broadcasted_iota(jnp.int32, (2 * DIFF_HD, 1), 0) < DIFF_HD
    for h in range(HEADS):
        qh = qt[h * 2 * DIFF_HD:(h + 1) * 2 * DIFF_HD, :]
        qbd_s[h, :, 0:QB] = jnp.where(first_branch, qh, 0.0).astype(BF16)
        qbd_s[h, :, QB:2 * QB] = jnp.where(first_branch, 0.0, qh).astype(BF16)
    m_s[...] = jnp.full(m_s.shape, NEG, F32)
    l_s[...] = jnp.zeros(l_s.shape, F32)
    acc_s[...] = jnp.zeros(acc_s.shape, F32)

    def online(h, st, shift, vt):
        m_old = m_s[h]
        m_new = jnp.maximum(m_old, jnp.max(st, axis=0, keepdims=True) + shift)
        alpha = jnp.exp2(m_old - m_new)
        pt = jnp.exp2(st - (m_new - shift))
        l_s[h] = alpha * l_s[h] + jnp.sum(pt, axis=0, keepdims=True)
        m_s[h] = m_new
        acc_s[h] = alpha * acc_s[h] + _dot(vt[h * DIFF_VD:(h + 1) * DIFF_VD, :], pt.astype(BF16))

    def fixed(h, st, vt):
        pt = jnp.exp2(st)
        l_s[h] = l_s[h] + jnp.sum(pt, axis=0, keepdims=True)
        acc_s[h] = acc_s[h] + _dot(vt[h * DIFF_VD:(h + 1) * DIFF_VD, :], pt.astype(BF16))

    n_far = jnp.maximum(i - 1, 0) // SUB

    def sweep(accumulate):
        def far_body(kb, carry):
            off = pl.multiple_of(kb * FAR_BLOCK, FAR_BLOCK)
            vt = vt_ref[:, pl.ds(off, FAR_BLOCK)].astype(BF16)
            for h in range(HEADS):
                st = _dot(k_ref[h, pl.ds(off, FAR_BLOCK), :].astype(BF16), qbd_s[h])
                accumulate(h, st, rb_ref[REL_BUCKETS - 1, h] * LOG2E, vt)
            return carry

        lax.fori_loop(0, n_far, far_body, 0)

        def near_body(kb, carry):
            first = kb * SUB
            blk0 = jnp.minimum(first, n_blocks - SUB)
            off = pl.multiple_of(blk0 * QB, QB)
            vt = vt_ref[:, pl.ds(off, FAR_BLOCK)].astype(BF16)
            kinds = []
            for c in range(SUB):
                blk = blk0 + c
                kind = jnp.where(blk == i, 2, jnp.where(blk == i - 1, 1, 0))
                kinds.append(jnp.where((blk > i) | (blk < first), 3, kind))
            for h in range(HEADS):
                st = _dot(k_ref[h, pl.ds(off, FAR_BLOCK), :].astype(BF16), qbd_s[h])
                parts = []
                for c in range(SUB):
                    tile = bias_s[kinds[c], h]
                    parts.append(st[c * QB:(c + 1) * QB, :] + jnp.concatenate([tile, tile], axis=1))
                accumulate(h, jnp.concatenate(parts, axis=0), 0.0, vt)
            return carry

        lax.fori_loop(n_far, i // SUB + 1, near_body, 0)

    bounded = top_ref[layer, HEADS] > 0.5

    def window(j):
        first = j * SUB
        return first, jnp.minimum(first, n_blocks - SUB)

    def score(j, buf):
        _, blk0 = window(j)
        off = pl.multiple_of(blk0 * QB, QB)
        for h in range(HEADS):
            buf[h] = _dot(k_ref[h, pl.ds(off, FAR_BLOCK), :].astype(BF16), qbd_s[h])

    def consume(j, buf):
        first, blk0 = window(j)
        vt = vt_ref[:, pl.ds(pl.multiple_of(blk0 * QB, QB), FAR_BLOCK)].astype(BF16)
        kinds = []
        for c in range(SUB):
            blk = blk0 + c
            kind = jnp.where(blk == i, 2, jnp.where(blk == i - 1, 1, 0))
            kinds.append(jnp.where((blk > i) | (blk < first), 3, kind))
        for h in range(HEADS):
            parts = []
            for c in range(SUB):
                tile = bias_s[kinds[c] + 4, h]
                parts.append(buf[h, c * QB:(c + 1) * QB, :] + jnp.concatenate([tile, tile], axis=1))
            fixed(h, jnp.concatenate(parts, axis=0), vt)

    def consume_far(j, buf):
        vt = vt_ref[:, pl.ds(pl.multiple_of(j * FAR_BLOCK, FAR_BLOCK), FAR_BLOCK)].astype(BF16)
        for h in range(HEADS):
            pt = jnp.exp2(buf[h])
            lf_s[h] = lf_s[h] + jnp.sum(pt, axis=0, keepdims=True)
            accf_s[h] = accf_s[h] + _dot(vt[h * DIFF_VD:(h + 1) * DIFF_VD, :], pt.astype(BF16))

    @pl.when(bounded)
    def _fixed():
        n_win = i // SUB + 1
        far_pairs = n_far // 2
        lf_s[...] = jnp.zeros(lf_s.shape, F32)
        accf_s[...] = jnp.zeros(accf_s.shape, F32)
        score(0, sta_s)

        def far_pair(m, carry):
            score(2 * m + 1, stb_s)
            consume_far(2 * m, sta_s)
            score(2 * m + 2, sta_s)
            consume_far(2 * m + 1, stb_s)
            return carry

        lax.fori_loop(0, far_pairs, far_pair, 0)
        w0 = 2 * far_pairs
        rest = n_win - w0

        def pair(m, carry):
            score(w0 + 2 * m + 1, stb_s)
            consume(w0 + 2 * m, sta_s)
            score(w0 + 2 * m + 2, sta_s)
            consume(w0 + 2 * m + 1, stb_s)
            return carry

        lax.fori_loop(0, rest // 2, pair, 0)

        @pl.when(rest % 2 == 1)
        def _tail():
            consume(n_win - 1, sta_s)

        for h in range(HEADS):
            far_scale = jnp.exp2(jnp.zeros((1, 1), F32) + (rb_ref[REL_BUCKETS - 1, h] * LOG2E - top_ref[layer, h]))
            l_s[h] = l_s[h] + far_scale * lf_s[h]
            acc_s[h] = acc_s[h] + far_scale * accf_s[h]

    @pl.when(jnp.logical_not(bounded))
    def _online():
        sweep(online)

    lam = _lambda(lamv_ref, lam_init)
    outs = []
    for h in range(HEADS):
        ot = acc_s[h] / l_s[h]
        d = ot[:, 0:QB] - lam * ot[:, QB:2 * QB]
        y = d * lax.rsqrt(jnp.mean(d * d, axis=0, keepdims=True) + EPS) * sub_ref[...] * (1.0 - lam_init)
        outs.append(y)
    o_ref[...] = jnp.concatenate(outs, axis=0).T


def _score_top(rel_bias, q_gain, k_gain):
    reach = DIFF_HD ** 0.5 * jnp.max(jnp.abs(q_gain * k_gain), axis=-1, keepdims=True)
    top = (reach + jnp.max(rel_bias, axis=0)[None, :]) * LOG2E
    low = (-reach + jnp.min(rel_bias, axis=0)[None, :]) * LOG2E
    ok = jnp.all(top - low < MAX_EXP2_SPAN, axis=-1, keepdims=True).astype(F32)
    return jnp.concatenate([top, ok], axis=-1).astype(F32)


def _flash_call(bt, rel_bias, top, lamv, sub_col, dqt, dk, dvt, *, layer, n_seq, seq_rows, lam_init):
    rows = n_seq * seq_rows
    nq = seq_rows // SEQ_BLOCK
    assert seq_rows >= FAR_BLOCK
    smem = pl.BlockSpec(memory_space=pltpu.SMEM)
    return pl.pallas_call(
        functools.partial(_flash_kernel, layer=layer, lam_init=lam_init, n_blocks=nq),
        grid=(n_seq, nq),
        in_specs=[_whole_spec(bt), smem, smem, _layer_spec(lamv, layer), _layer_spec(sub_col, layer),
                  pl.BlockSpec((None, 256, SEQ_BLOCK), lambda b, i: (b, 0, i)),
                  pl.BlockSpec((None, HEADS, seq_rows, 2 * DIFF_HD), lambda b, i: (b, 0, 0, 0)),
                  pl.BlockSpec((None, 256, seq_rows), lambda b, i: (b, 0, 0))],
        out_specs=pl.BlockSpec((SEQ_BLOCK, 256), lambda b, i: (b * nq + i, 0)),
        out_shape=jax.ShapeDtypeStruct((rows, 256), F32),
        scratch_shapes=[pltpu.VMEM((8, HEADS, SEQ_BLOCK, SEQ_BLOCK), F32),
                        pltpu.VMEM((HEADS, 2 * DIFF_HD, 2 * SEQ_BLOCK), BF16),
                        pltpu.VMEM((HEADS, 1, 2 * SEQ_BLOCK), F32), pltpu.VMEM((HEADS, 1, 2 * SEQ_BLOCK), F32),
                        pltpu.VMEM((HEADS, DIFF_VD, 2 * SEQ_BLOCK), F32),
                        pltpu.VMEM((HEADS, FAR_BLOCK, 2 * SEQ_BLOCK), F32),
                        pltpu.VMEM((HEADS, FAR_BLOCK, 2 * SEQ_BLOCK), F32),
                        pltpu.VMEM((HEADS, 1, 2 * SEQ_BLOCK), F32),
                        pltpu.VMEM((HEADS, DIFF_VD, 2 * SEQ_BLOCK), F32)],
        compiler_params=pltpu.CompilerParams(dimension_semantics=("arbitrary", "arbitrary"),
                                             vmem_limit_bytes=VMEM_LIMIT),
        name="attn_prompt",
    )(bt, rel_bias, top, lamv, sub_col, dqt, dk, dvt)


def _decode_kernel(pt_ref, bt_ref, rb_ref, lamv_ref, sub_ref, q_ref, kn_ref, vn_ref, *refs,
                   pages_per_step, n_steps, dec_seq, lam_init):
    G = pages_per_step
    kt_refs = refs[:G]
    vt_refs = refs[G:2 * G]
    o_ref, bias_pg_s, bias_new_s, far_s, qs_s, m_s, l_s, acc_s = refs[2 * G:]
    del pt_ref
    b = pl.program_id(0)
    g = pl.program_id(1)
    last = g == n_steps - 1
    NQ = 2 * HEADS * dec_seq
    seqs_per_block = kn_ref.shape[0] // dec_seq

    row_head = lax.broadcasted_iota(jnp.int32, (NQ, 1), 0) // (2 * dec_seq)

    def bias_rows(bk):
        out = jnp.zeros((NQ, 1), F32)
        for h in range(HEADS):
            out = jnp.where(row_head == h, rb_ref[bk, h], out)
        return out

    @pl.when((b == 0) & (g == 0))
    def _tables():
        for src, dst in ((0, bias_pg_s), (1, bias_new_s)):
            bt = bt_ref[src]
            tile = jnp.zeros(bt.shape, F32)
            for bk in range(REL_BUCKETS):
                tile = jnp.where(bt == bk, bias_rows(bk), tile)
            dst[...] = jnp.where(bt < 0, NEG, tile)
        far_s[...] = jnp.zeros(far_s.shape, F32) + bias_rows(REL_BUCKETS - 1)

    @pl.when(g == 0)
    def _init():
        q = q_ref[...] * DIFF_HD ** -0.5
        lane = lax.broadcasted_iota(jnp.int32, (1, 256), 1)
        for hj in range(2 * HEADS):
            own = (lane >= hj * DIFF_HD) & (lane < (hj + 1) * DIFF_HD)
            qs_s[hj * dec_seq:(hj + 1) * dec_seq, :] = jnp.where(own, q, 0.0)
        m_s[...] = jnp.full(m_s.shape, NEG, F32)
        l_s[...] = jnp.zeros(l_s.shape, F32)
        acc_s[...] = jnp.zeros(acc_s.shape, F32)

    def update(scores, pv_fn):
        m_old = m_s[...]
        s_max = scores[0]
        for s in scores[1:]:
            s_max = jnp.maximum(s_max, s)
        m_new = jnp.maximum(m_old, jnp.max(s_max, axis=1, keepdims=True))
        alpha = jnp.exp(m_old - m_new)
        probs = [jnp.exp(s - m_new) for s in scores]
        p_sum = probs[0]
        for p in probs[1:]:
            p_sum = p_sum + p
        l_s[...] = alpha * l_s[...] + jnp.sum(p_sum, axis=1, keepdims=True)
        m_s[...] = m_new
        acc_s[...] = jnp.concatenate([alpha, alpha], axis=1) * acc_s[...] + pv_fn(probs)

    qs = qs_s[...].astype(BF16)
    scores = []
    for p in range(G):
        s = _dot(qs, kt_refs[p][...].astype(BF16))
        if p == G - 1:
            s = s + jnp.where(last, bias_pg_s[...], far_s[...])
        else:
            s = s + far_s[...]
        scores.append(s)

    def pv_pages(probs):
        out = None
        for p in range(G):
            t = _dot_nt(probs[p].astype(BF16), vt_refs[p][...].astype(BF16))
            out = t if out is None else out + t
        return out

    update(scores, pv_pages)

    @pl.when(last)
    def _finish():
        key_seq = lax.broadcasted_iota(jnp.int32, (1, kn_ref.shape[0]), 1) // dec_seq
        s_new = _dot_nt(qs, kn_ref[...].astype(BF16))
        s_new = jnp.where(key_seq == b % seqs_per_block, s_new + bias_new_s[...], NEG)
        update([s_new], lambda probs: _dot(probs[0].astype(BF16), vn_ref[...].astype(BF16)))
        l_all = l_s[...]
        o_all = acc_s[...] / jnp.concatenate([l_all, l_all], axis=1)
        lam = _lambda(lamv_ref, lam_init)
        for h in range(HEADS):
            r1 = (2 * h) * dec_seq
            r2 = (2 * h + 1) * dec_seq
            cols = slice(h * DIFF_VD, (h + 1) * DIFF_VD)
            d = o_all[r1:r1 + dec_seq, cols] - lam * o_all[r2:r2 + dec_seq, cols]
            y = d * lax.rsqrt(jnp.mean(d * d, axis=-1, keepdims=True) + EPS) * sub_ref[...] * (1.0 - lam_init)
            o_ref[:, cols] = y


def _decode_call(page_table, bt, rel_bias, lamv, sub, dq, dk, dv, cache_kt, cache_vt, *,
                 layer, n_seq, dec_seq, row0, lam_init):
    n_pages = page_table.shape[1]
    page = cache_kt.shape[3]
    G = min(PAGES_PER_STEP, n_pages)
    n_steps = n_pages // G
    nq = 2 * HEADS * dec_seq
    new_rows = LANE
    per_blk = new_rows // dec_seq
    own = pl.BlockSpec((dec_seq, 256), lambda b, g, pt: (row0 // dec_seq + b, 0))
    new = pl.BlockSpec((new_rows, 256), lambda b, g, pt: (row0 // new_rows + b // per_blk, 0))

    def page_spec(p):
        return pl.BlockSpec((None, None, 256, page),
                            lambda b, g, pt: (layer, pt[b * n_pages + g * G + p], 0, 0))

    in_specs = ([_whole_spec(bt), pl.BlockSpec(memory_space=pltpu.SMEM), _layer_spec(lamv, layer),
                 _layer_spec(sub, layer), own, new, new]
                + [page_spec(p) for p in range(G)] + [page_spec(p) for p in range(G)])
    args = [bt, rel_bias, lamv, sub, dq, dk, dv] + [cache_kt] * G + [cache_vt] * G
    return pl.pallas_call(
        functools.partial(_decode_kernel, pages_per_step=G, n_steps=n_steps, dec_seq=dec_seq, lam_init=lam_init),
        grid_spec=pltpu.PrefetchScalarGridSpec(
            num_scalar_prefetch=1,
            grid=(n_seq, n_steps),
            in_specs=in_specs,
            out_specs=pl.BlockSpec((dec_seq, 256), lambda b, g, pt: (b, 0)),
            scratch_shapes=[pltpu.VMEM((nq, page), F32), pltpu.VMEM((nq, new_rows), F32),
                            pltpu.VMEM((nq, page), F32), pltpu.VMEM((nq, 256), F32),
                            pltpu.VMEM((nq, LANE), F32), pltpu.VMEM((nq, LANE), F32), pltpu.VMEM((nq, 256), F32)]),
        out_shape=jax.ShapeDtypeStruct((n_seq * dec_seq, 256), F32),
        compiler_params=pltpu.CompilerParams(dimension_semantics=("arbitrary", "arbitrary"),
                                             vmem_limit_bytes=VMEM_LIMIT),
        name="attn_sample",
    )(page_table.reshape(-1), *args)


def _rope_tables(pos):
    half = RET_DK // 2
    inv = ROPE_BASE ** (-np.arange(half, dtype=np.float64) / half)
    ang = np.asarray(pos, np.float64)[:, None] * inv[None, :]
    cos = np.concatenate([np.cos(ang), np.cos(ang)], axis=1)
    sin = np.concatenate([-np.sin(ang), np.sin(ang)], axis=1)
    return (jnp.asarray(np.tile(cos, (1, HEADS)), F32), jnp.asarray(np.tile(sin, (1, HEADS)), F32))


def _prompt_bucket_tiles():
    t = np.arange(SEQ_BLOCK)
    rel = t[None, :] - t[:, None]
    diag = np.where(rel >= 0, _t5_bucket_np(rel), -1)
    sub = _t5_bucket_np(rel + SEQ_BLOCK)
    return jnp.asarray(np.stack([diag, sub]).astype(np.int32))


def _sample_bucket_tiles(page, dec_seq):
    assert page == LANE
    iq = (np.arange(2 * HEADS * dec_seq) % dec_seq)[:, None]
    past = _t5_bucket_np(page + iq - np.arange(page)[None, :])
    rel_new = iq - (np.arange(LANE) % dec_seq)[None, :]
    new = np.where(rel_new >= 0, _t5_bucket_np(rel_new), -1)
    return jnp.asarray(np.stack([past, new]).astype(np.int32))


def kernel(x_prompt, x_sample, cache_k, cache_v, page_table, state_ret, state_gla, state_conv, meta_tokens,
           rel_bias, norm_ffn1, ffn1_gate, ffn1_up, ffn1_down, norm_mix, w_in, b_in, conv_w, conv_b, conv_ln_g,
           conv_ln_b, ret_gn, gla_alpha_w, gla_alpha_b, gla_gn, q_norm, k_norm, lam_q1, lam_k1, lam_q2, lam_k2,
           diff_subln, w_branch, w_out, norm_ffn2, ffn2_gate, ffn2_up, ffn2_down):
    B, S, D = x_prompt.shape
    DB, DS, _ = x_sample.shape
    depth = w_in.shape[0]
    L = S + N_META
    Lp = -(-L // SEQ_BLOCK) * SEQ_BLOCK
    n_chunks = Lp // SEQ_BLOCK
    n_pool, page = cache_k.shape[1], cache_k.shape[2]
    past_len = page_table.shape[1] * page
    rows_p = B * Lp
    rows_s = DB * DS
    rows_sp = -(-rows_s // ROW_TILE) * ROW_TILE
    assert Lp % ROW_TILE == 0 and rows_s % LANE == 0 and DS % 8 == 0

    rows = rows_p + rows_sp
    h = None
    pad_s = lambda a: jnp.pad(a, ((0, rows_sp - rows_s), (0, 0)))

    cos_p, sin_p = _rope_tables(np.arange(Lp))
    cos_s, sin_s = _rope_tables(past_len + np.arange(DS))
    bt_prompt = _prompt_bucket_tiles()
    bt_sample = _sample_bucket_tiles(page, DS)
    ckt = jnp.transpose(cache_k, (0, 1, 3, 4, 2)).reshape(depth, n_pool, HEADS * 2 * DIFF_HD, page)
    cvt = jnp.transpose(cache_v, (0, 1, 3, 4, 2)).reshape(depth, n_pool, HEADS * DIFF_VD, page)
    zeros_conv = jnp.zeros((1, B, CONV_K - 1, CONV_C), F32)
    zeros_ret = jnp.zeros((1, B, HEADS, RET_DK, RET_DV), F32)
    zeros_gla = jnp.zeros((1, B, HEADS, GLA_DK, GLA_DV), F32)

    row3 = lambda a: a.reshape(depth, 1, -1).astype(F32)
    w = dict(
        g1=row3(norm_ffn1), wg1=ffn1_gate.astype(BF16), wu1=ffn1_up.astype(BF16), wd1=ffn1_down.astype(BF16),
        gm=row3(norm_mix),
        wlin=w_in[:, :, :N_LIN].astype(BF16), blin=row3(b_in[:, :N_LIN]),
        wlow=w_in[:, :, OFF_GLOW:OFF_DIFF].astype(BF16), blow=row3(b_in[:, OFF_GLOW:OFF_DIFF]),
        aw=gla_alpha_w.astype(BF16), ab=row3(gla_alpha_b),
        wdf=w_in[:, :, OFF_DIFF:OFF_GATES].astype(BF16), bdf=row3(b_in[:, OFF_DIFF:OFF_GATES]),
        qg=row3(jnp.tile(q_norm, (1, 2 * HEADS))), kg=row3(jnp.tile(k_norm, (1, 2 * HEADS))),
        wgt=w_in[:, :, OFF_GATES:].astype(BF16), bgt=row3(b_in[:, OFF_GATES:]),
        wb=w_branch.astype(BF16), wo=w_out.astype(BF16), g2=row3(norm_ffn2),
        wg2=ffn2_gate.astype(BF16), wu2=ffn2_up.astype(BF16), wd2=ffn2_down.astype(BF16),
        cw=conv_w, cb=row3(conv_b), lng=row3(conv_ln_g), lnb=row3(conv_ln_b),
        rgn=row3(ret_gn), ggn=row3(gla_gn),
    )
    lamv = jnp.stack([lam_q1, lam_k1, lam_q2, lam_k2], axis=1).astype(F32)
    sub = row3(diff_subln)
    sub_col = diff_subln.reshape(depth, -1, 1).astype(F32)
    top = _score_top(rel_bias, q_norm, k_norm)

    per_layer = []
    for l in range(depth):
        lam_init = 0.8 - 0.6 * math.exp(-0.3 * l)
        h1, zlin, gdec, dq, dk, dv, qt, kt, vt, kh = _head_call(
            h, w, layer=l, n_seq=B, seq_rows=Lp, rows=rows,
            model_inputs=(x_prompt, x_sample.reshape(rows_s, D), meta_tokens.astype(F32)))

        brl_p, rp, gp, cp = _mix_call(zlin, gdec, cos_p, sin_p, zeros_conv, zeros_ret, zeros_gla, w,
                                      layer=l, state_layer=0, n_seq=B, chunk=SEQ_BLOCK, n_chunks=n_chunks,
                                      seq_len=L, row_block0=0, group=MIX_GROUP, name="mix_prompt")
        brl_s, rs, gs, cs = _mix_call(zlin, gdec, cos_s, sin_s, state_conv, state_ret, state_gla, w,
                                      layer=l, state_layer=l, n_seq=DB, chunk=DS, n_chunks=1, seq_len=DS,
                                      row_block0=rows_p // DS, group=MIX_GROUP_SAMPLE, name="mix_sample")
        brd_p = _flash_call(bt_prompt, rel_bias, top, lamv, sub_col, qt, kh, vt,
                            layer=l, n_seq=B, seq_rows=Lp, lam_init=lam_init)
        brd_s = _decode_call(page_table, bt_sample, rel_bias, lamv, sub, dq, dk, dv, ckt, cvt,
                             layer=l, n_seq=DB, dec_seq=DS, row0=rows_p, lam_init=lam_init)

        h = _tail_call(h1, brl_p, pad_s(brl_s.reshape(rows_s, -1)), brd_p, pad_s(brd_s), w, layer=l,
                       final=(B, S, rows_s) if l == depth - 1 else None)
        per_layer.append(dict(kt=kt, vt=vt, ks=dk[rows_p:rows_p + rows_s], vs=dv[rows_p:rows_p + rows_s],
                              rp=rp, rs=rs, gp=gp, gs=gs, cp=cp, cs=cs))

    st = lambda k: jnp.stack([p[k] for p in per_layer], axis=0)
    seq_major = lambda t: jnp.transpose(t.reshape(depth, B, HEADS, -1, Lp)[..., :L], (0, 1, 4, 2, 3))
    y_prompt, y_rows = h
    y_sample = y_rows.reshape(DB, DS, D)
    return (y_prompt, y_sample, seq_major(st('kt')), seq_major(st('vt')),
            st('ks').reshape(depth, DB, DS, HEADS, 2 * DIFF_HD), st('vs').reshape(depth, DB, DS, HEADS, DIFF_VD),
            st('rp'), st('rs'), st('gp'), st('gs'), st('cp'), st('cs'))
```

```python
import functools
import math

import numpy as np
import jax
import jax.numpy as jnp
from jax import lax
from jax.experimental import pallas as pl
from jax.experimental.pallas import tpu as pltpu

F32 = jnp.float32
BF16 = jnp.bfloat16

D_MODEL = 1024
N_META = 16
N_BRANCH = 4
BRANCH_W = 256
D_FF = 2816
CONV_C = 256
CONV_K = 31
HEADS = 4
RET_DK = 64
RET_DV = 64
GLA_DK = 32
GLA_DV = 64
GLA_RANK = 16
GLA_TAU = 16.0
DIFF_HD = 32
DIFF_VD = 64
REL_BUCKETS = 32
REL_MAX_DIST = 128
ROPE_BASE = 10000.0
EPS = 1e-6
NEG = -1e30
LOG2E = math.log2(math.e)
MAX_EXP2_SPAN = 100.0

N_LIN = 2304
OFF_GLOW = N_LIN
OFF_DIFF = OFF_GLOW + GLA_RANK
OFF_GATES = OFF_DIFF + 3 * 256

LANE = 128
ROW_TILE = 384
FFN_CHUNKS = 1
MXU_N = 256
SEQ_BLOCK = 128
FAR_BLOCK = 512
HIST = 32
VMEM_LIMIT = 60 * 1024 * 1024
MIX_GROUP = 4
MIX_GROUP_SAMPLE = 4
PAGES_PER_STEP = 64


def _sigmoid(x):
    return 1.0 / (1.0 + jnp.exp(-x))


def _silu(x):
    return x * _sigmoid(x)


def _log_sigmoid(x):
    return jnp.minimum(x, 0.0) - jnp.log1p(jnp.exp(-jnp.abs(x)))


def _rms(x, g):
    return x * lax.rsqrt(jnp.mean(x * x, axis=-1, keepdims=True) + EPS) * g


def _dot(a, b):
    return jnp.dot(a, b, preferred_element_type=F32)


def _dot_nt(a, b):
    return lax.dot_general(a, b, (((1,), (1,)), ((), ())), preferred_element_type=F32)


def _dot_tn(a, b):
    return lax.dot_general(a, b, (((0,), (0,)), ((), ())), preferred_element_type=F32)


def _split_bf16(x):
    hi = x.astype(BF16)
    lo = (x - hi.astype(F32)).astype(BF16)
    return hi, lo


def _group_avg(n, group, low=True):
    r = lax.broadcasted_iota(jnp.int32, (n, n), 0) // group
    c = lax.broadcasted_iota(jnp.int32, (n, n), 1) // group
    return jnp.where(r == c, 1.0 / group, 0.0).astype(BF16 if low else F32)


def _group_mean(x, avg, low=True):
    if not low:
        return _dot(x, avg)
    hi, lo = _split_bf16(x)
    return _dot(hi, avg) + _dot(lo, avg)


def _layer_spec(a, layer, single_buffer=False):
    nd = a.ndim - 1
    kw = dict(pipeline_mode=pl.Buffered(1)) if single_buffer else {}
    return pl.BlockSpec((None,) + a.shape[1:], lambda *_: (layer,) + (0,) * nd, **kw)


def _whole_spec(a):
    return pl.BlockSpec(a.shape, lambda *_: (0,) * a.ndim)


def _ffn(x, wg_ref, wu_ref, wd_ref):
    tiles = D_FF // MXU_N
    edges = [(tiles * c // FFN_CHUNKS) * MXU_N for c in range(FFN_CHUNKS + 1)]
    out = None
    for c in range(FFN_CHUNKS):
        cols = slice(edges[c], edges[c + 1])
        act = (_silu(_dot(x, wg_ref[:, cols])) * _dot(x, wu_ref[:, cols])).astype(BF16)
        part = _dot(act, wd_ref[cols, :])
        out = part if out is None else out + part
    return out


def _assemble_rows(xcur_ref, xprev_ref, xlast_ref, xs_ref, meta_ref, *, prompt_tiles, per_seq):
    i = pl.program_id(0)
    j = i % per_seq
    cut = ROW_TILE - N_META
    prev_tail = xprev_ref[cut:, :]
    cur_head = xcur_ref[0:cut, :]
    x_last = xlast_ref[...]
    pad = jnp.zeros((cut - x_last.shape[0], D_MODEL), F32)
    body = jnp.concatenate([prev_tail, cur_head], axis=0)
    first = jnp.concatenate([meta_ref[...], cur_head], axis=0)
    last = jnp.concatenate([prev_tail, x_last, pad], axis=0)
    xs = xs_ref[...]
    sample = jnp.concatenate([xs, jnp.zeros((ROW_TILE - xs.shape[0], D_MODEL), F32)], axis=0)
    rows = jnp.where(j == 0, first, jnp.where(j == per_seq - 1, last, body))
    return jnp.where(i >= prompt_tiles, sample, rows)


def _head_kernel(*refs, prompt_tiles, per_seq, first_layer):
    n_in = 5 if first_layer else 1
    (g1_ref, wg_ref, wu_ref, wd_ref, gm_ref, wlin_ref, blin_ref, wlow_ref, blow_ref,
     aw_ref, ab_ref, wdf_ref, bdf_ref, qg_ref, kg_ref,
     h1_ref, zlin_ref, gdec_ref, dq_ref, dk_ref, dv_ref, qt_ref, kt_ref, vt_ref, kh_ref) = refs[n_in:]
    if first_layer:
        h = _assemble_rows(*refs[:n_in], prompt_tiles=prompt_tiles, per_seq=per_seq)
    else:
        h = refs[0][...]
    h1 = h + 0.5 * _ffn(_rms(h, g1_ref[...]).astype(BF16), wg_ref, wu_ref, wd_ref)
    h1_ref[...] = h1
    x = _rms(h1, gm_ref[...]).astype(BF16)
    zlin_ref[...] = _dot(x, wlin_ref[...]) + blin_ref[...]
    g_low = _dot(x, wlow_ref[...]) + blow_ref[...]
    g_pre = _dot(g_low.astype(BF16), aw_ref[...]) + ab_ref[...]
    gdec_ref[...] = _log_sigmoid(g_pre) * (1.0 / GLA_TAU)
    zd = _dot(x, wdf_ref[...]) + bdf_ref[...]
    d_q = zd[:, 0:256]
    d_k = zd[:, 256:512]
    d_v = zd[:, 512:768]
    avg = _group_avg(256, DIFF_HD)
    q_n = d_q * lax.rsqrt(_group_mean(d_q * d_q, avg) + EPS) * qg_ref[...]
    dq_ref[...] = q_n
    k_n = d_k * lax.rsqrt(_group_mean(d_k * d_k, avg) + EPS) * kg_ref[...]
    dk_ref[...] = k_n
    dv_ref[...] = d_v

    @pl.when(pl.program_id(0) < prompt_tiles)
    def _transposed():
        qt_ref[...] = q_n.T
        kt_ref[...] = k_n.T
        vt_ref[...] = d_v.T
        for h in range(HEADS):
            kh_ref[h] = k_n[:, h * 2 * DIFF_HD:(h + 1) * 2 * DIFF_HD]


def _head_call(h, w, *, layer, n_seq, seq_rows, rows, model_inputs=None):
    per_seq = seq_rows // ROW_TILE
    p_tiles = n_seq * per_seq
    row = lambda n: pl.BlockSpec((ROW_TILE, n), lambda i: (i, 0))
    if h is not None:
        data, data_specs = [h], [row(D_MODEL)]
    else:
        x_prompt, x_rows, meta = model_inputs
        seq = x_prompt.shape[1]
        x_last = seq - ((per_seq - 1) * ROW_TILE)
        assert rows == (p_tiles + 1) * ROW_TILE and x_rows.shape[0] <= ROW_TILE and per_seq >= 2
        assert 0 < x_last <= ROW_TILE - N_META and x_last % 8 == 0 and ((per_seq - 1) * ROW_TILE) % x_last == 0

        def seq_tile(i):
            t = jnp.minimum(i, p_tiles - 1)
            return t // per_seq, t % per_seq

        cur = lambda i: (seq_tile(i)[0], jnp.minimum(seq_tile(i)[1], per_seq - 2), 0)
        prev = lambda i: (seq_tile(i)[0], jnp.maximum(seq_tile(i)[1] - 1, 0), 0)
        last = lambda i: (seq_tile(i)[0], (per_seq - 1) * ROW_TILE // x_last, 0)
        data = [x_prompt, x_prompt, x_prompt, x_rows, meta]
        data_specs = [pl.BlockSpec((None, ROW_TILE, D_MODEL), cur), pl.BlockSpec((None, ROW_TILE, D_MODEL), prev),
                      pl.BlockSpec((None, x_last, D_MODEL), last), _whole_spec(x_rows), _whole_spec(meta)]

    def col_map(i):
        t = jnp.minimum(i, p_tiles - 1)
        return (t // per_seq, 0, t % per_seq)

    col = pl.BlockSpec((None, 256, ROW_TILE), col_map)
    heads = pl.BlockSpec((None, HEADS, ROW_TILE, 2 * DIFF_HD),
                         lambda i: (col_map(i)[0], 0, col_map(i)[2], 0))
    consts = [w['g1'], w['wg1'], w['wu1'], w['wd1'], w['gm'], w['wlin'], w['blin'], w['wlow'], w['blow'],
              w['aw'], w['ab'], w['wdf'], w['bdf'], w['qg'], w['kg']]
    widths = (D_MODEL, N_LIN, LANE, 256, 256, 256)
    return pl.pallas_call(
        functools.partial(_head_kernel, prompt_tiles=p_tiles, per_seq=per_seq, first_layer=h is None),
        grid=(rows // ROW_TILE,),
        in_specs=data_specs + [_layer_spec(c, layer, single_buffer=True) for c in consts],
        out_specs=[row(n) for n in widths] + [col, col, col, heads],
        out_shape=([jax.ShapeDtypeStruct((rows, n), F32) for n in widths]
                   + [jax.ShapeDtypeStruct((n_seq, 256, seq_rows), F32)] * 3
                   + [jax.ShapeDtypeStruct((n_seq, HEADS, seq_rows, 2 * DIFF_HD), F32)]),
        compiler_params=pltpu.CompilerParams(dimension_semantics=("arbitrary",),
                                             vmem_limit_bytes=VMEM_LIMIT),
        name="head",
    )(*data, *consts)


def _tail_kernel(h1_ref, brlp_ref, brls_ref, brdp_ref, brds_ref, gm_ref, wgt_ref, bgt_ref, wb_ref, wo_ref, g2_ref,
                 wg_ref, wu_ref, wd_ref, *outs, prompt_tiles, last_layer):
    h1 = h1_ref[...]
    x = _rms(h1, gm_ref[...]).astype(BF16)
    is_sample = pl.program_id(0) >= prompt_tiles
    merged = None
    for n in range(N_BRANCH):
        if n < 3:
            cols = slice(n * BRANCH_W, (n + 1) * BRANCH_W)
            br_p = brlp_ref[:, :, cols].reshape(ROW_TILE, BRANCH_W)
            br = jnp.where(is_sample, brls_ref[:, cols], br_p)
        else:
            br = jnp.where(is_sample, brds_ref[...], brdp_ref[...])
        gate = _dot(x, wgt_ref[:, n * D_MODEL:(n + 1) * D_MODEL]) + bgt_ref[:, n * D_MODEL:(n + 1) * D_MODEL]
        term = _dot(br.astype(BF16), wb_ref[n]) * _sigmoid(gate)
        merged = term if merged is None else merged + term
    h2 = h1 + _dot(merged.astype(BF16), wo_ref[...])
    h3 = h2 + 0.5 * _ffn(_rms(h2, g2_ref[...]).astype(BF16), wg_ref, wu_ref, wd_ref)
    if not last_layer:
        outs[0][...] = h3
        return

    y_ref, ys_ref, carry_s = outs
    cut = ROW_TILE - N_META

    @pl.when(pl.program_id(0) == 0)
    def _no_previous_tile():
        carry_s[...] = jnp.zeros(carry_s.shape, F32)

    y_ref[0:cut, :] = carry_s[...]
    y_ref[cut:ROW_TILE, :] = h3[0:N_META, :]
    carry_s[...] = h3[N_META:, :]

    @pl.when(is_sample)
    def _sample_rows():
        ys_ref[...] = h3[0:ys_ref.shape[0], :]


def _tail_call(h1, brl_p, brl_s, brd_p, brd_s, w, *, layer, final=None):
    rows = h1.shape[0]
    p_tiles = brd_p.shape[0] // ROW_TILE
    group, chunk = brl_p.shape[2], brl_p.shape[3]
    per_tile = ROW_TILE // chunk
    per_seq = brl_p.shape[1] // per_tile
    row = lambda n: pl.BlockSpec((ROW_TILE, n), lambda i: (i, 0))
    row_p = lambda n: pl.BlockSpec((ROW_TILE, n), lambda i: (jnp.minimum(i, p_tiles - 1), 0))
    row_s = lambda n: pl.BlockSpec((ROW_TILE, n), lambda i: (jnp.maximum(i - p_tiles, 0), 0))

    def mixer_map(i):
        t = jnp.minimum(i, p_tiles - 1)
        seq = t // per_seq
        return (seq // group, t % per_seq, seq % group, 0, 0)

    mixer = pl.BlockSpec((None, per_tile, None, chunk, 3 * BRANCH_W), mixer_map)
    consts = [w['gm'], w['wgt'], w['bgt'], w['wb'], w['wo'], w['g2'], w['wg2'], w['wu2'], w['wd2']]
    if final is None:
        out_specs = row(D_MODEL)
        out_shape = jax.ShapeDtypeStruct((rows, D_MODEL), F32)
        scratch = []
    else:
        n_seq, seq, rows_s = final
        assert rows == (p_tiles + 1) * ROW_TILE and rows_s <= ROW_TILE

        def prev_tile(i):
            t = jnp.minimum(jnp.maximum(i - 1, 0), p_tiles - 1)
            return (t // per_seq, t % per_seq, 0)

        out_specs = [pl.BlockSpec((None, ROW_TILE, D_MODEL), prev_tile),
                     pl.BlockSpec((rows_s, D_MODEL), lambda i: (0, 0))]
        out_shape = [jax.ShapeDtypeStruct((n_seq, seq, D_MODEL), F32),
                     jax.ShapeDtypeStruct((rows_s, D_MODEL), F32)]
        scratch = [pltpu.VMEM((ROW_TILE - N_META, D_MODEL), F32)]
    return pl.pallas_call(
        functools.partial(_tail_kernel, prompt_tiles=p_tiles, last_layer=final is not None),
        grid=(rows // ROW_TILE,),
        in_specs=[row(D_MODEL), mixer, row_s(3 * BRANCH_W), row_p(BRANCH_W), row_s(BRANCH_W)]
                 + [_layer_spec(c, layer, single_buffer=True) for c in consts],
        out_specs=out_specs,
        out_shape=out_shape,
        scratch_shapes=scratch,
        compiler_params=pltpu.CompilerParams(dimension_semantics=("arbitrary",),
                                             vmem_limit_bytes=VMEM_LIMIT),
        name="tail",
    )(h1, brl_p, brl_s, brd_p, brd_s, *consts)


def _mix_kernel(*refs, chunk, n_chunks, seq_len, group):
    G = group
    z_refs, gd_refs = refs[0:G], refs[G:2 * G]
    cos_ref, sin_ref = refs[2 * G:2 * G + 2]
    cpast_refs, sr0_refs, sg0_refs = (refs[2 + (2 + k) * G:2 + (3 + k) * G] for k in range(3))
    consts = refs[2 + 5 * G:8 + 5 * G]
    br_ref, sr_out, sg_out, cs_out, sr_s, sg_s, u_s, ush_s = refs[8 + 5 * G:]
    C = chunk
    c = pl.program_id(1)

    @pl.when(c == 0)
    def _init():
        sr_s[...] = jnp.zeros(sr_s.shape, F32)
        sg_s[...] = jnp.zeros(sg_s.shape, F32)
        for g in range(G):
            for h in range(HEADS):
                sr_s[g, h * RET_DK:(h + 1) * RET_DK, h * RET_DV:(h + 1) * RET_DV] = sr0_refs[g][0, h]
                sg_s[g, h * GLA_DK:(h + 1) * GLA_DK, h * GLA_DV:(h + 1) * GLA_DV] = sg0_refs[g][0, h]
            u_s[g, 0:HIST - (CONV_K - 1), :] = jnp.zeros((HIST - (CONV_K - 1), CONV_C), F32)
            u_s[g, HIST - (CONV_K - 1):HIST, :] = cpast_refs[g][0]

    @pl.when(c > 0)
    def _shift():
        for g in range(G):
            u_s[g, 0:HIST, :] = u_s[g, C:C + HIST, :]

    for g in range(G):
        _mix_member(z_refs[g], gd_refs[g], cos_ref, sin_ref, *consts, br_ref.at[g], sr_s.at[g], sg_s.at[g],
                    u_s.at[g], ush_s.at[g], c=c, chunk=chunk, n_chunks=n_chunks, seq_len=seq_len)

    @pl.when(c == n_chunks - 1)
    def _final():
        n_last = seq_len - (n_chunks - 1) * C
        for g in range(G):
            for h in range(HEADS):
                sr_out[g, h] = sr_s[g, h * RET_DK:(h + 1) * RET_DK, h * RET_DV:(h + 1) * RET_DV]
                sg_out[g, h] = sg_s[g, h * GLA_DK:(h + 1) * GLA_DK, h * GLA_DV:(h + 1) * GLA_DV]
            cs_out[g] = u_s[g, HIST + n_last - (CONV_K - 1):HIST + n_last, :]


def _mix_member(z_ref, gd_ref, cos_ref, sin_ref, cw_ref, cb_ref, lng_ref, lnb_ref, rgn_ref, ggn_ref,
                br_ref, sr_s, sg_s, u_s, ush_s, *, c, chunk, n_chunks, seq_len):
    C = chunk
    low = C >= 16
    cast = (lambda a: a.astype(BF16)) if low else (lambda a: a)

    t_col = lax.broadcasted_iota(jnp.int32, (C, 1), 0)
    s_row = lax.broadcasted_iota(jnp.int32, (1, C), 1)
    causal = t_col >= s_row
    tf = t_col.astype(F32)
    padded = n_chunks * C > seq_len
    if padded:
        valid = (c * C + t_col) < seq_len
        nvf = jnp.zeros((1, 1), F32) + jnp.minimum(seq_len - c * C, C).astype(F32)
        keep = lambda a: jnp.where(valid, a, 0.0)
    else:
        nvf = jnp.full((1, 1), float(C), F32)
        keep = lambda a: a

    u = z_ref[:, 0:256] * _sigmoid(z_ref[:, 256:512])
    u_s[HIST:HIST + C, :] = u
    span = C + HIST - 8
    for s in range(1, 8):
        ush_s[s - 1] = u_s[s:s + span, :]
    acc = jnp.zeros((C, CONV_C), F32)
    for j in range(CONV_K):
        lo = HIST - (CONV_K - 1) + j
        q, r = lo - lo % 8, lo % 8
        tap = u_s[q:q + C, :] if r == 0 else ush_s[r - 1, q:q + C, :]
        acc = acc + cw_ref[j:j + 1, :] * tap
    conv = acc + cb_ref[...]
    xc = conv - jnp.mean(conv, axis=-1, keepdims=True)
    ln = xc * lax.rsqrt(jnp.mean(xc * xc, axis=-1, keepdims=True) + EPS) * lng_ref[...] + lnb_ref[...]
    br_ref[:, 0:256] = _silu(ln)

    lane = lax.broadcasted_iota(jnp.int32, (1, 256), 1)
    first_half = (lane % RET_DK) < (RET_DK // 2)
    cos = cos_ref[...]
    sin = sin_ref[...]

    def rope(a):
        swapped = jnp.where(first_half, pltpu.roll(a, 256 - RET_DK // 2, 1), pltpu.roll(a, RET_DK // 2, 1))
        return a * cos + swapped * sin

    rq = rope(z_ref[:, 512:768])
    rk = keep(rope(z_ref[:, 768:1024]) * RET_DK ** -0.5)
    rv = keep(z_ref[:, 1024:1280])
    lane_head = lane // RET_DV
    row_head = lax.broadcasted_iota(jnp.int32, (HEADS * RET_DK, 1), 0) // RET_DK
    log_gamma = [math.log1p(-(2.0 ** (-5 - h))) for h in range(HEADS)]
    lg_lane = jnp.zeros((1, 256), F32)
    lg_row = jnp.zeros((HEADS * RET_DK, 1), F32)
    for h in range(HEADS):
        lg_lane = jnp.where(lane_head == h, log_gamma[h], lg_lane)
        lg_row = jnp.where(row_head == h, log_gamma[h], lg_row)
    avg = _group_avg(256, RET_DV, low)
    dts = (t_col - s_row).astype(F32)
    rk_c = cast(rk)
    o = _dot(cast(rq * jnp.exp((tf + 1.0) * lg_lane)), cast(sr_s[...]))
    for h in range(HEADS):
        own = lane_head == h
        decay = jnp.where(causal, jnp.exp(dts * log_gamma[h]), 0.0)
        a = _dot_nt(cast(jnp.where(own, rq, 0.0)), rk_c) * decay
        o = o + _dot(cast(a), cast(jnp.where(own, rv, 0.0)))
    update = _dot_tn(cast(rk * jnp.exp((nvf - 1.0 - tf) * lg_lane)), cast(rv))
    sr_s[...] = jnp.exp(nvf * lg_row) * sr_s[...] + jnp.where(row_head == lane_head, update, 0.0)
    oc = o - _group_mean(o, avg, low)
    y = oc * lax.rsqrt(_group_mean(oc * oc, avg, low) + EPS) * rgn_ref[...]
    br_ref[:, 256:512] = _silu(z_ref[:, 1280:1536]) * y

    gq = z_ref[:, 1536:1664] * GLA_DK ** -0.5
    gk = keep(z_ref[:, 1664:1792])
    gv = keep(z_ref[:, 1792:2048])
    g = keep(gd_ref[...])
    tri = jnp.where(causal, 1.0, 0.0)
    if low:
        g_hi, g_lo = _split_bf16(g)
        tri = tri.astype(BF16)
        bcum = _dot(tri, g_hi) + _dot(tri, g_lo)
    else:
        bcum = _dot(tri, g)
    mid = C // 2 - 1
    b_mid = bcum[mid:mid + 1, :]
    b_last = bcum[C - 1:C, :]
    q_intra = gq * jnp.exp(bcum - b_mid)
    k_intra = gk * jnp.exp(b_mid - bcum)
    q_inter = gq * jnp.exp(bcum)
    k_state = gk * jnp.exp(b_last - bcum)
    e_last = jnp.where(t_col == C - 1, jnp.exp(bcum), 0.0)
    ones = jnp.ones((C, HEADS * GLA_DV), BF16 if low else F32)
    if low:
        e_hi, e_lo = _split_bf16(e_last)
        e_rows = _dot_tn(e_hi, ones) + _dot_tn(e_lo, ones)
    else:
        e_rows = _dot_tn(e_last, ones)
    key_head = lax.broadcasted_iota(jnp.int32, (1, HEADS * GLA_DK), 1) // GLA_DK
    krow_head = lax.broadcasted_iota(jnp.int32, (HEADS * GLA_DK, 1), 0) // GLA_DK
    k_c = cast(k_intra)
    o = _dot(cast(q_inter), cast(sg_s[...]))
    for h in range(HEADS):
        a = jnp.where(causal, _dot_nt(cast(jnp.where(key_head == h, q_intra, 0.0)), k_c), 0.0)
        o = o + _dot(cast(a), cast(jnp.where(lane_head == h, gv, 0.0)))
    update = _dot_tn(cast(k_state), cast(gv))
    sg_s[...] = e_rows * sg_s[...] + jnp.where(krow_head == lane_head, update, 0.0)
    y = o * lax.rsqrt(_group_mean(o * o, avg, low) + EPS) * ggn_ref[...]
    br_ref[:, 512:768] = _silu(z_ref[:, 2048:2304]) * y

def _mix_call(zlin, gdec, cos, sin, conv_past, ret0, gla0, w, *, layer, state_layer, n_seq, chunk, n_chunks,
              seq_len, row_block0, group, name):
    G = group
    assert n_seq % G == 0
    blk = lambda n, g: pl.BlockSpec((chunk, n), lambda s, c: (row_block0 + (s * G + g) * n_chunks + c, 0))
    grouped = lambda shp: pl.BlockSpec((None, G) + shp, lambda s, c: (s, 0) + (0,) * len(shp))
    state = lambda shp, g: pl.BlockSpec((None, 1) + shp,
                                        lambda s, c: (state_layer, s * G + g) + (0,) * len(shp))
    consts = [w['cw'], w['cb'], w['lng'], w['lnb'], w['rgn'], w['ggn']]
    members = range(G)
    in_specs = ([blk(N_LIN, g) for g in members] + [blk(LANE, g) for g in members]
                + [pl.BlockSpec((chunk, 256), lambda s, c: (c, 0)), pl.BlockSpec((chunk, 256), lambda s, c: (c, 0))]
                + [state((CONV_K - 1, CONV_C), g) for g in members]
                + [state((HEADS, RET_DK, RET_DV), g) for g in members]
                + [state((HEADS, GLA_DK, GLA_DV), g) for g in members]
                + [_layer_spec(a, layer) for a in consts])
    args = [zlin] * G + [gdec] * G + [cos, sin] + [conv_past] * G + [ret0] * G + [gla0] * G + consts
    br, sr, sg, cs = pl.pallas_call(
        functools.partial(_mix_kernel, chunk=chunk, n_chunks=n_chunks, seq_len=seq_len, group=G),
        grid=(n_seq // G, n_chunks),
        in_specs=in_specs,
        out_specs=[pl.BlockSpec((None, None, G, chunk, 3 * BRANCH_W), lambda s, c: (s, c, 0, 0, 0)),
                   grouped((HEADS, RET_DK, RET_DV)), grouped((HEADS, GLA_DK, GLA_DV)),
                   grouped((CONV_K - 1, CONV_C))],
        out_shape=[jax.ShapeDtypeStruct((n_seq // G, n_chunks, G, chunk, 3 * BRANCH_W), F32),
                   jax.ShapeDtypeStruct((n_seq // G, G, HEADS, RET_DK, RET_DV), F32),
                   jax.ShapeDtypeStruct((n_seq // G, G, HEADS, GLA_DK, GLA_DV), F32),
                   jax.ShapeDtypeStruct((n_seq // G, G, CONV_K - 1, CONV_C), F32)],
        scratch_shapes=[pltpu.VMEM((G, HEADS * RET_DK, HEADS * RET_DV), F32),
                        pltpu.VMEM((G, HEADS * GLA_DK, HEADS * GLA_DV), F32),
                        pltpu.VMEM((G, HIST + chunk, CONV_C), F32),
                        pltpu.VMEM((G, 7, HIST + chunk - 8, CONV_C), F32)],
        compiler_params=pltpu.CompilerParams(dimension_semantics=("arbitrary", "arbitrary")),
        name=name,
    )(*args)
    flat = lambda a: a.reshape((n_seq,) + a.shape[2:])
    return br, flat(sr), flat(sg), flat(cs)


def _t5_bucket_np(rel):
    rel = np.asarray(rel)
    n = np.maximum(rel, 0)
    max_exact = REL_BUCKETS // 2
    nf = np.maximum(n, 1).astype(np.float64)
    large = max_exact + (np.log(nf / max_exact) / math.log(REL_MAX_DIST / max_exact)
                         * (REL_BUCKETS - max_exact)).astype(np.int64)
    large = np.minimum(large, REL_BUCKETS - 1)
    return np.where(n < max_exact, n, large).astype(np.int32)


def _lambda(lamv_ref, lam_init):
    a = jnp.sum(lamv_ref[0:1, :] * lamv_ref[1:2, :], axis=-1, keepdims=True)
    b = jnp.sum(lamv_ref[2:3, :] * lamv_ref[3:4, :], axis=-1, keepdims=True)
    return jnp.exp(a) - jnp.exp(b) + lam_init


def _flash_kernel(bt_ref, rb_ref, top_ref, lamv_ref, sub_ref, qt_ref, k_ref, vt_ref, o_ref,
                  bias_s, qbd_s, m_s, l_s, acc_s, sta_s, stb_s, lf_s, accf_s, *, layer, lam_init, n_blocks):
    QB = SEQ_BLOCK
    SUB = FAR_BLOCK // QB
    b = pl.program_id(0)
    i = pl.program_id(1)

    @pl.when((b == 0) & (i == 0))
    def _tables():
        for kind, src in ((1, 1), (2, 0)):
            bt = bt_ref[src]
            tiles = [jnp.zeros((QB, QB), F32) for _ in range(HEADS)]
            for bk in range(REL_BUCKETS):
                hit = bt == bk
                for h in range(HEADS):
                    tiles[h] = jnp.where(hit, rb_ref[bk, h] * LOG2E, tiles[h])
            for h in range(HEADS):
                bias_s[kind, h] = jnp.where(bt < 0, NEG, tiles[h])
        for h in range(HEADS):
            bias_s[0, h] = jnp.zeros((QB, QB), F32) + rb_ref[REL_BUCKETS - 1, h] * LOG2E
            bias_s[3, h] = jnp.full((QB, QB), NEG, F32)
            for kind in range(4):
                bias_s[4 + kind, h] = bias_s[kind, h] - top_ref[layer, h]

    def prepare():
        qt = qt_ref[...] * (DIFF_HD ** -0.5 * LOG2E)
        first_branch = lax.broadcasted_iota(jnp.int32, (2 * DIFF_HD, 1), 0) < DIFF_HD
        for h in range(HEADS):
            qh = qt[h * 2 * DIFF_HD:(h + 1) * 2 * DIFF_HD, :]
            qbd_s[h, :, 0:QB] = jnp.where(first_branch, qh, 0.0).astype(BF16)
            qbd_s[h, :, QB:2 * QB] = jnp.where(first_branch, 0.0, qh).astype(BF16)
        m_s[...] = jnp.full(m_s.shape, NEG, F32)
        l_s[...] = jnp.zeros(l_s.shape, F32)
        acc_s[...] = jnp.zeros(acc_s.shape, F32)

    def online(h, st, shift, vt):
        m_old = m_s[h]
        m_new = jnp.maximum(m_old, jnp.max(st, axis=0, keepdims=True) + shift)
        alpha = jnp.exp2(m_old - m_new)
        pt = jnp.exp2(st - (m_new - shift))
        l_s[h] = alpha * l_s[h] + jnp.sum(pt, axis=0, keepdims=True)
        m_s[h] = m_new
        acc_s[h] = alpha * acc_s[h] + _dot(vt[h * DIFF_VD:(h + 1) * DIFF_VD, :], pt.astype(BF16))

    def fixed(h, st, vt):
        pt = jnp.exp2(st)
        l_s[h] = l_s[h] + jnp.sum(pt, axis=0, keepdims=True)
        acc_s[h] = acc_s[h] + _dot(vt[h * DIFF_VD:(h + 1) * DIFF_VD, :], pt.astype(BF16))

    n_far = jnp.maximum(i - 1, 0) // SUB

    def sweep(accumulate):
        def far_body(kb, carry):
            off = pl.multiple_of(kb * FAR_BLOCK, FAR_BLOCK)
            vt = vt_ref[:, pl.ds(off, FAR_BLOCK)].astype(BF16)
            for h in range(HEADS):
                st = _dot(k_ref[h, pl.ds(off, FAR_BLOCK), :].astype(BF16), qbd_s[h])
                accumulate(h, st, rb_ref[REL_BUCKETS - 1, h] * LOG2E, vt)
            return carry

        lax.fori_loop(0, n_far, far_body, 0)

        def near_body(kb, carry):
            first = kb * SUB
            blk0 = jnp.minimum(first, n_blocks - SUB)
            off = pl.multiple_of(blk0 * QB, QB)
            vt = vt_ref[:, pl.ds(off, FAR_BLOCK)].astype(BF16)
            kinds = []
            for c in range(SUB):
                blk = blk0 + c
                kind = jnp.where(blk == i, 2, jnp.where(blk == i - 1, 1, 0))
                kinds.append(jnp.where((blk > i) | (blk < first), 3, kind))
            for h in range(HEADS):
                st = _dot(k_ref[h, pl.ds(off, FAR_BLOCK), :].astype(BF16), qbd_s[h])
                parts = []
                for c in range(SUB):
                    tile = bias_s[kinds[c], h]
                    parts.append(st[c * QB:(c + 1) * QB, :] + jnp.concatenate([tile, tile], axis=1))
                accumulate(h, jnp.concatenate(parts, axis=0), 0.0, vt)
            return carry

        lax.fori_loop(n_far, i // SUB + 1, near_body, 0)

    bounded = top_ref[layer, HEADS] > 0.5

    def window(j):
        first = j * SUB
        return first, jnp.minimum(first, n_blocks - SUB)

    def score(j, buf):
        _, blk0 = window(j)
        off = pl.multiple_of(blk0 * QB, QB)
        for h in range(HEADS):
            buf[h] = _dot(k_ref[h, pl.ds(off, FAR_BLOCK), :].astype(BF16), qbd_s[h])

    def consume(j, buf):
        first, blk0 = window(j)
        vt = vt_ref[:, pl.ds(pl.multiple_of(blk0 * QB, QB), FAR_BLOCK)].astype(BF16)
        kinds = []
        for c in range(SUB):
            blk = blk0 + c
            kind = jnp.where(blk == i, 2, jnp.where(blk == i - 1, 1, 0))
            kinds.append(jnp.where((blk > i) | (blk < first), 3, kind))
        for h in range(HEADS):
            parts = []
            for c in range(SUB):
                tile = bias_s[kinds[c] + 4, h]
                parts.append(buf[h, c * QB:(c + 1) * QB, :] + jnp.concatenate([tile, tile], axis=1))
            fixed(h, jnp.concatenate(parts, axis=0), vt)

    def consume_far(j, buf):
        vt = vt_ref[:, pl.ds(pl.multiple_of(j * FAR_BLOCK, FAR_BLOCK), FAR_BLOCK)].astype(BF16)
        for h in range(HEADS):
            pt = jnp.exp2(buf[h])
            lf_s[h] = lf_s[h] + jnp.sum(pt, axis=0, keepdims=True)
            accf_s[h] = accf_s[h] + _dot(vt[h * DIFF_VD:(h + 1) * DIFF_VD, :], pt.astype(BF16))

    @pl.when(bounded)
    def _fixed():
        n_win = i // SUB + 1
        far_pairs = n_far // 2
        prepare()
        lf_s[...] = jnp.zeros(lf_s.shape, F32)
        accf_s[...] = jnp.zeros(accf_s.shape, F32)
        score(0, sta_s)

        def far_pair(m):
            score(2 * m + 1, stb_s)
            consume_far(2 * m, sta_s)
            score(2 * m + 2, sta_s)
            consume_far(2 * m + 1, stb_s)

        def far_quad(q, carry):
            far_pair(2 * q)
            far_pair(2 * q + 1)
            return carry

        lax.fori_loop(0, far_pairs // 2, far_quad, 0)

        @pl.when(far_pairs % 2 == 1)
        def _odd_far_pair():
            far_pair(far_pairs - 1)

        w0 = 2 * far_pairs
        rest = n_win - w0

        def pair(m, carry):
            score(w0 + 2 * m + 1, stb_s)
            consume(w0 + 2 * m, sta_s)
            score(w0 + 2 * m + 2, sta_s)
            consume(w0 + 2 * m + 1, stb_s)
            return carry

        lax.fori_loop(0, rest // 2, pair, 0)

        @pl.when(rest % 2 == 1)
        def _tail():
            consume(n_win - 1, sta_s)

        for h in range(HEADS):
            far_scale = jnp.exp2(jnp.zeros((1, 1), F32) + (rb_ref[REL_BUCKETS - 1, h] * LOG2E - top_ref[layer, h]))
            l_s[h] = l_s[h] + far_scale * lf_s[h]
            acc_s[h] = acc_s[h] + far_scale * accf_s[h]

    @pl.when(jnp.logical_not(bounded))
    def _online():
        prepare()
        sweep(online)

    lam = _lambda(lamv_ref, lam_init)
    outs = []
    for h in range(HEADS):
        ot = acc_s[h] / l_s[h]
        d = ot[:, 0:QB] - lam * ot[:, QB:2 * QB]
        y = d * lax.rsqrt(jnp.mean(d * d, axis=0, keepdims=True) + EPS) * sub_ref[...] * (1.0 - lam_init)
        outs.append(y)
    o_ref[...] = jnp.concatenate(outs, axis=0).T


def _score_top(rel_bias, q_gain, k_gain):
    reach = DIFF_HD ** 0.5 * jnp.max(jnp.abs(q_gain * k_gain), axis=-1, keepdims=True)
    top = (reach + jnp.max(rel_bias, axis=0)[None, :]) * LOG2E
    low = (-reach + jnp.min(rel_bias, axis=0)[None, :]) * LOG2E
    ok = jnp.all(top - low < MAX_EXP2_SPAN, axis=-1, keepdims=True).astype(F32)
    return jnp.concatenate([top, ok], axis=-1).astype(F32)


def _flash_call(bt, rel_bias, top, lamv, sub_col, dqt, dk, dvt, *, layer, n_seq, seq_rows, lam_init):
    rows = n_seq * seq_rows
    nq = seq_rows // SEQ_BLOCK
    assert seq_rows >= FAR_BLOCK
    smem = pl.BlockSpec(memory_space=pltpu.SMEM)
    return pl.pallas_call(
        functools.partial(_flash_kernel, layer=layer, lam_init=lam_init, n_blocks=nq),
        grid=(n_seq, nq),
        in_specs=[_whole_spec(bt), smem, smem, _layer_spec(lamv, layer), _layer_spec(sub_col, layer),
                  pl.BlockSpec((None, 256, SEQ_BLOCK), lambda b, i: (b, 0, i)),
                  pl.BlockSpec((None, HEADS, seq_rows, 2 * DIFF_HD), lambda b, i: (b, 0, 0, 0)),
                  pl.BlockSpec((None, 256, seq_rows), lambda b, i: (b, 0, 0))],
        out_specs=pl.BlockSpec((SEQ_BLOCK, 256), lambda b, i: (b * nq + i, 0)),
        out_shape=jax.ShapeDtypeStruct((rows, 256), F32),
        scratch_shapes=[pltpu.VMEM((8, HEADS, SEQ_BLOCK, SEQ_BLOCK), F32),
                        pltpu.VMEM((HEADS, 2 * DIFF_HD, 2 * SEQ_BLOCK), BF16),
                        pltpu.VMEM((HEADS, 1, 2 * SEQ_BLOCK), F32), pltpu.VMEM((HEADS, 1, 2 * SEQ_BLOCK), F32),
                        pltpu.VMEM((HEADS, DIFF_VD, 2 * SEQ_BLOCK), F32),
                        pltpu.VMEM((HEADS, FAR_BLOCK, 2 * SEQ_BLOCK), F32),
                        pltpu.VMEM((HEADS, FAR_BLOCK, 2 * SEQ_BLOCK), F32),
                        pltpu.VMEM((HEADS, 1, 2 * SEQ_BLOCK), F32),
                        pltpu.VMEM((HEADS, DIFF_VD, 2 * SEQ_BLOCK), F32)],
        compiler_params=pltpu.CompilerParams(dimension_semantics=("arbitrary", "arbitrary"),
                                             vmem_limit_bytes=VMEM_LIMIT),
        name="attn_prompt",
    )(bt, rel_bias, top, lamv, sub_col, dqt, dk, dvt)


def _decode_kernel(pt_ref, bt_ref, rb_ref, lamv_ref, sub_ref, q_ref, kn_ref, vn_ref, *refs,
                   pages_per_step, n_steps, dec_seq, lam_init):
    G = pages_per_step
    kt_refs = refs[:G]
    vt_refs = refs[G:2 * G]
    o_ref, bias_pg_s, bias_new_s, far_s, qs_s, m_s, l_s, acc_s = refs[2 * G:]
    del pt_ref
    b = pl.program_id(0)
    g = pl.program_id(1)
    last = g == n_steps - 1
    NQ = 2 * HEADS * dec_seq
    seqs_per_block = kn_ref.shape[0] // dec_seq

    row_head = lax.broadcasted_iota(jnp.int32, (NQ, 1), 0) // (2 * dec_seq)

    def bias_rows(bk):
        out = jnp.zeros((NQ, 1), F32)
        for h in range(HEADS):
            out = jnp.where(row_head == h, rb_ref[bk, h], out)
        return out

    @pl.when((b == 0) & (g == 0))
    def _tables():
        for src, dst in ((0, bias_pg_s), (1, bias_new_s)):
            bt = bt_ref[src]
            tile = jnp.zeros(bt.shape, F32)
            for bk in range(REL_BUCKETS):
                tile = jnp.where(bt == bk, bias_rows(bk), tile)
            dst[...] = jnp.where(bt < 0, NEG, tile)
        far_s[...] = jnp.zeros(far_s.shape, F32) + bias_rows(REL_BUCKETS - 1)

    @pl.when(g == 0)
    def _init():
        q = q_ref[...] * DIFF_HD ** -0.5
        lane = lax.broadcasted_iota(jnp.int32, (1, 256), 1)
        for hj in range(2 * HEADS):
            own = (lane >= hj * DIFF_HD) & (lane < (hj + 1) * DIFF_HD)
            qs_s[hj * dec_seq:(hj + 1) * dec_seq, :] = jnp.where(own, q, 0.0)
        m_s[...] = jnp.full(m_s.shape, NEG, F32)
        l_s[...] = jnp.zeros(l_s.shape, F32)
        acc_s[...] = jnp.zeros(acc_s.shape, F32)

    def update(scores, pv_fn):
        m_old = m_s[...]
        s_max = scores[0]
        for s in scores[1:]:
            s_max = jnp.maximum(s_max, s)
        m_new = jnp.maximum(m_old, jnp.max(s_max, axis=1, keepdims=True))
        alpha = jnp.exp(m_old - m_new)
        probs = [jnp.exp(s - m_new) for s in scores]
        p_sum = probs[0]
        for p in probs[1:]:
            p_sum = p_sum + p
        l_s[...] = alpha * l_s[...] + jnp.sum(p_sum, axis=1, keepdims=True)
        m_s[...] = m_new
        acc_s[...] = jnp.concatenate([alpha, alpha], axis=1) * acc_s[...] + pv_fn(probs)

    qs = qs_s[...].astype(BF16)
    scores = []
    for p in range(G):
        s = _dot(qs, kt_refs[p][...].astype(BF16))
        if p == G - 1:
            s = s + jnp.where(last, bias_pg_s[...], far_s[...])
        else:
            s = s + far_s[...]
        scores.append(s)

    def pv_pages(probs):
        out = None
        for p in range(G):
            t = _dot_nt(probs[p].astype(BF16), vt_refs[p][...].astype(BF16))
            out = t if out is None else out + t
        return out

    update(scores, pv_pages)

    @pl.when(last)
    def _finish():
        key_seq = lax.broadcasted_iota(jnp.int32, (1, kn_ref.shape[0]), 1) // dec_seq
        s_new = _dot_nt(qs, kn_ref[...].astype(BF16))
        s_new = jnp.where(key_seq == b % seqs_per_block, s_new + bias_new_s[...], NEG)
        update([s_new], lambda probs: _dot(probs[0].astype(BF16), vn_ref[...].astype(BF16)))
        l_all = l_s[...]
        o_all = acc_s[...] / jnp.concatenate([l_all, l_all], axis=1)
        lam = _lambda(lamv_ref, lam_init)
        for h in range(HEADS):
            r1 = (2 * h) * dec_seq
            r2 = (2 * h + 1) * dec_seq
            cols = slice(h * DIFF_VD, (h + 1) * DIFF_VD)
            d = o_all[r1:r1 + dec_seq, cols] - lam * o_all[r2:r2 + dec_seq, cols]
            y = d * lax.rsqrt(jnp.mean(d * d, axis=-1, keepdims=True) + EPS) * sub_ref[...] * (1.0 - lam_init)
            o_ref[:, cols] = y


def _decode_call(page_table, bt, rel_bias, lamv, sub, dq, dk, dv, cache_kt, cache_vt, *,
                 layer, n_seq, dec_seq, row0, lam_init):
    n_pages = page_table.shape[1]
    page = cache_kt.shape[3]
    G = min(PAGES_PER_STEP, n_pages)
    n_steps = n_pages // G
    nq = 2 * HEADS * dec_seq
    new_rows = LANE
    per_blk = new_rows // dec_seq
    own = pl.BlockSpec((dec_seq, 256), lambda b, g, pt: (row0 // dec_seq + b, 0))
    new = pl.BlockSpec((new_rows, 256), lambda b, g, pt: (row0 // new_rows + b // per_blk, 0))

    def page_spec(p):
        return pl.BlockSpec((None, None, 256, page),
                            lambda b, g, pt: (layer, pt[b * n_pages + g * G + p], 0, 0))

    in_specs = ([_whole_spec(bt), pl.BlockSpec(memory_space=pltpu.SMEM), _layer_spec(lamv, layer),
                 _layer_spec(sub, layer), own, new, new]
                + [page_spec(p) for p in range(G)] + [page_spec(p) for p in range(G)])
    args = [bt, rel_bias, lamv, sub, dq, dk, dv] + [cache_kt] * G + [cache_vt] * G
    return pl.pallas_call(
        functools.partial(_decode_kernel, pages_per_step=G, n_steps=n_steps, dec_seq=dec_seq, lam_init=lam_init),
        grid_spec=pltpu.PrefetchScalarGridSpec(
            num_scalar_prefetch=1,
            grid=(n_seq, n_steps),
            in_specs=in_specs,
            out_specs=pl.BlockSpec((dec_seq, 256), lambda b, g, pt: (b, 0)),
            scratch_shapes=[pltpu.VMEM((nq, page), F32), pltpu.VMEM((nq, new_rows), F32),
                            pltpu.VMEM((nq, page), F32), pltpu.VMEM((nq, 256), F32),
                            pltpu.VMEM((nq, LANE), F32), pltpu.VMEM((nq, LANE), F32), pltpu.VMEM((nq, 256), F32)]),
        out_shape=jax.ShapeDtypeStruct((n_seq * dec_seq, 256), F32),
        compiler_params=pltpu.CompilerParams(dimension_semantics=("arbitrary", "arbitrary"),
                                             vmem_limit_bytes=VMEM_LIMIT),
        name="attn_sample",
    )(page_table.reshape(-1), *args)


def _rope_tables(pos):
    half = RET_DK // 2
    inv = ROPE_BASE ** (-np.arange(half, dtype=np.float64) / half)
    ang = np.asarray(pos, np.float64)[:, None] * inv[None, :]
    cos = np.concatenate([np.cos(ang), np.cos(ang)], axis=1)
    sin = np.concatenate([-np.sin(ang), np.sin(ang)], axis=1)
    return (jnp.asarray(np.tile(cos, (1, HEADS)), F32), jnp.asarray(np.tile(sin, (1, HEADS)), F32))


def _prompt_bucket_tiles():
    t = np.arange(SEQ_BLOCK)
    rel = t[None, :] - t[:, None]
    diag = np.where(rel >= 0, _t5_bucket_np(rel), -1)
    sub = _t5_bucket_np(rel + SEQ_BLOCK)
    return jnp.asarray(np.stack([diag, sub]).astype(np.int32))


def _sample_bucket_tiles(page, dec_seq):
    assert page == LANE
    iq = (np.arange(2 * HEADS * dec_seq) % dec_seq)[:, None]
    past = _t5_bucket_np(page + iq - np.arange(page)[None, :])
    rel_new = iq - (np.arange(LANE) % dec_seq)[None, :]
    new = np.where(rel_new >= 0, _t5_bucket_np(rel_new), -1)
    return jnp.asarray(np.stack([past, new]).astype(np.int32))


def kernel(x_prompt, x_sample, cache_k, cache_v, page_table, state_ret, state_gla, state_conv, meta_tokens,
           rel_bias, norm_ffn1, ffn1_gate, ffn1_up, ffn1_down, norm_mix, w_in, b_in, conv_w, conv_b, conv_ln_g,
           conv_ln_b, ret_gn, gla_alpha_w, gla_alpha_b, gla_gn, q_norm, k_norm, lam_q1, lam_k1, lam_q2, lam_k2,
           diff_subln, w_branch, w_out, norm_ffn2, ffn2_gate, ffn2_up, ffn2_down):
    B, S, D = x_prompt.shape
    DB, DS, _ = x_sample.shape
    depth = w_in.shape[0]
    L = S + N_META
    Lp = -(-L // SEQ_BLOCK) * SEQ_BLOCK
    n_chunks = Lp // SEQ_BLOCK
    n_pool, page = cache_k.shape[1], cache_k.shape[2]
    past_len = page_table.shape[1] * page
    rows_p = B * Lp
    rows_s = DB * DS
    rows_sp = -(-rows_s // ROW_TILE) * ROW_TILE
    assert Lp % ROW_TILE == 0 and rows_s % LANE == 0 and DS % 8 == 0

    rows = rows_p + rows_sp
    h = None
    pad_s = lambda a: jnp.pad(a, ((0, rows_sp - rows_s), (0, 0)))

    cos_p, sin_p = _rope_tables(np.arange(Lp))
    cos_s, sin_s = _rope_tables(past_len + np.arange(DS))
    bt_prompt = _prompt_bucket_tiles()
    bt_sample = _sample_bucket_tiles(page, DS)
    ckt = jnp.transpose(cache_k, (0, 1, 3, 4, 2)).reshape(depth, n_pool, HEADS * 2 * DIFF_HD, page)
    cvt = jnp.transpose(cache_v, (0, 1, 3, 4, 2)).reshape(depth, n_pool, HEADS * DIFF_VD, page)
    zeros_conv = jnp.zeros((1, B, CONV_K - 1, CONV_C), F32)
    zeros_ret = jnp.zeros((1, B, HEADS, RET_DK, RET_DV), F32)
    zeros_gla = jnp.zeros((1, B, HEADS, GLA_DK, GLA_DV), F32)

    row3 = lambda a: a.reshape(depth, 1, -1).astype(F32)
    w = dict(
        g1=row3(norm_ffn1), wg1=ffn1_gate.astype(BF16), wu1=ffn1_up.astype(BF16), wd1=ffn1_down.astype(BF16),
        gm=row3(norm_mix),
        wlin=w_in[:, :, :N_LIN].astype(BF16), blin=row3(b_in[:, :N_LIN]),
        wlow=w_in[:, :, OFF_GLOW:OFF_DIFF].astype(BF16), blow=row3(b_in[:, OFF_GLOW:OFF_DIFF]),
        aw=gla_alpha_w.astype(BF16), ab=row3(gla_alpha_b),
        wdf=w_in[:, :, OFF_DIFF:OFF_GATES].astype(BF16), bdf=row3(b_in[:, OFF_DIFF:OFF_GATES]),
        qg=row3(jnp.tile(q_norm, (1, 2 * HEADS))), kg=row3(jnp.tile(k_norm, (1, 2 * HEADS))),
        wgt=w_in[:, :, OFF_GATES:].astype(BF16), bgt=row3(b_in[:, OFF_GATES:]),
        wb=w_branch.astype(BF16), wo=w_out.astype(BF16), g2=row3(norm_ffn2),
        wg2=ffn2_gate.astype(BF16), wu2=ffn2_up.astype(BF16), wd2=ffn2_down.astype(BF16),
        cw=conv_w, cb=row3(conv_b), lng=row3(conv_ln_g), lnb=row3(conv_ln_b),
        rgn=row3(ret_gn), ggn=row3(gla_gn),
    )
    lamv = jnp.stack([lam_q1, lam_k1, lam_q2, lam_k2], axis=1).astype(F32)
    sub = row3(diff_subln)
    sub_col = diff_subln.reshape(depth, -1, 1).astype(F32)
    top = _score_top(rel_bias, q_norm, k_norm)

    per_layer = []
    for l in range(depth):
        lam_init = 0.8 - 0.6 * math.exp(-0.3 * l)
        h1, zlin, gdec, dq, dk, dv, qt, kt, vt, kh = _head_call(
            h, w, layer=l, n_seq=B, seq_rows=Lp, rows=rows,
            model_inputs=(x_prompt, x_sample.reshape(rows_s, D), meta_tokens.astype(F32)))

        brl_p, rp, gp, cp = _mix_call(zlin, gdec, cos_p, sin_p, zeros_conv, zeros_ret, zeros_gla, w,
                                      layer=l, state_layer=0, n_seq=B, chunk=SEQ_BLOCK, n_chunks=n_chunks,
                                      seq_len=L, row_block0=0, group=MIX_GROUP, name="mix_prompt")
        brl_s, rs, gs, cs = _mix_call(zlin, gdec, cos_s, sin_s, state_conv, state_ret, state_gla, w,
                                      layer=l, state_layer=l, n_seq=DB, chunk=DS, n_chunks=1, seq_len=DS,
                                      row_block0=rows_p // DS, group=MIX_GROUP_SAMPLE, name="mix_sample")
        brd_p = _flash_call(bt_prompt, rel_bias, top, lamv, sub_col, qt, kh, vt,
                            layer=l, n_seq=B, seq_rows=Lp, lam_init=lam_init)
        brd_s = _decode_call(page_table, bt_sample, rel_bias, lamv, sub, dq, dk, dv, ckt, cvt,
                             layer=l, n_seq=DB, dec_seq=DS, row0=rows_p, lam_init=lam_init)

        h = _tail_call(h1, brl_p, pad_s(brl_s.reshape(rows_s, -1)), brd_p, pad_s(brd_s), w, layer=l,
                       final=(B, S, rows_s) if l == depth - 1 else None)
        per_layer.append(dict(kt=kt, vt=vt, ks=dk[rows_p:rows_p + rows_s], vs=dv[rows_p:rows_p + rows_s],
                              rp=rp, rs=rs, gp=gp, gs=gs, cp=cp, cs=cs))

    st = lambda k: jnp.stack([p[k] for p in per_layer], axis=0)
    seq_major = lambda t: jnp.transpose(t.reshape(depth, B, HEADS, -1, Lp)[..., :L], (0, 1, 4, 2, 3))
    y_prompt, y_rows = h
    y_sample = y_rows.reshape(DB, DS, D)
    return (y_prompt, y_sample, seq_major(st('kt')), seq_major(st('vt')),
            st('ks').reshape(depth, DB, DS, HEADS, 2 * DIFF_HD), st('vs').reshape(depth, DB, DS, HEADS, DIFF_VD),
            st('rp'), st('rs'), st('gp'), st('gs'), st('cp'), st('cs'))
```

```python
import functools
import math

import numpy as np
import jax
import jax.numpy as jnp
from jax import lax
from jax.experimental import pallas as pl
from jax.experimental.pallas import tpu as pltpu

F32 = jnp.float32
BF16 = jnp.bfloat16

D_MODEL = 1024
N_META = 16
N_BRANCH = 4
BRANCH_W = 256
D_FF = 2816
CONV_C = 256
CONV_K = 31
HEADS = 4
RET_DK = 64
RET_DV = 64
GLA_DK = 32
GLA_DV = 64
GLA_RANK = 16
GLA_TAU = 16.0
DIFF_HD = 32
DIFF_VD = 64
REL_BUCKETS = 32
REL_MAX_DIST = 128
ROPE_BASE = 10000.0
EPS = 1e-6
NEG = -1e30
LOG2E = math.log2(math.e)
MAX_EXP2_SPAN = 100.0

N_LIN = 2304
OFF_GLOW = N_LIN
OFF_DIFF = OFF_GLOW + GLA_RANK
OFF_GATES = OFF_DIFF + 3 * 256

LANE = 128
ROW_TILE = 384
FFN_CHUNKS = 1
MXU_N = 256
SEQ_BLOCK = 128
FAR_BLOCK = 512
HIST = 32
VMEM_LIMIT = 60 * 1024 * 1024
MIX_GROUP = 4
MIX_GROUP_SAMPLE = 4
PAGES_PER_STEP = 64


def _sigmoid(x):
    return 1.0 / (1.0 + jnp.exp(-x))


def _silu(x):
    return x * _sigmoid(x)


def _log_sigmoid(x):
    return jnp.minimum(x, 0.0) - jnp.log1p(jnp.exp(-jnp.abs(x)))


def _rms(x, g):
    return x * lax.rsqrt(jnp.mean(x * x, axis=-1, keepdims=True) + EPS) * g


def _dot(a, b):
    return jnp.dot(a, b, preferred_element_type=F32)


def _dot_nt(a, b):
    return lax.dot_general(a, b, (((1,), (1,)), ((), ())), preferred_element_type=F32)


def _dot_tn(a, b):
    return lax.dot_general(a, b, (((0,), (0,)), ((), ())), preferred_element_type=F32)


def _split_bf16(x):
    hi = x.astype(BF16)
    lo = (x - hi.astype(F32)).astype(BF16)
    return hi, lo


def _group_avg(n, group, low=True):
    r = lax.broadcasted_iota(jnp.int32, (n, n), 0) // group
    c = lax.broadcasted_iota(jnp.int32, (n, n), 1) // group
    return jnp.where(r == c, 1.0 / group, 0.0).astype(BF16 if low else F32)


def _group_mean(x, avg, low=True):
    if not low:
        return _dot(x, avg)
    hi, lo = _split_bf16(x)
    return _dot(hi, avg) + _dot(lo, avg)


def _layer_spec(a, layer, single_buffer=False):
    nd = a.ndim - 1
    kw = dict(pipeline_mode=pl.Buffered(1)) if single_buffer else {}
    return pl.BlockSpec((None,) + a.shape[1:], lambda *_: (layer,) + (0,) * nd, **kw)


def _whole_spec(a):
    return pl.BlockSpec(a.shape, lambda *_: (0,) * a.ndim)


def _ffn(x, wg_ref, wu_ref, wd_ref):
    tiles = D_FF // MXU_N
    edges = [(tiles * c // FFN_CHUNKS) * MXU_N for c in range(FFN_CHUNKS + 1)]
    out = None
    for c in range(FFN_CHUNKS):
        cols = slice(edges[c], edges[c + 1])
        act = (_silu(_dot(x, wg_ref[:, cols])) * _dot(x, wu_ref[:, cols])).astype(BF16)
        part = _dot(act, wd_ref[cols, :])
        out = part if out is None else out + part
    return out


def _assemble_rows(xcur_ref, xprev_ref, xlast_ref, xs_ref, meta_ref, *, prompt_tiles, per_seq):
    i = pl.program_id(0)
    j = i % per_seq
    cut = ROW_TILE - N_META
    prev_tail = xprev_ref[cut:, :]
    cur_head = xcur_ref[0:cut, :]
    x_last = xlast_ref[...]
    pad = jnp.zeros((cut - x_last.shape[0], D_MODEL), F32)
    body = jnp.concatenate([prev_tail, cur_head], axis=0)
    first = jnp.concatenate([meta_ref[...], cur_head], axis=0)
    last = jnp.concatenate([prev_tail, x_last, pad], axis=0)
    xs = xs_ref[...]
    sample = jnp.concatenate([xs, jnp.zeros((ROW_TILE - xs.shape[0], D_MODEL), F32)], axis=0)
    rows = jnp.where(j == 0, first, jnp.where(j == per_seq - 1, last, body))
    return jnp.where(i >= prompt_tiles, sample, rows)


def _head_kernel(*refs, prompt_tiles, per_seq, first_layer):
    n_in = 5 if first_layer else 1
    (g1_ref, wg_ref, wu_ref, wd_ref, gm_ref, wlin_ref, blin_ref, wlow_ref, blow_ref,
     aw_ref, ab_ref, wdf_ref, bdf_ref, qg_ref, kg_ref,
     h1_ref, zlin_ref, gdec_ref, dq_ref, dk_ref, dv_ref, qt_ref, kt_ref, vt_ref, kh_ref) = refs[n_in:]
    if first_layer:
        h = _assemble_rows(*refs[:n_in], prompt_tiles=prompt_tiles, per_seq=per_seq)
    else:
        h = refs[0][...]
    h1 = h + 0.5 * _ffn(_rms(h, g1_ref[...]).astype(BF16), wg_ref, wu_ref, wd_ref)
    h1_ref[...] = h1
    x = _rms(h1, gm_ref[...]).astype(BF16)
    zlin_ref[...] = _dot(x, wlin_ref[...]) + blin_ref[...]
    g_low = _dot(x, wlow_ref[...]) + blow_ref[...]
    g_pre = _dot(g_low.astype(BF16), aw_ref[...]) + ab_ref[...]
    gdec_ref[...] = _log_sigmoid(g_pre) * (1.0 / GLA_TAU)
    zd = _dot(x, wdf_ref[...]) + bdf_ref[...]
    d_q = zd[:, 0:256]
    d_k = zd[:, 256:512]
    d_v = zd[:, 512:768]
    avg = _group_avg(256, DIFF_HD)
    q_n = d_q * lax.rsqrt(_group_mean(d_q * d_q, avg) + EPS) * qg_ref[...]
    dq_ref[...] = q_n
    k_n = d_k * lax.rsqrt(_group_mean(d_k * d_k, avg) + EPS) * kg_ref[...]
    dk_ref[...] = k_n
    dv_ref[...] = d_v

    @pl.when(pl.program_id(0) < prompt_tiles)
    def _transposed():
        qt_ref[...] = q_n.T
        kt_ref[...] = k_n.T
        vt_ref[...] = d_v.T
        for h in range(HEADS):
            kh_ref[h] = k_n[:, h * 2 * DIFF_HD:(h + 1) * 2 * DIFF_HD]


def _head_call(h, w, *, layer, n_seq, seq_rows, rows, model_inputs=None):
    per_seq = seq_rows // ROW_TILE
    p_tiles = n_seq * per_seq
    row = lambda n: pl.BlockSpec((ROW_TILE, n), lambda i: (i, 0))
    if h is not None:
        data, data_specs = [h], [row(D_MODEL)]
    else:
        x_prompt, x_rows, meta = model_inputs
        seq = x_prompt.shape[1]
        x_last = seq - ((per_seq - 1) * ROW_TILE)
        assert rows == (p_tiles + 1) * ROW_TILE and x_rows.shape[0] <= ROW_TILE and per_seq >= 2
        assert 0 < x_last <= ROW_TILE - N_META and x_last % 8 == 0 and ((per_seq - 1) * ROW_TILE) % x_last == 0

        def seq_tile(i):
            t = jnp.minimum(i, p_tiles - 1)
            return t // per_seq, t % per_seq

        cur = lambda i: (seq_tile(i)[0], jnp.minimum(seq_tile(i)[1], per_seq - 2), 0)
        prev = lambda i: (seq_tile(i)[0], jnp.maximum(seq_tile(i)[1] - 1, 0), 0)
        last = lambda i: (seq_tile(i)[0], (per_seq - 1) * ROW_TILE // x_last, 0)
        data = [x_prompt, x_prompt, x_prompt, x_rows, meta]
        data_specs = [pl.BlockSpec((None, ROW_TILE, D_MODEL), cur), pl.BlockSpec((None, ROW_TILE, D_MODEL), prev),
                      pl.BlockSpec((None, x_last, D_MODEL), last), _whole_spec(x_rows), _whole_spec(meta)]

    def col_map(i):
        t = jnp.minimum(i, p_tiles - 1)
        return (t // per_seq, 0, t % per_seq)

    col = pl.BlockSpec((None, 256, ROW_TILE), col_map)
    heads = pl.BlockSpec((None, HEADS, ROW_TILE, 2 * DIFF_HD),
                         lambda i: (col_map(i)[0], 0, col_map(i)[2], 0))
    consts = [w['g1'], w['wg1'], w['wu1'], w['wd1'], w['gm'], w['wlin'], w['blin'], w['wlow'], w['blow'],
              w['aw'], w['ab'], w['wdf'], w['bdf'], w['qg'], w['kg']]
    widths = (D_MODEL, N_LIN, LANE, 256, 256, 256)
    return pl.pallas_call(
        functools.partial(_head_kernel, prompt_tiles=p_tiles, per_seq=per_seq, first_layer=h is None),
        grid=(rows // ROW_TILE,),
        in_specs=data_specs + [_layer_spec(c, layer, single_buffer=True) for c in consts],
        out_specs=[row(n) for n in widths] + [col, col, col, heads],
        out_shape=([jax.ShapeDtypeStruct((rows, n), F32) for n in widths]
                   + [jax.ShapeDtypeStruct((n_seq, 256, seq_rows), F32)] * 3
                   + [jax.ShapeDtypeStruct((n_seq, HEADS, seq_rows, 2 * DIFF_HD), F32)]),
        compiler_params=pltpu.CompilerParams(dimension_semantics=("arbitrary",),
                                             vmem_limit_bytes=VMEM_LIMIT),
        name="head",
    )(*data, *consts)


def _tail_kernel(h1_ref, brlp_ref, brls_ref, brdp_ref, brds_ref, gm_ref, wgt_ref, bgt_ref, wb_ref, wo_ref, g2_ref,
                 wg_ref, wu_ref, wd_ref, *outs, prompt_tiles, last_layer):
    h1 = h1_ref[...]
    x = _rms(h1, gm_ref[...]).astype(BF16)
    is_sample = pl.program_id(0) >= prompt_tiles
    merged = None
    for n in range(N_BRANCH):
        if n < 3:
            cols = slice(n * BRANCH_W, (n + 1) * BRANCH_W)
            br_p = brlp_ref[:, :, cols].reshape(ROW_TILE, BRANCH_W)
            br = jnp.where(is_sample, brls_ref[:, cols], br_p)
        else:
            br = jnp.where(is_sample, brds_ref[...], brdp_ref[...])
        gate = _dot(x, wgt_ref[:, n * D_MODEL:(n + 1) * D_MODEL]) + bgt_ref[:, n * D_MODEL:(n + 1) * D_MODEL]
        term = _dot(br.astype(BF16), wb_ref[n]) * _sigmoid(gate)
        merged = term if merged is None else merged + term
    h2 = h1 + _dot(merged.astype(BF16), wo_ref[...])
    h3 = h2 + 0.5 * _ffn(_rms(h2, g2_ref[...]).astype(BF16), wg_ref, wu_ref, wd_ref)
    if not last_layer:
        outs[0][...] = h3
        return

    y_ref, ys_ref, carry_s = outs
    cut = ROW_TILE - N_META

    @pl.when(pl.program_id(0) == 0)
    def _no_previous_tile():
        carry_s[...] = jnp.zeros(carry_s.shape, F32)

    y_ref[0:cut, :] = carry_s[...]
    y_ref[cut:ROW_TILE, :] = h3[0:N_META, :]
    carry_s[...] = h3[N_META:, :]

    @pl.when(is_sample)
    def _sample_rows():
        ys_ref[...] = h3[0:ys_ref.shape[0], :]


def _tail_call(h1, brl_p, brl_s, brd_p, brd_s, w, *, layer, final=None):
    rows = h1.shape[0]
    p_tiles = brd_p.shape[0] // ROW_TILE
    group, chunk = brl_p.shape[2], brl_p.shape[3]
    per_tile = ROW_TILE // chunk
    per_seq = brl_p.shape[1] // per_tile
    row = lambda n: pl.BlockSpec((ROW_TILE, n), lambda i: (i, 0))
    row_p = lambda n: pl.BlockSpec((ROW_TILE, n), lambda i: (jnp.minimum(i, p_tiles - 1), 0))
    row_s = lambda n: pl.BlockSpec((ROW_TILE, n), lambda i: (jnp.maximum(i - p_tiles, 0), 0))

    def mixer_map(i):
        t = jnp.minimum(i, p_tiles - 1)
        seq = t // per_seq
        return (seq // group, t % per_seq, seq % group, 0, 0)

    mixer = pl.BlockSpec((None, per_tile, None, chunk, 3 * BRANCH_W), mixer_map)
    consts = [w['gm'], w['wgt'], w['bgt'], w['wb'], w['wo'], w['g2'], w['wg2'], w['wu2'], w['wd2']]
    if final is None:
        out_specs = row(D_MODEL)
        out_shape = jax.ShapeDtypeStruct((rows, D_MODEL), F32)
        scratch = []
    else:
        n_seq, seq, rows_s = final
        assert rows == (p_tiles + 1) * ROW_TILE and rows_s <= ROW_TILE

        def prev_tile(i):
            t = jnp.minimum(jnp.maximum(i - 1, 0), p_tiles - 1)
            return (t // per_seq, t % per_seq, 0)

        out_specs = [pl.BlockSpec((None, ROW_TILE, D_MODEL), prev_tile),
                     pl.BlockSpec((rows_s, D_MODEL), lambda i: (0, 0))]
        out_shape = [jax.ShapeDtypeStruct((n_seq, seq, D_MODEL), F32),
                     jax.ShapeDtypeStruct((rows_s, D_MODEL), F32)]
        scratch = [pltpu.VMEM((ROW_TILE - N_META, D_MODEL), F32)]
    return pl.pallas_call(
        functools.partial(_tail_kernel, prompt_tiles=p_tiles, last_layer=final is not None),
        grid=(rows // ROW_TILE,),
        in_specs=[row(D_MODEL), mixer, row_s(3 * BRANCH_W), row_p(BRANCH_W), row_s(BRANCH_W)]
                 + [_layer_spec(c, layer, single_buffer=True) for c in consts],
        out_specs=out_specs,
        out_shape=out_shape,
        scratch_shapes=scratch,
        compiler_params=pltpu.CompilerParams(dimension_semantics=("arbitrary",),
                                             vmem_limit_bytes=VMEM_LIMIT),
        name="tail",
    )(h1, brl_p, brl_s, brd_p, brd_s, *consts)


def _mix_kernel(*refs, chunk, n_chunks, seq_len, group):
    G = group
    z_refs, gd_refs = refs[0:G], refs[G:2 * G]
    cos_ref, sin_ref = refs[2 * G:2 * G + 2]
    cpast_refs, sr0_refs, sg0_refs = (refs[2 + (2 + k) * G:2 + (3 + k) * G] for k in range(3))
    consts = refs[2 + 5 * G:8 + 5 * G]
    br_ref, sr_out, sg_out, cs_out, sr_s, sg_s, u_s, ush_s = refs[8 + 5 * G:]
    C = chunk
    c = pl.program_id(1)

    @pl.when(c == 0)
    def _init():
        sr_s[...] = jnp.zeros(sr_s.shape, F32)
        sg_s[...] = jnp.zeros(sg_s.shape, F32)
        for g in range(G):
            for h in range(HEADS):
                sr_s[g, h * RET_DK:(h + 1) * RET_DK, h * RET_DV:(h + 1) * RET_DV] = sr0_refs[g][0, h]
                sg_s[g, h * GLA_DK:(h + 1) * GLA_DK, h * GLA_DV:(h + 1) * GLA_DV] = sg0_refs[g][0, h]
            u_s[g, 0:HIST - (CONV_K - 1), :] = jnp.zeros((HIST - (CONV_K - 1), CONV_C), F32)
            u_s[g, HIST - (CONV_K - 1):HIST, :] = cpast_refs[g][0]

    @pl.when(c > 0)
    def _shift():
        for g in range(G):
            u_s[g, 0:HIST, :] = u_s[g, C:C + HIST, :]

    for g in range(G):
        _mix_member(z_refs[g], gd_refs[g], cos_ref, sin_ref, *consts, br_ref.at[g], sr_s.at[g], sg_s.at[g],
                    u_s.at[g], ush_s.at[g], c=c, chunk=chunk, n_chunks=n_chunks, seq_len=seq_len)

    @pl.when(c == n_chunks - 1)
    def _final():
        n_last = seq_len - (n_chunks - 1) * C
        for g in range(G):
            for h in range(HEADS):
                sr_out[g, h] = sr_s[g, h * RET_DK:(h + 1) * RET_DK, h * RET_DV:(h + 1) * RET_DV]
                sg_out[g, h] = sg_s[g, h * GLA_DK:(h + 1) * GLA_DK, h * GLA_DV:(h + 1) * GLA_DV]
            cs_out[g] = u_s[g, HIST + n_last - (CONV_K - 1):HIST + n_last, :]


def _mix_member(z_ref, gd_ref, cos_ref, sin_ref, cw_ref, cb_ref, lng_ref, lnb_ref, rgn_ref, ggn_ref,
                br_ref, sr_s, sg_s, u_s, ush_s, *, c, chunk, n_chunks, seq_len):
    C = chunk
    low = C >= 16
    cast = (lambda a: a.astype(BF16)) if low else (lambda a: a)

    t_col = lax.broadcasted_iota(jnp.int32, (C, 1), 0)
    s_row = lax.broadcasted_iota(jnp.int32, (1, C), 1)
    causal = t_col >= s_row
    tf = t_col.astype(F32)
    padded = n_chunks * C > seq_len
    if padded:
        valid = (c * C + t_col) < seq_len
        nvf = jnp.zeros((1, 1), F32) + jnp.minimum(seq_len - c * C, C).astype(F32)
        keep = lambda a: jnp.where(valid, a, 0.0)
    else:
        nvf = jnp.full((1, 1), float(C), F32)
        keep = lambda a: a

    u = z_ref[:, 0:256] * _sigmoid(z_ref[:, 256:512])
    u_s[HIST:HIST + C, :] = u
    span = C + HIST - 8
    for s in range(1, 8):
        ush_s[s - 1] = u_s[s:s + span, :]
    acc = jnp.zeros((C, CONV_C), F32)
    for j in range(CONV_K):
        lo = HIST - (CONV_K - 1) + j
        q, r = lo - lo % 8, lo % 8
        tap = u_s[q:q + C, :] if r == 0 else ush_s[r - 1, q:q + C, :]
        acc = acc + cw_ref[j:j + 1, :] * tap
    conv = acc + cb_ref[...]
    xc = conv - jnp.mean(conv, axis=-1, keepdims=True)
    ln = xc * lax.rsqrt(jnp.mean(xc * xc, axis=-1, keepdims=True) + EPS) * lng_ref[...] + lnb_ref[...]
    br_ref[:, 0:256] = _silu(ln)

    lane = lax.broadcasted_iota(jnp.int32, (1, 256), 1)
    first_half = (lane % RET_DK) < (RET_DK // 2)
    cos = cos_ref[...]
    sin = sin_ref[...]

    def rope(a):
        swapped = jnp.where(first_half, pltpu.roll(a, 256 - RET_DK // 2, 1), pltpu.roll(a, RET_DK // 2, 1))
        return a * cos + swapped * sin

    rq = rope(z_ref[:, 512:768])
    rk = keep(rope(z_ref[:, 768:1024]) * RET_DK ** -0.5)
    rv = keep(z_ref[:, 1024:1280])
    lane_head = lane // RET_DV
    row_head = lax.broadcasted_iota(jnp.int32, (HEADS * RET_DK, 1), 0) // RET_DK
    log_gamma = [math.log1p(-(2.0 ** (-5 - h))) for h in range(HEADS)]
    lg_lane = jnp.zeros((1, 256), F32)
    lg_row = jnp.zeros((HEADS * RET_DK, 1), F32)
    for h in range(HEADS):
        lg_lane = jnp.where(lane_head == h, log_gamma[h], lg_lane)
        lg_row = jnp.where(row_head == h, log_gamma[h], lg_row)
    avg = _group_avg(256, RET_DV, low)
    dts = (t_col - s_row).astype(F32)
    rk_c = cast(rk)
    o = _dot(cast(rq * jnp.exp((tf + 1.0) * lg_lane)), cast(sr_s[...]))
    for h in range(HEADS):
        own = lane_head == h
        decay = jnp.where(causal, jnp.exp(dts * log_gamma[h]), 0.0)
        a = _dot_nt(cast(jnp.where(own, rq, 0.0)), rk_c) * decay
        o = o + _dot(cast(a), cast(jnp.where(own, rv, 0.0)))
    update = _dot_tn(cast(rk * jnp.exp((nvf - 1.0 - tf) * lg_lane)), cast(rv))
    sr_s[...] = jnp.exp(nvf * lg_row) * sr_s[...] + jnp.where(row_head == lane_head, update, 0.0)
    oc = o - _group_mean(o, avg, low)
    y = oc * lax.rsqrt(_group_mean(oc * oc, avg, low) + EPS) * rgn_ref[...]
    br_ref[:, 256:512] = _silu(z_ref[:, 1280:1536]) * y

    gq = z_ref[:, 1536:1664] * GLA_DK ** -0.5
    gk = keep(z_ref[:, 1664:1792])
    gv = keep(z_ref[:, 1792:2048])
    g = keep(gd_ref[...])
    tri = jnp.where(causal, 1.0, 0.0)
    if low:
        g_hi, g_lo = _split_bf16(g)
        tri = tri.astype(BF16)
        bcum = _dot(tri, g_hi) + _dot(tri, g_lo)
    else:
        bcum = _dot(tri, g)
    mid = C // 2 - 1
    b_mid = bcum[mid:mid + 1, :]
    b_last = bcum[C - 1:C, :]
    q_intra = gq * jnp.exp(bcum - b_mid)
    k_intra = gk * jnp.exp(b_mid - bcum)
    q_inter = gq * jnp.exp(bcum)
    k_state = gk * jnp.exp(b_last - bcum)
    e_last = jnp.where(t_col == C - 1, jnp.exp(bcum), 0.0)
    ones = jnp.ones((C, HEADS * GLA_DV), BF16 if low else F32)
    if low:
        e_hi, e_lo = _split_bf16(e_last)
        e_rows = _dot_tn(e_hi, ones) + _dot_tn(e_lo, ones)
    else:
        e_rows = _dot_tn(e_last, ones)
    key_head = lax.broadcasted_iota(jnp.int32, (1, HEADS * GLA_DK), 1) // GLA_DK
    krow_head = lax.broadcasted_iota(jnp.int32, (HEADS * GLA_DK, 1), 0) // GLA_DK
    k_c = cast(k_intra)
    o = _dot(cast(q_inter), cast(sg_s[...]))
    for h in range(HEADS):
        a = jnp.where(causal, _dot_nt(cast(jnp.where(key_head == h, q_intra, 0.0)), k_c), 0.0)
        o = o + _dot(cast(a), cast(jnp.where(lane_head == h, gv, 0.0)))
    update = _dot_tn(cast(k_state), cast(gv))
    sg_s[...] = e_rows * sg_s[...] + jnp.where(krow_head == lane_head, update, 0.0)
    y = o * lax.rsqrt(_group_mean(o * o, avg, low) + EPS) * ggn_ref[...]
    br_ref[:, 512:768] = _silu(z_ref[:, 2048:2304]) * y

def _mix_call(zlin, gdec, cos, sin, conv_past, ret0, gla0, w, *, layer, state_layer, n_seq, chunk, n_chunks,
              seq_len, row_block0, group, name):
    G = group
    assert n_seq % G == 0
    blk = lambda n, g: pl.BlockSpec((chunk, n), lambda s, c: (row_block0 + (s * G + g) * n_chunks + c, 0))
    grouped = lambda shp: pl.BlockSpec((None, G) + shp, lambda s, c: (s, 0) + (0,) * len(shp))
    state = lambda shp, g: pl.BlockSpec((None, 1) + shp,
                                        lambda s, c: (state_layer, s * G + g) + (0,) * len(shp))
    consts = [w['cw'], w['cb'], w['lng'], w['lnb'], w['rgn'], w['ggn']]
    members = range(G)
    in_specs = ([blk(N_LIN, g) for g in members] + [blk(LANE, g) for g in members]
                + [pl.BlockSpec((chunk, 256), lambda s, c: (c, 0)), pl.BlockSpec((chunk, 256), lambda s, c: (c, 0))]
                + [state((CONV_K - 1, CONV_C), g) for g in members]
                + [state((HEADS, RET_DK, RET_DV), g) for g in members]
                + [state((HEADS, GLA_DK, GLA_DV), g) for g in members]
                + [_layer_spec(a, layer) for a in consts])
    args = [zlin] * G + [gdec] * G + [cos, sin] + [conv_past] * G + [ret0] * G + [gla0] * G + consts
    br, sr, sg, cs = pl.pallas_call(
        functools.partial(_mix_kernel, chunk=chunk, n_chunks=n_chunks, seq_len=seq_len, group=G),
        grid=(n_seq // G, n_chunks),
        in_specs=in_specs,
        out_specs=[pl.BlockSpec((None, None, G, chunk, 3 * BRANCH_W), lambda s, c: (s, c, 0, 0, 0)),
                   grouped((HEADS, RET_DK, RET_DV)), grouped((HEADS, GLA_DK, GLA_DV)),
                   grouped((CONV_K - 1, CONV_C))],
        out_shape=[jax.ShapeDtypeStruct((n_seq // G, n_chunks, G, chunk, 3 * BRANCH_W), F32),
                   jax.ShapeDtypeStruct((n_seq // G, G, HEADS, RET_DK, RET_DV), F32),
                   jax.ShapeDtypeStruct((n_seq // G, G, HEADS, GLA_DK, GLA_DV), F32),
                   jax.ShapeDtypeStruct((n_seq // G, G, CONV_K - 1, CONV_C), F32)],
        scratch_shapes=[pltpu.VMEM((G, HEADS * RET_DK, HEADS * RET_DV), F32),
                        pltpu.VMEM((G, HEADS * GLA_DK, HEADS * GLA_DV), F32),
                        pltpu.VMEM((G, HIST + chunk, CONV_C), F32),
                        pltpu.VMEM((G, 7, HIST + chunk - 8, CONV_C), F32)],
        compiler_params=pltpu.CompilerParams(dimension_semantics=("arbitrary", "arbitrary")),
        name=name,
    )(*args)
    flat = lambda a: a.reshape((n_seq,) + a.shape[2:])
    return br, flat(sr), flat(sg), flat(cs)


def _t5_bucket_np(rel):
    rel = np.asarray(rel)
    n = np.maximum(rel, 0)
    max_exact = REL_BUCKETS // 2
    nf = np.maximum(n, 1).astype(np.float64)
    large = max_exact + (np.log(nf / max_exact) / math.log(REL_MAX_DIST / max_exact)
                         * (REL_BUCKETS - max_exact)).astype(np.int64)
    large = np.minimum(large, REL_BUCKETS - 1)
    return np.where(n < max_exact, n, large).astype(np.int32)


def _lambda(lamv_ref, lam_init):
    a = jnp.sum(lamv_ref[0:1, :] * lamv_ref[1:2, :], axis=-1, keepdims=True)
    b = jnp.sum(lamv_ref[2:3, :] * lamv_ref[3:4, :], axis=-1, keepdims=True)
    return jnp.exp(a) - jnp.exp(b) + lam_init


def _flash_kernel(bt_ref, rb_ref, top_ref, lamv_ref, sub_ref, qt_ref, k_ref, vt_ref, o_ref,
                  bias_s, qbd_s, m_s, l_s, acc_s, sta_s, stb_s, lf_s, accf_s, *, layer, lam_init, n_blocks):
    QB = SEQ_BLOCK
    SUB = FAR_BLOCK // QB
    b = pl.program_id(0)
    i = pl.program_id(1)

    @pl.when((b == 0) & (i == 0))
    def _tables():
        for kind, src in ((1, 1), (2, 0)):
            bt = bt_ref[src]
            tiles = [jnp.zeros((QB, QB), F32) for _ in range(HEADS)]
            for bk in range(REL_BUCKETS):
                hit = bt == bk
                for h in range(HEADS):
                    tiles[h] = jnp.where(hit, rb_ref[bk, h] * LOG2E, tiles[h])
            for h in range(HEADS):
                bias_s[kind, h] = jnp.where(bt < 0, NEG, tiles[h])
        for h in range(HEADS):
            bias_s[0, h] = jnp.zeros((QB, QB), F32) + rb_ref[REL_BUCKETS - 1, h] * LOG2E
            bias_s[3, h] = jnp.full((QB, QB), NEG, F32)
            for kind in range(4):
                bias_s[4 + kind, h] = bias_s[kind, h] - top_ref[layer, h]

    qt = qt_ref[...] * (DIFF_HD ** -0.5 * LOG2E)
    first_branch = lax.broadcasted_iota(jnp.int32, (2 * DIFF_HD, 1), 0) < DIFF_HD
    for h in range(HEADS):
        qh = qt[h * 2 * DIFF_HD:(h + 1) * 2 * DIFF_HD, :]
        qbd_s[h, :, 0:QB] = jnp.where(first_branch, qh, 0.0).astype(BF16)
        qbd_s[h, :, QB:2 * QB] = jnp.where(first_branch, 0.0, qh).astype(BF16)
    m_s[...] = jnp.full(m_s.shape, NEG, F32)
    l_s[...] = jnp.zeros(l_s.shape, F32)
    acc_s[...] = jnp.zeros(acc_s.shape, F32)

    def online(h, st, shift, vt):
        m_old = m_s[h]
        m_new = jnp.maximum(m_old, jnp.max(st, axis=0, keepdims=True) + shift)
        alpha = jnp.exp2(m_old - m_new)
        pt = jnp.exp2(st - (m_new - shift))
        l_s[h] = alpha * l_s[h] + jnp.sum(pt, axis=0, keepdims=True)
        m_s[h] = m_new
        acc_s[h] = alpha * acc_s[h] + _dot(vt[h * DIFF_VD:(h + 1) * DIFF_VD, :], pt.astype(BF16))

    def fixed(h, st, vt):
        pt = jnp.exp2(st)
        l_s[h] = l_s[h] + jnp.sum(pt, axis=0, keepdims=True)
        acc_s[h] = acc_s[h] + _dot(vt[h * DIFF_VD:(h + 1) * DIFF_VD, :], pt.astype(BF16))

    n_far = jnp.maximum(i - 1, 0) // SUB

    def sweep(accumulate):
        def far_body(kb, carry):
            off = pl.multiple_of(kb * FAR_BLOCK, FAR_BLOCK)
            vt = vt_ref[:, pl.ds(off, FAR_BLOCK)].astype(BF16)
            for h in range(HEADS):
                st = _dot(k_ref[h, pl.ds(off, FAR_BLOCK), :].astype(BF16), qbd_s[h])
                accumulate(h, st, rb_ref[REL_BUCKETS - 1, h] * LOG2E, vt)
            return carry

        lax.fori_loop(0, n_far, far_body, 0)

        def near_body(kb, carry):
            first = kb * SUB
            blk0 = jnp.minimum(first, n_blocks - SUB)
            off = pl.multiple_of(blk0 * QB, QB)
            vt = vt_ref[:, pl.ds(off, FAR_BLOCK)].astype(BF16)
            kinds = []
            for c in range(SUB):
                blk = blk0 + c
                kind = jnp.where(blk == i, 2, jnp.where(blk == i - 1, 1, 0))
                kinds.append(jnp.where((blk > i) | (blk < first), 3, kind))
            for h in range(HEADS):
                st = _dot(k_ref[h, pl.ds(off, FAR_BLOCK), :].astype(BF16), qbd_s[h])
                parts = []
                for c in range(SUB):
                    tile = bias_s[kinds[c], h]
                    parts.append(st[c * QB:(c + 1) * QB, :] + jnp.concatenate([tile, tile], axis=1))
                accumulate(h, jnp.concatenate(parts, axis=0), 0.0, vt)
            return carry

        lax.fori_loop(n_far, i // SUB + 1, near_body, 0)

    bounded = top_ref[layer, HEADS] > 0.5

    def window(j):
        first = j * SUB
        return first, jnp.minimum(first, n_blocks - SUB)

    def score(j, buf):
        _, blk0 = window(j)
        off = pl.multiple_of(blk0 * QB, QB)
        for h in range(HEADS):
            buf[h] = _dot(k_ref[h, pl.ds(off, FAR_BLOCK), :].astype(BF16), qbd_s[h])

    def consume(j, buf):
        first, blk0 = window(j)
        vt = vt_ref[:, pl.ds(pl.multiple_of(blk0 * QB, QB), FAR_BLOCK)].astype(BF16)
        kinds = []
        for c in range(SUB):
            blk = blk0 + c
            kind = jnp.where(blk == i, 2, jnp.where(blk == i - 1, 1, 0))
            kinds.append(jnp.where((blk > i) | (blk < first), 3, kind))
        for h in range(HEADS):
            parts = []
            for c in range(SUB):
                tile = bias_s[kinds[c] + 4, h]
                parts.append(buf[h, c * QB:(c + 1) * QB, :] + jnp.concatenate([tile, tile], axis=1))
            fixed(h, jnp.concatenate(parts, axis=0), vt)

    def consume_far(j, buf):
        vt = vt_ref[:, pl.ds(pl.multiple_of(j * FAR_BLOCK, FAR_BLOCK), FAR_BLOCK)].astype(BF16)
        for h in range(HEADS):
            pt = jnp.exp2(buf[h])
            lf_s[h] = lf_s[h] + jnp.sum(pt, axis=0, keepdims=True)
            accf_s[h] = accf_s[h] + _dot(vt[h * DIFF_VD:(h + 1) * DIFF_VD, :], pt.astype(BF16))

    @pl.when(bounded)
    def _fixed():
        n_win = i // SUB + 1
        far_pairs = n_far // 2
        lf_s[...] = jnp.zeros(lf_s.shape, F32)
        accf_s[...] = jnp.zeros(accf_s.shape, F32)
        score(0, sta_s)

        def far_pair(m, carry):
            score(2 * m + 1, stb_s)
            consume_far(2 * m, sta_s)
            score(2 * m + 2, sta_s)
            consume_far(2 * m + 1, stb_s)
            return carry

        lax.fori_loop(0, far_pairs, far_pair, 0)
        w0 = 2 * far_pairs
        rest = n_win - w0

        def pair(m, carry):
            score(w0 + 2 * m + 1, stb_s)
            consume(w0 + 2 * m, sta_s)
            score(w0 + 2 * m + 2, sta_s)
            consume(w0 + 2 * m + 1, stb_s)
            return carry

        lax.fori_loop(0, rest // 2, pair, 0)

        @pl.when(rest % 2 == 1)
        def _tail():
            consume(n_win - 1, sta_s)

        for h in range(HEADS):
            far_scale = jnp.exp2(jnp.zeros((1, 1), F32) + (rb_ref[REL_BUCKETS - 1, h] * LOG2E - top_ref[layer, h]))
            l_s[h] = l_s[h] + far_scale * lf_s[h]
            acc_s[h] = acc_s[h] + far_scale * accf_s[h]

    @pl.when(jnp.logical_not(bounded))
    def _online():
        sweep(online)

    lam = _lambda(lamv_ref, lam_init)
    outs = []
    for h in range(HEADS):
        ot = acc_s[h] / l_s[h]
        d = ot[:, 0:QB] - lam * ot[:, QB:2 * QB]
        y = d * lax.rsqrt(jnp.mean(d * d, axis=0, keepdims=True) + EPS) * sub_ref[...] * (1.0 - lam_init)
        outs.append(y)
    o_ref[...] = jnp.concatenate(outs, axis=0).T


def _score_top(rel_bias, q_gain, k_gain):
    reach = DIFF_HD ** 0.5 * jnp.max(jnp.abs(q_gain * k_gain), axis=-1, keepdims=True)
    top = (reach + jnp.max(rel_bias, axis=0)[None, :]) * LOG2E
    low = (-reach + jnp.min(rel_bias, axis=0)[None, :]) * LOG2E
    ok = jnp.all(top - low < MAX_EXP2_SPAN, axis=-1, keepdims=True).astype(F32)
    return jnp.concatenate([top, ok], axis=-1).astype(F32)


def _flash_call(bt, rel_bias, top, lamv, sub_col, dqt, dk, dvt, *, layer, n_seq, seq_rows, lam_init):
    rows = n_seq * seq_rows
    nq = seq_rows // SEQ_BLOCK
    assert seq_rows >= FAR_BLOCK
    smem = pl.BlockSpec(memory_space=pltpu.SMEM)
    return pl.pallas_call(
        functools.partial(_flash_kernel, layer=layer, lam_init=lam_init, n_blocks=nq),
        grid=(n_seq, nq),
        in_specs=[_whole_spec(bt), smem, smem, _layer_spec(lamv, layer), _layer_spec(sub_col, layer),
                  pl.BlockSpec((None, 256, SEQ_BLOCK), lambda b, i: (b, 0, i)),
                  pl.BlockSpec((None, HEADS, seq_rows, 2 * DIFF_HD), lambda b, i: (b, 0, 0, 0)),
                  pl.BlockSpec((None, 256, seq_rows), lambda b, i: (b, 0, 0))],
        out_specs=pl.BlockSpec((SEQ_BLOCK, 256), lambda b, i: (b * nq + i, 0)),
        out_shape=jax.ShapeDtypeStruct((rows, 256), F32),
        scratch_shapes=[pltpu.VMEM((8, HEADS, SEQ_BLOCK, SEQ_BLOCK), F32),
                        pltpu.VMEM((HEADS, 2 * DIFF_HD, 2 * SEQ_BLOCK), BF16),
                        pltpu.VMEM((HEADS, 1, 2 * SEQ_BLOCK), F32), pltpu.VMEM((HEADS, 1, 2 * SEQ_BLOCK), F32),
                        pltpu.VMEM((HEADS, DIFF_VD, 2 * SEQ_BLOCK), F32),
                        pltpu.VMEM((HEADS, FAR_BLOCK, 2 * SEQ_BLOCK), F32),
                        pltpu.VMEM((HEADS, FAR_BLOCK, 2 * SEQ_BLOCK), F32),
                        pltpu.VMEM((HEADS, 1, 2 * SEQ_BLOCK), F32),
                        pltpu.VMEM((HEADS, DIFF_VD, 2 * SEQ_BLOCK), F32)],
        compiler_params=pltpu.CompilerParams(dimension_semantics=("arbitrary", "arbitrary"),
                                             vmem_limit_bytes=VMEM_LIMIT),
        name="attn_prompt",
    )(bt, rel_bias, top, lamv, sub_col, dqt, dk, dvt)


def _decode_kernel(pt_ref, bt_ref, rb_ref, lamv_ref, sub_ref, q_ref, kn_ref, vn_ref, kt_hbm, vt_hbm, o_ref,
                   bias_pg_s, bias_new_s, far_s, qs_s, m_s, l_s, acc_s, kbuf, vbuf, sem,
                   *, layer, pages_per_step, n_steps, n_seq, dec_seq, lam_init):
    G = pages_per_step
    b = pl.program_id(0)
    g = pl.program_id(1)
    last = g == n_steps - 1

    t = b * n_steps + g
    slot = t % 2

    def fetch(step, to_slot):
        def issue(p, carry):
            page_id = pt_ref[step * G + p]
            pltpu.make_async_copy(kt_hbm.at[layer, page_id], kbuf.at[to_slot, p], sem.at[to_slot, 0]).start()
            pltpu.make_async_copy(vt_hbm.at[layer, page_id], vbuf.at[to_slot, p], sem.at[to_slot, 1]).start()
            return carry

        lax.fori_loop(0, G, issue, 0)

    @pl.when(t == 0)
    def _first_pages():
        fetch(0, 0)

    @pl.when(t + 1 < n_seq * n_steps)
    def _next_pages():
        fetch(t + 1, 1 - slot)

    pltpu.make_async_copy(kt_hbm.at[layer, pl.ds(0, G)], kbuf.at[slot], sem.at[slot, 0]).wait()
    pltpu.make_async_copy(vt_hbm.at[layer, pl.ds(0, G)], vbuf.at[slot], sem.at[slot, 1]).wait()
    NQ = 2 * HEADS * dec_seq
    seqs_per_block = kn_ref.shape[0] // dec_seq

    row_head = lax.broadcasted_iota(jnp.int32, (NQ, 1), 0) // (2 * dec_seq)

    def bias_rows(bk):
        out = jnp.zeros((NQ, 1), F32)
        for h in range(HEADS):
            out = jnp.where(row_head == h, rb_ref[bk, h], out)
        return out

    @pl.when((b == 0) & (g == 0))
    def _tables():
        for src, dst in ((0, bias_pg_s), (1, bias_new_s)):
            bt = bt_ref[src]
            tile = jnp.zeros(bt.shape, F32)
            for bk in range(REL_BUCKETS):
                tile = jnp.where(bt == bk, bias_rows(bk), tile)
            dst[...] = jnp.where(bt < 0, NEG, tile)
        far_s[...] = jnp.zeros(far_s.shape, F32) + bias_rows(REL_BUCKETS - 1)

    @pl.when(g == 0)
    def _init():
        q = q_ref[...] * DIFF_HD ** -0.5
        lane = lax.broadcasted_iota(jnp.int32, (1, 256), 1)
        for hj in range(2 * HEADS):
            own = (lane >= hj * DIFF_HD) & (lane < (hj + 1) * DIFF_HD)
            qs_s[hj * dec_seq:(hj + 1) * dec_seq, :] = jnp.where(own, q, 0.0)
        m_s[...] = jnp.full(m_s.shape, NEG, F32)
        l_s[...] = jnp.zeros(l_s.shape, F32)
        acc_s[...] = jnp.zeros(acc_s.shape, F32)

    def update(scores, pv_fn):
        m_old = m_s[...]
        s_max = scores[0]
        for s in scores[1:]:
            s_max = jnp.maximum(s_max, s)
        m_new = jnp.maximum(m_old, jnp.max(s_max, axis=1, keepdims=True))
        alpha = jnp.exp(m_old - m_new)
        probs = [jnp.exp(s - m_new) for s in scores]
        p_sum = probs[0]
        for p in probs[1:]:
            p_sum = p_sum + p
        l_s[...] = alpha * l_s[...] + jnp.sum(p_sum, axis=1, keepdims=True)
        m_s[...] = m_new
        acc_s[...] = jnp.concatenate([alpha, alpha], axis=1) * acc_s[...] + pv_fn(probs)

    qs = qs_s[...].astype(BF16)
    scores = []
    for p in range(G):
        s = _dot(qs, kbuf[slot, p].astype(BF16))
        if p == G - 1:
            s = s + jnp.where(last, bias_pg_s[...], far_s[...])
        else:
            s = s + far_s[...]
        scores.append(s)

    def pv_pages(probs):
        out = None
        for p in range(G):
            pv = _dot_nt(probs[p].astype(BF16), vbuf[slot, p].astype(BF16))
            out = pv if out is None else out + pv
        return out

    update(scores, pv_pages)

    @pl.when(last)
    def _finish():
        key_seq = lax.broadcasted_iota(jnp.int32, (1, kn_ref.shape[0]), 1) // dec_seq
        s_new = _dot_nt(qs, kn_ref[...].astype(BF16))
        s_new = jnp.where(key_seq == b % seqs_per_block, s_new + bias_new_s[...], NEG)
        update([s_new], lambda probs: _dot(probs[0].astype(BF16), vn_ref[...].astype(BF16)))
        l_all = l_s[...]
        o_all = acc_s[...] / jnp.concatenate([l_all, l_all], axis=1)
        lam = _lambda(lamv_ref, lam_init)
        for h in range(HEADS):
            r1 = (2 * h) * dec_seq
            r2 = (2 * h + 1) * dec_seq
            cols = slice(h * DIFF_VD, (h + 1) * DIFF_VD)
            d = o_all[r1:r1 + dec_seq, cols] - lam * o_all[r2:r2 + dec_seq, cols]
            y = d * lax.rsqrt(jnp.mean(d * d, axis=-1, keepdims=True) + EPS) * sub_ref[...] * (1.0 - lam_init)
            o_ref[:, cols] = y


def _decode_call(page_table, bt, rel_bias, lamv, sub, dq, dk, dv, cache_kt, cache_vt, *,
                 layer, n_seq, dec_seq, row0, lam_init):
    n_pages = page_table.shape[1]
    page = cache_kt.shape[3]
    G = min(PAGES_PER_STEP, n_pages)
    n_steps = n_pages // G
    nq = 2 * HEADS * dec_seq
    new_rows = LANE
    per_blk = new_rows // dec_seq
    own = pl.BlockSpec((dec_seq, 256), lambda b, g, pt: (row0 // dec_seq + b, 0))
    new = pl.BlockSpec((new_rows, 256), lambda b, g, pt: (row0 // new_rows + b // per_blk, 0))

    assert n_pages == n_steps * G
    hbm = pl.BlockSpec(memory_space=pl.ANY)
    in_specs = [_whole_spec(bt), pl.BlockSpec(memory_space=pltpu.SMEM), _layer_spec(lamv, layer),
                _layer_spec(sub, layer), own, new, new, hbm, hbm]
    args = [bt, rel_bias, lamv, sub, dq, dk, dv, cache_kt, cache_vt]
    return pl.pallas_call(
        functools.partial(_decode_kernel, layer=layer, pages_per_step=G, n_steps=n_steps, n_seq=n_seq,
                          dec_seq=dec_seq, lam_init=lam_init),
        grid_spec=pltpu.PrefetchScalarGridSpec(
            num_scalar_prefetch=1,
            grid=(n_seq, n_steps),
            in_specs=in_specs,
            out_specs=pl.BlockSpec((dec_seq, 256), lambda b, g, pt: (b, 0)),
            scratch_shapes=[pltpu.VMEM((nq, page), F32), pltpu.VMEM((nq, new_rows), F32),
                            pltpu.VMEM((nq, page), F32), pltpu.VMEM((nq, 256), F32),
                            pltpu.VMEM((nq, LANE), F32), pltpu.VMEM((nq, LANE), F32), pltpu.VMEM((nq, 256), F32),
                            pltpu.VMEM((2, G, 256, page), F32), pltpu.VMEM((2, G, 256, page), F32),
                            pltpu.SemaphoreType.DMA((2, 2))]),
        out_shape=jax.ShapeDtypeStruct((n_seq * dec_seq, 256), F32),
        compiler_params=pltpu.CompilerParams(dimension_semantics=("arbitrary", "arbitrary"),
                                             vmem_limit_bytes=VMEM_LIMIT),
        name="attn_sample",
    )(page_table.reshape(-1), *args)


def _rope_tables(pos):
    half = RET_DK // 2
    inv = ROPE_BASE ** (-np.arange(half, dtype=np.float64) / half)
    ang = np.asarray(pos, np.float64)[:, None] * inv[None, :]
    cos = np.concatenate([np.cos(ang), np.cos(ang)], axis=1)
    sin = np.concatenate([-np.sin(ang), np.sin(ang)], axis=1)
    return (jnp.asarray(np.tile(cos, (1, HEADS)), F32), jnp.asarray(np.tile(sin, (1, HEADS)), F32))


def _prompt_bucket_tiles():
    t = np.arange(SEQ_BLOCK)
    rel = t[None, :] - t[:, None]
    diag = np.where(rel >= 0, _t5_bucket_np(rel), -1)
    sub = _t5_bucket_np(rel + SEQ_BLOCK)
    return jnp.asarray(np.stack([diag, sub]).astype(np.int32))


def _sample_bucket_tiles(page, dec_seq):
    assert page == LANE
    iq = (np.arange(2 * HEADS * dec_seq) % dec_seq)[:, None]
    past = _t5_bucket_np(page + iq - np.arange(page)[None, :])
    rel_new = iq - (np.arange(LANE) % dec_seq)[None, :]
    new = np.where(rel_new >= 0, _t5_bucket_np(rel_new), -1)
    return jnp.asarray(np.stack([past, new]).astype(np.int32))


def kernel(x_prompt, x_sample, cache_k, cache_v, page_table, state_ret, state_gla, state_conv, meta_tokens,
           rel_bias, norm_ffn1, ffn1_gate, ffn1_up, ffn1_down, norm_mix, w_in, b_in, conv_w, conv_b, conv_ln_g,
           conv_ln_b, ret_gn, gla_alpha_w, gla_alpha_b, gla_gn, q_norm, k_norm, lam_q1, lam_k1, lam_q2, lam_k2,
           diff_subln, w_branch, w_out, norm_ffn2, ffn2_gate, ffn2_up, ffn2_down):
    B, S, D = x_prompt.shape
    DB, DS, _ = x_sample.shape
    depth = w_in.shape[0]
    L = S + N_META
    Lp = -(-L // SEQ_BLOCK) * SEQ_BLOCK
    n_chunks = Lp // SEQ_BLOCK
    n_pool, page = cache_k.shape[1], cache_k.shape[2]
    past_len = page_table.shape[1] * page
    rows_p = B * Lp
    rows_s = DB * DS
    rows_sp = -(-rows_s // ROW_TILE) * ROW_TILE
    assert Lp % ROW_TILE == 0 and rows_s % LANE == 0 and DS % 8 == 0

    rows = rows_p + rows_sp
    h = None
    pad_s = lambda a: jnp.pad(a, ((0, rows_sp - rows_s), (0, 0)))

    cos_p, sin_p = _rope_tables(np.arange(Lp))
    cos_s, sin_s = _rope_tables(past_len + np.arange(DS))
    bt_prompt = _prompt_bucket_tiles()
    bt_sample = _sample_bucket_tiles(page, DS)
    ckt = jnp.transpose(cache_k, (0, 1, 3, 4, 2)).reshape(depth, n_pool, HEADS * 2 * DIFF_HD, page)
    cvt = jnp.transpose(cache_v, (0, 1, 3, 4, 2)).reshape(depth, n_pool, HEADS * DIFF_VD, page)
    zeros_conv = jnp.zeros((1, B, CONV_K - 1, CONV_C), F32)
    zeros_ret = jnp.zeros((1, B, HEADS, RET_DK, RET_DV), F32)
    zeros_gla = jnp.zeros((1, B, HEADS, GLA_DK, GLA_DV), F32)

    row3 = lambda a: a.reshape(depth, 1, -1).astype(F32)
    w = dict(
        g1=row3(norm_ffn1), wg1=ffn1_gate.astype(BF16), wu1=ffn1_up.astype(BF16), wd1=ffn1_down.astype(BF16),
        gm=row3(norm_mix),
        wlin=w_in[:, :, :N_LIN].astype(BF16), blin=row3(b_in[:, :N_LIN]),
        wlow=w_in[:, :, OFF_GLOW:OFF_DIFF].astype(BF16), blow=row3(b_in[:, OFF_GLOW:OFF_DIFF]),
        aw=gla_alpha_w.astype(BF16), ab=row3(gla_alpha_b),
        wdf=w_in[:, :, OFF_DIFF:OFF_GATES].astype(BF16), bdf=row3(b_in[:, OFF_DIFF:OFF_GATES]),
        qg=row3(jnp.tile(q_norm, (1, 2 * HEADS))), kg=row3(jnp.tile(k_norm, (1, 2 * HEADS))),
        wgt=w_in[:, :, OFF_GATES:].astype(BF16), bgt=row3(b_in[:, OFF_GATES:]),
        wb=w_branch.astype(BF16), wo=w_out.astype(BF16), g2=row3(norm_ffn2),
        wg2=ffn2_gate.astype(BF16), wu2=ffn2_up.astype(BF16), wd2=ffn2_down.astype(BF16),
        cw=conv_w, cb=row3(conv_b), lng=row3(conv_ln_g), lnb=row3(conv_ln_b),
        rgn=row3(ret_gn), ggn=row3(gla_gn),
    )
    lamv = jnp.stack([lam_q1, lam_k1, lam_q2, lam_k2], axis=1).astype(F32)
    sub = row3(diff_subln)
    sub_col = diff_subln.reshape(depth, -1, 1).astype(F32)
    top = _score_top(rel_bias, q_norm, k_norm)

    per_layer = []
    for l in range(depth):
        lam_init = 0.8 - 0.6 * math.exp(-0.3 * l)
        h1, zlin, gdec, dq, dk, dv, qt, kt, vt, kh = _head_call(
            h, w, layer=l, n_seq=B, seq_rows=Lp, rows=rows,
            model_inputs=(x_prompt, x_sample.reshape(rows_s, D), meta_tokens.astype(F32)))

        brl_p, rp, gp, cp = _mix_call(zlin, gdec, cos_p, sin_p, zeros_conv, zeros_ret, zeros_gla, w,
                                      layer=l, state_layer=0, n_seq=B, chunk=SEQ_BLOCK, n_chunks=n_chunks,
                                      seq_len=L, row_block0=0, group=MIX_GROUP, name="mix_prompt")
        brl_s, rs, gs, cs = _mix_call(zlin, gdec, cos_s, sin_s, state_conv, state_ret, state_gla, w,
                                      layer=l, state_layer=l, n_seq=DB, chunk=DS, n_chunks=1, seq_len=DS,
                                      row_block0=rows_p // DS, group=MIX_GROUP_SAMPLE, name="mix_sample")
        brd_p = _flash_call(bt_prompt, rel_bias, top, lamv, sub_col, qt, kh, vt,
                            layer=l, n_seq=B, seq_rows=Lp, lam_init=lam_init)
        brd_s = _decode_call(page_table, bt_sample, rel_bias, lamv, sub, dq, dk, dv, ckt, cvt,
                             layer=l, n_seq=DB, dec_seq=DS, row0=rows_p, lam_init=lam_init)

        h = _tail_call(h1, brl_p, pad_s(brl_s.reshape(rows_s, -1)), brd_p, pad_s(brd_s), w, layer=l,
                       final=(B, S, rows_s) if l == depth - 1 else None)
        per_layer.append(dict(kt=kt, vt=vt, ks=dk[rows_p:rows_p + rows_s], vs=dv[rows_p:rows_p + rows_s],
                              rp=rp, rs=rs, gp=gp, gs=gs, cp=cp, cs=cs))

    st = lambda k: jnp.stack([p[k] for p in per_layer], axis=0)
    seq_major = lambda t: jnp.transpose(t.reshape(depth, B, HEADS, -1, Lp)[..., :L], (0, 1, 4, 2, 3))
    y_prompt, y_rows = h
    y_sample = y_rows.reshape(DB, DS, D)
    return (y_prompt, y_sample, seq_major(st('kt')), seq_major(st('vt')),
            st('ks').reshape(depth, DB, DS, HEADS, 2 * DIFF_HD), st('vs').reshape(depth, DB, DS, HEADS, DIFF_VD),
            st('rp'), st('rs'), st('gp'), st('gs'), st('cp'), st('cs'))
```

```python
import functools
import math

import numpy as np
import jax
import jax.numpy as jnp
from jax import lax
from jax.experimental import pallas as pl
from jax.experimental.pallas import tpu as pltpu

F32 = jnp.float32
BF16 = jnp.bfloat16

D_MODEL = 1024
N_META = 16
N_BRANCH = 4
BRANCH_W = 256
D_FF = 2816
CONV_C = 256
CONV_K = 31
HEADS = 4
RET_DK = 64
RET_DV = 64
GLA_DK = 32
GLA_DV = 64
GLA_RANK = 16
GLA_TAU = 16.0
DIFF_HD = 32
DIFF_VD = 64
REL_BUCKETS = 32
REL_MAX_DIST = 128
ROPE_BASE = 10000.0
EPS = 1e-6
NEG = -1e30
LOG2E = math.log2(math.e)
MAX_EXP2_SPAN = 100.0

N_LIN = 2304
OFF_GLOW = N_LIN
OFF_DIFF = OFF_GLOW + GLA_RANK
OFF_GATES = OFF_DIFF + 3 * 256

LANE = 128
ROW_TILE = 384
FFN_CHUNKS = 1
MXU_N = 256
SEQ_BLOCK = 128
FAR_BLOCK = 512
HIST = 32
VMEM_LIMIT = 60 * 1024 * 1024
MIX_GROUP = 4
MIX_GROUP_SAMPLE = 4
PAGES_PER_STEP = 64


def _sigmoid(x):
    return 1.0 / (1.0 + jnp.exp(-x))


def _silu(x):
    return x * _sigmoid(x)


def _log_sigmoid(x):
    return jnp.minimum(x, 0.0) - jnp.log1p(jnp.exp(-jnp.abs(x)))


def _rms(x, g):
    return x * lax.rsqrt(jnp.mean(x * x, axis=-1, keepdims=True) + EPS) * g


def _dot(a, b):
    return jnp.dot(a, b, preferred_element_type=F32)


def _dot_nt(a, b):
    return lax.dot_general(a, b, (((1,), (1,)), ((), ())), preferred_element_type=F32)


def _dot_tn(a, b):
    return lax.dot_general(a, b, (((0,), (0,)), ((), ())), preferred_element_type=F32)


def _split_bf16(x):
    hi = x.astype(BF16)
    lo = (x - hi.astype(F32)).astype(BF16)
    return hi, lo


def _group_avg(n, group, low=True):
    r = lax.broadcasted_iota(jnp.int32, (n, n), 0) // group
    c = lax.broadcasted_iota(jnp.int32, (n, n), 1) // group
    return jnp.where(r == c, 1.0 / group, 0.0).astype(BF16 if low else F32)


def _group_mean(x, avg, low=True):
    if not low:
        return _dot(x, avg)
    hi, lo = _split_bf16(x)
    return _dot(hi, avg) + _dot(lo, avg)


def _layer_spec(a, layer, single_buffer=False):
    nd = a.ndim - 1
    kw = dict(pipeline_mode=pl.Buffered(1)) if single_buffer else {}
    return pl.BlockSpec((None,) + a.shape[1:], lambda *_: (layer,) + (0,) * nd, **kw)


def _whole_spec(a):
    return pl.BlockSpec(a.shape, lambda *_: (0,) * a.ndim)


def _ffn(x, wg_ref, wu_ref, wd_ref):
    tiles = D_FF // MXU_N
    edges = [(tiles * c // FFN_CHUNKS) * MXU_N for c in range(FFN_CHUNKS + 1)]
    out = None
    for c in range(FFN_CHUNKS):
        cols = slice(edges[c], edges[c + 1])
        act = (_silu(_dot(x, wg_ref[:, cols])) * _dot(x, wu_ref[:, cols])).astype(BF16)
        part = _dot(act, wd_ref[cols, :])
        out = part if out is None else out + part
    return out


def _assemble_rows(xcur_ref, xprev_ref, xlast_ref, xs_ref, meta_ref, *, prompt_tiles, per_seq):
    i = pl.program_id(0)
    j = i % per_seq
    cut = ROW_TILE - N_META
    prev_tail = xprev_ref[cut:, :]
    cur_head = xcur_ref[0:cut, :]
    x_last = xlast_ref[...]
    pad = jnp.zeros((cut - x_last.shape[0], D_MODEL), F32)
    body = jnp.concatenate([prev_tail, cur_head], axis=0)
    first = jnp.concatenate([meta_ref[...], cur_head], axis=0)
    last = jnp.concatenate([prev_tail, x_last, pad], axis=0)
    xs = xs_ref[...]
    sample = jnp.concatenate([xs, jnp.zeros((ROW_TILE - xs.shape[0], D_MODEL), F32)], axis=0)
    rows = jnp.where(j == 0, first, jnp.where(j == per_seq - 1, last, body))
    return jnp.where(i >= prompt_tiles, sample, rows)


def _head_kernel(*refs, prompt_tiles, per_seq, first_layer):
    n_in = 5 if first_layer else 1
    (g1_ref, wg_ref, wu_ref, wd_ref, gm_ref, wlin_ref, blin_ref, wlow_ref, blow_ref,
     aw_ref, ab_ref, wdf_ref, bdf_ref, qg_ref, kg_ref,
     h1_ref, zlin_ref, gdec_ref, dq_ref, dk_ref, dv_ref, qt_ref, kt_ref, vt_ref, kh_ref) = refs[n_in:]
    if first_layer:
        h = _assemble_rows(*refs[:n_in], prompt_tiles=prompt_tiles, per_seq=per_seq)
    else:
        h = refs[0][...]
    h1 = h + 0.5 * _ffn(_rms(h, g1_ref[...]).astype(BF16), wg_ref, wu_ref, wd_ref)
    h1_ref[...] = h1
    x = _rms(h1, gm_ref[...]).astype(BF16)
    zlin_ref[...] = _dot(x, wlin_ref[...]) + blin_ref[...]
    g_low = _dot(x, wlow_ref[...]) + blow_ref[...]
    g_pre = _dot(g_low.astype(BF16), aw_ref[...]) + ab_ref[...]
    gdec_ref[...] = _log_sigmoid(g_pre) * (1.0 / GLA_TAU)
    zd = _dot(x, wdf_ref[...]) + bdf_ref[...]
    d_q = zd[:, 0:256]
    d_k = zd[:, 256:512]
    d_v = zd[:, 512:768]
    avg = _group_avg(256, DIFF_HD)
    q_n = d_q * lax.rsqrt(_group_mean(d_q * d_q, avg) + EPS) * qg_ref[...]
    dq_ref[...] = q_n
    k_n = d_k * lax.rsqrt(_group_mean(d_k * d_k, avg) + EPS) * kg_ref[...]
    dk_ref[...] = k_n
    dv_ref[...] = d_v

    @pl.when(pl.program_id(0) < prompt_tiles)
    def _transposed():
        qt_ref[...] = q_n.T
        kt_ref[...] = k_n.T
        vt_ref[...] = d_v.T
        for h in range(HEADS):
            kh_ref[h] = k_n[:, h * 2 * DIFF_HD:(h + 1) * 2 * DIFF_HD]


def _head_call(h, w, *, layer, n_seq, seq_rows, rows, model_inputs=None):
    per_seq = seq_rows // ROW_TILE
    p_tiles = n_seq * per_seq
    row = lambda n: pl.BlockSpec((ROW_TILE, n), lambda i: (i, 0))
    if h is not None:
        data, data_specs = [h], [row(D_MODEL)]
    else:
        x_prompt, x_rows, meta = model_inputs
        seq = x_prompt.shape[1]
        x_last = seq - ((per_seq - 1) * ROW_TILE)
        assert rows == (p_tiles + 1) * ROW_TILE and x_rows.shape[0] <= ROW_TILE and per_seq >= 2
        assert 0 < x_last <= ROW_TILE - N_META and x_last % 8 == 0 and ((per_seq - 1) * ROW_TILE) % x_last == 0

        def seq_tile(i):
            t = jnp.minimum(i, p_tiles - 1)
            return t // per_seq, t % per_seq

        cur = lambda i: (seq_tile(i)[0], jnp.minimum(seq_tile(i)[1], per_seq - 2), 0)
        prev = lambda i: (seq_tile(i)[0], jnp.maximum(seq_tile(i)[1] - 1, 0), 0)
        last = lambda i: (seq_tile(i)[0], (per_seq - 1) * ROW_TILE // x_last, 0)
        data = [x_prompt, x_prompt, x_prompt, x_rows, meta]
        data_specs = [pl.BlockSpec((None, ROW_TILE, D_MODEL), cur), pl.BlockSpec((None, ROW_TILE, D_MODEL), prev),
                      pl.BlockSpec((None, x_last, D_MODEL), last), _whole_spec(x_rows), _whole_spec(meta)]

    def col_map(i):
        t = jnp.minimum(i, p_tiles - 1)
        return (t // per_seq, 0, t % per_seq)

    col = pl.BlockSpec((None, 256, ROW_TILE), col_map)
    heads = pl.BlockSpec((None, HEADS, ROW_TILE, 2 * DIFF_HD),
                         lambda i: (col_map(i)[0], 0, col_map(i)[2], 0))
    consts = [w['g1'], w['wg1'], w['wu1'], w['wd1'], w['gm'], w['wlin'], w['blin'], w['wlow'], w['blow'],
              w['aw'], w['ab'], w['wdf'], w['bdf'], w['qg'], w['kg']]
    widths = (D_MODEL, N_LIN, LANE, 256, 256, 256)
    return pl.pallas_call(
        functools.partial(_head_kernel, prompt_tiles=p_tiles, per_seq=per_seq, first_layer=h is None),
        grid=(rows // ROW_TILE,),
        in_specs=data_specs + [_layer_spec(c, layer, single_buffer=True) for c in consts],
        out_specs=[row(n) for n in widths] + [col, col, col, heads],
        out_shape=([jax.ShapeDtypeStruct((rows, n), F32) for n in widths]
                   + [jax.ShapeDtypeStruct((n_seq, 256, seq_rows), F32)] * 3
                   + [jax.ShapeDtypeStruct((n_seq, HEADS, seq_rows, 2 * DIFF_HD), F32)]),
        compiler_params=pltpu.CompilerParams(dimension_semantics=("arbitrary",),
                                             vmem_limit_bytes=VMEM_LIMIT),
        name="head",
    )(*data, *consts)


def _tail_kernel(h1_ref, brlp_ref, brls_ref, brdp_ref, brds_ref, gm_ref, wgt_ref, bgt_ref, wb_ref, wo_ref, g2_ref,
                 wg_ref, wu_ref, wd_ref, *outs, prompt_tiles, last_layer):
    h1 = h1_ref[...]
    x = _rms(h1, gm_ref[...]).astype(BF16)
    is_sample = pl.program_id(0) >= prompt_tiles
    merged = None
    for n in range(N_BRANCH):
        if n < 3:
            cols = slice(n * BRANCH_W, (n + 1) * BRANCH_W)
            br_p = brlp_ref[:, :, cols].reshape(ROW_TILE, BRANCH_W)
            br = jnp.where(is_sample, brls_ref[:, cols], br_p)
        else:
            br = jnp.where(is_sample, brds_ref[...], brdp_ref[...])
        gate = _dot(x, wgt_ref[:, n * D_MODEL:(n + 1) * D_MODEL]) + bgt_ref[:, n * D_MODEL:(n + 1) * D_MODEL]
        term = _dot(br.astype(BF16), wb_ref[n]) * _sigmoid(gate)
        merged = term if merged is None else merged + term
    h2 = h1 + _dot(merged.astype(BF16), wo_ref[...])
    h3 = h2 + 0.5 * _ffn(_rms(h2, g2_ref[...]).astype(BF16), wg_ref, wu_ref, wd_ref)
    if not last_layer:
        outs[0][...] = h3
        return

    y_ref, ys_ref, carry_s = outs
    cut = ROW_TILE - N_META

    @pl.when(pl.program_id(0) == 0)
    def _no_previous_tile():
        carry_s[...] = jnp.zeros(carry_s.shape, F32)

    y_ref[0:cut, :] = carry_s[...]
    y_ref[cut:ROW_TILE, :] = h3[0:N_META, :]
    carry_s[...] = h3[N_META:, :]

    @pl.when(is_sample)
    def _sample_rows():
        ys_ref[...] = h3[0:ys_ref.shape[0], :]


def _tail_call(h1, brl_p, brl_s, brd_p, brd_s, w, *, layer, final=None):
    rows = h1.shape[0]
    p_tiles = brd_p.shape[0] // ROW_TILE
    group, chunk = brl_p.shape[2], brl_p.shape[3]
    per_tile = ROW_TILE // chunk
    per_seq = brl_p.shape[1] // per_tile
    row = lambda n: pl.BlockSpec((ROW_TILE, n), lambda i: (i, 0))
    row_p = lambda n: pl.BlockSpec((ROW_TILE, n), lambda i: (jnp.minimum(i, p_tiles - 1), 0))
    row_s = lambda n: pl.BlockSpec((ROW_TILE, n), lambda i: (jnp.maximum(i - p_tiles, 0), 0))

    def mixer_map(i):
        t = jnp.minimum(i, p_tiles - 1)
        seq = t // per_seq
        return (seq // group, t % per_seq, seq % group, 0, 0)

    mixer = pl.BlockSpec((None, per_tile, None, chunk, 3 * BRANCH_W), mixer_map)
    consts = [w['gm'], w['wgt'], w['bgt'], w['wb'], w['wo'], w['g2'], w['wg2'], w['wu2'], w['wd2']]
    if final is None:
        out_specs = row(D_MODEL)
        out_shape = jax.ShapeDtypeStruct((rows, D_MODEL), F32)
        scratch = []
    else:
        n_seq, seq, rows_s = final
        assert rows == (p_tiles + 1) * ROW_TILE and rows_s <= ROW_TILE

        def prev_tile(i):
            t = jnp.minimum(jnp.maximum(i - 1, 0), p_tiles - 1)
            return (t // per_seq, t % per_seq, 0)

        out_specs = [pl.BlockSpec((None, ROW_TILE, D_MODEL), prev_tile),
                     pl.BlockSpec((rows_s, D_MODEL), lambda i: (0, 0))]
        out_shape = [jax.ShapeDtypeStruct((n_seq, seq, D_MODEL), F32),
                     jax.ShapeDtypeStruct((rows_s, D_MODEL), F32)]
        scratch = [pltpu.VMEM((ROW_TILE - N_META, D_MODEL), F32)]
    return pl.pallas_call(
        functools.partial(_tail_kernel, prompt_tiles=p_tiles, last_layer=final is not None),
        grid=(rows // ROW_TILE,),
        in_specs=[row(D_MODEL), mixer, row_s(3 * BRANCH_W), row_p(BRANCH_W), row_s(BRANCH_W)]
                 + [_layer_spec(c, layer, single_buffer=True) for c in consts],
        out_specs=out_specs,
        out_shape=out_shape,
        scratch_shapes=scratch,
        compiler_params=pltpu.CompilerParams(dimension_semantics=("arbitrary",),
                                             vmem_limit_bytes=VMEM_LIMIT),
        name="tail",
    )(h1, brl_p, brl_s, brd_p, brd_s, *consts)


def _mix_kernel(*refs, chunk, n_chunks, seq_len, group):
    G = group
    z_refs, gd_refs = refs[0:G], refs[G:2 * G]
    cos_ref, sin_ref = refs[2 * G:2 * G + 2]
    cpast_refs, sr0_refs, sg0_refs = (refs[2 + (2 + k) * G:2 + (3 + k) * G] for k in range(3))
    consts = refs[2 + 5 * G:8 + 5 * G]
    br_ref, sr_out, sg_out, cs_out, sr_s, sg_s, u_s, ush_s = refs[8 + 5 * G:]
    C = chunk
    c = pl.program_id(1)

    @pl.when(c == 0)
    def _init():
        sr_s[...] = jnp.zeros(sr_s.shape, F32)
        sg_s[...] = jnp.zeros(sg_s.shape, F32)
        for g in range(G):
            for h in range(HEADS):
                sr_s[g, h * RET_DK:(h + 1) * RET_DK, h * RET_DV:(h + 1) * RET_DV] = sr0_refs[g][0, h]
                sg_s[g, h * GLA_DK:(h + 1) * GLA_DK, h * GLA_DV:(h + 1) * GLA_DV] = sg0_refs[g][0, h]
            u_s[g, 0:HIST - (CONV_K - 1), :] = jnp.zeros((HIST - (CONV_K - 1), CONV_C), F32)
            u_s[g, HIST - (CONV_K - 1):HIST, :] = cpast_refs[g][0]

    @pl.when(c > 0)
    def _shift():
        for g in range(G):
            u_s[g, 0:HIST, :] = u_s[g, C:C + HIST, :]

    for g in range(G):
        _mix_member(z_refs[g], gd_refs[g], cos_ref, sin_ref, *consts, br_ref.at[g], sr_s.at[g], sg_s.at[g],
                    u_s.at[g], ush_s.at[g], c=c, chunk=chunk, n_chunks=n_chunks, seq_len=seq_len)

    @pl.when(c == n_chunks - 1)
    def _final():
        n_last = seq_len - (n_chunks - 1) * C
        for g in range(G):
            for h in range(HEADS):
                sr_out[g, h] = sr_s[g, h * RET_DK:(h + 1) * RET_DK, h * RET_DV:(h + 1) * RET_DV]
                sg_out[g, h] = sg_s[g, h * GLA_DK:(h + 1) * GLA_DK, h * GLA_DV:(h + 1) * GLA_DV]
            cs_out[g] = u_s[g, HIST + n_last - (CONV_K - 1):HIST + n_last, :]


def _mix_member(z_ref, gd_ref, cos_ref, sin_ref, cw_ref, cb_ref, lng_ref, lnb_ref, rgn_ref, ggn_ref,
                br_ref, sr_s, sg_s, u_s, ush_s, *, c, chunk, n_chunks, seq_len):
    C = chunk
    low = C >= 16
    cast = (lambda a: a.astype(BF16)) if low else (lambda a: a)

    t_col = lax.broadcasted_iota(jnp.int32, (C, 1), 0)
    s_row = lax.broadcasted_iota(jnp.int32, (1, C), 1)
    causal = t_col >= s_row
    tf = t_col.astype(F32)
    padded = n_chunks * C > seq_len
    if padded:
        valid = (c * C + t_col) < seq_len
        nvf = jnp.zeros((1, 1), F32) + jnp.minimum(seq_len - c * C, C).astype(F32)
        keep = lambda a: jnp.where(valid, a, 0.0)
    else:
        nvf = jnp.full((1, 1), float(C), F32)
        keep = lambda a: a

    u = z_ref[:, 0:256] * _sigmoid(z_ref[:, 256:512])
    u_s[HIST:HIST + C, :] = u
    span = C + HIST - 8
    for s in range(1, 8):
        ush_s[s - 1] = u_s[s:s + span, :]
    acc = jnp.zeros((C, CONV_C), F32)
    for j in range(CONV_K):
        lo = HIST - (CONV_K - 1) + j
        q, r = lo - lo % 8, lo % 8
        tap = u_s[q:q + C, :] if r == 0 else ush_s[r - 1, q:q + C, :]
        acc = acc + cw_ref[j:j + 1, :] * tap
    conv = acc + cb_ref[...]
    xc = conv - jnp.mean(conv, axis=-1, keepdims=True)
    ln = xc * lax.rsqrt(jnp.mean(xc * xc, axis=-1, keepdims=True) + EPS) * lng_ref[...] + lnb_ref[...]
    br_ref[:, 0:256] = _silu(ln)

    lane = lax.broadcasted_iota(jnp.int32, (1, 256), 1)
    first_half = (lane % RET_DK) < (RET_DK // 2)
    cos = cos_ref[...]
    sin = sin_ref[...]

    def rope(a):
        swapped = jnp.where(first_half, pltpu.roll(a, 256 - RET_DK // 2, 1), pltpu.roll(a, RET_DK // 2, 1))
        return a * cos + swapped * sin

    rq = rope(z_ref[:, 512:768])
    rk = keep(rope(z_ref[:, 768:1024]) * RET_DK ** -0.5)
    rv = keep(z_ref[:, 1024:1280])
    lane_head = lane // RET_DV
    row_head = lax.broadcasted_iota(jnp.int32, (HEADS * RET_DK, 1), 0) // RET_DK
    log_gamma = [math.log1p(-(2.0 ** (-5 - h))) for h in range(HEADS)]
    lg_lane = jnp.zeros((1, 256), F32)
    lg_row = jnp.zeros((HEADS * RET_DK, 1), F32)
    for h in range(HEADS):
        lg_lane = jnp.where(lane_head == h, log_gamma[h], lg_lane)
        lg_row = jnp.where(row_head == h, log_gamma[h], lg_row)
    avg = _group_avg(256, RET_DV, low)
    dts = (t_col - s_row).astype(F32)
    rk_c = cast(rk)
    o = _dot(cast(rq * jnp.exp((tf + 1.0) * lg_lane)), cast(sr_s[...]))
    for h in range(HEADS):
        own = lane_head == h
        decay = jnp.where(causal, jnp.exp(dts * log_gamma[h]), 0.0)
        a = _dot_nt(cast(jnp.where(own, rq, 0.0)), rk_c) * decay
        o = o + _dot(cast(a), cast(jnp.where(own, rv, 0.0)))
    update = _dot_tn(cast(rk * jnp.exp((nvf - 1.0 - tf) * lg_lane)), cast(rv))
    sr_s[...] = jnp.exp(nvf * lg_row) * sr_s[...] + jnp.where(row_head == lane_head, update, 0.0)
    oc = o - _group_mean(o, avg, low)
    y = oc * lax.rsqrt(_group_mean(oc * oc, avg, low) + EPS) * rgn_ref[...]
    br_ref[:, 256:512] = _silu(z_ref[:, 1280:1536]) * y

    gq = z_ref[:, 1536:1664] * GLA_DK ** -0.5
    gk = keep(z_ref[:, 1664:1792])
    gv = keep(z_ref[:, 1792:2048])
    g = keep(gd_ref[...])
    tri = jnp.where(causal, 1.0, 0.0)
    if low:
        g_hi, g_lo = _split_bf16(g)
        tri = tri.astype(BF16)
        bcum = _dot(tri, g_hi) + _dot(tri, g_lo)
    else:
        bcum = _dot(tri, g)
    mid = C // 2 - 1
    b_mid = bcum[mid:mid + 1, :]
    b_last = bcum[C - 1:C, :]
    q_intra = gq * jnp.exp(bcum - b_mid)
    k_intra = gk * jnp.exp(b_mid - bcum)
    q_inter = gq * jnp.exp(bcum)
    k_state = gk * jnp.exp(b_last - bcum)
    e_last = jnp.where(t_col == C - 1, jnp.exp(bcum), 0.0)
    ones = jnp.ones((C, HEADS * GLA_DV), BF16 if low else F32)
    if low:
        e_hi, e_lo = _split_bf16(e_last)
        e_rows = _dot_tn(e_hi, ones) + _dot_tn(e_lo, ones)
    else:
        e_rows = _dot_tn(e_last, ones)
    key_head = lax.broadcasted_iota(jnp.int32, (1, HEADS * GLA_DK), 1) // GLA_DK
    krow_head = lax.broadcasted_iota(jnp.int32, (HEADS * GLA_DK, 1), 0) // GLA_DK
    k_c = cast(k_intra)
    o = _dot(cast(q_inter), cast(sg_s[...]))
    for h in range(HEADS):
        a = jnp.where(causal, _dot_nt(cast(jnp.where(key_head == h, q_intra, 0.0)), k_c), 0.0)
        o = o + _dot(cast(a), cast(jnp.where(lane_head == h, gv, 0.0)))
    update = _dot_tn(cast(k_state), cast(gv))
    sg_s[...] = e_rows * sg_s[...] + jnp.where(krow_head == lane_head, update, 0.0)
    y = o * lax.rsqrt(_group_mean(o * o, avg, low) + EPS) * ggn_ref[...]
    br_ref[:, 512:768] = _silu(z_ref[:, 2048:2304]) * y

def _mix_call(zlin, gdec, cos, sin, conv_past, ret0, gla0, w, *, layer, state_layer, n_seq, chunk, n_chunks,
              seq_len, row_block0, group, name):
    G = group
    assert n_seq % G == 0
    blk = lambda n, g: pl.BlockSpec((chunk, n), lambda s, c: (row_block0 + (s * G + g) * n_chunks + c, 0))
    grouped = lambda shp: pl.BlockSpec((None, G) + shp, lambda s, c: (s, 0) + (0,) * len(shp))
    state = lambda shp, g: pl.BlockSpec((None, 1) + shp,
                                        lambda s, c: (state_layer, s * G + g) + (0,) * len(shp))
    consts = [w['cw'], w['cb'], w['lng'], w['lnb'], w['rgn'], w['ggn']]
    members = range(G)
    in_specs = ([blk(N_LIN, g) for g in members] + [blk(LANE, g) for g in members]
                + [pl.BlockSpec((chunk, 256), lambda s, c: (c, 0)), pl.BlockSpec((chunk, 256), lambda s, c: (c, 0))]
                + [state((CONV_K - 1, CONV_C), g) for g in members]
                + [state((HEADS, RET_DK, RET_DV), g) for g in members]
                + [state((HEADS, GLA_DK, GLA_DV), g) for g in members]
                + [_layer_spec(a, layer) for a in consts])
    args = [zlin] * G + [gdec] * G + [cos, sin] + [conv_past] * G + [ret0] * G + [gla0] * G + consts
    br, sr, sg, cs = pl.pallas_call(
        functools.partial(_mix_kernel, chunk=chunk, n_chunks=n_chunks, seq_len=seq_len, group=G),
        grid=(n_seq // G, n_chunks),
        in_specs=in_specs,
        out_specs=[pl.BlockSpec((None, None, G, chunk, 3 * BRANCH_W), lambda s, c: (s, c, 0, 0, 0)),
                   grouped((HEADS, RET_DK, RET_DV)), grouped((HEADS, GLA_DK, GLA_DV)),
                   grouped((CONV_K - 1, CONV_C))],
        out_shape=[jax.ShapeDtypeStruct((n_seq // G, n_chunks, G, chunk, 3 * BRANCH_W), F32),
                   jax.ShapeDtypeStruct((n_seq // G, G, HEADS, RET_DK, RET_DV), F32),
                   jax.ShapeDtypeStruct((n_seq // G, G, HEADS, GLA_DK, GLA_DV), F32),
                   jax.ShapeDtypeStruct((n_seq // G, G, CONV_K - 1, CONV_C), F32)],
        scratch_shapes=[pltpu.VMEM((G, HEADS * RET_DK, HEADS * RET_DV), F32),
                        pltpu.VMEM((G, HEADS * GLA_DK, HEADS * GLA_DV), F32),
                        pltpu.VMEM((G, HIST + chunk, CONV_C), F32),
                        pltpu.VMEM((G, 7, HIST + chunk - 8, CONV_C), F32)],
        compiler_params=pltpu.CompilerParams(dimension_semantics=("arbitrary", "arbitrary")),
        name=name,
    )(*args)
    flat = lambda a: a.reshape((n_seq,) + a.shape[2:])
    return br, flat(sr), flat(sg), flat(cs)


def _t5_bucket_np(rel):
    rel = np.asarray(rel)
    n = np.maximum(rel, 0)
    max_exact = REL_BUCKETS // 2
    nf = np.maximum(n, 1).astype(np.float64)
    large = max_exact + (np.log(nf / max_exact) / math.log(REL_MAX_DIST / max_exact)
                         * (REL_BUCKETS - max_exact)).astype(np.int64)
    large = np.minimum(large, REL_BUCKETS - 1)
    return np.where(n < max_exact, n, large).astype(np.int32)


def _lambda(lamv_ref, lam_init):
    a = jnp.sum(lamv_ref[0:1, :] * lamv_ref[1:2, :], axis=-1, keepdims=True)
    b = jnp.sum(lamv_ref[2:3, :] * lamv_ref[3:4, :], axis=-1, keepdims=True)
    return jnp.exp(a) - jnp.exp(b) + lam_init


def _flash_kernel(bt_ref, rb_ref, top_ref, lamv_ref, sub_ref, qt_ref, k_ref, vt_ref, o_ref,
                  bias_s, qbd_s, m_s, l_s, acc_s, sta_s, stb_s, lf_s, accf_s, *, layer, lam_init, n_blocks):
    QB = SEQ_BLOCK
    SUB = FAR_BLOCK // QB
    b = pl.program_id(0)
    i = pl.program_id(1)

    @pl.when((b == 0) & (i == 0))
    def _tables():
        for kind, src in ((1, 1), (2, 0)):
            bt = bt_ref[src]
            tiles = [jnp.zeros((QB, QB), F32) for _ in range(HEADS)]
            for bk in range(REL_BUCKETS):
                hit = bt == bk
                for h in range(HEADS):
                    tiles[h] = jnp.where(hit, rb_ref[bk, h] * LOG2E, tiles[h])
            for h in range(HEADS):
                bias_s[kind, h] = jnp.where(bt < 0, NEG, tiles[h])
        for h in range(HEADS):
            bias_s[0, h] = jnp.zeros((QB, QB), F32) + rb_ref[REL_BUCKETS - 1, h] * LOG2E
            bias_s[3, h] = jnp.full((QB, QB), NEG, F32)
            for kind in range(4):
                bias_s[4 + kind, h] = bias_s[kind, h] - top_ref[layer, h]

    qt = qt_ref[...] * (DIFF_HD ** -0.5 * LOG2E)
    first_branch = lax.broadcasted_iota(jnp.int32, (2 * DIFF_HD, 1), 0) < DIFF_HD
    for h in range(HEADS):
        qh = qt[h * 2 * DIFF_HD:(h + 1) * 2 * DIFF_HD, :]
        qbd_s[h, :, 0:QB] = jnp.where(first_branch, qh, 0.0).astype(BF16)
        qbd_s[h, :, QB:2 * QB] = jnp.where(first_branch, 0.0, qh).astype(BF16)
    m_s[...] = jnp.full(m_s.shape, NEG, F32)
    l_s[...] = jnp.zeros(l_s.shape, F32)
    acc_s[...] = jnp.zeros(acc_s.shape, F32)

    def online(h, st, shift, vt):
        m_old = m_s[h]
        m_new = jnp.maximum(m_old, jnp.max(st, axis=0, keepdims=True) + shift)
        alpha = jnp.exp2(m_old - m_new)
        pt = jnp.exp2(st - (m_new - shift))
        l_s[h] = alpha * l_s[h] + jnp.sum(pt, axis=0, keepdims=True)
        m_s[h] = m_new
        acc_s[h] = alpha * acc_s[h] + _dot(vt[h * DIFF_VD:(h + 1) * DIFF_VD, :], pt.astype(BF16))

    def fixed(h, st, vt):
        pt = jnp.exp2(st)
        l_s[h] = l_s[h] + jnp.sum(pt, axis=0, keepdims=True)
        acc_s[h] = acc_s[h] + _dot(vt[h * DIFF_VD:(h + 1) * DIFF_VD, :], pt.astype(BF16))

    n_far = jnp.maximum(i - 1, 0) // SUB

    def sweep(accumulate):
        def far_body(kb, carry):
            off = pl.multiple_of(kb * FAR_BLOCK, FAR_BLOCK)
            vt = vt_ref[:, pl.ds(off, FAR_BLOCK)].astype(BF16)
            for h in range(HEADS):
                st = _dot(k_ref[h, pl.ds(off, FAR_BLOCK), :].astype(BF16), qbd_s[h])
                accumulate(h, st, rb_ref[REL_BUCKETS - 1, h] * LOG2E, vt)
            return carry

        lax.fori_loop(0, n_far, far_body, 0)

        def near_body(kb, carry):
            first = kb * SUB
            blk0 = jnp.minimum(first, n_blocks - SUB)
            off = pl.multiple_of(blk0 * QB, QB)
            vt = vt_ref[:, pl.ds(off, FAR_BLOCK)].astype(BF16)
            kinds = []
            for c in range(SUB):
                blk = blk0 + c
                kind = jnp.where(blk == i, 2, jnp.where(blk == i - 1, 1, 0))
                kinds.append(jnp.where((blk > i) | (blk < first), 3, kind))
            for h in range(HEADS):
                st = _dot(k_ref[h, pl.ds(off, FAR_BLOCK), :].astype(BF16), qbd_s[h])
                parts = []
                for c in range(SUB):
                    tile = bias_s[kinds[c], h]
                    parts.append(st[c * QB:(c + 1) * QB, :] + jnp.concatenate([tile, tile], axis=1))
                accumulate(h, jnp.concatenate(parts, axis=0), 0.0, vt)
            return carry

        lax.fori_loop(n_far, i // SUB + 1, near_body, 0)

    bounded = top_ref[layer, HEADS] > 0.5

    def window(j):
        first = j * SUB
        return first, jnp.minimum(first, n_blocks - SUB)

    def score(j, buf):
        _, blk0 = window(j)
        off = pl.multiple_of(blk0 * QB, QB)
        for h in range(HEADS):
            buf[h] = _dot(k_ref[h, pl.ds(off, FAR_BLOCK), :].astype(BF16), qbd_s[h])

    def consume(j, buf):
        first, blk0 = window(j)
        vt = vt_ref[:, pl.ds(pl.multiple_of(blk0 * QB, QB), FAR_BLOCK)].astype(BF16)
        kinds = []
        for c in range(SUB):
            blk = blk0 + c
            kind = jnp.where(blk == i, 2, jnp.where(blk == i - 1, 1, 0))
            kinds.append(jnp.where((blk > i) | (blk < first), 3, kind))
        for h in range(HEADS):
            parts = []
            for c in range(SUB):
                tile = bias_s[kinds[c] + 4, h]
                parts.append(buf[h, c * QB:(c + 1) * QB, :] + jnp.concatenate([tile, tile], axis=1))
            fixed(h, jnp.concatenate(parts, axis=0), vt)

    def consume_far(j, buf):
        vt = vt_ref[:, pl.ds(pl.multiple_of(j * FAR_BLOCK, FAR_BLOCK), FAR_BLOCK)].astype(BF16)
        for h in range(HEADS):
            pt = jnp.exp2(buf[h])
            lf_s[h] = lf_s[h] + jnp.sum(pt, axis=0, keepdims=True)
            accf_s[h] = accf_s[h] + _dot(vt[h * DIFF_VD:(h + 1) * DIFF_VD, :], pt.astype(BF16))

    @pl.when(bounded)
    def _fixed():
        n_win = i // SUB + 1
        far_pairs = n_far // 2
        lf_s[...] = jnp.zeros(lf_s.shape, F32)
        accf_s[...] = jnp.zeros(accf_s.shape, F32)
        score(0, sta_s)

        def far_pair(m, carry):
            score(2 * m + 1, stb_s)
            consume_far(2 * m, sta_s)
            score(2 * m + 2, sta_s)
            consume_far(2 * m + 1, stb_s)
            return carry

        lax.fori_loop(0, far_pairs, far_pair, 0)
        w0 = 2 * far_pairs
        rest = n_win - w0

        def pair(m, carry):
            score(w0 + 2 * m + 1, stb_s)
            consume(w0 + 2 * m, sta_s)
            score(w0 + 2 * m + 2, sta_s)
            consume(w0 + 2 * m + 1, stb_s)
            return carry

        lax.fori_loop(0, rest // 2, pair, 0)

        @pl.when(rest % 2 == 1)
        def _tail():
            consume(n_win - 1, sta_s)

        for h in range(HEADS):
            far_scale = jnp.exp2(jnp.zeros((1, 1), F32) + (rb_ref[REL_BUCKETS - 1, h] * LOG2E - top_ref[layer, h]))
            l_s[h] = l_s[h] + far_scale * lf_s[h]
            acc_s[h] = acc_s[h] + far_scale * accf_s[h]

    @pl.when(jnp.logical_not(bounded))
    def _online():
        sweep(online)

    lam = _lambda(lamv_ref, lam_init)
    outs = []
    for h in range(HEADS):
        ot = acc_s[h] / l_s[h]
        d = ot[:, 0:QB] - lam * ot[:, QB:2 * QB]
        y = d * lax.rsqrt(jnp.mean(d * d, axis=0, keepdims=True) + EPS) * sub_ref[...] * (1.0 - lam_init)
        outs.append(y)
    o_ref[...] = jnp.concatenate(outs, axis=0).T


def _score_top(rel_bias, q_gain, k_gain):
    reach = DIFF_HD ** 0.5 * jnp.max(jnp.abs(q_gain * k_gain), axis=-1, keepdims=True)
    top = (reach + jnp.max(rel_bias, axis=0)[None, :]) * LOG2E
    low = (-reach + jnp.min(rel_bias, axis=0)[None, :]) * LOG2E
    ok = jnp.all(top - low < MAX_EXP2_SPAN, axis=-1, keepdims=True).astype(F32)
    return jnp.concatenate([top, ok], axis=-1).astype(F32)


def _flash_call(bt, rel_bias, top, lamv, sub_col, dqt, dk, dvt, *, layer, n_seq, seq_rows, lam_init):
    rows = n_seq * seq_rows
    nq = seq_rows // SEQ_BLOCK
    assert seq_rows >= FAR_BLOCK
    smem = pl.BlockSpec(memory_space=pltpu.SMEM)
    return pl.pallas_call(
        functools.partial(_flash_kernel, layer=layer, lam_init=lam_init, n_blocks=nq),
        grid=(n_seq, nq),
        in_specs=[_whole_spec(bt), smem, smem, _layer_spec(lamv, layer), _layer_spec(sub_col, layer),
                  pl.BlockSpec((None, 256, SEQ_BLOCK), lambda b, i: (b, 0, i)),
                  pl.BlockSpec((None, HEADS, seq_rows, 2 * DIFF_HD), lambda b, i: (b, 0, 0, 0)),
                  pl.BlockSpec((None, 256, seq_rows), lambda b, i: (b, 0, 0))],
        out_specs=pl.BlockSpec((SEQ_BLOCK, 256), lambda b, i: (b * nq + i, 0)),
        out_shape=jax.ShapeDtypeStruct((rows, 256), F32),
        scratch_shapes=[pltpu.VMEM((8, HEADS, SEQ_BLOCK, SEQ_BLOCK), F32),
                        pltpu.VMEM((HEADS, 2 * DIFF_HD, 2 * SEQ_BLOCK), BF16),
                        pltpu.VMEM((HEADS, 1, 2 * SEQ_BLOCK), F32), pltpu.VMEM((HEADS, 1, 2 * SEQ_BLOCK), F32),
                        pltpu.VMEM((HEADS, DIFF_VD, 2 * SEQ_BLOCK), F32),
                        pltpu.VMEM((HEADS, FAR_BLOCK, 2 * SEQ_BLOCK), F32),
                        pltpu.VMEM((HEADS, FAR_BLOCK, 2 * SEQ_BLOCK), F32),
                        pltpu.VMEM((HEADS, 1, 2 * SEQ_BLOCK), F32),
                        pltpu.VMEM((HEADS, DIFF_VD, 2 * SEQ_BLOCK), F32)],
        compiler_params=pltpu.CompilerParams(dimension_semantics=("arbitrary", "arbitrary"),
                                             vmem_limit_bytes=VMEM_LIMIT),
        name="attn_prompt",
    )(bt, rel_bias, top, lamv, sub_col, dqt, dk, dvt)


def _decode_kernel(pt_ref, bt_ref, rb_ref, lamv_ref, sub_ref, q_ref, kn_ref, vn_ref, kt_hbm, vt_hbm, o_ref,
                   bias_pg_s, bias_new_s, far_s, qs_s, m_s, l_s, acc_s, kbuf, vbuf, sem,
                   *, layer, pages_per_step, n_steps, n_seq, dec_seq, lam_init):
    G = pages_per_step
    b = pl.program_id(0)
    g = pl.program_id(1)
    last = g == n_steps - 1

    t = b * n_steps + g
    slot = t % 2

    def fetch(step, to_slot):
        def issue(p, carry):
            page_id = pt_ref[step * G + p]
            pltpu.make_async_copy(kt_hbm.at[layer, page_id], kbuf.at[to_slot, p], sem.at[to_slot, 0]).start(0)
            pltpu.make_async_copy(vt_hbm.at[layer, page_id], vbuf.at[to_slot, p], sem.at[to_slot, 1]).start(1)
            return carry

        lax.fori_loop(0, G, issue, 0)

    @pl.when(t == 0)
    def _first_pages():
        fetch(0, 0)

    @pl.when(t + 1 < n_seq * n_steps)
    def _next_pages():
        fetch(t + 1, 1 - slot)

    pltpu.make_async_copy(kt_hbm.at[layer, pl.ds(0, G)], kbuf.at[slot], sem.at[slot, 0]).wait()
    pltpu.make_async_copy(vt_hbm.at[layer, pl.ds(0, G)], vbuf.at[slot], sem.at[slot, 1]).wait()
    NQ = 2 * HEADS * dec_seq
    seqs_per_block = kn_ref.shape[0] // dec_seq

    row_head = lax.broadcasted_iota(jnp.int32, (NQ, 1), 0) // (2 * dec_seq)

    def bias_rows(bk):
        out = jnp.zeros((NQ, 1), F32)
        for h in range(HEADS):
            out = jnp.where(row_head == h, rb_ref[bk, h], out)
        return out

    @pl.when((b == 0) & (g == 0))
    def _tables():
        for src, dst in ((0, bias_pg_s), (1, bias_new_s)):
            bt = bt_ref[src]
            tile = jnp.zeros(bt.shape, F32)
            for bk in range(REL_BUCKETS):
                tile = jnp.where(bt == bk, bias_rows(bk), tile)
            dst[...] = jnp.where(bt < 0, NEG, tile)
        far_s[...] = jnp.zeros(far_s.shape, F32) + bias_rows(REL_BUCKETS - 1)

    @pl.when(g == 0)
    def _init():
        q = q_ref[...] * DIFF_HD ** -0.5
        lane = lax.broadcasted_iota(jnp.int32, (1, 256), 1)
        for hj in range(2 * HEADS):
            own = (lane >= hj * DIFF_HD) & (lane < (hj + 1) * DIFF_HD)
            qs_s[hj * dec_seq:(hj + 1) * dec_seq, :] = jnp.where(own, q, 0.0)
        m_s[...] = jnp.full(m_s.shape, NEG, F32)
        l_s[...] = jnp.zeros(l_s.shape, F32)
        acc_s[...] = jnp.zeros(acc_s.shape, F32)

    def update(scores, pv_fn):
        m_old = m_s[...]
        s_max = scores[0]
        for s in scores[1:]:
            s_max = jnp.maximum(s_max, s)
        m_new = jnp.maximum(m_old, jnp.max(s_max, axis=1, keepdims=True))
        alpha = jnp.exp(m_old - m_new)
        probs = [jnp.exp(s - m_new) for s in scores]
        p_sum = probs[0]
        for p in probs[1:]:
            p_sum = p_sum + p
        l_s[...] = alpha * l_s[...] + jnp.sum(p_sum, axis=1, keepdims=True)
        m_s[...] = m_new
        acc_s[...] = jnp.concatenate([alpha, alpha], axis=1) * acc_s[...] + pv_fn(probs)

    qs = qs_s[...].astype(BF16)
    scores = []
    for p in range(G):
        s = _dot(qs, kbuf[slot, p].astype(BF16))
        if p == G - 1:
            s = s + jnp.where(last, bias_pg_s[...], far_s[...])
        else:
            s = s + far_s[...]
        scores.append(s)

    def pv_pages(probs):
        out = None
        for p in range(G):
            pv = _dot_nt(probs[p].astype(BF16), vbuf[slot, p].astype(BF16))
            out = pv if out is None else out + pv
        return out

    update(scores, pv_pages)

    @pl.when(last)
    def _finish():
        key_seq = lax.broadcasted_iota(jnp.int32, (1, kn_ref.shape[0]), 1) // dec_seq
        s_new = _dot_nt(qs, kn_ref[...].astype(BF16))
        s_new = jnp.where(key_seq == b % seqs_per_block, s_new + bias_new_s[...], NEG)
        update([s_new], lambda probs: _dot(probs[0].astype(BF16), vn_ref[...].astype(BF16)))
        l_all = l_s[...]
        o_all = acc_s[...] / jnp.concatenate([l_all, l_all], axis=1)
        lam = _lambda(lamv_ref, lam_init)
        for h in range(HEADS):
            r1 = (2 * h) * dec_seq
            r2 = (2 * h + 1) * dec_seq
            cols = slice(h * DIFF_VD, (h + 1) * DIFF_VD)
            d = o_all[r1:r1 + dec_seq, cols] - lam * o_all[r2:r2 + dec_seq, cols]
            y = d * lax.rsqrt(jnp.mean(d * d, axis=-1, keepdims=True) + EPS) * sub_ref[...] * (1.0 - lam_init)
            o_ref[:, cols] = y


def _decode_call(page_table, bt, rel_bias, lamv, sub, dq, dk, dv, cache_kt, cache_vt, *,
                 layer, n_seq, dec_seq, row0, lam_init):
    n_pages = page_table.shape[1]
    page = cache_kt.shape[3]
    G = min(PAGES_PER_STEP, n_pages)
    n_steps = n_pages // G
    nq = 2 * HEADS * dec_seq
    new_rows = LANE
    per_blk = new_rows // dec_seq
    own = pl.BlockSpec((dec_seq, 256), lambda b, g, pt: (row0 // dec_seq + b, 0))
    new = pl.BlockSpec((new_rows, 256), lambda b, g, pt: (row0 // new_rows + b // per_blk, 0))

    assert n_pages == n_steps * G
    hbm = pl.BlockSpec(memory_space=pl.ANY)
    in_specs = [_whole_spec(bt), pl.BlockSpec(memory_space=pltpu.SMEM), _layer_spec(lamv, layer),
                _layer_spec(sub, layer), own, new, new, hbm, hbm]
    args = [bt, rel_bias, lamv, sub, dq, dk, dv, cache_kt, cache_vt]
    return pl.pallas_call(
        functools.partial(_decode_kernel, layer=layer, pages_per_step=G, n_steps=n_steps, n_seq=n_seq,
                          dec_seq=dec_seq, lam_init=lam_init),
        grid_spec=pltpu.PrefetchScalarGridSpec(
            num_scalar_prefetch=1,
            grid=(n_seq, n_steps),
            in_specs=in_specs,
            out_specs=pl.BlockSpec((dec_seq, 256), lambda b, g, pt: (b, 0)),
            scratch_shapes=[pltpu.VMEM((nq, page), F32), pltpu.VMEM((nq, new_rows), F32),
                            pltpu.VMEM((nq, page), F32), pltpu.VMEM((nq, 256), F32),
                            pltpu.VMEM((nq, LANE), F32), pltpu.VMEM((nq, LANE), F32), pltpu.VMEM((nq, 256), F32),
                            pltpu.VMEM((2, G, 256, page), F32), pltpu.VMEM((2, G, 256, page), F32),
                            pltpu.SemaphoreType.DMA((2, 2))]),
        out_shape=jax.ShapeDtypeStruct((n_seq * dec_seq, 256), F32),
        compiler_params=pltpu.CompilerParams(dimension_semantics=("arbitrary", "arbitrary"),
                                             vmem_limit_bytes=VMEM_LIMIT),
        name="attn_sample",
    )(page_table.reshape(-1), *args)


def _rope_tables(pos):
    half = RET_DK // 2
    inv = ROPE_BASE ** (-np.arange(half, dtype=np.float64) / half)
    ang = np.asarray(pos, np.float64)[:, None] * inv[None, :]
    cos = np.concatenate([np.cos(ang), np.cos(ang)], axis=1)
    sin = np.concatenate([-np.sin(ang), np.sin(ang)], axis=1)
    return (jnp.asarray(np.tile(cos, (1, HEADS)), F32), jnp.asarray(np.tile(sin, (1, HEADS)), F32))


def _prompt_bucket_tiles():
    t = np.arange(SEQ_BLOCK)
    rel = t[None, :] - t[:, None]
    diag = np.where(rel >= 0, _t5_bucket_np(rel), -1)
    sub = _t5_bucket_np(rel + SEQ_BLOCK)
    return jnp.asarray(np.stack([diag, sub]).astype(np.int32))


def _sample_bucket_tiles(page, dec_seq):
    assert page == LANE
    iq = (np.arange(2 * HEADS * dec_seq) % dec_seq)[:, None]
    past = _t5_bucket_np(page + iq - np.arange(page)[None, :])
    rel_new = iq - (np.arange(LANE) % dec_seq)[None, :]
    new = np.where(rel_new >= 0, _t5_bucket_np(rel_new), -1)
    return jnp.asarray(np.stack([past, new]).astype(np.int32))


def kernel(x_prompt, x_sample, cache_k, cache_v, page_table, state_ret, state_gla, state_conv, meta_tokens,
           rel_bias, norm_ffn1, ffn1_gate, ffn1_up, ffn1_down, norm_mix, w_in, b_in, conv_w, conv_b, conv_ln_g,
           conv_ln_b, ret_gn, gla_alpha_w, gla_alpha_b, gla_gn, q_norm, k_norm, lam_q1, lam_k1, lam_q2, lam_k2,
           diff_subln, w_branch, w_out, norm_ffn2, ffn2_gate, ffn2_up, ffn2_down):
    B, S, D = x_prompt.shape
    DB, DS, _ = x_sample.shape
    depth = w_in.shape[0]
    L = S + N_META
    Lp = -(-L // SEQ_BLOCK) * SEQ_BLOCK
    n_chunks = Lp // SEQ_BLOCK
    n_pool, page = cache_k.shape[1], cache_k.shape[2]
    past_len = page_table.shape[1] * page
    rows_p = B * Lp
    rows_s = DB * DS
    rows_sp = -(-rows_s // ROW_TILE) * ROW_TILE
    assert Lp % ROW_TILE == 0 and rows_s % LANE == 0 and DS % 8 == 0

    rows = rows_p + rows_sp
    h = None
    pad_s = lambda a: jnp.pad(a, ((0, rows_sp - rows_s), (0, 0)))

    cos_p, sin_p = _rope_tables(np.arange(Lp))
    cos_s, sin_s = _rope_tables(past_len + np.arange(DS))
    bt_prompt = _prompt_bucket_tiles()
    bt_sample = _sample_bucket_tiles(page, DS)
    ckt = jnp.transpose(cache_k, (0, 1, 3, 4, 2)).reshape(depth, n_pool, HEADS * 2 * DIFF_HD, page)
    cvt = jnp.transpose(cache_v, (0, 1, 3, 4, 2)).reshape(depth, n_pool, HEADS * DIFF_VD, page)
    zeros_conv = jnp.zeros((1, B, CONV_K - 1, CONV_C), F32)
    zeros_ret = jnp.zeros((1, B, HEADS, RET_DK, RET_DV), F32)
    zeros_gla = jnp.zeros((1, B, HEADS, GLA_DK, GLA_DV), F32)

    row3 = lambda a: a.reshape(depth, 1, -1).astype(F32)
    w = dict(
        g1=row3(norm_ffn1), wg1=ffn1_gate.astype(BF16), wu1=ffn1_up.astype(BF16), wd1=ffn1_down.astype(BF16),
        gm=row3(norm_mix),
        wlin=w_in[:, :, :N_LIN].astype(BF16), blin=row3(b_in[:, :N_LIN]),
        wlow=w_in[:, :, OFF_GLOW:OFF_DIFF].astype(BF16), blow=row3(b_in[:, OFF_GLOW:OFF_DIFF]),
        aw=gla_alpha_w.astype(BF16), ab=row3(gla_alpha_b),
        wdf=w_in[:, :, OFF_DIFF:OFF_GATES].astype(BF16), bdf=row3(b_in[:, OFF_DIFF:OFF_GATES]),
        qg=row3(jnp.tile(q_norm, (1, 2 * HEADS))), kg=row3(jnp.tile(k_norm, (1, 2 * HEADS))),
        wgt=w_in[:, :, OFF_GATES:].astype(BF16), bgt=row3(b_in[:, OFF_GATES:]),
        wb=w_branch.astype(BF16), wo=w_out.astype(BF16), g2=row3(norm_ffn2),
        wg2=ffn2_gate.astype(BF16), wu2=ffn2_up.astype(BF16), wd2=ffn2_down.astype(BF16),
        cw=conv_w, cb=row3(conv_b), lng=row3(conv_ln_g), lnb=row3(conv_ln_b),
        rgn=row3(ret_gn), ggn=row3(gla_gn),
    )
    lamv = jnp.stack([lam_q1, lam_k1, lam_q2, lam_k2], axis=1).astype(F32)
    sub = row3(diff_subln)
    sub_col = diff_subln.reshape(depth, -1, 1).astype(F32)
    top = _score_top(rel_bias, q_norm, k_norm)

    per_layer = []
    for l in range(depth):
        lam_init = 0.8 - 0.6 * math.exp(-0.3 * l)
        h1, zlin, gdec, dq, dk, dv, qt, kt, vt, kh = _head_call(
            h, w, layer=l, n_seq=B, seq_rows=Lp, rows=rows,
            model_inputs=(x_prompt, x_sample.reshape(rows_s, D), meta_tokens.astype(F32)))

        brl_p, rp, gp, cp = _mix_call(zlin, gdec, cos_p, sin_p, zeros_conv, zeros_ret, zeros_gla, w,
                                      layer=l, state_layer=0, n_seq=B, chunk=SEQ_BLOCK, n_chunks=n_chunks,
                                      seq_len=L, row_block0=0, group=MIX_GROUP, name="mix_prompt")
        brl_s, rs, gs, cs = _mix_call(zlin, gdec, cos_s, sin_s, state_conv, state_ret, state_gla, w,
                                      layer=l, state_layer=l, n_seq=DB, chunk=DS, n_chunks=1, seq_len=DS,
                                      row_block0=rows_p // DS, group=MIX_GROUP_SAMPLE, name="mix_sample")
        brd_p = _flash_call(bt_prompt, rel_bias, top, lamv, sub_col, qt, kh, vt,
                            layer=l, n_seq=B, seq_rows=Lp, lam_init=lam_init)
        brd_s = _decode_call(page_table, bt_sample, rel_bias, lamv, sub, dq, dk, dv, ckt, cvt,
                             layer=l, n_seq=DB, dec_seq=DS, row0=rows_p, lam_init=lam_init)

        h = _tail_call(h1, brl_p, pad_s(brl_s.reshape(rows_s, -1)), brd_p, pad_s(brd_s), w, layer=l,
                       final=(B, S, rows_s) if l == depth - 1 else None)
        per_layer.append(dict(kt=kt, vt=vt, ks=dk[rows_p:rows_p + rows_s], vs=dv[rows_p:rows_p + rows_s],
                              rp=rp, rs=rs, gp=gp, gs=gs, cp=cp, cs=cs))

    st = lambda k: jnp.stack([p[k] for p in per_layer], axis=0)
    seq_major = lambda t: jnp.transpose(t.reshape(depth, B, HEADS, -1, Lp)[..., :L], (0, 1, 4, 2, 3))
    y_prompt, y_rows = h
    y_sample = y_rows.reshape(DB, DS, D)
    return (y_prompt, y_sample, seq_major(st('kt')), seq_major(st('vt')),
            st('ks').reshape(depth, DB, DS, HEADS, 2 * DIFF_HD), st('vs').reshape(depth, DB, DS, HEADS, DIFF_VD),
            st('rp'), st('rs'), st('gp'), st('gs'), st('cp'), st('cs'))
```
